```python
import math
import jax, jax.numpy as jnp
from jax import lax
import numpy as np

D_MODEL = 2048
BATCH = 1
SEQ = 16384
DEPTH = 2
DEC_BATCH = 32
DEC_SEQ = 64
PAST_LEN = 2048

CHUNK = 64
MIX_WIDTH = D_MODEL
SSD_HEAD_DIM = 64
SSD_WIDTH = MIX_WIDTH // 2
SSD_HEADS = SSD_WIDTH // SSD_HEAD_DIM
SSD_GROUPS = 2
SSD_STATE = 128
SSD_CONV = 4
SSD_BLOCK = CHUNK
SSD_CONV_DIM = SSD_WIDTH + 2 * SSD_GROUPS * SSD_STATE
DIFF_HD = 64
DIFF_VD = 2 * DIFF_HD
DIFF_WIDTH = MIX_WIDTH // 4
DIFF_HEADS = DIFF_WIDTH // DIFF_VD
ATTN_QBLOCK = 128
GLA_DK = 64
GLA_DV = 128
GLA_WIDTH = MIX_WIDTH // 4
GLA_HEADS = GLA_WIDTH // GLA_DV
GLA_GATE_RANK = 16
GLA_TAU = 16.0
GLA_BLOCK = 16
D_FF = 4 * D_MODEL
EPS = 1e-6

IN_SPLITS = (SSD_WIDTH, SSD_CONV_DIM, SSD_HEADS,
             DIFF_HEADS * 2 * DIFF_HD, DIFF_HEADS * 2 * DIFF_HD, DIFF_WIDTH,
             GLA_HEADS * GLA_DK, GLA_HEADS * GLA_DK, GLA_WIDTH, GLA_GATE_RANK, GLA_WIDTH)
IN_COLS = sum(IN_SPLITS)

kernel_name = 'hymba_ssd_diffattn_gla_stream_step'


def _rms(x):
    xf = x.astype(jnp.float32)
    return xf * lax.rsqrt(jnp.mean(xf * xf, axis=-1, keepdims=True) + EPS)


def _pad_time(a, pad):
    if pad == 0:
        return a
    widths = [(0, 0)] * a.ndim
    widths[1] = (0, pad)
    return jnp.pad(a, widths)


def _carry_chunks(h0, decay, states):
    def step(h, inp):
        d, s = inp
        return d * h + s, h
    h_last, h_in = lax.scan(step, h0, (jnp.moveaxis(decay, 1, 0), jnp.moveaxis(states, 1, 0)))
    return jnp.moveaxis(h_in, 0, 1), h_last


def _ssd_scan(x, dt, a, bm, cm, h0):
    bsz, L = x.shape[:2]
    hg = SSD_HEADS // SSD_GROUPS
    q = SSD_BLOCK
    pad = (-L) % q
    x, dt, bm, cm = (_pad_time(t, pad) for t in (x, dt, bm, cm))
    nc = (L + pad) // q
    x = x.reshape(bsz, nc, q, SSD_GROUPS, hg, SSD_HEAD_DIM)
    dt = dt.reshape(bsz, nc, q, SSD_GROUPS, hg)
    bm = bm.reshape(bsz, nc, q, SSD_GROUPS, SSD_STATE)
    cm = cm.reshape(bsz, nc, q, SSD_GROUPS, SSD_STATE)
    cs = jnp.cumsum(dt * a.reshape(SSD_GROUPS, hg), axis=2)
    xdt = x * dt[..., None]
    causal = jnp.tril(jnp.ones((q, q), bool))[:, :, None, None]
    seg = cs[:, :, :, None] - cs[:, :, None]
    lmat = jnp.exp(jnp.where(causal, seg, -jnp.inf))
    cb = jnp.einsum('bctgn,bcsgn->bctsg', cm, bm)
    y_diag = jnp.einsum('bctsgh,bcsghp->bctghp', cb[..., None] * lmat, xdt)
    xdt_end = xdt * jnp.exp(cs[:, :, -1:] - cs)[..., None]
    states = jnp.einsum('bcsgn,bcsghp->bcghpn', bm, xdt_end)
    h0g = h0.reshape(bsz, SSD_GROUPS, hg, SSD_HEAD_DIM, SSD_STATE)
    h_in, h_last = _carry_chunks(h0g, jnp.exp(cs[:, :, -1])[..., None, None], states)
    y_off = jnp.einsum('bctgn,bcghpn->bctghp', cm, h_in) * jnp.exp(cs)[..., None]
    y = (y_diag + y_off).reshape(bsz, nc * q, SSD_HEADS, SSD_HEAD_DIM)[:, :L]
    return y, h_last.reshape(bsz, SSD_HEADS, SSD_HEAD_DIM, SSD_STATE)


def _gla_scan(q, k, v, log_a, s0):
    bsz, L = q.shape[:2]
    c = GLA_BLOCK
    pad = (-L) % c
    q, k, v, log_a = (_pad_time(t, pad) for t in (q, k, v, log_a))
    nc = (L + pad) // c
    q = q.reshape(bsz, nc, c, GLA_HEADS, GLA_DK)
    k = k.reshape(bsz, nc, c, GLA_HEADS, GLA_DK)
    log_a = log_a.reshape(bsz, nc, c, GLA_HEADS, GLA_DK)
    v = v.reshape(bsz, nc, c, GLA_HEADS, GLA_DV)
    b = jnp.cumsum(log_a, axis=2)
    qt = q * jnp.exp(b)
    kt = k * jnp.exp(-b)
    causal = jnp.tril(jnp.ones((c, c), bool))
    att = jnp.where(causal, jnp.einsum('bcthk,bcshk->bchts', qt, kt), 0.0)
    o = jnp.einsum('bchts,bcshv->bcthv', att, v)
    states = jnp.einsum('bcshk,bcshv->bchkv', k * jnp.exp(b[:, :, -1:] - b), v)
    s_in, s_last = _carry_chunks(s0, jnp.exp(b[:, :, -1])[..., None], states)
    o = o + jnp.einsum('bcthk,bchkv->bcthv', qt, s_in)
    return o.reshape(bsz, nc * c, GLA_HEADS, GLA_DV)[:, :L], s_last


def _causal_conv(u, prev, w, bias):
    L = u.shape[1]
    full = jnp.concatenate([prev.astype(u.dtype), u], axis=1)
    y = bias + full[:, 0:L] * w[0]
    for j in range(1, SSD_CONV):
        y = y + full[:, j:j + L] * w[j]
    return jax.nn.silu(y), full[:, -(SSD_CONV - 1):]


def _diff_core(q, k, v, q_pos, k_pos, lam, slopes):
    s = jnp.einsum('bqhcd,bkhcd->bhcqk', q, k) * (DIFF_HD ** -0.5)
    dist = jnp.abs(q_pos[:, None] - k_pos[None, :]).astype(jnp.float32)
    s = s - slopes[None, :, None, None, None] * dist
    allowed = (k_pos[None, :] // CHUNK) <= (q_pos[:, None] // CHUNK)
    p = jax.nn.softmax(jnp.where(allowed, s, -jnp.inf), axis=-1)
    a = p[:, :, 0] - lam * p[:, :, 1]
    return jnp.einsum('bhqk,bkhv->bqhv', a, v)


def _diff_attention(q, k, v, q_pos, k_pos, lam, slopes):
    bsz, lq = q.shape[:2]
    if lq <= ATTN_QBLOCK:
        return _diff_core(q, k, v, q_pos, k_pos, lam, slopes)
    nb = lq // ATTN_QBLOCK
    qb = q.reshape(bsz, nb, ATTN_QBLOCK, DIFF_HEADS, 2, DIFF_HD).swapaxes(0, 1)
    pb = q_pos.reshape(nb, ATTN_QBLOCK)
    out = lax.map(lambda t: _diff_core(t[0], k, v, t[1], k_pos, lam, slopes), (qb, pb))
    return out.swapaxes(0, 1).reshape(bsz, lq, DIFF_HEADS, DIFF_VD)


def _layer(x, q_pos, k_past, v_past, conv_prev, h_prev, s_prev, layer_idx,
           norm1_g, w_in, ssd_conv_w, ssd_conv_b, ssd_dt_bias, ssd_a_log, ssd_d, ssd_norm_g,
           diff_qn_g, diff_kn_g, diff_lambda, diff_out_g, gla_wa2, gla_ba, gla_norm_g,
           w_out, norm2_g, w_mlp1, w_mlp2):
    f32 = jnp.float32
    bsz, L, _ = x.shape
    dty = x.dtype
    h = (_rms(x) * norm1_g.astype(f32)).astype(dty)
    proj = h @ w_in
    z, xbc, dt_raw, dq, dk, dv, gq, gk, gv, ga, gg = jnp.split(
        proj, np.cumsum(IN_SPLITS)[:-1].tolist(), axis=-1)

    xbc, conv_new = _causal_conv(xbc, conv_prev, ssd_conv_w, ssd_conv_b)
    xs, bm, cm = jnp.split(xbc, [SSD_WIDTH, SSD_WIDTH + SSD_GROUPS * SSD_STATE], axis=-1)
    xs_h = xs.astype(f32).reshape(bsz, L, SSD_HEADS, SSD_HEAD_DIM)
    dt = jax.nn.softplus(dt_raw.astype(f32) + ssd_dt_bias.astype(f32))
    a = -jnp.exp(ssd_a_log.astype(f32))
    y, h_new = _ssd_scan(xs_h, dt, a,
                         bm.astype(f32).reshape(bsz, L, SSD_GROUPS, SSD_STATE),
                         cm.astype(f32).reshape(bsz, L, SSD_GROUPS, SSD_STATE),
                         h_prev.astype(f32))
    y = y + ssd_d.astype(f32)[:, None] * xs_h
    y = y.reshape(bsz, L, SSD_WIDTH) * jax.nn.silu(z.astype(f32))
    y = _rms(y.reshape(bsz, L, SSD_GROUPS, SSD_WIDTH // SSD_GROUPS)).reshape(bsz, L, SSD_WIDTH)
    y = y * ssd_norm_g.astype(f32)

    q = _rms(dq.reshape(bsz, L, DIFF_HEADS, 2, DIFF_HD)) * diff_qn_g.astype(f32)
    k = _rms(dk.reshape(bsz, L, DIFF_HEADS, 2, DIFF_HD)) * diff_kn_g.astype(f32)
    k_rows = k.astype(dty).reshape(bsz, L, DIFF_HEADS, 2 * DIFF_HD)
    v_rows = dv.reshape(bsz, L, DIFF_HEADS, DIFF_VD)
    if k_past is None:
        k_all, v_all, k_pos = k_rows, v_rows, q_pos
    else:
        k_all = jnp.concatenate([k_past.astype(dty), k_rows], axis=1)
        v_all = jnp.concatenate([v_past.astype(dty), v_rows], axis=1)
        k_pos = jnp.arange(k_past.shape[1] + L)
    lam_init = 0.8 - 0.6 * math.exp(-0.3 * layer_idx)
    lq1, lk1, lq2, lk2 = diff_lambda.astype(f32)
    lam = jnp.exp(jnp.sum(lq1 * lk1)) - jnp.exp(jnp.sum(lq2 * lk2)) + lam_init
    slopes = jnp.exp2(-8.0 * jnp.arange(1, DIFF_HEADS + 1, dtype=f32) / DIFF_HEADS)
    o = _diff_attention(q, k_all.astype(f32).reshape(bsz, -1, DIFF_HEADS, 2, DIFF_HD),
                        v_all.astype(f32), q_pos, k_pos, lam, slopes)
    o = (_rms(o) * diff_out_g.astype(f32) * (1.0 - lam_init)).reshape(bsz, L, DIFF_WIDTH)

    gq_h = gq.astype(f32).reshape(bsz, L, GLA_HEADS, GLA_DK) * (GLA_DK ** -0.5)
    gk_h = gk.astype(f32).reshape(bsz, L, GLA_HEADS, GLA_DK)
    gv_h = gv.astype(f32).reshape(bsz, L, GLA_HEADS, GLA_DV)
    log_a = jax.nn.log_sigmoid((ga @ gla_wa2).astype(f32) + gla_ba.astype(f32)) / GLA_TAU
    o_g, s_new = _gla_scan(gq_h, gk_h, gv_h, log_a.reshape(bsz, L, GLA_HEADS, GLA_DK),
                           s_prev.astype(f32))
    o_g = (_rms(o_g) * gla_norm_g.astype(f32)).reshape(bsz, L, GLA_WIDTH) * jax.nn.silu(gg.astype(f32))

    x = x + jnp.concatenate([y, o, o_g], axis=-1).astype(dty) @ w_out
    h2 = (_rms(x) * norm2_g.astype(f32)).astype(dty)
    x = x + jnp.square(jax.nn.relu(h2 @ w_mlp1)) @ w_mlp2
    return x, k_rows, v_rows, conv_new, h_new.astype(dty), s_new.astype(dty)


def setup_inputs(seed: int = 0) -> dict:
    key = jax.random.key(seed)
    ks = jax.random.split(key, 32)
    f32 = jnp.float32
    nrm = lambda k, shape, s: jax.random.normal(k, shape, f32) * s
    dt0 = jnp.exp(jax.random.uniform(ks[9], (DEPTH, SSD_HEADS), f32) * (math.log(0.1) - math.log(1e-3)) + math.log(1e-3))
    return {
        'x_prompt': nrm(ks[0], (BATCH, SEQ, D_MODEL), 1.0),
        'x_sample': nrm(ks[1], (DEC_BATCH, DEC_SEQ, D_MODEL), 1.0),
        'cache_diff_k': nrm(ks[2], (DEPTH, DEC_BATCH, PAST_LEN, DIFF_HEADS, 2 * DIFF_HD), 1.0),
        'cache_diff_v': nrm(ks[3], (DEPTH, DEC_BATCH, PAST_LEN, DIFF_HEADS, DIFF_VD), 1.0),
        'state_ssd_conv': nrm(ks[4], (DEPTH, DEC_BATCH, SSD_CONV - 1, SSD_CONV_DIM), 1.0),
        'state_ssd': nrm(ks[5], (DEPTH, DEC_BATCH, SSD_HEADS, SSD_HEAD_DIM, SSD_STATE), 0.1),
        'state_gla': nrm(ks[6], (DEPTH, DEC_BATCH, GLA_HEADS, GLA_DK, GLA_DV), 0.5),
        'norm1_g': 1.0 + nrm(ks[7], (DEPTH, D_MODEL), 0.02),
        'w_in': nrm(ks[8], (DEPTH, D_MODEL, IN_COLS), D_MODEL ** -0.5),
        'ssd_conv_w': nrm(ks[10], (DEPTH, SSD_CONV, SSD_CONV_DIM), SSD_CONV ** -0.5),
        'ssd_conv_b': nrm(ks[11], (DEPTH, SSD_CONV_DIM), 0.01),
        'ssd_dt_bias': dt0 + jnp.log(-jnp.expm1(-dt0)),
        'ssd_a_log': jnp.log(jax.random.uniform(ks[12], (DEPTH, SSD_HEADS), f32, 1.0, 16.0)),
        'ssd_d': 1.0 + nrm(ks[13], (DEPTH, SSD_HEADS), 0.1),
        'ssd_norm_g': 1.0 + nrm(ks[14], (DEPTH, SSD_WIDTH), 0.02),
        'diff_qn_g': 1.0 + nrm(ks[15], (DEPTH, DIFF_HD), 0.02),
        'diff_kn_g': 1.0 + nrm(ks[16], (DEPTH, DIFF_HD), 0.02),
        'diff_lambda': nrm(ks[17], (DEPTH, 4, DIFF_HD), 0.1),
        'diff_out_g': 1.0 + nrm(ks[18], (DEPTH, DIFF_VD), 0.02),
        'gla_wa2': nrm(ks[19], (DEPTH, GLA_GATE_RANK, GLA_HEADS * GLA_DK), GLA_GATE_RANK ** -0.5),
        'gla_ba': nrm(ks[20], (DEPTH, GLA_HEADS * GLA_DK), 0.01),
        'gla_norm_g': 1.0 + nrm(ks[21], (DEPTH, GLA_DV), 0.02),
        'w_out': nrm(ks[22], (DEPTH, MIX_WIDTH, D_MODEL), MIX_WIDTH ** -0.5),
        'norm2_g': 1.0 + nrm(ks[23], (DEPTH, D_MODEL), 0.02),
        'w_mlp1': nrm(ks[24], (DEPTH, D_MODEL, D_FF), D_MODEL ** -0.5),
        'w_mlp2': nrm(ks[25], (DEPTH, D_FF, D_MODEL), D_FF ** -0.5),
    }


def reference(x_prompt, x_sample, cache_diff_k, cache_diff_v, state_ssd_conv, state_ssd, state_gla,
              norm1_g, w_in, ssd_conv_w, ssd_conv_b, ssd_dt_bias, ssd_a_log, ssd_d, ssd_norm_g,
              diff_qn_g, diff_kn_g, diff_lambda, diff_out_g, gla_wa2, gla_ba, gla_norm_g,
              w_out, norm2_g, w_mlp1, w_mlp2):
    def params(l):
        return (norm1_g[l], w_in[l], ssd_conv_w[l], ssd_conv_b[l], ssd_dt_bias[l], ssd_a_log[l],
                ssd_d[l], ssd_norm_g[l], diff_qn_g[l], diff_kn_g[l], diff_lambda[l], diff_out_g[l],
                gla_wa2[l], gla_ba[l], gla_norm_g[l], w_out[l], norm2_g[l], w_mlp1[l], w_mlp2[l])

    bp, lp = x_prompt.shape[:2]
    dty = x_prompt.dtype
    pos_p = jnp.arange(lp)
    conv0 = jnp.zeros((bp, SSD_CONV - 1, SSD_CONV_DIM), dty)
    h0 = jnp.zeros((bp, SSD_HEADS, SSD_HEAD_DIM, SSD_STATE), dty)
    s0 = jnp.zeros((bp, GLA_HEADS, GLA_DK, GLA_DV), dty)
    y_p = x_prompt
    outs_p = []
    for l in range(DEPTH):
        y_p, *st = _layer(y_p, pos_p, None, None, conv0, h0, s0, l, *params(l))
        outs_p.append(st)
    kp, vp, cp, hp, sp = (jnp.stack(f) for f in zip(*outs_p))

    ls = x_sample.shape[1]
    pos_s = cache_diff_k.shape[2] + jnp.arange(ls)
    y_s = x_sample
    outs_s = []
    for l in range(DEPTH):
        y_s, *st = _layer(y_s, pos_s, cache_diff_k[l], cache_diff_v[l], state_ssd_conv[l],
                          state_ssd[l], state_gla[l], l, *params(l))
        outs_s.append(st)
    k_s, v_s, c_s, h_s, s_s = (jnp.stack(f) for f in zip(*outs_s))

    return (y_p, y_s, kp, vp, cp, hp, sp, k_s, v_s, c_s, h_s, s_s)
```

```python
import functools
import math

import numpy as np
import jax
import jax.numpy as jnp
from jax import lax
from jax.experimental import pallas as pl
from jax.experimental.pallas import tpu as pltpu

F32 = jnp.float32
BF16 = jnp.bfloat16
HI = lax.Precision.HIGHEST

D_MODEL = 2048
CHUNK = 64
SSD_HEAD_DIM = 64
SSD_WIDTH = 1024
SSD_HEADS = 16
SSD_GROUPS = 2
SSD_STATE = 128
SSD_CONV = 4
SSD_CONV_DIM = SSD_WIDTH + 2 * SSD_GROUPS * SSD_STATE
DIFF_HD = 64
DIFF_VD = 128
DIFF_WIDTH = 512
DIFF_HEADS = 4
GLA_DK = 64
GLA_DV = 128
GLA_WIDTH = 512
GLA_HEADS = 4
GLA_RANK = 16
GLA_TAU = 16.0
GLA_BLOCK = 16
D_FF = 4 * D_MODEL
EPS = 1e-6

LANE = 128
VMEM_CAP = 56 << 20

C_Z, C_XS, C_BC, C_DQ, C_DK, C_DV, C_GV, C_GG, C_GQ, C_GK, C_SM = (
    0, 1024, 2048, 2560, 3072, 3584, 4096, 4608, 5120, 5376, 5632)
PACKED = 5760
PROJ_TN = 1152


def _pick(m, cands):
    for c in cands:
        if m % c == 0:
            return c
    raise ValueError(f"no tile for {m}")


def _cparams(sem, vmem_bytes):
    return pltpu.CompilerParams(dimension_semantics=sem,
                                vmem_limit_bytes=int(min(VMEM_CAP, max(vmem_bytes, 16 << 20))))


def _sigmoid(x):
    return 1.0 / (1.0 + jnp.exp(-x))


def _softplus(x):
    return jnp.maximum(x, 0.0) + jnp.log1p(jnp.exp(-jnp.abs(x)))


def _in_proj_kernel(x_ref, g_ref, w_ref, o_ref, h_ref):
    @pl.when(pl.program_id(1) == 0)
    def _():
        x = x_ref[...]
        ms = jnp.mean(x * x, axis=-1, keepdims=True)
        h_ref[...] = (x * lax.rsqrt(ms + EPS) * g_ref[...]).astype(BF16)

    o_ref[...] = jnp.dot(h_ref[...], w_ref[...], preferred_element_type=F32)


def _in_proj(x, g, w):
    m = x.shape[0]
    tm = _pick(m, (1024, 512, 256, 128, 64))
    tn = PROJ_TN
    vmem = 2 * tm * D_MODEL * 4 + tm * D_MODEL * 2 + 2 * D_MODEL * tn * 2 + 2 * tm * tn * 4 + (4 << 20)
    return pl.pallas_call(
        _in_proj_kernel,
        grid=(m // tm, PACKED // tn),
        in_specs=[pl.BlockSpec((tm, D_MODEL), lambda i, j: (i, 0)),
                  pl.BlockSpec((1, D_MODEL), lambda i, j: (0, 0)),
                  pl.BlockSpec((D_MODEL, tn), lambda i, j: (0, j))],
        out_specs=pl.BlockSpec((tm, tn), lambda i, j: (i, j)),
        out_shape=jax.ShapeDtypeStruct((m, PACKED), F32),
        scratch_shapes=[pltpu.VMEM((tm, D_MODEL), BF16)],
        compiler_params=_cparams(("parallel", "arbitrary"), vmem),
        name="in_proj",
    )(x, g, w)


def _ssd_kernel(ncp, z_ref, xs_ref, bc_ref, sm_ref, st_in_ref, cpx_ref, cpb_ref,
                cwx_ref, cbx_ref, cwb_ref, cbb_ref, dtb_ref, alog_ref, dexp_ref, ng_ref, e_ref,
                y_ref, st_out_ref, st_scr, fx_scr, fb_scr):
    c = pl.program_id(0)
    q = CHUNK

    @pl.when(jnp.logical_or(c == 0, c >= ncp))
    def _():
        st_scr[...] = st_in_ref[0]
        fx_scr[0:8, :] = cpx_ref[0]
        fb_scr[0:8, :] = cpb_ref[0]

    def conv(u_ref, f_scr, w_ref, b_ref):
        u = u_ref[...]
        f_scr[8:8 + q, :] = u
        y = b_ref[...] + f_scr[5:5 + q, :] * w_ref[0:1, :]
        for j in range(1, SSD_CONV):
            y = y + f_scr[5 + j:5 + j + q, :] * w_ref[j:j + 1, :]
        f_scr[0:8, :] = u[q - 8:q, :]
        return y * _sigmoid(y)

    xs = conv(xs_ref, fx_scr, cwx_ref, cbx_ref)
    bcv = conv(bc_ref, fb_scr, cwb_ref, cbb_ref)

    dt = _softplus(sm_ref[...] + dtb_ref[...])
    da = dt * (-jnp.exp(alog_ref[...]))
    r64 = lax.broadcasted_iota(jnp.int32, (q, q), 0)
    c64 = lax.broadcasted_iota(jnp.int32, (q, q), 1)
    tril = (c64 <= r64).astype(F32)
    cs = jnp.dot(tril, da, precision=HI, preferred_element_type=F32)
    both = jnp.dot(jnp.concatenate([dt, cs], axis=0), e_ref[...], precision=HI,
                   preferred_element_type=F32)
    dt_e = both[0:q]
    cs_e = both[q:2 * q]

    row = lax.broadcasted_iota(jnp.int32, (q, SSD_WIDTH), 0)
    sidx = jnp.bitwise_and(lax.broadcasted_iota(jnp.int32, (q, SSD_WIDTH), 1), q - 1)
    cs_row = jnp.sum(jnp.where(sidx == row, cs_e, 0.0), axis=0, keepdims=True)
    cs_last = cs_e[q - 1:q, :]
    lmat = jnp.exp(jnp.where(sidx <= row, cs_e - cs_row, -jnp.inf))

    xdt = xs * dt_e
    xdt_end = (xdt * jnp.exp(cs_last - cs_e)).astype(BF16)
    xdt_b = xdt.astype(BF16)
    ecs = jnp.exp(cs_e)
    st = st_scr[...]
    st_b = st.astype(BF16)
    bcb = bcv.astype(BF16)

    r128 = lax.broadcasted_iota(jnp.int32, (2 * q, LANE), 0)
    c128 = lax.broadcasted_iota(jnp.int32, (2 * q, LANE), 1)
    bd2 = (r128 // q) == (c128 // q)

    hg = SSD_HEADS // SSD_GROUPS
    gw = hg * SSD_HEAD_DIM
    y_parts = []
    new_states = []
    for g in range(SSD_GROUPS):
        bm = bcb[:, g * SSD_STATE:(g + 1) * SSD_STATE]
        cm = bcb[:, (SSD_GROUPS + g) * SSD_STATE:(SSD_GROUPS + g + 1) * SSD_STATE]
        bm_rep = jnp.concatenate([bm] * hg, axis=0)
        cbt = lax.dot_general(cm, bm_rep, (((1,), (1,)), ((), ())),
                              preferred_element_type=F32)
        m_all = (cbt * lmat[:, g * gw:(g + 1) * gw]).astype(BF16)
        y_off = jnp.dot(cm, st_b[:, g * gw:(g + 1) * gw], preferred_element_type=F32)
        for j in range(gw // LANE):
            col = g * gw + j * LANE
            xj = xdt_b[:, col:col + LANE]
            xd = jnp.where(bd2, jnp.concatenate([xj, xj], axis=0), jnp.zeros((), BF16))
            y_parts.append(jnp.dot(m_all[:, j * LANE:(j + 1) * LANE], xd, preferred_element_type=F32)
                           + y_off[:, j * LANE:(j + 1) * LANE] * ecs[:, col:col + LANE])
        new_states.append(lax.dot_general(bm, xdt_end[:, g * gw:(g + 1) * gw], (((0,), (0,)), ((), ())),
                                          preferred_element_type=F32))
    y = jnp.concatenate(y_parts, axis=1)
    st_new = st * jnp.exp(cs_last) + jnp.concatenate(new_states, axis=1)
    st_scr[...] = st_new
    st_out_ref[0] = st_new

    y = y + dexp_ref[...] * xs
    zv = z_ref[...]
    y = y * (zv * _sigmoid(zv))
    outs = []
    for g in range(SSD_GROUPS):
        yg = y[:, g * gw:(g + 1) * gw]
        ms = jnp.mean(yg * yg, axis=-1, keepdims=True)
        outs.append(yg * lax.rsqrt(ms + EPS))
    y_ref[...] = jnp.concatenate(outs, axis=1) * ng_ref[...]


def _ssd(proj, ncp, st_in, cpx, cpb, cwx, cbx, cwb, cbb, dtb, alog, dexp, ng, emat):
    m = proj.shape[0]
    nchunks = m // CHUNK
    q = CHUNK
    smap = lambda c: (jnp.maximum(c - (ncp - 1), 0), 0, 0)
    const2 = lambda c: (0, 0)
    return pl.pallas_call(
        functools.partial(_ssd_kernel, ncp),
        grid=(nchunks,),
        in_specs=[pl.BlockSpec((q, 1024), lambda c: (c, C_Z // 1024)),
                  pl.BlockSpec((q, 1024), lambda c: (c, C_XS // 1024)),
                  pl.BlockSpec((q, 512), lambda c: (c, C_BC // 512)),
                  pl.BlockSpec((q, LANE), lambda c: (c, C_SM // LANE)),
                  pl.BlockSpec((1, SSD_STATE, SSD_WIDTH), smap),
                  pl.BlockSpec((1, 8, 1024), smap),
                  pl.BlockSpec((1, 8, 512), smap),
                  pl.BlockSpec((SSD_CONV, 1024), const2),
                  pl.BlockSpec((1, 1024), const2),
                  pl.BlockSpec((SSD_CONV, 512), const2),
                  pl.BlockSpec((1, 512), const2),
                  pl.BlockSpec((1, LANE), const2),
                  pl.BlockSpec((1, LANE), const2),
                  pl.BlockSpec((1, SSD_WIDTH), const2),
                  pl.BlockSpec((1, SSD_WIDTH), const2),
                  pl.BlockSpec((LANE, SSD_WIDTH), const2)],
        out_specs=[pl.BlockSpec((q, SSD_WIDTH), lambda c: (c, 0)),
                   pl.BlockSpec((1, SSD_STATE, SSD_WIDTH), smap)],
        out_shape=[jax.ShapeDtypeStruct((m, SSD_WIDTH), F32),
                   jax.ShapeDtypeStruct(st_in.shape, F32)],
        scratch_shapes=[pltpu.VMEM((SSD_STATE, SSD_WIDTH), F32),
                        pltpu.VMEM((8 + q, 1024), F32),
                        pltpu.VMEM((8 + q, 512), F32)],
        compiler_params=_cparams(("arbitrary",), 32 << 20),
        name="ssd",
    )(proj, proj, proj, proj, st_in, cpx, cpb, cwx, cbx, cwb, cbb, dtb, alog, dexp, ng, emat)


def _gla_kernel(ncp, gq_ref, gk_ref, gv_ref, gg_ref, sm_ref, s_in_ref, wa_ref, ba_ref, ng_ref,
                o_ref, s_out_ref, s_scr):
    c = pl.program_id(0)
    q = CHUNK
    blk = GLA_BLOCK
    kw = GLA_HEADS * GLA_DK

    @pl.when(jnp.logical_or(c == 0, c >= ncp))
    def _():
        s_scr[...] = s_in_ref[0]

    pre = jnp.dot(sm_ref[...].astype(BF16), wa_ref[...], preferred_element_type=F32) + ba_ref[...]
    log_a = -_softplus(-pre) * (1.0 / GLA_TAU)
    r64 = lax.broadcasted_iota(jnp.int32, (q, q), 0)
    c64 = lax.broadcasted_iota(jnp.int32, (q, q), 1)
    same_blk = (r64 // blk) == (c64 // blk)
    causal = jnp.logical_and(same_blk, c64 <= r64)
    b = jnp.dot(causal.astype(F32), log_a, precision=HI, preferred_element_type=F32)
    gk = gk_ref[...]
    qt = gq_ref[...] * (GLA_DK ** -0.5) * jnp.exp(b)
    kt = (gk * jnp.exp(-b)).astype(BF16)
    b_last = jnp.concatenate(
        [jnp.broadcast_to(b[(j + 1) * blk - 1:(j + 1) * blk, :], (blk, kw)) for j in range(q // blk)], axis=0)
    ktil = (gk * jnp.exp(b_last - b)).astype(BF16)
    qt_b = qt.astype(BF16)
    gv_b = gv_ref[...].astype(BF16)

    lane_head = lax.broadcasted_iota(jnp.int32, (q, kw), 1) // GLA_DK
    o_intra = []
    for h in range(GLA_HEADS):
        qm = jnp.where(lane_head == h, qt_b, jnp.zeros((), BF16))
        att = lax.dot_general(qm, kt, (((1,), (1,)), ((), ())), preferred_element_type=F32)
        att = jnp.where(causal, att, 0.0).astype(BF16)
        o_intra.append(jnp.dot(att, gv_b[:, h * GLA_DV:(h + 1) * GLA_DV], preferred_element_type=F32))
    o_intra = jnp.concatenate(o_intra, axis=1)

    rs = lax.broadcasted_iota(jnp.int32, (GLA_WIDTH, kw), 0) // GLA_DV
    cs_ = lax.broadcasted_iota(jnp.int32, (GLA_WIDTH, kw), 1) // GLA_DK
    bd = rs == cs_
    sw = s_scr[...]
    o_inter = []
    for j in range(q // blk):
        sl = slice(j * blk, (j + 1) * blk)
        o_inter.append(lax.dot_general(qt_b[sl], sw.astype(BF16), (((1,), (1,)), ((), ())),
                                       preferred_element_type=F32))
        upd = lax.dot_general(gv_b[sl], ktil[sl], (((0,), (0,)), ((), ())),
                              preferred_element_type=F32)
        decay = jnp.exp(b[(j + 1) * blk - 1:(j + 1) * blk, :])
        sw = sw * decay + jnp.where(bd, upd, 0.0)
    s_scr[...] = sw
    s_out_ref[0] = sw
    o = o_intra + jnp.concatenate(o_inter, axis=0)

    gg = gg_ref[...]
    gate = gg * _sigmoid(gg)
    outs = []
    for h in range(GLA_HEADS):
        oh = o[:, h * GLA_DV:(h + 1) * GLA_DV]
        ms = jnp.mean(oh * oh, axis=-1, keepdims=True)
        outs.append(oh * lax.rsqrt(ms + EPS) * ng_ref[...])
    o_ref[...] = jnp.concatenate(outs, axis=1) * gate


def _gla(proj, ncp, s_in, wa_pad, ba, ng):
    m = proj.shape[0]
    q = CHUNK
    kw = GLA_HEADS * GLA_DK
    smap = lambda c: (jnp.maximum(c - (ncp - 1), 0), 0, 0)
    const2 = lambda c: (0, 0)
    return pl.pallas_call(
        functools.partial(_gla_kernel, ncp),
        grid=(m // q,),
        in_specs=[pl.BlockSpec((q, kw), lambda c: (c, C_GQ // kw)),
                  pl.BlockSpec((q, kw), lambda c: (c, C_GK // kw)),
                  pl.BlockSpec((q, GLA_WIDTH), lambda c: (c, C_GV // GLA_WIDTH)),
                  pl.BlockSpec((q, GLA_WIDTH), lambda c: (c, C_GG // GLA_WIDTH)),
                  pl.BlockSpec((q, LANE), lambda c: (c, C_SM // LANE)),
                  pl.BlockSpec((1, GLA_WIDTH, kw), smap),
                  pl.BlockSpec((LANE, kw), const2),
                  pl.BlockSpec((1, kw), const2),
                  pl.BlockSpec((1, GLA_DV), const2)],
        out_specs=[pl.BlockSpec((q, GLA_WIDTH), lambda c: (c, 0)),
                   pl.BlockSpec((1, GLA_WIDTH, kw), smap)],
        out_shape=[jax.ShapeDtypeStruct((m, GLA_WIDTH), F32),
                   jax.ShapeDtypeStruct(s_in.shape, F32)],
        scratch_shapes=[pltpu.VMEM((GLA_WIDTH, kw), F32)],
        compiler_params=_cparams(("arbitrary",), 24 << 20),
        name="gla",
    )(proj, proj, proj, proj, proj, s_in, wa_pad, ba, ng)


def _attn_prep_kernel(dq_ref, dk_ref, dv_ref, qg_ref, kg_ref, seg_ref,
                      qa_ref, qb_ref, kn_ref, kb_ref, vt_ref):
    seg = seg_ref[...]

    def qknorm(x, g):
        ms = jnp.dot(x * x, seg, precision=HI, preferred_element_type=F32) * (1.0 / DIFF_HD)
        return x * lax.rsqrt(ms + EPS) * g

    qn = qknorm(dq_ref[...], qg_ref[...]) * (DIFF_HD ** -0.5)
    first = (lax.broadcasted_iota(jnp.int32, qn.shape, 1) // DIFF_HD) % 2 == 0
    qa_ref[...] = jnp.where(first, qn, 0.0).astype(BF16)
    qb_ref[...] = jnp.where(first, 0.0, qn).astype(BF16)
    kn = qknorm(dk_ref[...], kg_ref[...])
    kn_ref[...] = kn
    kb_ref[...] = kn.astype(BF16)
    vt_ref[...] = dv_ref[...].T.astype(BF16)


def _attn_prep(proj, qg, kg, segmat):
    m = proj.shape[0]
    tm = _pick(m, (512, 256, 128))
    w = DIFF_WIDTH
    row = lambda off: pl.BlockSpec((tm, w), lambda i: (i, off // w))
    const2 = lambda i: (0, 0)
    return pl.pallas_call(
        _attn_prep_kernel,
        grid=(m // tm,),
        in_specs=[row(C_DQ), row(C_DK), row(C_DV),
                  pl.BlockSpec((1, w), const2), pl.BlockSpec((1, w), const2),
                  pl.BlockSpec((w, w), const2)],
        out_specs=[pl.BlockSpec((tm, w), lambda i: (i, 0))] * 4 + [pl.BlockSpec((w, tm), lambda i: (0, i))],
        out_shape=[jax.ShapeDtypeStruct((m, w), BF16), jax.ShapeDtypeStruct((m, w), BF16),
                   jax.ShapeDtypeStruct((m, w), F32), jax.ShapeDtypeStruct((m, w), BF16),
                   jax.ShapeDtypeStruct((w, m), BF16)],
        compiler_params=_cparams(("parallel",), 32 << 20),
        name="attn_prep",
    )(proj, proj, proj, qg, kg, segmat)


def _lambda_full(lam_ref, lam_init):
    l = lam_ref[...]
    a = jnp.sum(l[0:1] * l[1:2], axis=-1, keepdims=True)
    b = jnp.sum(l[2:3] * l[3:4], axis=-1, keepdims=True)
    return jnp.exp(a) - jnp.exp(b) + lam_init


def _attn_prompt_kernel(t, lam_init, qi_ref, ki_ref, slope_ref, qa_ref, qb_ref, kb_ref, vt_ref, lam_ref, og_ref,
                        o_ref, m_scr, l_scr, acc_scr, b0_scr, bd_scr):
    h = pl.program_id(0)
    p = pl.program_id(1)
    qi = qi_ref[p]
    ki = ki_ref[p]
    slope = slope_ref[h]

    @pl.when(p == 0)
    def _():
        kr = lax.broadcasted_iota(jnp.int32, (t, t), 0)
        qc = lax.broadcasted_iota(jnp.int32, (t, t), 1)
        b0_scr[...] = slope * kr.astype(F32)
        diag = slope * jnp.where(kr <= qc, kr, 2 * qc - kr).astype(F32)
        bd_scr[...] = jnp.where(kr // CHUNK <= qc // CHUNK, diag, -jnp.inf)

    @pl.when(ki == 0)
    def _():
        m_scr[...] = jnp.full(m_scr.shape, -jnp.inf, F32)
        l_scr[...] = jnp.zeros(l_scr.shape, F32)
        acc_scr[...] = jnp.zeros(acc_scr.shape, F32)

    def step(bias_ref):
        k = kb_ref[...]
        vt = vt_ref[...]
        shift = slope * ((ki - qi) * t).astype(F32)
        for c, q_ref in enumerate((qa_ref, qb_ref)):
            s = lax.dot_general(k, q_ref[...], (((1,), (1,)), ((), ())),
                                preferred_element_type=F32) + bias_ref[...]
            m_old = m_scr[c]
            m_new = jnp.maximum(m_old, jnp.max(s, axis=0, keepdims=True) + shift)
            pt = jnp.exp(s - (m_new - shift))
            alpha = jnp.exp(m_old - m_new)
            l_scr[c] = alpha * l_scr[c] + jnp.sum(pt, axis=0, keepdims=True)
            acc_scr[c] = alpha * acc_scr[c] + jnp.dot(vt, pt.astype(BF16), preferred_element_type=F32)
            m_scr[c] = m_new

    @pl.when(ki < qi)
    def _():
        step(b0_scr)

    @pl.when(ki == qi)
    def _():
        step(bd_scr)
        lam = _lambda_full(lam_ref, lam_init)
        ot = acc_scr[0] / l_scr[0] - lam * (acc_scr[1] / l_scr[1])
        ms = jnp.mean(ot * ot, axis=0, keepdims=True)
        ot = ot * lax.rsqrt(ms + EPS)
        o_ref[...] = ot.T * (og_ref[...] * (1.0 - lam_init))


def _attn_prompt(lp, lam_init, qa, qb, kb, vt, slopes, lam_p, og):
    t = _pick(lp, (512, 256, 128, 64))
    nq = lp // t
    pairs = [(i, j) for i in range(nq) for j in range(i + 1)]
    qi_l = jnp.asarray(np.array([a for a, _ in pairs], np.int32))
    ki_l = jnp.asarray(np.array([b for _, b in pairs], np.int32))
    grid_spec = pltpu.PrefetchScalarGridSpec(
        num_scalar_prefetch=3,
        grid=(DIFF_HEADS, len(pairs)),
        in_specs=[pl.BlockSpec((t, LANE), lambda h, p, qi, ki, sl: (qi[p], h)),
                  pl.BlockSpec((t, LANE), lambda h, p, qi, ki, sl: (qi[p], h)),
                  pl.BlockSpec((t, LANE), lambda h, p, qi, ki, sl: (ki[p], h)),
                  pl.BlockSpec((LANE, t), lambda h, p, qi, ki, sl: (h, ki[p])),
                  pl.BlockSpec((4, DIFF_HD), lambda h, p, qi, ki, sl: (0, 0)),
                  pl.BlockSpec((1, DIFF_VD), lambda h, p, qi, ki, sl: (0, 0))],
        out_specs=pl.BlockSpec((t, LANE), lambda h, p, qi, ki, sl: (qi[p], h)),
        scratch_shapes=[pltpu.VMEM((2, 1, t), F32), pltpu.VMEM((2, 1, t), F32),
                        pltpu.VMEM((2, DIFF_VD, t), F32),
                        pltpu.VMEM((t, t), F32), pltpu.VMEM((t, t), F32)])
    return pl.pallas_call(
        functools.partial(_attn_prompt_kernel, t, lam_init),
        grid_spec=grid_spec,
        out_shape=jax.ShapeDtypeStruct((lp, DIFF_WIDTH), F32),
        compiler_params=_cparams(("arbitrary", "arbitrary"), 40 << 20),
        name="attn_prompt",
    )(qi_l, ki_l, slopes, qa, qb, kb, vt, lam_p, og)


def _attn_sample_kernel(past, lam_init, slope_ref, qa_ref, qb_ref, kn_ref, vn_ref, kc_ref, vc_ref, lam_ref, og_ref,
                        o_ref):
    q = CHUNK
    lam = _lambda_full(lam_ref, lam_init)
    kpos = lax.broadcasted_iota(jnp.int32, (1, past), 1).astype(F32)
    r = lax.broadcasted_iota(jnp.int32, (q, q), 0)
    cc = lax.broadcasted_iota(jnp.int32, (q, q), 1)
    new_bias = (past + r - jnp.abs(r - cc)).astype(F32)
    outs = []
    for h in range(DIFF_HEADS):
        slope = slope_ref[h]
        sl = slice(h * LANE, (h + 1) * LANE)
        kc = kc_ref[0, :, sl].astype(BF16)
        vc = vc_ref[0, :, sl].astype(BF16)
        kn = kn_ref[:, sl].astype(BF16)
        vn = vn_ref[:, sl].astype(BF16)
        res = []
        for q_ref in (qa_ref, qb_ref):
            qv = q_ref[:, sl]
            s_c = lax.dot_general(qv, kc, (((1,), (1,)), ((), ())), preferred_element_type=F32) + slope * kpos
            s_n = lax.dot_general(qv, kn, (((1,), (1,)), ((), ())), preferred_element_type=F32) + slope * new_bias
            mx = jnp.maximum(jnp.max(s_c, axis=-1, keepdims=True), jnp.max(s_n, axis=-1, keepdims=True))
            p_c = jnp.exp(s_c - mx)
            p_n = jnp.exp(s_n - mx)
            den = jnp.sum(p_c, axis=-1, keepdims=True) + jnp.sum(p_n, axis=-1, keepdims=True)
            num = (jnp.dot(p_c.astype(BF16), vc, preferred_element_type=F32)
                   + jnp.dot(p_n.astype(BF16), vn, preferred_element_type=F32))
            res.append(num / den)
        o = res[0] - lam * res[1]
        ms = jnp.mean(o * o, axis=-1, keepdims=True)
        outs.append(o * lax.rsqrt(ms + EPS) * (og_ref[...] * (1.0 - lam_init)))
    o_ref[...] = jnp.concatenate(outs, axis=1)


def _attn_sample(lp, nb, lam_init, qa, qb, kn, proj, cache_k, cache_v, slopes, lam_p, og):
    past = cache_k.shape[1]
    q = CHUNK
    w = DIFF_WIDTH
    off = lp // q
    grid_spec = pltpu.PrefetchScalarGridSpec(
        num_scalar_prefetch=1,
        grid=(nb,),
        in_specs=[pl.BlockSpec((q, w), lambda b, sl: (off + b, 0)),
                  pl.BlockSpec((q, w), lambda b, sl: (off + b, 0)),
                  pl.BlockSpec((q, w), lambda b, sl: (off + b, 0)),
                  pl.BlockSpec((q, w), lambda b, sl: (off + b, C_DV // w)),
                  pl.BlockSpec((1, past, w), lambda b, sl: (b, 0, 0)),
                  pl.BlockSpec((1, past, w), lambda b, sl: (b, 0, 0)),
                  pl.BlockSpec((4, DIFF_HD), lambda b, sl: (0, 0)),
                  pl.BlockSpec((1, DIFF_VD), lambda b, sl: (0, 0))],
        out_specs=pl.BlockSpec((q, w), lambda b, sl: (b, 0)))
    return pl.pallas_call(
        functools.partial(_attn_sample_kernel, past, lam_init),
        grid_spec=grid_spec,
        out_shape=jax.ShapeDtypeStruct((nb * q, w), F32),
        compiler_params=_cparams(("arbitrary",), 40 << 20),
        name="attn_sample",
    )(slopes, qa, qb, kn, proj, cache_k, cache_v, lam_p, og)


def _out_proj_kernel(x_ref, ys_ref, ya_ref, yg_ref, w_ref, o_ref):
    acc = jnp.dot(ys_ref[...].astype(BF16), w_ref[0:SSD_WIDTH, :], preferred_element_type=F32)
    acc = acc + jnp.dot(ya_ref[...].astype(BF16), w_ref[SSD_WIDTH:SSD_WIDTH + DIFF_WIDTH, :],
                        preferred_element_type=F32)
    acc = acc + jnp.dot(yg_ref[...].astype(BF16), w_ref[SSD_WIDTH + DIFF_WIDTH:, :],
                        preferred_element_type=F32)
    o_ref[...] = x_ref[...] + acc


def _out_proj(x, ys, ya, yg, w):
    m = x.shape[0]
    tm = _pick(m, (512, 256, 128, 64))
    vmem = 2 * (2 * tm * D_MODEL * 4) + 2 * (tm * D_MODEL * 4) + 2 * D_MODEL * D_MODEL * 2 + (4 << 20)
    return pl.pallas_call(
        _out_proj_kernel,
        grid=(m // tm,),
        in_specs=[pl.BlockSpec((tm, D_MODEL), lambda i: (i, 0)),
                  pl.BlockSpec((tm, SSD_WIDTH), lambda i: (i, 0)),
                  pl.BlockSpec((tm, DIFF_WIDTH), lambda i: (i, 0)),
                  pl.BlockSpec((tm, GLA_WIDTH), lambda i: (i, 0)),
                  pl.BlockSpec((D_MODEL, D_MODEL), lambda i: (0, 0))],
        out_specs=pl.BlockSpec((tm, D_MODEL), lambda i: (i, 0)),
        out_shape=jax.ShapeDtypeStruct((m, D_MODEL), F32),
        compiler_params=_cparams(("parallel",), vmem),
        name="out_proj",
    )(x, ys, ya, yg, w)


def _mlp_kernel(x_ref, g_ref, w1_ref, w2_ref, o_ref, h_ref):
    @pl.when(pl.program_id(1) == 0)
    def _():
        x = x_ref[...]
        ms = jnp.mean(x * x, axis=-1, keepdims=True)
        h_ref[...] = (x * lax.rsqrt(ms + EPS) * g_ref[...]).astype(BF16)
        o_ref[...] = x

    a = jnp.dot(h_ref[...], w1_ref[...], preferred_element_type=F32)
    a = jnp.square(jnp.maximum(a, 0.0)).astype(BF16)
    o_ref[...] += jnp.dot(a, w2_ref[...], preferred_element_type=F32)


def _mlp(x, g, w1, w2):
    m = x.shape[0]
    tm = _pick(m, (512, 256, 128, 64))
    tf = 1024
    vmem = 4 * tm * D_MODEL * 4 + tm * D_MODEL * 2 + 4 * D_MODEL * tf * 2 + 2 * tm * tf * 4 + (4 << 20)
    return pl.pallas_call(
        _mlp_kernel,
        grid=(m // tm, D_FF // tf),
        in_specs=[pl.BlockSpec((tm, D_MODEL), lambda i, f: (i, 0)),
                  pl.BlockSpec((1, D_MODEL), lambda i, f: (0, 0)),
                  pl.BlockSpec((D_MODEL, tf), lambda i, f: (0, f)),
                  pl.BlockSpec((tf, D_MODEL), lambda i, f: (f, 0))],
        out_specs=pl.BlockSpec((tm, D_MODEL), lambda i, f: (i, 0)),
        out_shape=jax.ShapeDtypeStruct((m, D_MODEL), F32),
        scratch_shapes=[pltpu.VMEM((tm, D_MODEL), BF16)],
        compiler_params=_cparams(("parallel", "arbitrary"), vmem),
        name="mlp",
    )(x, g, w1, w2)


def _pack_w_in(w):
    return jnp.concatenate(
        [w[:, 0:2560], w[:, 2576:4112], w[:, 4624:5136], w[:, 5152:5664], w[:, 4112:4624],
         w[:, 2560:2576], w[:, 5136:5152], jnp.zeros((w.shape[0], PACKED - 5664), w.dtype)],
        axis=1).astype(BF16)


def _pad_lanes(v, width=LANE):
    v = v.reshape(1, -1)
    return jnp.pad(v, ((0, 0), (0, width - v.shape[1])))


def kernel(x_prompt, x_sample, cache_diff_k, cache_diff_v, state_ssd_conv, state_ssd, state_gla, norm1_g, w_in, ssd_conv_w, ssd_conv_b, ssd_dt_bias, ssd_a_log, ssd_d, ssd_norm_g, diff_qn_g, diff_kn_g, diff_lambda, diff_out_g, gla_wa2, gla_ba, gla_norm_g, w_out, norm2_g, w_mlp1, w_mlp2):
    bp, lp, d = x_prompt.shape
    nb, ls, _ = x_sample.shape
    depth = w_in.shape[0]
    past = cache_diff_k.shape[2]
    assert bp == 1 and d == D_MODEL and ls == CHUNK and lp % CHUNK == 0 and past % CHUNK == 0
    ncp = lp // CHUNK
    m = lp + nb * ls

    x = jnp.concatenate([x_prompt.reshape(lp, d), x_sample.reshape(nb * ls, d)], axis=0)

    emat = (jnp.arange(LANE)[:, None] == (jnp.arange(SSD_WIDTH)[None, :] // SSD_HEAD_DIM)).astype(F32)
    segmat = ((jnp.arange(DIFF_WIDTH)[:, None] // DIFF_HD) == (jnp.arange(DIFF_WIDTH)[None, :] // DIFF_HD)).astype(F32)
    slopes = jnp.exp2(-8.0 * jnp.arange(1, DIFF_HEADS + 1, dtype=F32) / DIFF_HEADS)
    eye_h = jnp.eye(GLA_HEADS, dtype=F32)

    outs = {k: [] for k in ("kp", "vp", "cp", "hp", "sp", "ks", "vs", "cs", "hs", "ss")}
    for l in range(depth):
        lam_init = 0.8 - 0.6 * math.exp(-0.3 * l)
        proj = _in_proj(x, norm1_g[l].reshape(1, d), _pack_w_in(w_in[l]))

        st_all = jnp.concatenate([jnp.zeros((1,) + state_ssd.shape[2:], F32), state_ssd[l]], axis=0)
        st_in = st_all.reshape(nb + 1, SSD_WIDTH, SSD_STATE).transpose(0, 2, 1)
        cprev = jnp.concatenate([jnp.zeros((1, SSD_CONV - 1, SSD_CONV_DIM), F32), state_ssd_conv[l]], axis=0)
        cprev = jnp.pad(cprev, ((0, 0), (8 - (SSD_CONV - 1), 0), (0, 0)))
        cw, cb = ssd_conv_w[l], ssd_conv_b[l].reshape(1, -1)
        y_ssd, st_out = _ssd(
            proj, ncp, st_in, cprev[:, :, :SSD_WIDTH], cprev[:, :, SSD_WIDTH:],
            cw[:, :SSD_WIDTH], cb[:, :SSD_WIDTH], cw[:, SSD_WIDTH:], cb[:, SSD_WIDTH:],
            _pad_lanes(ssd_dt_bias[l]), _pad_lanes(ssd_a_log[l]),
            jnp.repeat(ssd_d[l], SSD_HEAD_DIM).reshape(1, -1), ssd_norm_g[l].reshape(1, -1), emat)
        h_all = st_out.transpose(0, 2, 1).reshape(nb + 1, SSD_HEADS, SSD_HEAD_DIM, SSD_STATE)
        outs["hp"].append(h_all[0:1])
        outs["hs"].append(h_all[1:])
        xbc_raw = proj[:, C_XS:C_XS + SSD_CONV_DIM]
        outs["cp"].append(xbc_raw[lp - (SSD_CONV - 1):lp][None])
        outs["cs"].append(xbc_raw[lp:].reshape(nb, ls, SSD_CONV_DIM)[:, ls - (SSD_CONV - 1):])

        s_all = jnp.concatenate([jnp.zeros((1,) + state_gla.shape[2:], F32), state_gla[l]], axis=0)
        s_in = jnp.einsum('bhkv,hg->bhvgk', s_all, eye_h).reshape(nb + 1, GLA_WIDTH, GLA_HEADS * GLA_DK)
        wa_pad = jnp.zeros((LANE, GLA_HEADS * GLA_DK), F32).at[GLA_RANK:2 * GLA_RANK].set(gla_wa2[l]).astype(BF16)
        y_gla, s_out = _gla(proj, ncp, s_in, wa_pad, gla_ba[l].reshape(1, -1), gla_norm_g[l].reshape(1, -1))
        s5 = s_out.reshape(nb + 1, GLA_HEADS, GLA_DV, GLA_HEADS, GLA_DK)
        s_new = jnp.stack([s5[:, h, :, h, :] for h in range(GLA_HEADS)], axis=1).transpose(0, 1, 3, 2)
        outs["sp"].append(s_new[0:1])
        outs["ss"].append(s_new[1:])

        qg = jnp.tile(diff_qn_g[l], 2 * DIFF_HEADS).reshape(1, -1)
        kg = jnp.tile(diff_kn_g[l], 2 * DIFF_HEADS).reshape(1, -1)
        qa, qb, kn, kb, vt = _attn_prep(proj, qg, kg, segmat)
        og = diff_out_g[l].reshape(1, -1)
        ya_p = _attn_prompt(lp, lam_init, qa, qb, kb, vt, slopes, diff_lambda[l], og)
        ya_s = _attn_sample(lp, nb, lam_init, qa, qb, kn, proj,
                            cache_diff_k[l].reshape(nb, past, DIFF_WIDTH),
                            cache_diff_v[l].reshape(nb, past, DIFF_WIDTH), slopes, diff_lambda[l], og)
        y_att = jnp.concatenate([ya_p, ya_s], axis=0)
        v_rows = proj[:, C_DV:C_DV + DIFF_WIDTH]
        outs["kp"].append(kn[:lp].reshape(1, lp, DIFF_HEADS, 2 * DIFF_HD))
        outs["ks"].append(kn[lp:].reshape(nb, ls, DIFF_HEADS, 2 * DIFF_HD))
        outs["vp"].append(v_rows[:lp].reshape(1, lp, DIFF_HEADS, DIFF_VD))
        outs["vs"].append(v_rows[lp:].reshape(nb, ls, DIFF_HEADS, DIFF_VD))

        x = _out_proj(x, y_ssd, y_att, y_gla, w_out[l].astype(BF16))
        x = _mlp(x, norm2_g[l].reshape(1, d), w_mlp1[l].astype(BF16), w_mlp2[l].astype(BF16))

    st = {k: jnp.stack(v) for k, v in outs.items()}
    return (x[:lp].reshape(1, lp, d), x[lp:].reshape(nb, ls, d),
            st["kp"], st["vp"], st["cp"], st["hp"], st["sp"],
            st["ks"], st["vs"], st["cs"], st["hs"], st["ss"])
```

```python
import functools
import math

import numpy as np
import jax
import jax.numpy as jnp
from jax import lax
from jax.experimental import pallas as pl
from jax.experimental.pallas import tpu as pltpu

F32 = jnp.float32
BF16 = jnp.bfloat16
HI = lax.Precision.HIGHEST

D_MODEL = 2048
CHUNK = 64
SSD_HEAD_DIM = 64
SSD_WIDTH = 1024
SSD_HEADS = 16
SSD_GROUPS = 2
SSD_STATE = 128
SSD_CONV = 4
SSD_CONV_DIM = SSD_WIDTH + 2 * SSD_GROUPS * SSD_STATE
DIFF_HD = 64
DIFF_VD = 128
DIFF_WIDTH = 512
DIFF_HEADS = 4
GLA_DK = 64
GLA_DV = 128
GLA_WIDTH = 512
GLA_HEADS = 4
GLA_RANK = 16
GLA_TAU = 16.0
GLA_BLOCK = 16
D_FF = 4 * D_MODEL
EPS = 1e-6

LOG2E = math.log2(math.e)
LANE = 128
SUBLANE = 8
VMEM_CAP = 56 << 20

C_Z, C_XS, C_BC, C_DQ, C_DK, C_DV, C_GV, C_GG, C_GQ, C_GK, C_SM = (
    0, 1024, 2048, 2560, 3072, 3584, 4096, 4608, 5120, 5376, 5632)
PACKED = 5760
PROJ_TN = 1920


def _pick(m, cands):
    for c in cands:
        if m % c == 0:
            return c
    raise ValueError(f"no tile for {m}")


def _cparams(sem, vmem_bytes):
    return pltpu.CompilerParams(dimension_semantics=sem,
                                vmem_limit_bytes=int(min(VMEM_CAP, max(vmem_bytes, 16 << 20))))


def _sigmoid(x):
    return 1.0 / (1.0 + jnp.exp(-x))


def _softplus(x):
    return jnp.maximum(x, 0.0) + jnp.log1p(jnp.exp(-jnp.abs(x)))


def _rms_bf16(x, g):
    ms = jnp.mean(x * x, axis=-1, keepdims=True)
    return (x * lax.rsqrt(ms + EPS) * g).astype(BF16)


def _in_proj_kernel(npt, xp_ref, xs_ref, g_ref, w_ref, o_ref, h_ref):
    i = pl.program_id(0)

    @pl.when(jnp.logical_and(pl.program_id(1) == 0, i < npt))
    def _():
        h_ref[...] = _rms_bf16(xp_ref[...], g_ref[...])

    @pl.when(jnp.logical_and(pl.program_id(1) == 0, i >= npt))
    def _():
        h_ref[...] = _rms_bf16(xs_ref[...], g_ref[...])

    o_ref[...] = jnp.dot(h_ref[...], w_ref[...], preferred_element_type=F32)


def _split_maps(npt):
    return (lambda i, *_: (jnp.minimum(i, npt - 1), 0)), (lambda i, *_: (jnp.maximum(i - npt, 0), 0))


def _in_proj(xp, xs, g, w):
    lp, ms = xp.shape[0], xs.shape[0]
    tm = _pick(math.gcd(lp, ms), (512, 256, 128, 64))
    tn = PROJ_TN
    npt = lp // tm
    pmap, smap = _split_maps(npt)
    vmem = 4 * tm * D_MODEL * 4 + tm * D_MODEL * 2 + 2 * D_MODEL * tn * 2 + 2 * tm * tn * 4 + (4 << 20)
    return pl.pallas_call(
        functools.partial(_in_proj_kernel, npt),
        grid=((lp + ms) // tm, PACKED // tn),
        in_specs=[pl.BlockSpec((tm, D_MODEL), pmap),
                  pl.BlockSpec((tm, D_MODEL), smap),
                  pl.BlockSpec((1, D_MODEL), lambda i, j: (0, 0)),
                  pl.BlockSpec((D_MODEL, tn), lambda i, j: (0, j))],
        out_specs=pl.BlockSpec((tm, tn), lambda i, j: (i, j)),
        out_shape=jax.ShapeDtypeStruct((lp + ms, PACKED), F32),
        scratch_shapes=[pltpu.VMEM((tm, D_MODEL), BF16)],
        compiler_params=_cparams(("parallel", "arbitrary"), vmem),
        name="in_proj",
    )(xp, xs, g, w)


def _ssd_kernel(ncp, z_ref, xs_ref, bc_ref, sm_ref, st_in_ref, cpx_ref, cpb_ref,
                cwx_ref, cbx_ref, cwb_ref, cbb_ref, dtb_ref, alog_ref, dexp_ref, ng_ref, e_ref,
                y_ref, st_out_ref, st_scr, fx_scr, fb_scr):
    c = pl.program_id(0)
    q = CHUNK

    @pl.when(jnp.logical_or(c == 0, c >= ncp))
    def _():
        st_scr[...] = st_in_ref[0]
        fx_scr[0:8, :] = cpx_ref[0]
        fb_scr[0:8, :] = cpb_ref[0]

    def conv(u_ref, f_scr, w_ref, b_ref):
        u = u_ref[...]
        f_scr[8:8 + q, :] = u
        y = b_ref[...] + f_scr[5:5 + q, :] * w_ref[0:1, :]
        for j in range(1, SSD_CONV):
            y = y + f_scr[5 + j:5 + j + q, :] * w_ref[j:j + 1, :]
        f_scr[0:8, :] = u[q - 8:q, :]
        return y * _sigmoid(y)

    xs = conv(xs_ref, fx_scr, cwx_ref, cbx_ref)
    bcv = conv(bc_ref, fb_scr, cwb_ref, cbb_ref)

    dt = _softplus(sm_ref[...] + dtb_ref[...])
    da = dt * (-jnp.exp(alog_ref[...]))
    r64 = lax.broadcasted_iota(jnp.int32, (q, q), 0)
    c64 = lax.broadcasted_iota(jnp.int32, (q, q), 1)
    tril = (c64 <= r64).astype(F32)
    cs = jnp.dot(tril, da, precision=HI, preferred_element_type=F32)
    both = jnp.dot(jnp.concatenate([dt, cs], axis=0), e_ref[...], precision=HI,
                   preferred_element_type=F32)
    dt_e = both[0:q]
    cs_e = both[q:2 * q]

    row = lax.broadcasted_iota(jnp.int32, (q, SSD_WIDTH), 0)
    sidx = jnp.bitwise_and(lax.broadcasted_iota(jnp.int32, (q, SSD_WIDTH), 1), q - 1)
    cs_row = jnp.sum(jnp.where(sidx == row, cs_e, 0.0), axis=0, keepdims=True)
    cs_last = cs_e[q - 1:q, :]
    lmat = jnp.exp(jnp.where(sidx <= row, cs_e - cs_row, -jnp.inf))

    xdt = xs * dt_e
    xdt_end = (xdt * jnp.exp(cs_last - cs_e)).astype(BF16)
    xdt_b = xdt.astype(BF16)
    ecs = jnp.exp(cs_e)
    st = st_scr[...]
    st_b = st.astype(BF16)
    bcb = bcv.astype(BF16)

    r128 = lax.broadcasted_iota(jnp.int32, (2 * q, LANE), 0)
    c128 = lax.broadcasted_iota(jnp.int32, (2 * q, LANE), 1)
    bd2 = (r128 // q) == (c128 // q)

    hg = SSD_HEADS // SSD_GROUPS
    gw = hg * SSD_HEAD_DIM
    y_parts = []
    new_states = []
    for g in range(SSD_GROUPS):
        bm = bcb[:, g * SSD_STATE:(g + 1) * SSD_STATE]
        cm = bcb[:, (SSD_GROUPS + g) * SSD_STATE:(SSD_GROUPS + g + 1) * SSD_STATE]
        bm_rep = jnp.concatenate([bm] * hg, axis=0)
        cbt = lax.dot_general(cm, bm_rep, (((1,), (1,)), ((), ())),
                              preferred_element_type=F32)
        m_all = (cbt * lmat[:, g * gw:(g + 1) * gw]).astype(BF16)
        y_off = jnp.dot(cm, st_b[:, g * gw:(g + 1) * gw], preferred_element_type=F32)
        for j in range(gw // LANE):
            col = g * gw + j * LANE
            xj = xdt_b[:, col:col + LANE]
            xd = jnp.where(bd2, jnp.concatenate([xj, xj], axis=0), jnp.zeros((), BF16))
            y_parts.append(jnp.dot(m_all[:, j * LANE:(j + 1) * LANE], xd, preferred_element_type=F32)
                           + y_off[:, j * LANE:(j + 1) * LANE] * ecs[:, col:col + LANE])
        new_states.append(lax.dot_general(bm, xdt_end[:, g * gw:(g + 1) * gw], (((0,), (0,)), ((), ())),
                                          preferred_element_type=F32))
    y = jnp.concatenate(y_parts, axis=1)
    st_new = st * jnp.exp(cs_last) + jnp.concatenate(new_states, axis=1)
    st_scr[...] = st_new
    st_out_ref[0] = st_new

    y = y + dexp_ref[...] * xs
    zv = z_ref[...]
    y = y * (zv * _sigmoid(zv))
    outs = []
    for g in range(SSD_GROUPS):
        yg = y[:, g * gw:(g + 1) * gw]
        ms = jnp.mean(yg * yg, axis=-1, keepdims=True)
        outs.append(yg * lax.rsqrt(ms + EPS))
    y_ref[...] = jnp.concatenate(outs, axis=1) * ng_ref[...]


def _ssd(proj, ncp, st_in, cpx, cpb, cwx, cbx, cwb, cbb, dtb, alog, dexp, ng, emat):
    m = proj.shape[0]
    nchunks = m // CHUNK
    q = CHUNK
    smap = lambda c: (jnp.maximum(c - (ncp - 1), 0), 0, 0)
    const2 = lambda c: (0, 0)
    return pl.pallas_call(
        functools.partial(_ssd_kernel, ncp),
        grid=(nchunks,),
        in_specs=[pl.BlockSpec((q, 1024), lambda c: (c, C_Z // 1024)),
                  pl.BlockSpec((q, 1024), lambda c: (c, C_XS // 1024)),
                  pl.BlockSpec((q, 512), lambda c: (c, C_BC // 512)),
                  pl.BlockSpec((q, LANE), lambda c: (c, C_SM // LANE)),
                  pl.BlockSpec((1, SSD_STATE, SSD_WIDTH), smap),
                  pl.BlockSpec((1, 8, 1024), smap),
                  pl.BlockSpec((1, 8, 512), smap),
                  pl.BlockSpec((SSD_CONV, 1024), const2),
                  pl.BlockSpec((1, 1024), const2),
                  pl.BlockSpec((SSD_CONV, 512), const2),
                  pl.BlockSpec((1, 512), const2),
                  pl.BlockSpec((1, LANE), const2),
                  pl.BlockSpec((1, LANE), const2),
                  pl.BlockSpec((1, SSD_WIDTH), const2),
                  pl.BlockSpec((1, SSD_WIDTH), const2),
                  pl.BlockSpec((LANE, SSD_WIDTH), const2)],
        out_specs=[pl.BlockSpec((q, SSD_WIDTH), lambda c: (c, 0)),
                   pl.BlockSpec((1, SSD_STATE, SSD_WIDTH), smap)],
        out_shape=[jax.ShapeDtypeStruct((m, SSD_WIDTH), F32),
                   jax.ShapeDtypeStruct(st_in.shape, F32)],
        scratch_shapes=[pltpu.VMEM((SSD_STATE, SSD_WIDTH), F32),
                        pltpu.VMEM((8 + q, 1024), F32),
                        pltpu.VMEM((8 + q, 512), F32)],
        compiler_params=_cparams(("arbitrary",), 32 << 20),
        name="ssd",
    )(proj, proj, proj, proj, st_in, cpx, cpb, cwx, cbx, cwb, cbb, dtb, alog, dexp, ng, emat)


def _gla_kernel(ncp, gq_ref, gk_ref, gv_ref, gg_ref, sm_ref, s_in_ref, wa_ref, ba_ref, ng_ref,
                o_ref, s_out_ref, s_scr):
    c = pl.program_id(0)
    q = CHUNK
    blk = GLA_BLOCK
    kw = GLA_HEADS * GLA_DK

    @pl.when(jnp.logical_or(c == 0, c >= ncp))
    def _():
        s_scr[...] = s_in_ref[0]

    pre = jnp.dot(sm_ref[...].astype(BF16), wa_ref[...], preferred_element_type=F32) + ba_ref[...]
    log_a = -_softplus(-pre) * (1.0 / GLA_TAU)
    r64 = lax.broadcasted_iota(jnp.int32, (q, q), 0)
    c64 = lax.broadcasted_iota(jnp.int32, (q, q), 1)
    same_blk = (r64 // blk) == (c64 // blk)
    causal = jnp.logical_and(same_blk, c64 <= r64)
    b = jnp.dot(causal.astype(F32), log_a, precision=HI, preferred_element_type=F32)
    gk = gk_ref[...]
    qt = gq_ref[...] * (GLA_DK ** -0.5) * jnp.exp(b)
    kt = (gk * jnp.exp(-b)).astype(BF16)
    b_last = jnp.concatenate(
        [jnp.broadcast_to(b[(j + 1) * blk - 1:(j + 1) * blk, :], (blk, kw)) for j in range(q // blk)], axis=0)
    ktil = (gk * jnp.exp(b_last - b)).astype(BF16)
    qt_b = qt.astype(BF16)
    gv_b = gv_ref[...].astype(BF16)

    lane_head = lax.broadcasted_iota(jnp.int32, (q, kw), 1) // GLA_DK
    o_intra = []
    for h in range(GLA_HEADS):
        qm = jnp.where(lane_head == h, qt_b, jnp.zeros((), BF16))
        att = lax.dot_general(qm, kt, (((1,), (1,)), ((), ())), preferred_element_type=F32)
        att = jnp.where(causal, att, 0.0).astype(BF16)
        o_intra.append(jnp.dot(att, gv_b[:, h * GLA_DV:(h + 1) * GLA_DV], preferred_element_type=F32))
    o_intra = jnp.concatenate(o_intra, axis=1)

    rs = lax.broadcasted_iota(jnp.int32, (GLA_WIDTH, kw), 0) // GLA_DV
    cs_ = lax.broadcasted_iota(jnp.int32, (GLA_WIDTH, kw), 1) // GLA_DK
    bd = rs == cs_
    sw = s_scr[...]
    o_inter = []
    for j in range(q // blk):
        sl = slice(j * blk, (j + 1) * blk)
        o_inter.append(lax.dot_general(qt_b[sl], sw.astype(BF16), (((1,), (1,)), ((), ())),
                                       preferred_element_type=F32))
        upd = lax.dot_general(gv_b[sl], ktil[sl], (((0,), (0,)), ((), ())),
                              preferred_element_type=F32)
        decay = jnp.exp(b[(j + 1) * blk - 1:(j + 1) * blk, :])
        sw = sw * decay + jnp.where(bd, upd, 0.0)
    s_scr[...] = sw
    s_out_ref[0] = sw
    o = o_intra + jnp.concatenate(o_inter, axis=0)

    gg = gg_ref[...]
    gate = gg * _sigmoid(gg)
    outs = []
    for h in range(GLA_HEADS):
        oh = o[:, h * GLA_DV:(h + 1) * GLA_DV]
        ms = jnp.mean(oh * oh, axis=-1, keepdims=True)
        outs.append(oh * lax.rsqrt(ms + EPS) * ng_ref[...])
    o_ref[...] = jnp.concatenate(outs, axis=1) * gate


def _gla(proj, ncp, s_in, wa_pad, ba, ng):
    m = proj.shape[0]
    q = CHUNK
    kw = GLA_HEADS * GLA_DK
    smap = lambda c: (jnp.maximum(c - (ncp - 1), 0), 0, 0)
    const2 = lambda c: (0, 0)
    return pl.pallas_call(
        functools.partial(_gla_kernel, ncp),
        grid=(m // q,),
        in_specs=[pl.BlockSpec((q, kw), lambda c: (c, C_GQ // kw)),
                  pl.BlockSpec((q, kw), lambda c: (c, C_GK // kw)),
                  pl.BlockSpec((q, GLA_WIDTH), lambda c: (c, C_GV // GLA_WIDTH)),
                  pl.BlockSpec((q, GLA_WIDTH), lambda c: (c, C_GG // GLA_WIDTH)),
                  pl.BlockSpec((q, LANE), lambda c: (c, C_SM // LANE)),
                  pl.BlockSpec((1, GLA_WIDTH, kw), smap),
                  pl.BlockSpec((LANE, kw), const2),
                  pl.BlockSpec((1, kw), const2),
                  pl.BlockSpec((1, GLA_DV), const2)],
        out_specs=[pl.BlockSpec((q, GLA_WIDTH), lambda c: (c, 0)),
                   pl.BlockSpec((1, GLA_WIDTH, kw), smap)],
        out_shape=[jax.ShapeDtypeStruct((m, GLA_WIDTH), F32),
                   jax.ShapeDtypeStruct(s_in.shape, F32)],
        scratch_shapes=[pltpu.VMEM((GLA_WIDTH, kw), F32)],
        compiler_params=_cparams(("arbitrary",), 24 << 20),
        name="gla",
    )(proj, proj, proj, proj, proj, s_in, wa_pad, ba, ng)


def _attn_prep_kernel(dq_ref, dk_ref, dv_ref, qg_ref, kg_ref, seg_ref,
                      qa_ref, qb_ref, kn_ref, kb_ref, vt_ref):
    seg = seg_ref[...]

    def qknorm(x, g):
        ms = jnp.dot(x * x, seg, precision=HI, preferred_element_type=F32) * (1.0 / DIFF_HD)
        return x * lax.rsqrt(ms + EPS) * g

    qn = qknorm(dq_ref[...], qg_ref[...]) * (DIFF_HD ** -0.5 * LOG2E)
    first = (lax.broadcasted_iota(jnp.int32, qn.shape, 1) // DIFF_HD) % 2 == 0
    qa_ref[...] = jnp.where(first, qn, 0.0).astype(BF16)
    qb_ref[...] = jnp.where(first, 0.0, qn).astype(BF16)
    kn = qknorm(dk_ref[...], kg_ref[...])
    kn_ref[...] = kn
    kb_ref[...] = kn.astype(BF16)
    vt_ref[...] = dv_ref[...].T.astype(BF16)


def _attn_prep(proj, qg, kg, segmat):
    m = proj.shape[0]
    tm = _pick(m, (512, 256, 128))
    w = DIFF_WIDTH
    row = lambda off: pl.BlockSpec((tm, w), lambda i: (i, off // w))
    const2 = lambda i: (0, 0)
    return pl.pallas_call(
        _attn_prep_kernel,
        grid=(m // tm,),
        in_specs=[row(C_DQ), row(C_DK), row(C_DV),
                  pl.BlockSpec((1, w), const2), pl.BlockSpec((1, w), const2),
                  pl.BlockSpec((w, w), const2)],
        out_specs=[pl.BlockSpec((tm, w), lambda i: (i, 0))] * 4 + [pl.BlockSpec((w, tm), lambda i: (0, i))],
        out_shape=[jax.ShapeDtypeStruct((m, w), BF16), jax.ShapeDtypeStruct((m, w), BF16),
                   jax.ShapeDtypeStruct((m, w), F32), jax.ShapeDtypeStruct((m, w), BF16),
                   jax.ShapeDtypeStruct((w, m), BF16)],
        compiler_params=_cparams(("parallel",), 32 << 20),
        name="attn_prep",
    )(proj, proj, proj, qg, kg, segmat)


def _lambda_full(lam_ref, lam_init):
    l = lam_ref[...]
    a = jnp.sum(l[0:1] * l[1:2], axis=-1, keepdims=True)
    b = jnp.sum(l[2:3] * l[3:4], axis=-1, keepdims=True)
    return jnp.exp(a) - jnp.exp(b) + lam_init


ATTN_GUARD = 64.0


def _fold8(x, op):
    r = x[0:SUBLANE]
    for i in range(1, x.shape[0] // SUBLANE):
        r = op(r, x[i * SUBLANE:(i + 1) * SUBLANE])
    return r


def _attn_prompt_kernel(t, kbk, lam_init, qi_ref, ki_ref, slope_ref, qa_ref, qb_ref, kb_ref, vt_ref, lam_ref,
                        og_ref, o_ref, m_scr, l_scr, acc_scr, b0_scr, bd_scr, s0_scr):
    h = pl.program_id(0)
    p = pl.program_id(1)
    qi = qi_ref[p]
    ki = ki_ref[p]
    slope = slope_ref[h]
    nb = t // kbk
    w = 2 * t
    nt = (((1,), (1,)), ((), ()))

    @pl.when(p == 0)
    def _():
        b0_scr[...] = slope * lax.broadcasted_iota(jnp.int32, (t, LANE), 0).astype(F32)
        kr = lax.broadcasted_iota(jnp.int32, (t, t), 0)
        qc = lax.broadcasted_iota(jnp.int32, (t, t), 1)
        diag = slope * jnp.where(kr <= qc, kr, 2 * qc - kr).astype(F32)
        bd_scr[...] = jnp.where(kr // CHUNK <= qc // CHUNK, diag, -jnp.inf)

    @pl.when(ki == 0)
    def _():
        m_scr[...] = jnp.full(m_scr.shape, -jnp.inf, F32)
        l_scr[...] = jnp.zeros(l_scr.shape, F32)
        acc_scr[...] = jnp.zeros(acc_scr.shape, F32)

    def bias_block(diag, rows):
        if diag:
            blk = bd_scr[rows, :]
            return jnp.concatenate([blk, blk], axis=1)
        blk = b0_scr[rows, :]
        return jnp.concatenate([blk] * (w // LANE), axis=1)

    def tile(diag):
        shift = slope * ((ki - qi) * t).astype(F32)
        qcat = jnp.concatenate([qa_ref[...], qb_ref[...]], axis=0)

        def scores(rows):
            return lax.dot_general(kb_ref[rows, :], qcat, nt, preferred_element_type=F32) + bias_block(diag, rows)

        m_old = m_scr[...]
        s0 = scores(slice(0, kbk))
        s0_scr[...] = s0
        m_used = jnp.maximum(m_old, jnp.max(s0, axis=0, keepdims=True) + shift)
        mb = m_used - shift
        tmax8 = lsum8 = pv = None
        for b in range(nb):
            rows = slice(b * kbk, (b + 1) * kbk)
            s = s0_scr[...] if b == 0 else scores(rows)
            pt = jnp.exp2(s - mb)
            fm, fs = _fold8(s, jnp.maximum), _fold8(pt, jnp.add)
            d = jnp.dot(vt_ref[:, rows], pt.astype(BF16), preferred_element_type=F32)
            tmax8, lsum8, pv = (fm, fs, d) if b == 0 else (jnp.maximum(tmax8, fm), lsum8 + fs, pv + d)
        tmax = jnp.max(tmax8, axis=0, keepdims=True) + shift
        m_new = jnp.maximum(m_used, tmax)
        ok = jnp.max(tmax - m_used) <= ATTN_GUARD

        @pl.when(ok)
        def _():
            a_old = jnp.exp2(m_old - m_new)
            a_cur = jnp.exp2(m_used - m_new)
            l_scr[...] = a_old * l_scr[...] + a_cur * jnp.sum(lsum8, axis=0, keepdims=True)
            acc_scr[...] = a_old * acc_scr[...] + a_cur * pv
            m_scr[...] = m_new

        @pl.when(jnp.logical_not(ok))
        def _():
            def body(b, carry):
                rows = pl.ds(pl.multiple_of(b * kbk, kbk), kbk)
                s = scores(rows) + shift
                m_o = m_scr[...]
                m_n = jnp.maximum(m_o, jnp.max(s, axis=0, keepdims=True))
                pt = jnp.exp2(s - m_n)
                a = jnp.exp2(m_o - m_n)
                l_scr[...] = a * l_scr[...] + jnp.sum(pt, axis=0, keepdims=True)
                acc_scr[...] = a * acc_scr[...] + jnp.dot(vt_ref[:, rows], pt.astype(BF16),
                                                          preferred_element_type=F32)
                m_scr[...] = m_n
                return carry

            lax.fori_loop(0, nb, body, 0)

    @pl.when(ki < qi)
    def _():
        tile(False)

    @pl.when(ki == qi)
    def _():
        tile(True)
        lam = _lambda_full(lam_ref, lam_init)
        on = acc_scr[...] / l_scr[...]
        ot = on[:, 0:t] - lam * on[:, t:w]
        ms = jnp.mean(ot * ot, axis=0, keepdims=True)
        o_ref[...] = (ot * lax.rsqrt(ms + EPS)).T * (og_ref[...] * (1.0 - lam_init))


def _attn_prompt(lp, lam_init, qa, qb, kb, vt, slopes, lam_p, og):
    t = _pick(lp, (1024, 512, 256))
    kbk = 256
    nq = lp // t
    pairs = [(i, j) for i in range(nq) for j in range(i + 1)]
    qi_l = jnp.asarray(np.array([a for a, _ in pairs], np.int32))
    ki_l = jnp.asarray(np.array([b for _, b in pairs], np.int32))
    grid_spec = pltpu.PrefetchScalarGridSpec(
        num_scalar_prefetch=3,
        grid=(DIFF_HEADS, len(pairs)),
        in_specs=[pl.BlockSpec((t, LANE), lambda h, p, qi, ki, sl: (qi[p], h)),
                  pl.BlockSpec((t, LANE), lambda h, p, qi, ki, sl: (qi[p], h)),
                  pl.BlockSpec((t, LANE), lambda h, p, qi, ki, sl: (ki[p], h)),
                  pl.BlockSpec((LANE, t), lambda h, p, qi, ki, sl: (h, ki[p])),
                  pl.BlockSpec((4, DIFF_HD), lambda h, p, qi, ki, sl: (0, 0)),
                  pl.BlockSpec((1, DIFF_VD), lambda h, p, qi, ki, sl: (0, 0))],
        out_specs=pl.BlockSpec((t, LANE), lambda h, p, qi, ki, sl: (qi[p], h)),
        scratch_shapes=[pltpu.VMEM((1, 2 * t), F32), pltpu.VMEM((1, 2 * t), F32),
                        pltpu.VMEM((DIFF_VD, 2 * t), F32),
                        pltpu.VMEM((t, LANE), F32), pltpu.VMEM((t, t), F32),
                        pltpu.VMEM((kbk, 2 * t), F32)])
    return pl.pallas_call(
        functools.partial(_attn_prompt_kernel, t, kbk, lam_init),
        grid_spec=grid_spec,
        out_shape=jax.ShapeDtypeStruct((lp, DIFF_WIDTH), F32),
        compiler_params=_cparams(("arbitrary", "arbitrary"), 40 << 20),
        name="attn_prompt",
    )(qi_l, ki_l, slopes, qa, qb, kb, vt, lam_p, og)


def _attn_sample_kernel(past, lam_init, slope_ref, qa_ref, qb_ref, kn_ref, vn_ref, kc_ref, vc_ref, lam_ref, og_ref,
                        o_ref):
    q = CHUNK
    lam = _lambda_full(lam_ref, lam_init)
    kpos = lax.broadcasted_iota(jnp.int32, (1, past), 1).astype(F32)
    r = lax.broadcasted_iota(jnp.int32, (q, q), 0)
    cc = lax.broadcasted_iota(jnp.int32, (q, q), 1)
    new_bias = (past + r - jnp.abs(r - cc)).astype(F32)
    outs = []
    for h in range(DIFF_HEADS):
        slope = slope_ref[h]
        sl = slice(h * LANE, (h + 1) * LANE)
        kc = kc_ref[0, 0, :, h, :].astype(BF16)
        vc = vc_ref[0, 0, :, h, :].astype(BF16)
        kn = kn_ref[:, sl].astype(BF16)
        vn = vn_ref[:, sl].astype(BF16)
        res = []
        for q_ref in (qa_ref, qb_ref):
            qv = q_ref[:, sl]
            s_c = lax.dot_general(qv, kc, (((1,), (1,)), ((), ())), preferred_element_type=F32) + slope * kpos
            s_n = lax.dot_general(qv, kn, (((1,), (1,)), ((), ())), preferred_element_type=F32) + slope * new_bias
            mx = jnp.maximum(jnp.max(s_c, axis=-1, keepdims=True), jnp.max(s_n, axis=-1, keepdims=True))
            p_c = jnp.exp2(s_c - mx)
            p_n = jnp.exp2(s_n - mx)
            den = jnp.sum(p_c, axis=-1, keepdims=True) + jnp.sum(p_n, axis=-1, keepdims=True)
            num = (jnp.dot(p_c.astype(BF16), vc, preferred_element_type=F32)
                   + jnp.dot(p_n.astype(BF16), vn, preferred_element_type=F32))
            res.append(num / den)
        o = res[0] - lam * res[1]
        ms = jnp.mean(o * o, axis=-1, keepdims=True)
        outs.append(o * lax.rsqrt(ms + EPS) * (og_ref[...] * (1.0 - lam_init)))
    o_ref[...] = jnp.concatenate(outs, axis=1)


def _attn_sample(lp, nb, layer, lam_init, qa, qb, kn, proj, cache_k, cache_v, slopes, lam_p, og):
    past = cache_k.shape[2]
    q = CHUNK
    w = DIFF_WIDTH
    off = lp // q
    cache_block = (1, 1, past, DIFF_HEADS, DIFF_VD)
    grid_spec = pltpu.PrefetchScalarGridSpec(
        num_scalar_prefetch=1,
        grid=(nb,),
        in_specs=[pl.BlockSpec((q, w), lambda b, sl: (off + b, 0)),
                  pl.BlockSpec((q, w), lambda b, sl: (off + b, 0)),
                  pl.BlockSpec((q, w), lambda b, sl: (off + b, 0)),
                  pl.BlockSpec((q, w), lambda b, sl: (off + b, C_DV // w)),
                  pl.BlockSpec(cache_block, lambda b, sl: (layer, b, 0, 0, 0)),
                  pl.BlockSpec(cache_block, lambda b, sl: (layer, b, 0, 0, 0)),
                  pl.BlockSpec((4, DIFF_HD), lambda b, sl: (0, 0)),
                  pl.BlockSpec((1, DIFF_VD), lambda b, sl: (0, 0))],
        out_specs=pl.BlockSpec((q, w), lambda b, sl: (b, 0)))
    return pl.pallas_call(
        functools.partial(_attn_sample_kernel, past, lam_init),
        grid_spec=grid_spec,
        out_shape=jax.ShapeDtypeStruct((nb * q, w), F32),
        compiler_params=_cparams(("arbitrary",), 48 << 20),
        name="attn_sample",
    )(slopes, qa, qb, kn, proj, cache_k, cache_v, lam_p, og)


def _out_proj_kernel(npt, xp_ref, xs_ref, ys_ref, yap_ref, yas_ref, yg_ref, w_ref, o_ref):
    is_prompt = pl.program_id(0) < npt
    ya = jnp.where(is_prompt, yap_ref[...], yas_ref[...])
    acc = jnp.dot(ys_ref[...].astype(BF16), w_ref[0:SSD_WIDTH, :], preferred_element_type=F32)
    acc = acc + jnp.dot(ya.astype(BF16), w_ref[SSD_WIDTH:SSD_WIDTH + DIFF_WIDTH, :],
                        preferred_element_type=F32)
    acc = acc + jnp.dot(yg_ref[...].astype(BF16), w_ref[SSD_WIDTH + DIFF_WIDTH:, :],
                        preferred_element_type=F32)
    o_ref[...] = jnp.where(is_prompt, xp_ref[...], xs_ref[...]) + acc


def _out_proj(xp, xs, ys, yap, yas, yg, w):
    lp, ms = xp.shape[0], xs.shape[0]
    tm = _pick(math.gcd(lp, ms), (512, 256, 128, 64))
    npt = lp // tm
    pmap, smap = _split_maps(npt)
    vmem = 2 * (5 * tm * D_MODEL * 4) + D_MODEL * D_MODEL * 2 + (4 << 20)
    return pl.pallas_call(
        functools.partial(_out_proj_kernel, npt),
        grid=((lp + ms) // tm,),
        in_specs=[pl.BlockSpec((tm, D_MODEL), pmap),
                  pl.BlockSpec((tm, D_MODEL), smap),
                  pl.BlockSpec((tm, SSD_WIDTH), lambda i: (i, 0)),
                  pl.BlockSpec((tm, DIFF_WIDTH), pmap),
                  pl.BlockSpec((tm, DIFF_WIDTH), smap),
                  pl.BlockSpec((tm, GLA_WIDTH), lambda i: (i, 0)),
                  pl.BlockSpec((D_MODEL, D_MODEL), lambda i: (0, 0), pipeline_mode=pl.Buffered(1))],
        out_specs=pl.BlockSpec((tm, D_MODEL), lambda i: (i, 0)),
        out_shape=jax.ShapeDtypeStruct((lp + ms, D_MODEL), F32),
        compiler_params=_cparams(("parallel",), vmem),
        name="out_proj",
    )(xp, xs, ys, yap, yas, yg, w)


def _mlp_kernel(npt, x_ref, g_ref, w1_ref, w2_ref, op_ref, os_ref, h_ref):
    i = pl.program_id(0)
    first = pl.program_id(1) == 0

    @pl.when(first)
    def _():
        h_ref[...] = _rms_bf16(x_ref[...], g_ref[...])

    a = jnp.dot(h_ref[...], w1_ref[...], preferred_element_type=F32)
    a = jnp.square(jnp.maximum(a, 0.0)).astype(BF16)
    upd = jnp.dot(a, w2_ref[...], preferred_element_type=F32)

    def accumulate(o_ref):
        @pl.when(first)
        def _():
            o_ref[...] = x_ref[...] + upd

        @pl.when(jnp.logical_not(first))
        def _():
            o_ref[...] += upd

    @pl.when(i < npt)
    def _():
        accumulate(op_ref)

    @pl.when(i >= npt)
    def _():
        accumulate(os_ref)


def _mlp(x, lp, g, w1, w2):
    m = x.shape[0]
    ms = m - lp
    tm = _pick(math.gcd(lp, ms), (512, 256, 128, 64))
    tf = 1024
    npt = lp // tm
    pmap, smap = _split_maps(npt)
    vmem = 6 * tm * D_MODEL * 4 + tm * D_MODEL * 2 + 4 * D_MODEL * tf * 2 + 2 * tm * tf * 4 + (4 << 20)
    return pl.pallas_call(
        functools.partial(_mlp_kernel, npt),
        grid=(m // tm, D_FF // tf),
        in_specs=[pl.BlockSpec((tm, D_MODEL), lambda i, f: (i, 0)),
                  pl.BlockSpec((1, D_MODEL), lambda i, f: (0, 0)),
                  pl.BlockSpec((D_MODEL, tf), lambda i, f: (0, f)),
                  pl.BlockSpec((tf, D_MODEL), lambda i, f: (f, 0))],
        out_specs=[pl.BlockSpec((tm, D_MODEL), pmap), pl.BlockSpec((tm, D_MODEL), smap)],
        out_shape=[jax.ShapeDtypeStruct((lp, D_MODEL), F32), jax.ShapeDtypeStruct((ms, D_MODEL), F32)],
        scratch_shapes=[pltpu.VMEM((tm, D_MODEL), BF16)],
        compiler_params=_cparams(("arbitrary", "arbitrary"), vmem),
        name="mlp",
    )(x, g, w1, w2)


def _pack_w_in(w):
    return jnp.concatenate(
        [w[:, 0:2560], w[:, 2576:4112], w[:, 4624:5136], w[:, 5152:5664], w[:, 4112:4624],
         w[:, 2560:2576], w[:, 5136:5152], jnp.zeros((w.shape[0], PACKED - 5664), w.dtype)],
        axis=1).astype(BF16)


def _pad_lanes(v, width=LANE):
    v = v.reshape(1, -1)
    return jnp.pad(v, ((0, 0), (0, width - v.shape[1])))


def kernel(x_prompt, x_sample, cache_diff_k, cache_diff_v, state_ssd_conv, state_ssd, state_gla, norm1_g, w_in, ssd_conv_w, ssd_conv_b, ssd_dt_bias, ssd_a_log, ssd_d, ssd_norm_g, diff_qn_g, diff_kn_g, diff_lambda, diff_out_g, gla_wa2, gla_ba, gla_norm_g, w_out, norm2_g, w_mlp1, w_mlp2):
    bp, lp, d = x_prompt.shape
    nb, ls, _ = x_sample.shape
    depth = w_in.shape[0]
    past = cache_diff_k.shape[2]
    assert bp == 1 and d == D_MODEL and ls == CHUNK and lp % CHUNK == 0 and past % CHUNK == 0
    ncp = lp // CHUNK
    m = lp + nb * ls

    xp, xs = x_prompt.reshape(lp, d), x_sample.reshape(nb * ls, d)

    emat = (jnp.arange(LANE)[:, None] == (jnp.arange(SSD_WIDTH)[None, :] // SSD_HEAD_DIM)).astype(F32)
    segmat = ((jnp.arange(DIFF_WIDTH)[:, None] // DIFF_HD) == (jnp.arange(DIFF_WIDTH)[None, :] // DIFF_HD)).astype(F32)
    slopes = jnp.exp2(-8.0 * jnp.arange(1, DIFF_HEADS + 1, dtype=F32) / DIFF_HEADS) * LOG2E
    eye_h = jnp.eye(GLA_HEADS, dtype=F32)

    outs = {k: [] for k in ("kp", "vp", "cp", "hp", "sp", "ks", "vs", "cs", "hs", "ss")}
    for l in range(depth):
        lam_init = 0.8 - 0.6 * math.exp(-0.3 * l)
        proj = _in_proj(xp, xs, norm1_g[l].reshape(1, d), _pack_w_in(w_in[l]))

        st_all = jnp.concatenate([jnp.zeros((1,) + state_ssd.shape[2:], F32), state_ssd[l]], axis=0)
        st_in = st_all.reshape(nb + 1, SSD_WIDTH, SSD_STATE).transpose(0, 2, 1)
        cprev = jnp.concatenate([jnp.zeros((1, SSD_CONV - 1, SSD_CONV_DIM), F32), state_ssd_conv[l]], axis=0)
        cprev = jnp.pad(cprev, ((0, 0), (8 - (SSD_CONV - 1), 0), (0, 0)))
        cw, cb = ssd_conv_w[l], ssd_conv_b[l].reshape(1, -1)
        y_ssd, st_out = _ssd(
            proj, ncp, st_in, cprev[:, :, :SSD_WIDTH], cprev[:, :, SSD_WIDTH:],
            cw[:, :SSD_WIDTH], cb[:, :SSD_WIDTH], cw[:, SSD_WIDTH:], cb[:, SSD_WIDTH:],
            _pad_lanes(ssd_dt_bias[l]), _pad_lanes(ssd_a_log[l]),
            jnp.repeat(ssd_d[l], SSD_HEAD_DIM).reshape(1, -1), ssd_norm_g[l].reshape(1, -1), emat)
        h_all = st_out.transpose(0, 2, 1).reshape(nb + 1, SSD_HEADS, SSD_HEAD_DIM, SSD_STATE)
        outs["hp"].append(h_all[0:1])
        outs["hs"].append(h_all[1:])
        xbc_raw = proj[:, C_XS:C_XS + SSD_CONV_DIM]
        outs["cp"].append(xbc_raw[lp - (SSD_CONV - 1):lp][None])
        outs["cs"].append(xbc_raw[lp:].reshape(nb, ls, SSD_CONV_DIM)[:, ls - (SSD_CONV - 1):])

        s_all = jnp.concatenate([jnp.zeros((1,) + state_gla.shape[2:], F32), state_gla[l]], axis=0)
        s_in = jnp.einsum('bhkv,hg->bhvgk', s_all, eye_h).reshape(nb + 1, GLA_WIDTH, GLA_HEADS * GLA_DK)
        wa_pad = jnp.zeros((LANE, GLA_HEADS * GLA_DK), F32).at[GLA_RANK:2 * GLA_RANK].set(gla_wa2[l]).astype(BF16)
        y_gla, s_out = _gla(proj, ncp, s_in, wa_pad, gla_ba[l].reshape(1, -1), gla_norm_g[l].reshape(1, -1))
        s5 = s_out.reshape(nb + 1, GLA_HEADS, GLA_DV, GLA_HEADS, GLA_DK)
        s_new = jnp.stack([s5[:, h, :, h, :] for h in range(GLA_HEADS)], axis=1).transpose(0, 1, 3, 2)
        outs["sp"].append(s_new[0:1])
        outs["ss"].append(s_new[1:])

        qg = jnp.tile(diff_qn_g[l], 2 * DIFF_HEADS).reshape(1, -1)
        kg = jnp.tile(diff_kn_g[l], 2 * DIFF_HEADS).reshape(1, -1)
        qa, qb, kn, kb, vt = _attn_prep(proj, qg, kg, segmat)
        og = diff_out_g[l].reshape(1, -1)
        ya_p = _attn_prompt(lp, lam_init, qa, qb, kb, vt, slopes, diff_lambda[l], og)
        ya_s = _attn_sample(lp, nb, l, lam_init, qa, qb, kn, proj, cache_diff_k, cache_diff_v,
                            slopes, diff_lambda[l], og)
        v_rows = proj[:, C_DV:C_DV + DIFF_WIDTH]
        outs["kp"].append(kn[:lp].reshape(1, lp, DIFF_HEADS, 2 * DIFF_HD))
        outs["ks"].append(kn[lp:].reshape(nb, ls, DIFF_HEADS, 2 * DIFF_HD))
        outs["vp"].append(v_rows[:lp].reshape(1, lp, DIFF_HEADS, DIFF_VD))
        outs["vs"].append(v_rows[lp:].reshape(nb, ls, DIFF_HEADS, DIFF_VD))

        x1 = _out_proj(xp, xs, y_ssd, ya_p, ya_s, y_gla, w_out[l].astype(BF16))
        xp, xs = _mlp(x1, lp, norm2_g[l].reshape(1, d), w_mlp1[l].astype(BF16), w_mlp2[l].astype(BF16))

    st = {k: jnp.stack(v) for k, v in outs.items()}
    return (xp.reshape(1, lp, d), xs.reshape(nb, ls, d),
            st["kp"], st["vp"], st["cp"], st["hp"], st["sp"],
            st["ks"], st["vs"], st["cs"], st["hs"], st["ss"])
```

```python
import functools
import math

import numpy as np
import jax
import jax.numpy as jnp
from jax import lax
from jax.experimental import pallas as pl
from jax.experimental.pallas import tpu as pltpu

F32 = jnp.float32
BF16 = jnp.bfloat16
HI = lax.Precision.HIGHEST

D_MODEL = 2048
CHUNK = 64
SSD_HEAD_DIM = 64
SSD_WIDTH = 1024
SSD_HEADS = 16
SSD_GROUPS = 2
SSD_STATE = 128
SSD_CONV = 4
SSD_CONV_DIM = SSD_WIDTH + 2 * SSD_GROUPS * SSD_STATE
DIFF_HD = 64
DIFF_VD = 128
DIFF_WIDTH = 512
DIFF_HEADS = 4
GLA_DK = 64
GLA_DV = 128
GLA_WIDTH = 512
GLA_HEADS = 4
GLA_RANK = 16
GLA_TAU = 16.0
GLA_BLOCK = 16
D_FF = 4 * D_MODEL
EPS = 1e-6

LOG2E = math.log2(math.e)
LANE = 128
SUBLANE = 8
VMEM_CAP = 56 << 20

C_Z, C_XS, C_BC, C_DQ, C_DK, C_DV, C_GV, C_GG, C_GQ, C_GK, C_SM = (
    0, 1024, 2048, 2560, 3072, 3584, 4096, 4608, 5120, 5376, 5632)
PACKED = 5760
PROJ_TN = 1920


def _pick(m, cands):
    for c in cands:
        if m % c == 0:
            return c
    raise ValueError(f"no tile for {m}")


def _cparams(sem, vmem_bytes):
    return pltpu.CompilerParams(dimension_semantics=sem,
                                vmem_limit_bytes=int(min(VMEM_CAP, max(vmem_bytes, 16 << 20))))


def _sigmoid(x):
    return 1.0 / (1.0 + jnp.exp(-x))


def _softplus(x):
    return jnp.maximum(x, 0.0) + jnp.log1p(jnp.exp(-jnp.abs(x)))


def _rms_bf16(x, g):
    ms = jnp.mean(x * x, axis=-1, keepdims=True)
    return (x * lax.rsqrt(ms + EPS) * g).astype(BF16)


def _in_proj_kernel(npt, xp_ref, xs_ref, g_ref, w_ref, o_ref, h_ref):
    i = pl.program_id(0)

    @pl.when(jnp.logical_and(pl.program_id(1) == 0, i < npt))
    def _():
        h_ref[...] = _rms_bf16(xp_ref[...], g_ref[...])

    @pl.when(jnp.logical_and(pl.program_id(1) == 0, i >= npt))
    def _():
        h_ref[...] = _rms_bf16(xs_ref[...], g_ref[...])

    o_ref[...] = jnp.dot(h_ref[...], w_ref[...], preferred_element_type=F32)


def _split_maps(npt):
    return (lambda i, *_: (jnp.minimum(i, npt - 1), 0)), (lambda i, *_: (jnp.maximum(i - npt, 0), 0))


def _in_proj(xp, xs, g, w):
    lp, ms = xp.shape[0], xs.shape[0]
    tm = _pick(math.gcd(lp, ms), (512, 256, 128, 64))
    tn = PROJ_TN
    npt = lp // tm
    pmap, smap = _split_maps(npt)
    vmem = 4 * tm * D_MODEL * 4 + tm * D_MODEL * 2 + 2 * D_MODEL * tn * 2 + 2 * tm * tn * 4 + (4 << 20)
    return pl.pallas_call(
        functools.partial(_in_proj_kernel, npt),
        grid=((lp + ms) // tm, PACKED // tn),
        in_specs=[pl.BlockSpec((tm, D_MODEL), pmap),
                  pl.BlockSpec((tm, D_MODEL), smap),
                  pl.BlockSpec((1, D_MODEL), lambda i, j: (0, 0)),
                  pl.BlockSpec((D_MODEL, tn), lambda i, j: (0, j))],
        out_specs=pl.BlockSpec((tm, tn), lambda i, j: (i, j)),
        out_shape=jax.ShapeDtypeStruct((lp + ms, PACKED), F32),
        scratch_shapes=[pltpu.VMEM((tm, D_MODEL), BF16)],
        compiler_params=_cparams(("parallel", "arbitrary"), vmem),
        name="in_proj",
    )(xp, xs, g, w)


def _ssd_kernel(ncp, z_ref, xs_ref, bc_ref, sm_ref, st_in_ref, cpx_ref, cpb_ref,
                cwx_ref, cbx_ref, cwb_ref, cbb_ref, dtb_ref, alog_ref, dexp_ref, ng_ref, e_ref,
                y_ref, st_out_ref, st_scr, fx_scr, fb_scr):
    c = pl.program_id(0)
    q = CHUNK

    @pl.when(jnp.logical_or(c == 0, c >= ncp))
    def _():
        st_scr[...] = st_in_ref[0]
        fx_scr[0:8, :] = cpx_ref[0]
        fb_scr[0:8, :] = cpb_ref[0]

    def conv(u_ref, f_scr, w_ref, b_ref):
        u = u_ref[...]
        f_scr[8:8 + q, :] = u
        y = b_ref[...] + f_scr[5:5 + q, :] * w_ref[0:1, :]
        for j in range(1, SSD_CONV):
            y = y + f_scr[5 + j:5 + j + q, :] * w_ref[j:j + 1, :]
        f_scr[0:8, :] = u[q - 8:q, :]
        return y * _sigmoid(y)

    xs = conv(xs_ref, fx_scr, cwx_ref, cbx_ref)
    bcv = conv(bc_ref, fb_scr, cwb_ref, cbb_ref)

    dt = _softplus(sm_ref[...] + dtb_ref[...])
    da = dt * (-jnp.exp(alog_ref[...]))
    r64 = lax.broadcasted_iota(jnp.int32, (q, q), 0)
    c64 = lax.broadcasted_iota(jnp.int32, (q, q), 1)
    tril = (c64 <= r64).astype(F32)
    cs = jnp.dot(tril, da, precision=HI, preferred_element_type=F32)
    both = jnp.dot(jnp.concatenate([dt, cs], axis=0), e_ref[...], precision=HI,
                   preferred_element_type=F32)
    dt_e = both[0:q]
    cs_e = both[q:2 * q]

    row = lax.broadcasted_iota(jnp.int32, (q, SSD_WIDTH), 0)
    sidx = jnp.bitwise_and(lax.broadcasted_iota(jnp.int32, (q, SSD_WIDTH), 1), q - 1)
    cs_row = jnp.sum(jnp.where(sidx == row, cs_e, 0.0), axis=0, keepdims=True)
    cs_last = cs_e[q - 1:q, :]
    lmat = jnp.exp(jnp.where(sidx <= row, cs_e - cs_row, -jnp.inf))

    xdt = xs * dt_e
    xdt_end = (xdt * jnp.exp(cs_last - cs_e)).astype(BF16)
    xdt_b = xdt.astype(BF16)
    ecs = jnp.exp(cs_e)
    st = st_scr[...]
    st_b = st.astype(BF16)
    bcb = bcv.astype(BF16)

    r128 = lax.broadcasted_iota(jnp.int32, (2 * q, LANE), 0)
    c128 = lax.broadcasted_iota(jnp.int32, (2 * q, LANE), 1)
    bd2 = (r128 // q) == (c128 // q)

    hg = SSD_HEADS // SSD_GROUPS
    gw = hg * SSD_HEAD_DIM
    y_parts = []
    new_states = []
    for g in range(SSD_GROUPS):
        bm = bcb[:, g * SSD_STATE:(g + 1) * SSD_STATE]
        cm = bcb[:, (SSD_GROUPS + g) * SSD_STATE:(SSD_GROUPS + g + 1) * SSD_STATE]
        bm_rep = jnp.concatenate([bm] * hg, axis=0)
        cbt = lax.dot_general(cm, bm_rep, (((1,), (1,)), ((), ())),
                              preferred_element_type=F32)
        m_all = (cbt * lmat[:, g * gw:(g + 1) * gw]).astype(BF16)
        y_off = jnp.dot(cm, st_b[:, g * gw:(g + 1) * gw], preferred_element_type=F32)
        for j in range(gw // LANE):
            col = g * gw + j * LANE
            xj = xdt_b[:, col:col + LANE]
            xd = jnp.where(bd2, jnp.concatenate([xj, xj], axis=0), jnp.zeros((), BF16))
            y_parts.append(jnp.dot(m_all[:, j * LANE:(j + 1) * LANE], xd, preferred_element_type=F32)
                           + y_off[:, j * LANE:(j + 1) * LANE] * ecs[:, col:col + LANE])
        new_states.append(lax.dot_general(bm, xdt_end[:, g * gw:(g + 1) * gw], (((0,), (0,)), ((), ())),
                                          preferred_element_type=F32))
    y = jnp.concatenate(y_parts, axis=1)
    st_new = st * jnp.exp(cs_last) + jnp.concatenate(new_states, axis=1)
    st_scr[...] = st_new
    st_out_ref[0] = st_new

    y = y + dexp_ref[...] * xs
    zv = z_ref[...]
    y = y * (zv * _sigmoid(zv))
    outs = []
    for g in range(SSD_GROUPS):
        yg = y[:, g * gw:(g + 1) * gw]
        ms = jnp.mean(yg * yg, axis=-1, keepdims=True)
        outs.append(yg * lax.rsqrt(ms + EPS))
    y_ref[...] = jnp.concatenate(outs, axis=1) * ng_ref[...]


def _ssd(proj, ncp, st_in, cpx, cpb, cwx, cbx, cwb, cbb, dtb, alog, dexp, ng, emat):
    m = proj.shape[0]
    nchunks = m // CHUNK
    q = CHUNK
    smap = lambda c: (jnp.maximum(c - (ncp - 1), 0), 0, 0)
    const2 = lambda c: (0, 0)
    return pl.pallas_call(
        functools.partial(_ssd_kernel, ncp),
        grid=(nchunks,),
        in_specs=[pl.BlockSpec((q, 1024), lambda c: (c, C_Z // 1024)),
                  pl.BlockSpec((q, 1024), lambda c: (c, C_XS // 1024)),
                  pl.BlockSpec((q, 512), lambda c: (c, C_BC // 512)),
                  pl.BlockSpec((q, LANE), lambda c: (c, C_SM // LANE)),
                  pl.BlockSpec((1, SSD_STATE, SSD_WIDTH), smap),
                  pl.BlockSpec((1, 8, 1024), smap),
                  pl.BlockSpec((1, 8, 512), smap),
                  pl.BlockSpec((SSD_CONV, 1024), const2),
                  pl.BlockSpec((1, 1024), const2),
                  pl.BlockSpec((SSD_CONV, 512), const2),
                  pl.BlockSpec((1, 512), const2),
                  pl.BlockSpec((1, LANE), const2),
                  pl.BlockSpec((1, LANE), const2),
                  pl.BlockSpec((1, SSD_WIDTH), const2),
                  pl.BlockSpec((1, SSD_WIDTH), const2),
                  pl.BlockSpec((LANE, SSD_WIDTH), const2)],
        out_specs=[pl.BlockSpec((q, SSD_WIDTH), lambda c: (c, 0)),
                   pl.BlockSpec((1, SSD_STATE, SSD_WIDTH), smap)],
        out_shape=[jax.ShapeDtypeStruct((m, SSD_WIDTH), F32),
                   jax.ShapeDtypeStruct(st_in.shape, F32)],
        scratch_shapes=[pltpu.VMEM((SSD_STATE, SSD_WIDTH), F32),
                        pltpu.VMEM((8 + q, 1024), F32),
                        pltpu.VMEM((8 + q, 512), F32)],
        compiler_params=_cparams(("arbitrary",), 32 << 20),
        name="ssd",
    )(proj, proj, proj, proj, st_in, cpx, cpb, cwx, cbx, cwb, cbb, dtb, alog, dexp, ng, emat)


def _gla_kernel(ncp, gq_ref, gk_ref, gv_ref, gg_ref, sm_ref, s_in_ref, wa_ref, ba_ref, ng_ref,
                o_ref, s_out_ref, s_scr):
    c = pl.program_id(0)
    q = CHUNK
    blk = GLA_BLOCK
    kw = GLA_HEADS * GLA_DK

    @pl.when(jnp.logical_or(c == 0, c >= ncp))
    def _():
        s_scr[...] = s_in_ref[0]

    pre = jnp.dot(sm_ref[...].astype(BF16), wa_ref[...], preferred_element_type=F32) + ba_ref[...]
    log_a = -_softplus(-pre) * (1.0 / GLA_TAU)
    r64 = lax.broadcasted_iota(jnp.int32, (q, q), 0)
    c64 = lax.broadcasted_iota(jnp.int32, (q, q), 1)
    same_blk = (r64 // blk) == (c64 // blk)
    causal = jnp.logical_and(same_blk, c64 <= r64)
    b = jnp.dot(causal.astype(F32), log_a, precision=HI, preferred_element_type=F32)
    gk = gk_ref[...]
    qt = gq_ref[...] * (GLA_DK ** -0.5) * jnp.exp(b)
    kt = (gk * jnp.exp(-b)).astype(BF16)
    b_last = jnp.concatenate(
        [jnp.broadcast_to(b[(j + 1) * blk - 1:(j + 1) * blk, :], (blk, kw)) for j in range(q // blk)], axis=0)
    ktil = (gk * jnp.exp(b_last - b)).astype(BF16)
    qt_b = qt.astype(BF16)
    gv_b = gv_ref[...].astype(BF16)

    lane_head = lax.broadcasted_iota(jnp.int32, (q, kw), 1) // GLA_DK
    o_intra = []
    for h in range(GLA_HEADS):
        qm = jnp.where(lane_head == h, qt_b, jnp.zeros((), BF16))
        att = lax.dot_general(qm, kt, (((1,), (1,)), ((), ())), preferred_element_type=F32)
        att = jnp.where(causal, att, 0.0).astype(BF16)
        o_intra.append(jnp.dot(att, gv_b[:, h * GLA_DV:(h + 1) * GLA_DV], preferred_element_type=F32))
    o_intra = jnp.concatenate(o_intra, axis=1)

    rs = lax.broadcasted_iota(jnp.int32, (GLA_WIDTH, kw), 0) // GLA_DV
    cs_ = lax.broadcasted_iota(jnp.int32, (GLA_WIDTH, kw), 1) // GLA_DK
    bd = rs == cs_
    sw = s_scr[...]
    o_inter = []
    for j in range(q // blk):
        sl = slice(j * blk, (j + 1) * blk)
        o_inter.append(lax.dot_general(qt_b[sl], sw.astype(BF16), (((1,), (1,)), ((), ())),
                                       preferred_element_type=F32))
        upd = lax.dot_general(gv_b[sl], ktil[sl], (((0,), (0,)), ((), ())),
                              preferred_element_type=F32)
        decay = jnp.exp(b[(j + 1) * blk - 1:(j + 1) * blk, :])
        sw = sw * decay + jnp.where(bd, upd, 0.0)
    s_scr[...] = sw
    s_out_ref[0] = sw
    o = o_intra + jnp.concatenate(o_inter, axis=0)

    gg = gg_ref[...]
    gate = gg * _sigmoid(gg)
    outs = []
    for h in range(GLA_HEADS):
        oh = o[:, h * GLA_DV:(h + 1) * GLA_DV]
        ms = jnp.mean(oh * oh, axis=-1, keepdims=True)
        outs.append(oh * lax.rsqrt(ms + EPS) * ng_ref[...])
    o_ref[...] = jnp.concatenate(outs, axis=1) * gate


def _gla(proj, ncp, s_in, wa_pad, ba, ng):
    m = proj.shape[0]
    q = CHUNK
    kw = GLA_HEADS * GLA_DK
    smap = lambda c: (jnp.maximum(c - (ncp - 1), 0), 0, 0)
    const2 = lambda c: (0, 0)
    return pl.pallas_call(
        functools.partial(_gla_kernel, ncp),
        grid=(m // q,),
        in_specs=[pl.BlockSpec((q, kw), lambda c: (c, C_GQ // kw)),
                  pl.BlockSpec((q, kw), lambda c: (c, C_GK // kw)),
                  pl.BlockSpec((q, GLA_WIDTH), lambda c: (c, C_GV // GLA_WIDTH)),
                  pl.BlockSpec((q, GLA_WIDTH), lambda c: (c, C_GG // GLA_WIDTH)),
                  pl.BlockSpec((q, LANE), lambda c: (c, C_SM // LANE)),
                  pl.BlockSpec((1, GLA_WIDTH, kw), smap),
                  pl.BlockSpec((LANE, kw), const2),
                  pl.BlockSpec((1, kw), const2),
                  pl.BlockSpec((1, GLA_DV), const2)],
        out_specs=[pl.BlockSpec((q, GLA_WIDTH), lambda c: (c, 0)),
                   pl.BlockSpec((1, GLA_WIDTH, kw), smap)],
        out_shape=[jax.ShapeDtypeStruct((m, GLA_WIDTH), F32),
                   jax.ShapeDtypeStruct(s_in.shape, F32)],
        scratch_shapes=[pltpu.VMEM((GLA_WIDTH, kw), F32)],
        compiler_params=_cparams(("arbitrary",), 24 << 20),
        name="gla",
    )(proj, proj, proj, proj, proj, s_in, wa_pad, ba, ng)


def _attn_prep_kernel(dq_ref, dk_ref, dv_ref, qg_ref, kg_ref, seg_ref,
                      qa_ref, qb_ref, kn_ref, kb_ref, vt_ref):
    seg = seg_ref[...]

    def qknorm(x, g):
        ms = jnp.dot(x * x, seg, precision=HI, preferred_element_type=F32) * (1.0 / DIFF_HD)
        return x * lax.rsqrt(ms + EPS) * g

    qn = qknorm(dq_ref[...], qg_ref[...]) * (DIFF_HD ** -0.5 * LOG2E)
    first = (lax.broadcasted_iota(jnp.int32, qn.shape, 1) // DIFF_HD) % 2 == 0
    qa_ref[...] = jnp.where(first, qn, 0.0).astype(BF16)
    qb_ref[...] = jnp.where(first, 0.0, qn).astype(BF16)
    kn = qknorm(dk_ref[...], kg_ref[...])
    kn_ref[...] = kn
    kb_ref[...] = kn.astype(BF16)
    vt_ref[...] = dv_ref[...].T.astype(BF16)


def _attn_prep(proj, qg, kg, segmat):
    m = proj.shape[0]
    tm = _pick(m, (512, 256, 128))
    w = DIFF_WIDTH
    row = lambda off: pl.BlockSpec((tm, w), lambda i: (i, off // w))
    const2 = lambda i: (0, 0)
    return pl.pallas_call(
        _attn_prep_kernel,
        grid=(m // tm,),
        in_specs=[row(C_DQ), row(C_DK), row(C_DV),
                  pl.BlockSpec((1, w), const2), pl.BlockSpec((1, w), const2),
                  pl.BlockSpec((w, w), const2)],
        out_specs=[pl.BlockSpec((tm, w), lambda i: (i, 0))] * 4 + [pl.BlockSpec((w, tm), lambda i: (0, i))],
        out_shape=[jax.ShapeDtypeStruct((m, w), BF16), jax.ShapeDtypeStruct((m, w), BF16),
                   jax.ShapeDtypeStruct((m, w), F32), jax.ShapeDtypeStruct((m, w), BF16),
                   jax.ShapeDtypeStruct((w, m), BF16)],
        compiler_params=_cparams(("parallel",), 32 << 20),
        name="attn_prep",
    )(proj, proj, proj, qg, kg, segmat)


def _lambda_full(lam_ref, lam_init):
    l = lam_ref[...]
    a = jnp.sum(l[0:1] * l[1:2], axis=-1, keepdims=True)
    b = jnp.sum(l[2:3] * l[3:4], axis=-1, keepdims=True)
    return jnp.exp(a) - jnp.exp(b) + lam_init


ATTN_GUARD = 64.0


def _fold8(x, op):
    r = x[0:SUBLANE]
    for i in range(1, x.shape[0] // SUBLANE):
        r = op(r, x[i * SUBLANE:(i + 1) * SUBLANE])
    return r


def _attn_prompt_kernel(t, kbk, lam_init, qi_ref, ki_ref, slope_ref, qa_ref, qb_ref, kb_ref, vt_ref, lam_ref,
                        og_ref, o_ref, m_scr, l_scr, acc_scr, b0_scr, bd_scr, s0_scr):
    h = pl.program_id(0)
    p = pl.program_id(1)
    qi = qi_ref[p]
    ki = ki_ref[p]
    slope = slope_ref[h]
    nb = t // kbk
    w = 2 * t
    nt = (((1,), (1,)), ((), ()))

    @pl.when(p == 0)
    def _():
        b0_scr[...] = slope * lax.broadcasted_iota(jnp.int32, (kbk, LANE), 0).astype(F32)
        kr = lax.broadcasted_iota(jnp.int32, (t, t), 0)
        qc = lax.broadcasted_iota(jnp.int32, (t, t), 1)
        corr = slope * jnp.minimum(2 * (qc - kr), 0).astype(F32)
        bd_scr[...] = jnp.where(kr // CHUNK <= qc // CHUNK, corr, -jnp.inf)

    @pl.when(ki == 0)
    def _():
        m_scr[...] = jnp.full(m_scr.shape, -jnp.inf, F32)
        l_scr[...] = jnp.zeros(l_scr.shape, F32)
        acc_scr[...] = jnp.zeros(acc_scr.shape, F32)

    def tile(diag):
        shift = slope * ((ki - qi) * t).astype(F32)
        step = slope * float(kbk)
        qcat = jnp.concatenate([qa_ref[...], qb_ref[...]], axis=0)
        b0 = jnp.concatenate([b0_scr[...]] * (w // LANE), axis=1)

        def scores(rows):
            s = lax.dot_general(kb_ref[rows, :], qcat, nt, preferred_element_type=F32) + b0
            if diag:
                blk = bd_scr[rows, :]
                s = s + jnp.concatenate([blk, blk], axis=1)
            return s

        m_old = m_scr[...]
        s0 = scores(slice(0, kbk))
        s0_scr[...] = s0
        m_used = jnp.maximum(m_old, jnp.max(s0, axis=0, keepdims=True) + shift)
        if diag:
            lane = lax.broadcasted_iota(jnp.int32, (1, w), 1)
            last = ((lane % t) // kbk).astype(F32)
        else:
            last = float(nb - 1)
        m_ref = m_used + last * step
        emax8 = lsum8 = pv = None
        for b in range(nb):
            rows = slice(b * kbk, (b + 1) * kbk)
            s = s0_scr[...] if b == 0 else scores(rows)
            e = s - (m_ref - shift - b * step)
            pt = jnp.exp2(e)
            fm, fs = _fold8(e, jnp.maximum), _fold8(pt, jnp.add)
            d = jnp.dot(vt_ref[:, rows], pt.astype(BF16), preferred_element_type=F32)
            emax8, lsum8, pv = (fm, fs, d) if b == 0 else (jnp.maximum(emax8, fm), lsum8 + fs, d + pv)
        emax = jnp.max(emax8, axis=0, keepdims=True)
        m_new = m_ref + jnp.maximum(emax, 0.0)
        ok = jnp.max(emax) <= ATTN_GUARD

        @pl.when(ok)
        def _():
            a_old = jnp.exp2(m_old - m_new)
            a_cur = jnp.exp2(m_ref - m_new)
            l_scr[...] = a_old * l_scr[...] + a_cur * jnp.sum(lsum8, axis=0, keepdims=True)
            acc_scr[...] = a_old * acc_scr[...] + a_cur * pv
            m_scr[...] = m_new

        @pl.when(jnp.logical_not(ok))
        def _():
            def body(b, carry):
                rows = pl.ds(pl.multiple_of(b * kbk, kbk), kbk)
                s = scores(rows) + (shift + slope * (b * kbk).astype(F32))
                m_o = m_scr[...]
                m_n = jnp.maximum(m_o, jnp.max(s, axis=0, keepdims=True))
                pt = jnp.exp2(s - m_n)
                a = jnp.exp2(m_o - m_n)
                l_scr[...] = a * l_scr[...] + jnp.sum(pt, axis=0, keepdims=True)
                acc_scr[...] = a * acc_scr[...] + jnp.dot(vt_ref[:, rows], pt.astype(BF16),
                                                          preferred_element_type=F32)
                m_scr[...] = m_n
                return carry

            lax.fori_loop(0, nb, body, 0)

    @pl.when(ki < qi)
    def _():
        tile(False)

    @pl.when(ki == qi)
    def _():
        tile(True)
        lam = _lambda_full(lam_ref, lam_init)
        on = acc_scr[...] / l_scr[...]
        ot = on[:, 0:t] - lam * on[:, t:w]
        ms = jnp.mean(ot * ot, axis=0, keepdims=True)
        o_ref[...] = (ot * lax.rsqrt(ms + EPS)).T * (og_ref[...] * (1.0 - lam_init))


def _attn_prompt(lp, lam_init, qa, qb, kb, vt, slopes, lam_p, og):
    t = _pick(lp, (1024, 512, 256))
    kbk = 256
    nq = lp // t
    pairs = [(i, j) for i in range(nq) for j in range(i + 1)]
    qi_l = jnp.asarray(np.array([a for a, _ in pairs], np.int32))
    ki_l = jnp.asarray(np.array([b for _, b in pairs], np.int32))
    grid_spec = pltpu.PrefetchScalarGridSpec(
        num_scalar_prefetch=3,
        grid=(DIFF_HEADS, len(pairs)),
        in_specs=[pl.BlockSpec((t, LANE), lambda h, p, qi, ki, sl: (qi[p], h)),
                  pl.BlockSpec((t, LANE), lambda h, p, qi, ki, sl: (qi[p], h)),
                  pl.BlockSpec((t, LANE), lambda h, p, qi, ki, sl: (ki[p], h)),
                  pl.BlockSpec((LANE, t), lambda h, p, qi, ki, sl: (h, ki[p])),
                  pl.BlockSpec((4, DIFF_HD), lambda h, p, qi, ki, sl: (0, 0)),
                  pl.BlockSpec((1, DIFF_VD), lambda h, p, qi, ki, sl: (0, 0))],
        out_specs=pl.BlockSpec((t, LANE), lambda h, p, qi, ki, sl: (qi[p], h)),
        scratch_shapes=[pltpu.VMEM((1, 2 * t), F32), pltpu.VMEM((1, 2 * t), F32),
                        pltpu.VMEM((DIFF_VD, 2 * t), F32),
                        pltpu.VMEM((kbk, LANE), F32), pltpu.VMEM((t, t), F32),
                        pltpu.VMEM((kbk, 2 * t), F32)])
    return pl.pallas_call(
        functools.partial(_attn_prompt_kernel, t, kbk, lam_init),
        grid_spec=grid_spec,
        out_shape=jax.ShapeDtypeStruct((lp, DIFF_WIDTH), F32),
        compiler_params=_cparams(("arbitrary", "arbitrary"), 40 << 20),
        name="attn_prompt",
    )(qi_l, ki_l, slopes, qa, qb, kb, vt, lam_p, og)


def _attn_sample_kernel(past, lam_init, slope_ref, qa_ref, qb_ref, kn_ref, vn_ref, kc_ref, vc_ref, lam_ref, og_ref,
                        o_ref):
    q = CHUNK
    lam = _lambda_full(lam_ref, lam_init)
    kpos = lax.broadcasted_iota(jnp.int32, (1, past), 1).astype(F32)
    r = lax.broadcasted_iota(jnp.int32, (q, q), 0)
    cc = lax.broadcasted_iota(jnp.int32, (q, q), 1)
    new_bias = (past + r - jnp.abs(r - cc)).astype(F32)
    outs = []
    for h in range(DIFF_HEADS):
        slope = slope_ref[h]
        sl = slice(h * LANE, (h + 1) * LANE)
        kc = kc_ref[0, 0, :, h, :].astype(BF16)
        vc = vc_ref[0, 0, :, h, :].astype(BF16)
        kn = kn_ref[:, sl].astype(BF16)
        vn = vn_ref[:, sl].astype(BF16)
        res = []
        for q_ref in (qa_ref, qb_ref):
            qv = q_ref[:, sl]
            s_c = lax.dot_general(qv, kc, (((1,), (1,)), ((), ())), preferred_element_type=F32) + slope * kpos
            s_n = lax.dot_general(qv, kn, (((1,), (1,)), ((), ())), preferred_element_type=F32) + slope * new_bias
            mx = jnp.maximum(jnp.max(s_c, axis=-1, keepdims=True), jnp.max(s_n, axis=-1, keepdims=True))
            p_c = jnp.exp2(s_c - mx)
            p_n = jnp.exp2(s_n - mx)
            den = jnp.sum(p_c, axis=-1, keepdims=True) + jnp.sum(p_n, axis=-1, keepdims=True)
            num = (jnp.dot(p_c.astype(BF16), vc, preferred_element_type=F32)
                   + jnp.dot(p_n.astype(BF16), vn, preferred_element_type=F32))
            res.append(num / den)
        o = res[0] - lam * res[1]
        ms = jnp.mean(o * o, axis=-1, keepdims=True)
        outs.append(o * lax.rsqrt(ms + EPS) * (og_ref[...] * (1.0 - lam_init)))
    o_ref[...] = jnp.concatenate(outs, axis=1)


def _attn_sample(lp, nb, layer, lam_init, qa, qb, kn, proj, cache_k, cache_v, slopes, lam_p, og):
    past = cache_k.shape[2]
    q = CHUNK
    w = DIFF_WIDTH
    off = lp // q
    cache_block = (1, 1, past, DIFF_HEADS, DIFF_VD)
    grid_spec = pltpu.PrefetchScalarGridSpec(
        num_scalar_prefetch=1,
        grid=(nb,),
        in_specs=[pl.BlockSpec((q, w), lambda b, sl: (off + b, 0)),
                  pl.BlockSpec((q, w), lambda b, sl: (off + b, 0)),
                  pl.BlockSpec((q, w), lambda b, sl: (off + b, 0)),
                  pl.BlockSpec((q, w), lambda b, sl: (off + b, C_DV // w)),
                  pl.BlockSpec(cache_block, lambda b, sl: (layer, b, 0, 0, 0)),
                  pl.BlockSpec(cache_block, lambda b, sl: (layer, b, 0, 0, 0)),
                  pl.BlockSpec((4, DIFF_HD), lambda b, sl: (0, 0)),
                  pl.BlockSpec((1, DIFF_VD), lambda b, sl: (0, 0))],
        out_specs=pl.BlockSpec((q, w), lambda b, sl: (b, 0)))
    return pl.pallas_call(
        functools.partial(_attn_sample_kernel, past, lam_init),
        grid_spec=grid_spec,
        out_shape=jax.ShapeDtypeStruct((nb * q, w), F32),
        compiler_params=_cparams(("arbitrary",), 48 << 20),
        name="attn_sample",
    )(slopes, qa, qb, kn, proj, cache_k, cache_v, lam_p, og)


def _out_proj_kernel(npt, xp_ref, xs_ref, ys_ref, yap_ref, yas_ref, yg_ref, w_ref, o_ref):
    is_prompt = pl.program_id(0) < npt
    ya = jnp.where(is_prompt, yap_ref[...], yas_ref[...])
    acc = jnp.dot(ys_ref[...].astype(BF16), w_ref[0:SSD_WIDTH, :], preferred_element_type=F32)
    acc = acc + jnp.dot(ya.astype(BF16), w_ref[SSD_WIDTH:SSD_WIDTH + DIFF_WIDTH, :],
                        preferred_element_type=F32)
    acc = acc + jnp.dot(yg_ref[...].astype(BF16), w_ref[SSD_WIDTH + DIFF_WIDTH:, :],
                        preferred_element_type=F32)
    o_ref[...] = jnp.where(is_prompt, xp_ref[...], xs_ref[...]) + acc


def _out_proj(xp, xs, ys, yap, yas, yg, w):
    lp, ms = xp.shape[0], xs.shape[0]
    tm = _pick(math.gcd(lp, ms), (512, 256, 128, 64))
    npt = lp // tm
    pmap, smap = _split_maps(npt)
    vmem = 2 * (5 * tm * D_MODEL * 4) + D_MODEL * D_MODEL * 2 + (4 << 20)
    return pl.pallas_call(
        functools.partial(_out_proj_kernel, npt),
        grid=((lp + ms) // tm,),
        in_specs=[pl.BlockSpec((tm, D_MODEL), pmap),
                  pl.BlockSpec((tm, D_MODEL), smap),
                  pl.BlockSpec((tm, SSD_WIDTH), lambda i: (i, 0)),
                  pl.BlockSpec((tm, DIFF_WIDTH), pmap),
                  pl.BlockSpec((tm, DIFF_WIDTH), smap),
                  pl.BlockSpec((tm, GLA_WIDTH), lambda i: (i, 0)),
                  pl.BlockSpec((D_MODEL, D_MODEL), lambda i: (0, 0), pipeline_mode=pl.Buffered(1))],
        out_specs=pl.BlockSpec((tm, D_MODEL), lambda i: (i, 0)),
        out_shape=jax.ShapeDtypeStruct((lp + ms, D_MODEL), F32),
        compiler_params=_cparams(("parallel",), vmem),
        name="out_proj",
    )(xp, xs, ys, yap, yas, yg, w)


def _mlp_kernel(npt, x_ref, g_ref, w1_ref, w2_ref, op_ref, os_ref, h_ref):
    i = pl.program_id(0)
    first = pl.program_id(1) == 0

    @pl.when(first)
    def _():
        h_ref[...] = _rms_bf16(x_ref[...], g_ref[...])

    a = jnp.dot(h_ref[...], w1_ref[...], preferred_element_type=F32)
    a = jnp.square(jnp.maximum(a, 0.0)).astype(BF16)
    upd = jnp.dot(a, w2_ref[...], preferred_element_type=F32)

    def accumulate(o_ref):
        @pl.when(first)
        def _():
            o_ref[...] = x_ref[...] + upd

        @pl.when(jnp.logical_not(first))
        def _():
            o_ref[...] += upd

    @pl.when(i < npt)
    def _():
        accumulate(op_ref)

    @pl.when(i >= npt)
    def _():
        accumulate(os_ref)


def _mlp(x, lp, g, w1, w2):
    m = x.shape[0]
    ms = m - lp
    tm = _pick(math.gcd(lp, ms), (512, 256, 128, 64))
    tf = 1024
    npt = lp // tm
    pmap, smap = _split_maps(npt)
    vmem = 6 * tm * D_MODEL * 4 + tm * D_MODEL * 2 + 4 * D_MODEL * tf * 2 + 2 * tm * tf * 4 + (4 << 20)
    return pl.pallas_call(
        functools.partial(_mlp_kernel, npt),
        grid=(m // tm, D_FF // tf),
        in_specs=[pl.BlockSpec((tm, D_MODEL), lambda i, f: (i, 0)),
                  pl.BlockSpec((1, D_MODEL), lambda i, f: (0, 0)),
                  pl.BlockSpec((D_MODEL, tf), lambda i, f: (0, f)),
                  pl.BlockSpec((tf, D_MODEL), lambda i, f: (f, 0))],
        out_specs=[pl.BlockSpec((tm, D_MODEL), pmap), pl.BlockSpec((tm, D_MODEL), smap)],
        out_shape=[jax.ShapeDtypeStruct((lp, D_MODEL), F32), jax.ShapeDtypeStruct((ms, D_MODEL), F32)],
        scratch_shapes=[pltpu.VMEM((tm, D_MODEL), BF16)],
        compiler_params=_cparams(("arbitrary", "arbitrary"), vmem),
        name="mlp",
    )(x, g, w1, w2)


def _pack_w_in(w):
    return jnp.concatenate(
        [w[:, 0:2560], w[:, 2576:4112], w[:, 4624:5136], w[:, 5152:5664], w[:, 4112:4624],
         w[:, 2560:2576], w[:, 5136:5152], jnp.zeros((w.shape[0], PACKED - 5664), w.dtype)],
        axis=1).astype(BF16)


def _pad_lanes(v, width=LANE):
    v = v.reshape(1, -1)
    return jnp.pad(v, ((0, 0), (0, width - v.shape[1])))


def kernel(x_prompt, x_sample, cache_diff_k, cache_diff_v, state_ssd_conv, state_ssd, state_gla, norm1_g, w_in, ssd_conv_w, ssd_conv_b, ssd_dt_bias, ssd_a_log, ssd_d, ssd_norm_g, diff_qn_g, diff_kn_g, diff_lambda, diff_out_g, gla_wa2, gla_ba, gla_norm_g, w_out, norm2_g, w_mlp1, w_mlp2):
    bp, lp, d = x_prompt.shape
    nb, ls, _ = x_sample.shape
    depth = w_in.shape[0]
    past = cache_diff_k.shape[2]
    assert bp == 1 and d == D_MODEL and ls == CHUNK and lp % CHUNK == 0 and past % CHUNK == 0
    ncp = lp // CHUNK
    m = lp + nb * ls

    xp, xs = x_prompt.reshape(lp, d), x_sample.reshape(nb * ls, d)

    emat = (jnp.arange(LANE)[:, None] == (jnp.arange(SSD_WIDTH)[None, :] // SSD_HEAD_DIM)).astype(F32)
    segmat = ((jnp.arange(DIFF_WIDTH)[:, None] // DIFF_HD) == (jnp.arange(DIFF_WIDTH)[None, :] // DIFF_HD)).astype(F32)
    slopes = jnp.exp2(-8.0 * jnp.arange(1, DIFF_HEADS + 1, dtype=F32) / DIFF_HEADS) * LOG2E
    eye_h = jnp.eye(GLA_HEADS, dtype=F32)

    outs = {k: [] for k in ("kp", "vp", "cp", "hp", "sp", "ks", "vs", "cs", "hs", "ss")}
    for l in range(depth):
        lam_init = 0.8 - 0.6 * math.exp(-0.3 * l)
        proj = _in_proj(xp, xs, norm1_g[l].reshape(1, d), _pack_w_in(w_in[l]))

        st_all = jnp.concatenate([jnp.zeros((1,) + state_ssd.shape[2:], F32), state_ssd[l]], axis=0)
        st_in = st_all.reshape(nb + 1, SSD_WIDTH, SSD_STATE).transpose(0, 2, 1)
        cprev = jnp.concatenate([jnp.zeros((1, SSD_CONV - 1, SSD_CONV_DIM), F32), state_ssd_conv[l]], axis=0)
        cprev = jnp.pad(cprev, ((0, 0), (8 - (SSD_CONV - 1), 0), (0, 0)))
        cw, cb = ssd_conv_w[l], ssd_conv_b[l].reshape(1, -1)
        y_ssd, st_out = _ssd(
            proj, ncp, st_in, cprev[:, :, :SSD_WIDTH], cprev[:, :, SSD_WIDTH:],
            cw[:, :SSD_WIDTH], cb[:, :SSD_WIDTH], cw[:, SSD_WIDTH:], cb[:, SSD_WIDTH:],
            _pad_lanes(ssd_dt_bias[l]), _pad_lanes(ssd_a_log[l]),
            jnp.repeat(ssd_d[l], SSD_HEAD_DIM).reshape(1, -1), ssd_norm_g[l].reshape(1, -1), emat)
        h_all = st_out.transpose(0, 2, 1).reshape(nb + 1, SSD_HEADS, SSD_HEAD_DIM, SSD_STATE)
        outs["hp"].append(h_all[0:1])
        outs["hs"].append(h_all[1:])
        xbc_raw = proj[:, C_XS:C_XS + SSD_CONV_DIM]
        outs["cp"].append(xbc_raw[lp - (SSD_CONV - 1):lp][None])
        outs["cs"].append(xbc_raw[lp:].reshape(nb, ls, SSD_CONV_DIM)[:, ls - (SSD_CONV - 1):])

        s_all = jnp.concatenate([jnp.zeros((1,) + state_gla.shape[2:], F32), state_gla[l]], axis=0)
        s_in = jnp.einsum('bhkv,hg->bhvgk', s_all, eye_h).reshape(nb + 1, GLA_WIDTH, GLA_HEADS * GLA_DK)
        wa_pad = jnp.zeros((LANE, GLA_HEADS * GLA_DK), F32).at[GLA_RANK:2 * GLA_RANK].set(gla_wa2[l]).astype(BF16)
        y_gla, s_out = _gla(proj, ncp, s_in, wa_pad, gla_ba[l].reshape(1, -1), gla_norm_g[l].reshape(1, -1))
        s5 = s_out.reshape(nb + 1, GLA_HEADS, GLA_DV, GLA_HEADS, GLA_DK)
        s_new = jnp.stack([s5[:, h, :, h, :] for h in range(GLA_HEADS)], axis=1).transpose(0, 1, 3, 2)
        outs["sp"].append(s_new[0:1])
        outs["ss"].append(s_new[1:])

        qg = jnp.tile(diff_qn_g[l], 2 * DIFF_HEADS).reshape(1, -1)
        kg = jnp.tile(diff_kn_g[l], 2 * DIFF_HEADS).reshape(1, -1)
        qa, qb, kn, kb, vt = _attn_prep(proj, qg, kg, segmat)
        og = diff_out_g[l].reshape(1, -1)
        ya_p = _attn_prompt(lp, lam_init, qa, qb, kb, vt, slopes, diff_lambda[l], og)
        ya_s = _attn_sample(lp, nb, l, lam_init, qa, qb, kn, proj, cache_diff_k, cache_diff_v,
                            slopes, diff_lambda[l], og)
        v_rows = proj[:, C_DV:C_DV + DIFF_WIDTH]
        outs["kp"].append(kn[:lp].reshape(1, lp, DIFF_HEADS, 2 * DIFF_HD))
        outs["ks"].append(kn[lp:].reshape(nb, ls, DIFF_HEADS, 2 * DIFF_HD))
        outs["vp"].append(v_rows[:lp].reshape(1, lp, DIFF_HEADS, DIFF_VD))
        outs["vs"].append(v_rows[lp:].reshape(nb, ls, DIFF_HEADS, DIFF_VD))

        x1 = _out_proj(xp, xs, y_ssd, ya_p, ya_s, y_gla, w_out[l].astype(BF16))
        xp, xs = _mlp(x1, lp, norm2_g[l].reshape(1, d), w_mlp1[l].astype(BF16), w_mlp2[l].astype(BF16))

    st = {k: jnp.stack(v) for k, v in outs.items()}
    return (xp.reshape(1, lp, d), xs.reshape(nb, ls, d),
            st["kp"], st["vp"], st["cp"], st["hp"], st["sp"],
            st["ks"], st["vs"], st["cs"], st["hs"], st["ss"])
```

```python
import functools
import math

import numpy as np
import jax
import jax.numpy as jnp
from jax import lax
from jax.experimental import pallas as pl
from jax.experimental.pallas import tpu as pltpu

F32 = jnp.float32
BF16 = jnp.bfloat16
HI = lax.Precision.HIGHEST

D_MODEL = 2048
CHUNK = 64
SSD_HEAD_DIM = 64
SSD_WIDTH = 1024
SSD_HEADS = 16
SSD_GROUPS = 2
SSD_STATE = 128
SSD_CONV = 4
SSD_CONV_DIM = SSD_WIDTH + 2 * SSD_GROUPS * SSD_STATE
DIFF_HD = 64
DIFF_VD = 128
DIFF_WIDTH = 512
DIFF_HEADS = 4
GLA_DK = 64
GLA_DV = 128
GLA_WIDTH = 512
GLA_HEADS = 4
GLA_RANK = 16
GLA_TAU = 16.0
GLA_BLOCK = 16
D_FF = 4 * D_MODEL
EPS = 1e-6

LOG2E = math.log2(math.e)
LANE = 128
SUBLANE = 8
VMEM_CAP = 56 << 20

C_Z, C_XS, C_BC, C_DQ, C_DK, C_DV, C_GV, C_GG, C_GQ, C_GK, C_SM = (
    0, 1024, 2048, 2560, 3072, 3584, 4096, 4608, 5120, 5376, 5632)
PACKED = 5760
PROJ_TN = 1920


def _pick(m, cands):
    for c in cands:
        if m % c == 0:
            return c
    raise ValueError(f"no tile for {m}")


def _cparams(sem, vmem_bytes):
    return pltpu.CompilerParams(dimension_semantics=sem,
                                vmem_limit_bytes=int(min(VMEM_CAP, max(vmem_bytes, 16 << 20))))


def _sigmoid(x):
    return 1.0 / (1.0 + jnp.exp(-x))


def _softplus(x):
    return jnp.maximum(x, 0.0) + jnp.log1p(jnp.exp(-jnp.abs(x)))


def _rms_bf16(x, g):
    ms = jnp.mean(x * x, axis=-1, keepdims=True)
    return (x * lax.rsqrt(ms + EPS) * g).astype(BF16)


def _in_proj_kernel(npt, xp_ref, xs_ref, g_ref, w_ref, o_ref, h_ref):
    i = pl.program_id(0)

    @pl.when(jnp.logical_and(pl.program_id(1) == 0, i < npt))
    def _():
        h_ref[...] = _rms_bf16(xp_ref[...], g_ref[...])

    @pl.when(jnp.logical_and(pl.program_id(1) == 0, i >= npt))
    def _():
        h_ref[...] = _rms_bf16(xs_ref[...], g_ref[...])

    o_ref[...] = jnp.dot(h_ref[...], w_ref[...], preferred_element_type=F32)


def _split_maps(npt):
    return (lambda i, *_: (jnp.minimum(i, npt - 1), 0)), (lambda i, *_: (jnp.maximum(i - npt, 0), 0))


def _in_proj(xp, xs, g, w):
    lp, ms = xp.shape[0], xs.shape[0]
    tm = _pick(math.gcd(lp, ms), (512, 256, 128, 64))
    tn = PROJ_TN
    npt = lp // tm
    pmap, smap = _split_maps(npt)
    vmem = 4 * tm * D_MODEL * 4 + tm * D_MODEL * 2 + 2 * D_MODEL * tn * 2 + 2 * tm * tn * 4 + (4 << 20)
    return pl.pallas_call(
        functools.partial(_in_proj_kernel, npt),
        grid=((lp + ms) // tm, PACKED // tn),
        in_specs=[pl.BlockSpec((tm, D_MODEL), pmap),
                  pl.BlockSpec((tm, D_MODEL), smap),
                  pl.BlockSpec((1, D_MODEL), lambda i, j: (0, 0)),
                  pl.BlockSpec((D_MODEL, tn), lambda i, j: (0, j))],
        out_specs=pl.BlockSpec((tm, tn), lambda i, j: (i, j)),
        out_shape=jax.ShapeDtypeStruct((lp + ms, PACKED), F32),
        scratch_shapes=[pltpu.VMEM((tm, D_MODEL), BF16)],
        compiler_params=_cparams(("parallel", "arbitrary"), vmem),
        name="in_proj",
    )(xp, xs, g, w)


def _ssd_kernel(ncp, z_ref, xs_ref, bc_ref, sm_ref, st_in_ref, cpx_ref, cpb_ref,
                cwx_ref, cbx_ref, cwb_ref, cbb_ref, dtb_ref, alog_ref, dexp_ref, ng_ref, e_ref,
                y_ref, st_out_ref, st_scr, fx_scr, fb_scr):
    c = pl.program_id(0)
    q = CHUNK

    @pl.when(jnp.logical_or(c == 0, c >= ncp))
    def _():
        st_scr[...] = st_in_ref[0]
        fx_scr[0:8, :] = cpx_ref[0]
        fb_scr[0:8, :] = cpb_ref[0]

    def conv(u_ref, f_scr, w_ref, b_ref):
        u = u_ref[...]
        f_scr[8:8 + q, :] = u
        y = b_ref[...] + f_scr[5:5 + q, :] * w_ref[0:1, :]
        for j in range(1, SSD_CONV):
            y = y + f_scr[5 + j:5 + j + q, :] * w_ref[j:j + 1, :]
        f_scr[0:8, :] = u[q - 8:q, :]
        return y * _sigmoid(y)

    xs = conv(xs_ref, fx_scr, cwx_ref, cbx_ref)
    bcv = conv(bc_ref, fb_scr, cwb_ref, cbb_ref)

    dt = _softplus(sm_ref[...] + dtb_ref[...])
    da = dt * (-jnp.exp(alog_ref[...]))
    r64 = lax.broadcasted_iota(jnp.int32, (q, q), 0)
    c64 = lax.broadcasted_iota(jnp.int32, (q, q), 1)
    tril = (c64 <= r64).astype(F32)
    cs = jnp.dot(tril, da, precision=HI, preferred_element_type=F32)
    both = jnp.dot(jnp.concatenate([dt, cs], axis=0), e_ref[...], precision=HI,
                   preferred_element_type=F32)
    dt_e = both[0:q]
    cs_e = both[q:2 * q]

    row = lax.broadcasted_iota(jnp.int32, (q, SSD_WIDTH), 0)
    sidx = jnp.bitwise_and(lax.broadcasted_iota(jnp.int32, (q, SSD_WIDTH), 1), q - 1)
    cs_row = jnp.sum(jnp.where(sidx == row, cs_e, 0.0), axis=0, keepdims=True)
    cs_last = cs_e[q - 1:q, :]
    lmat = jnp.exp(jnp.where(sidx <= row, cs_e - cs_row, -jnp.inf))

    xdt = xs * dt_e
    xdt_end = (xdt * jnp.exp(cs_last - cs_e)).astype(BF16)
    xdt_b = xdt.astype(BF16)
    ecs = jnp.exp(cs_e)
    st = st_scr[...]
    st_b = st.astype(BF16)
    bcb = bcv.astype(BF16)

    r128 = lax.broadcasted_iota(jnp.int32, (2 * q, LANE), 0)
    c128 = lax.broadcasted_iota(jnp.int32, (2 * q, LANE), 1)
    bd2 = (r128 // q) == (c128 // q)

    hg = SSD_HEADS // SSD_GROUPS
    gw = hg * SSD_HEAD_DIM
    y_parts = []
    new_states = []
    for g in range(SSD_GROUPS):
        bm = bcb[:, g * SSD_STATE:(g + 1) * SSD_STATE]
        cm = bcb[:, (SSD_GROUPS + g) * SSD_STATE:(SSD_GROUPS + g + 1) * SSD_STATE]
        bm_rep = jnp.concatenate([bm] * hg, axis=0)
        cbt = lax.dot_general(cm, bm_rep, (((1,), (1,)), ((), ())),
                              preferred_element_type=F32)
        m_all = (cbt * lmat[:, g * gw:(g + 1) * gw]).astype(BF16)
        y_off = jnp.dot(cm, st_b[:, g * gw:(g + 1) * gw], preferred_element_type=F32)
        for j in range(gw // LANE):
            col = g * gw + j * LANE
            xj = xdt_b[:, col:col + LANE]
            xd = jnp.where(bd2, jnp.concatenate([xj, xj], axis=0), jnp.zeros((), BF16))
            y_parts.append(jnp.dot(m_all[:, j * LANE:(j + 1) * LANE], xd, preferred_element_type=F32)
                           + y_off[:, j * LANE:(j + 1) * LANE] * ecs[:, col:col + LANE])
        new_states.append(lax.dot_general(bm, xdt_end[:, g * gw:(g + 1) * gw], (((0,), (0,)), ((), ())),
                                          preferred_element_type=F32))
    y = jnp.concatenate(y_parts, axis=1)
    st_new = st * jnp.exp(cs_last) + jnp.concatenate(new_states, axis=1)
    st_scr[...] = st_new
    st_out_ref[0] = st_new

    y = y + dexp_ref[...] * xs
    zv = z_ref[...]
    y = y * (zv * _sigmoid(zv))
    outs = []
    for g in range(SSD_GROUPS):
        yg = y[:, g * gw:(g + 1) * gw]
        ms = jnp.mean(yg * yg, axis=-1, keepdims=True)
        outs.append(yg * lax.rsqrt(ms + EPS))
    y_ref[...] = jnp.concatenate(outs, axis=1) * ng_ref[...]


def _ssd(proj, ncp, st_in, cpx, cpb, cwx, cbx, cwb, cbb, dtb, alog, dexp, ng, emat):
    m = proj.shape[0]
    nchunks = m // CHUNK
    q = CHUNK
    smap = lambda c: (jnp.maximum(c - (ncp - 1), 0), 0, 0)
    const2 = lambda c: (0, 0)
    return pl.pallas_call(
        functools.partial(_ssd_kernel, ncp),
        grid=(nchunks,),
        in_specs=[pl.BlockSpec((q, 1024), lambda c: (c, C_Z // 1024)),
                  pl.BlockSpec((q, 1024), lambda c: (c, C_XS // 1024)),
                  pl.BlockSpec((q, 512), lambda c: (c, C_BC // 512)),
                  pl.BlockSpec((q, LANE), lambda c: (c, C_SM // LANE)),
                  pl.BlockSpec((1, SSD_STATE, SSD_WIDTH), smap),
                  pl.BlockSpec((1, 8, 1024), smap),
                  pl.BlockSpec((1, 8, 512), smap),
                  pl.BlockSpec((SSD_CONV, 1024), const2),
                  pl.BlockSpec((1, 1024), const2),
                  pl.BlockSpec((SSD_CONV, 512), const2),
                  pl.BlockSpec((1, 512), const2),
                  pl.BlockSpec((1, LANE), const2),
                  pl.BlockSpec((1, LANE), const2),
                  pl.BlockSpec((1, SSD_WIDTH), const2),
                  pl.BlockSpec((1, SSD_WIDTH), const2),
                  pl.BlockSpec((LANE, SSD_WIDTH), const2)],
        out_specs=[pl.BlockSpec((q, SSD_WIDTH), lambda c: (c, 0)),
                   pl.BlockSpec((1, SSD_STATE, SSD_WIDTH), smap)],
        out_shape=[jax.ShapeDtypeStruct((m, SSD_WIDTH), F32),
                   jax.ShapeDtypeStruct(st_in.shape, F32)],
        scratch_shapes=[pltpu.VMEM((SSD_STATE, SSD_WIDTH), F32),
                        pltpu.VMEM((8 + q, 1024), F32),
                        pltpu.VMEM((8 + q, 512), F32)],
        compiler_params=_cparams(("arbitrary",), 32 << 20),
        name="ssd",
    )(proj, proj, proj, proj, st_in, cpx, cpb, cwx, cbx, cwb, cbb, dtb, alog, dexp, ng, emat)


def _gla_kernel(ncp, gq_ref, gk_ref, gv_ref, gg_ref, sm_ref, s_in_ref, wa_ref, ba_ref, ng_ref,
                o_ref, s_out_ref, s_scr):
    c = pl.program_id(0)
    q = CHUNK
    blk = GLA_BLOCK
    kw = GLA_HEADS * GLA_DK

    @pl.when(jnp.logical_or(c == 0, c >= ncp))
    def _():
        s_scr[...] = s_in_ref[0]

    pre = jnp.dot(sm_ref[...].astype(BF16), wa_ref[...], preferred_element_type=F32) + ba_ref[...]
    log_a = -_softplus(-pre) * (1.0 / GLA_TAU)
    r64 = lax.broadcasted_iota(jnp.int32, (q, q), 0)
    c64 = lax.broadcasted_iota(jnp.int32, (q, q), 1)
    same_blk = (r64 // blk) == (c64 // blk)
    causal = jnp.logical_and(same_blk, c64 <= r64)
    b = jnp.dot(causal.astype(F32), log_a, precision=HI, preferred_element_type=F32)
    gk = gk_ref[...]
    qt = gq_ref[...] * (GLA_DK ** -0.5) * jnp.exp(b)
    kt = (gk * jnp.exp(-b)).astype(BF16)
    b_last = jnp.concatenate(
        [jnp.broadcast_to(b[(j + 1) * blk - 1:(j + 1) * blk, :], (blk, kw)) for j in range(q // blk)], axis=0)
    ktil = (gk * jnp.exp(b_last - b)).astype(BF16)
    qt_b = qt.astype(BF16)
    gv_b = gv_ref[...].astype(BF16)

    lane_head = lax.broadcasted_iota(jnp.int32, (q, kw), 1) // GLA_DK
    o_intra = []
    for h in range(GLA_HEADS):
        qm = jnp.where(lane_head == h, qt_b, jnp.zeros((), BF16))
        att = lax.dot_general(qm, kt, (((1,), (1,)), ((), ())), preferred_element_type=F32)
        att = jnp.where(causal, att, 0.0).astype(BF16)
        o_intra.append(jnp.dot(att, gv_b[:, h * GLA_DV:(h + 1) * GLA_DV], preferred_element_type=F32))
    o_intra = jnp.concatenate(o_intra, axis=1)

    rs = lax.broadcasted_iota(jnp.int32, (GLA_WIDTH, kw), 0) // GLA_DV
    cs_ = lax.broadcasted_iota(jnp.int32, (GLA_WIDTH, kw), 1) // GLA_DK
    bd = rs == cs_
    sw = s_scr[...]
    o_inter = []
    for j in range(q // blk):
        sl = slice(j * blk, (j + 1) * blk)
        o_inter.append(lax.dot_general(qt_b[sl], sw.astype(BF16), (((1,), (1,)), ((), ())),
                                       preferred_element_type=F32))
        upd = lax.dot_general(gv_b[sl], ktil[sl], (((0,), (0,)), ((), ())),
                              preferred_element_type=F32)
        decay = jnp.exp(b[(j + 1) * blk - 1:(j + 1) * blk, :])
        sw = sw * decay + jnp.where(bd, upd, 0.0)
    s_scr[...] = sw
    s_out_ref[0] = sw
    o = o_intra + jnp.concatenate(o_inter, axis=0)

    gg = gg_ref[...]
    gate = gg * _sigmoid(gg)
    outs = []
    for h in range(GLA_HEADS):
        oh = o[:, h * GLA_DV:(h + 1) * GLA_DV]
        ms = jnp.mean(oh * oh, axis=-1, keepdims=True)
        outs.append(oh * lax.rsqrt(ms + EPS) * ng_ref[...])
    o_ref[...] = jnp.concatenate(outs, axis=1) * gate


def _gla(proj, ncp, s_in, wa_pad, ba, ng):
    m = proj.shape[0]
    q = CHUNK
    kw = GLA_HEADS * GLA_DK
    smap = lambda c: (jnp.maximum(c - (ncp - 1), 0), 0, 0)
    const2 = lambda c: (0, 0)
    return pl.pallas_call(
        functools.partial(_gla_kernel, ncp),
        grid=(m // q,),
        in_specs=[pl.BlockSpec((q, kw), lambda c: (c, C_GQ // kw)),
                  pl.BlockSpec((q, kw), lambda c: (c, C_GK // kw)),
                  pl.BlockSpec((q, GLA_WIDTH), lambda c: (c, C_GV // GLA_WIDTH)),
                  pl.BlockSpec((q, GLA_WIDTH), lambda c: (c, C_GG // GLA_WIDTH)),
                  pl.BlockSpec((q, LANE), lambda c: (c, C_SM // LANE)),
                  pl.BlockSpec((1, GLA_WIDTH, kw), smap),
                  pl.BlockSpec((LANE, kw), const2),
                  pl.BlockSpec((1, kw), const2),
                  pl.BlockSpec((1, GLA_DV), const2)],
        out_specs=[pl.BlockSpec((q, GLA_WIDTH), lambda c: (c, 0)),
                   pl.BlockSpec((1, GLA_WIDTH, kw), smap)],
        out_shape=[jax.ShapeDtypeStruct((m, GLA_WIDTH), F32),
                   jax.ShapeDtypeStruct(s_in.shape, F32)],
        scratch_shapes=[pltpu.VMEM((GLA_WIDTH, kw), F32)],
        compiler_params=_cparams(("arbitrary",), 24 << 20),
        name="gla",
    )(proj, proj, proj, proj, proj, s_in, wa_pad, ba, ng)


def _attn_prep_kernel(dq_ref, dk_ref, dv_ref, qg_ref, kg_ref, seg_ref,
                      qa_ref, qb_ref, kn_ref, kb_ref, vt_ref):
    seg = seg_ref[...]

    def qknorm(x, g):
        ms = jnp.dot(x * x, seg, precision=HI, preferred_element_type=F32) * (1.0 / DIFF_HD)
        return x * lax.rsqrt(ms + EPS) * g

    qn = qknorm(dq_ref[...], qg_ref[...]) * (DIFF_HD ** -0.5 * LOG2E)
    first = (lax.broadcasted_iota(jnp.int32, qn.shape, 1) // DIFF_HD) % 2 == 0
    qa_ref[...] = jnp.where(first, qn, 0.0).astype(BF16)
    qb_ref[...] = jnp.where(first, 0.0, qn).astype(BF16)
    kn = qknorm(dk_ref[...], kg_ref[...])
    kn_ref[...] = kn
    kb_ref[...] = kn.astype(BF16)
    vt_ref[...] = dv_ref[...].T.astype(BF16)


def _attn_prep(proj, qg, kg, segmat):
    m = proj.shape[0]
    tm = _pick(m, (512, 256, 128))
    w = DIFF_WIDTH
    row = lambda off: pl.BlockSpec((tm, w), lambda i: (i, off // w))
    const2 = lambda i: (0, 0)
    return pl.pallas_call(
        _attn_prep_kernel,
        grid=(m // tm,),
        in_specs=[row(C_DQ), row(C_DK), row(C_DV),
                  pl.BlockSpec((1, w), const2), pl.BlockSpec((1, w), const2),
                  pl.BlockSpec((w, w), const2)],
        out_specs=[pl.BlockSpec((tm, w), lambda i: (i, 0))] * 4 + [pl.BlockSpec((w, tm), lambda i: (0, i))],
        out_shape=[jax.ShapeDtypeStruct((m, w), BF16), jax.ShapeDtypeStruct((m, w), BF16),
                   jax.ShapeDtypeStruct((m, w), F32), jax.ShapeDtypeStruct((m, w), BF16),
                   jax.ShapeDtypeStruct((w, m), BF16)],
        compiler_params=_cparams(("parallel",), 32 << 20),
        name="attn_prep",
    )(proj, proj, proj, qg, kg, segmat)


def _lambda_full(lam_ref, lam_init):
    l = lam_ref[...]
    a = jnp.sum(l[0:1] * l[1:2], axis=-1, keepdims=True)
    b = jnp.sum(l[2:3] * l[3:4], axis=-1, keepdims=True)
    return jnp.exp(a) - jnp.exp(b) + lam_init


ATTN_GUARD = 64.0


def _fold8(x, op):
    r = x[0:SUBLANE]
    for i in range(1, x.shape[0] // SUBLANE):
        r = op(r, x[i * SUBLANE:(i + 1) * SUBLANE])
    return r


def _attn_prompt_kernel(t, kbk, lam_init, qi_ref, ki_ref, slope_ref, qa_ref, qb_ref, kb_ref, vt_ref, lam_ref,
                        og_ref, o_ref, m_scr, l_scr, acc_scr, b0_scr, bd_scr, s0_scr):
    h = pl.program_id(0)
    p = pl.program_id(1)
    qi = qi_ref[p]
    ki = ki_ref[p]
    slope = slope_ref[h]
    nb = t // kbk
    w = 2 * t
    nt = (((1,), (1,)), ((), ()))

    @pl.when(p == 0)
    def _():
        b0_scr[...] = slope * lax.broadcasted_iota(jnp.int32, (kbk, LANE), 0).astype(F32)
        kr = lax.broadcasted_iota(jnp.int32, (t, t), 0)
        qc = lax.broadcasted_iota(jnp.int32, (t, t), 1)
        corr = slope * jnp.minimum(2 * (qc - kr), 0).astype(F32)
        bd_scr[...] = jnp.where(kr // CHUNK <= qc // CHUNK, corr, -jnp.inf)

    @pl.when(ki == 0)
    def _():
        m_scr[...] = jnp.full(m_scr.shape, -jnp.inf, F32)
        l_scr[...] = jnp.zeros(l_scr.shape, F32)
        acc_scr[...] = jnp.zeros(acc_scr.shape, F32)

    def tile(diag):
        shift = slope * ((ki - qi) * t).astype(F32)
        step = slope * float(kbk)
        qcat = jnp.concatenate([qa_ref[...], qb_ref[...]], axis=0)
        b0 = jnp.concatenate([b0_scr[...]] * (w // LANE), axis=1)

        def scores(rows):
            s = lax.dot_general(kb_ref[rows, :], qcat, nt, preferred_element_type=F32) + b0
            if diag:
                blk = bd_scr[rows, :]
                s = s + jnp.concatenate([blk, blk], axis=1)
            return s

        m_old = m_scr[...]
        s0 = scores(slice(0, kbk))
        s0_scr[...] = s0
        m_used = jnp.maximum(m_old, jnp.max(s0, axis=0, keepdims=True) + shift)
        if diag:
            lane = lax.broadcasted_iota(jnp.int32, (1, w), 1)
            last = ((lane % t) // kbk).astype(F32)
        else:
            last = float(nb - 1)
        m_ref = m_used + last * step
        emax8 = lsum8 = pv = None
        for b in range(nb):
            rows = slice(b * kbk, (b + 1) * kbk)
            s = s0_scr[...] if b == 0 else scores(rows)
            e = s - (m_ref - shift - b * step)
            pt = jnp.exp2(e)
            fm, fs = _fold8(e, jnp.maximum), _fold8(pt, jnp.add)
            d = jnp.dot(vt_ref[:, rows], pt.astype(BF16), preferred_element_type=F32)
            emax8, lsum8, pv = (fm, fs, d) if b == 0 else (jnp.maximum(emax8, fm), lsum8 + fs, d + pv)
        emax = jnp.max(emax8, axis=0, keepdims=True)
        m_new = m_ref + jnp.maximum(emax, 0.0)
        ok = jnp.max(emax) <= ATTN_GUARD

        @pl.when(ok)
        def _():
            a_old = jnp.exp2(m_old - m_new)
            a_cur = jnp.exp2(m_ref - m_new)
            l_scr[...] = a_old * l_scr[...] + a_cur * jnp.sum(lsum8, axis=0, keepdims=True)
            acc_scr[...] = a_old * acc_scr[...] + a_cur * pv
            m_scr[...] = m_new

        @pl.when(jnp.logical_not(ok))
        def _():
            def body(b, carry):
                rows = pl.ds(pl.multiple_of(b * kbk, kbk), kbk)
                s = scores(rows) + (shift + slope * (b * kbk).astype(F32))
                m_o = m_scr[...]
                m_n = jnp.maximum(m_o, jnp.max(s, axis=0, keepdims=True))
                pt = jnp.exp2(s - m_n)
                a = jnp.exp2(m_o - m_n)
                l_scr[...] = a * l_scr[...] + jnp.sum(pt, axis=0, keepdims=True)
                acc_scr[...] = a * acc_scr[...] + jnp.dot(vt_ref[:, rows], pt.astype(BF16),
                                                          preferred_element_type=F32)
                m_scr[...] = m_n
                return carry

            lax.fori_loop(0, nb, body, 0)

    @pl.when(ki < qi)
    def _():
        tile(False)

    @pl.when(ki == qi)
    def _():
        tile(True)
        lam = _lambda_full(lam_ref, lam_init)
        on = acc_scr[...] / l_scr[...]
        ot = on[:, 0:t] - lam * on[:, t:w]
        ms = jnp.mean(ot * ot, axis=0, keepdims=True)
        o_ref[...] = (ot * lax.rsqrt(ms + EPS)).T * (og_ref[...] * (1.0 - lam_init))


def _attn_prompt(lp, lam_init, qa, qb, kb, vt, slopes, lam_p, og):
    t = _pick(lp, (1024, 512, 256))
    kbk = 256
    nq = lp // t
    pairs = [(i, j) for i in range(nq) for j in range(i + 1)]
    qi_l = jnp.asarray(np.array([a for a, _ in pairs], np.int32))
    ki_l = jnp.asarray(np.array([b for _, b in pairs], np.int32))
    grid_spec = pltpu.PrefetchScalarGridSpec(
        num_scalar_prefetch=3,
        grid=(DIFF_HEADS, len(pairs)),
        in_specs=[pl.BlockSpec((t, LANE), lambda h, p, qi, ki, sl: (qi[p], h)),
                  pl.BlockSpec((t, LANE), lambda h, p, qi, ki, sl: (qi[p], h)),
                  pl.BlockSpec((t, LANE), lambda h, p, qi, ki, sl: (ki[p], h)),
                  pl.BlockSpec((LANE, t), lambda h, p, qi, ki, sl: (h, ki[p])),
                  pl.BlockSpec((4, DIFF_HD), lambda h, p, qi, ki, sl: (0, 0)),
                  pl.BlockSpec((1, DIFF_VD), lambda h, p, qi, ki, sl: (0, 0))],
        out_specs=pl.BlockSpec((t, LANE), lambda h, p, qi, ki, sl: (qi[p], h)),
        scratch_shapes=[pltpu.VMEM((1, 2 * t), F32), pltpu.VMEM((1, 2 * t), F32),
                        pltpu.VMEM((DIFF_VD, 2 * t), F32),
                        pltpu.VMEM((kbk, LANE), F32), pltpu.VMEM((t, t), F32),
                        pltpu.VMEM((kbk, 2 * t), F32)])
    return pl.pallas_call(
        functools.partial(_attn_prompt_kernel, t, kbk, lam_init),
        grid_spec=grid_spec,
        out_shape=jax.ShapeDtypeStruct((lp, DIFF_WIDTH), F32),
        compiler_params=_cparams(("arbitrary", "arbitrary"), 40 << 20),
        name="attn_prompt",
    )(qi_l, ki_l, slopes, qa, qb, kb, vt, lam_p, og)


def _attn_sample_kernel(past, lam_init, slope_ref, qa_ref, qb_ref, kn_ref, vn_ref, kc_ref, vc_ref, lam_ref, og_ref,
                        o_ref):
    q = CHUNK
    lam = _lambda_full(lam_ref, lam_init)
    kpos = lax.broadcasted_iota(jnp.int32, (1, past), 1).astype(F32)
    r = lax.broadcasted_iota(jnp.int32, (q, q), 0)
    cc = lax.broadcasted_iota(jnp.int32, (q, q), 1)
    new_bias = (past + r - jnp.abs(r - cc)).astype(F32)
    outs = []
    for h in range(DIFF_HEADS):
        slope = slope_ref[h]
        sl = slice(h * LANE, (h + 1) * LANE)
        kc = kc_ref[pl.ds(h, past, stride=DIFF_HEADS), :].astype(BF16)
        vc = vc_ref[pl.ds(h, past, stride=DIFF_HEADS), :].astype(BF16)
        kn = kn_ref[:, sl].astype(BF16)
        vn = vn_ref[:, sl].astype(BF16)
        res = []
        for q_ref in (qa_ref, qb_ref):
            qv = q_ref[:, sl]
            s_c = lax.dot_general(qv, kc, (((1,), (1,)), ((), ())), preferred_element_type=F32) + slope * kpos
            s_n = lax.dot_general(qv, kn, (((1,), (1,)), ((), ())), preferred_element_type=F32) + slope * new_bias
            mx = jnp.maximum(jnp.max(s_c, axis=-1, keepdims=True), jnp.max(s_n, axis=-1, keepdims=True))
            p_c = jnp.exp2(s_c - mx)
            p_n = jnp.exp2(s_n - mx)
            den = jnp.sum(p_c, axis=-1, keepdims=True) + jnp.sum(p_n, axis=-1, keepdims=True)
            num = (jnp.dot(p_c.astype(BF16), vc, preferred_element_type=F32)
                   + jnp.dot(p_n.astype(BF16), vn, preferred_element_type=F32))
            res.append(num / den)
        o = res[0] - lam * res[1]
        ms = jnp.mean(o * o, axis=-1, keepdims=True)
        outs.append(o * lax.rsqrt(ms + EPS) * (og_ref[...] * (1.0 - lam_init)))
    o_ref[...] = jnp.concatenate(outs, axis=1)


def _attn_sample(lp, nb, layer, lam_init, qa, qb, kn, proj, cache_k, cache_v, slopes, lam_p, og):
    depth, _, past = cache_k.shape[:3]
    q = CHUNK
    w = DIFF_WIDTH
    off = lp // q
    cache_k = cache_k.reshape(depth, nb, past * DIFF_HEADS, DIFF_VD)
    cache_v = cache_v.reshape(depth, nb, past * DIFF_HEADS, DIFF_VD)
    cache_block = (None, None, past * DIFF_HEADS, DIFF_VD)
    grid_spec = pltpu.PrefetchScalarGridSpec(
        num_scalar_prefetch=1,
        grid=(nb,),
        in_specs=[pl.BlockSpec((q, w), lambda b, sl: (off + b, 0)),
                  pl.BlockSpec((q, w), lambda b, sl: (off + b, 0)),
                  pl.BlockSpec((q, w), lambda b, sl: (off + b, 0)),
                  pl.BlockSpec((q, w), lambda b, sl: (off + b, C_DV // w)),
                  pl.BlockSpec(cache_block, lambda b, sl: (layer, b, 0, 0)),
                  pl.BlockSpec(cache_block, lambda b, sl: (layer, b, 0, 0)),
                  pl.BlockSpec((4, DIFF_HD), lambda b, sl: (0, 0)),
                  pl.BlockSpec((1, DIFF_VD), lambda b, sl: (0, 0))],
        out_specs=pl.BlockSpec((q, w), lambda b, sl: (b, 0)))
    return pl.pallas_call(
        functools.partial(_attn_sample_kernel, past, lam_init),
        grid_spec=grid_spec,
        out_shape=jax.ShapeDtypeStruct((nb * q, DIFF_WIDTH), F32),
        compiler_params=_cparams(("arbitrary",), 32 << 20),
        name="attn_sample",
    )(slopes, qa, qb, kn, proj, cache_k, cache_v, lam_p, og)


def _out_proj_kernel(npt, xp_ref, xs_ref, ys_ref, yap_ref, yas_ref, yg_ref, w_ref, o_ref):
    is_prompt = pl.program_id(0) < npt
    ya = jnp.where(is_prompt, yap_ref[...], yas_ref[...])
    acc = jnp.dot(ys_ref[...].astype(BF16), w_ref[0:SSD_WIDTH, :], preferred_element_type=F32)
    acc = acc + jnp.dot(ya.astype(BF16), w_ref[SSD_WIDTH:SSD_WIDTH + DIFF_WIDTH, :],
                        preferred_element_type=F32)
    acc = acc + jnp.dot(yg_ref[...].astype(BF16), w_ref[SSD_WIDTH + DIFF_WIDTH:, :],
                        preferred_element_type=F32)
    o_ref[...] = jnp.where(is_prompt, xp_ref[...], xs_ref[...]) + acc


def _out_proj(xp, xs, ys, yap, yas, yg, w):
    lp, ms = xp.shape[0], xs.shape[0]
    tm = _pick(math.gcd(lp, ms), (512, 256, 128, 64))
    npt = lp // tm
    pmap, smap = _split_maps(npt)
    vmem = 2 * (5 * tm * D_MODEL * 4) + D_MODEL * D_MODEL * 2 + (4 << 20)
    return pl.pallas_call(
        functools.partial(_out_proj_kernel, npt),
        grid=((lp + ms) // tm,),
        in_specs=[pl.BlockSpec((tm, D_MODEL), pmap),
                  pl.BlockSpec((tm, D_MODEL), smap),
                  pl.BlockSpec((tm, SSD_WIDTH), lambda i: (i, 0)),
                  pl.BlockSpec((tm, DIFF_WIDTH), pmap),
                  pl.BlockSpec((tm, DIFF_WIDTH), smap),
                  pl.BlockSpec((tm, GLA_WIDTH), lambda i: (i, 0)),
                  pl.BlockSpec((D_MODEL, D_MODEL), lambda i: (0, 0), pipeline_mode=pl.Buffered(1))],
        out_specs=pl.BlockSpec((tm, D_MODEL), lambda i: (i, 0)),
        out_shape=jax.ShapeDtypeStruct((lp + ms, D_MODEL), F32),
        compiler_params=_cparams(("parallel",), vmem),
        name="out_proj",
    )(xp, xs, ys, yap, yas, yg, w)


def _mlp_kernel(npt, x_ref, g_ref, w1_ref, w2_ref, op_ref, os_ref, h_ref):
    i = pl.program_id(0)
    first = pl.program_id(1) == 0

    @pl.when(first)
    def _():
        h_ref[...] = _rms_bf16(x_ref[...], g_ref[...])

    a = jnp.dot(h_ref[...], w1_ref[...], preferred_element_type=F32)
    a = jnp.square(jnp.maximum(a, 0.0)).astype(BF16)

    def accumulate(o_ref):
        @pl.when(first)
        def _():
            o_ref[...] = x_ref[...]

        o_ref[...] += jnp.dot(a, w2_ref[...], preferred_element_type=F32)

    @pl.when(i < npt)
    def _():
        accumulate(op_ref)

    @pl.when(i >= npt)
    def _():
        accumulate(os_ref)


def _mlp(x, lp, g, w1, w2):
    m = x.shape[0]
    ms = m - lp
    tm = _pick(math.gcd(lp, ms), (512, 256, 128, 64))
    tf = 1024
    npt = lp // tm
    pmap, smap = _split_maps(npt)
    vmem = 6 * tm * D_MODEL * 4 + tm * D_MODEL * 2 + 4 * D_MODEL * tf * 2 + 2 * tm * tf * 4 + (4 << 20)
    return pl.pallas_call(
        functools.partial(_mlp_kernel, npt),
        grid=(m // tm, D_FF // tf),
        in_specs=[pl.BlockSpec((tm, D_MODEL), lambda i, f: (i, 0)),
                  pl.BlockSpec((1, D_MODEL), lambda i, f: (0, 0)),
                  pl.BlockSpec((D_MODEL, tf), lambda i, f: (0, f)),
                  pl.BlockSpec((tf, D_MODEL), lambda i, f: (f, 0))],
        out_specs=[pl.BlockSpec((tm, D_MODEL), pmap), pl.BlockSpec((tm, D_MODEL), smap)],
        out_shape=[jax.ShapeDtypeStruct((lp, D_MODEL), F32), jax.ShapeDtypeStruct((ms, D_MODEL), F32)],
        scratch_shapes=[pltpu.VMEM((tm, D_MODEL), BF16)],
        compiler_params=_cparams(("arbitrary", "arbitrary"), vmem),
        name="mlp",
    )(x, g, w1, w2)


def _pack_w_in(w):
    return jnp.concatenate(
        [w[:, 0:2560], w[:, 2576:4112], w[:, 4624:5136], w[:, 5152:5664], w[:, 4112:4624],
         w[:, 2560:2576], w[:, 5136:5152], jnp.zeros((w.shape[0], PACKED - 5664), w.dtype)],
        axis=1).astype(BF16)


def _pad_lanes(v, width=LANE):
    v = v.reshape(1, -1)
    return jnp.pad(v, ((0, 0), (0, width - v.shape[1])))


def kernel(x_prompt, x_sample, cache_diff_k, cache_diff_v, state_ssd_conv, state_ssd, state_gla, norm1_g, w_in, ssd_conv_w, ssd_conv_b, ssd_dt_bias, ssd_a_log, ssd_d, ssd_norm_g, diff_qn_g, diff_kn_g, diff_lambda, diff_out_g, gla_wa2, gla_ba, gla_norm_g, w_out, norm2_g, w_mlp1, w_mlp2):
    bp, lp, d = x_prompt.shape
    nb, ls, _ = x_sample.shape
    depth = w_in.shape[0]
    past = cache_diff_k.shape[2]
    assert bp == 1 and d == D_MODEL and ls == CHUNK and lp % CHUNK == 0 and past % CHUNK == 0
    ncp = lp // CHUNK
    m = lp + nb * ls

    xp, xs = x_prompt.reshape(lp, d), x_sample.reshape(nb * ls, d)

    emat = (jnp.arange(LANE)[:, None] == (jnp.arange(SSD_WIDTH)[None, :] // SSD_HEAD_DIM)).astype(F32)
    segmat = ((jnp.arange(DIFF_WIDTH)[:, None] // DIFF_HD) == (jnp.arange(DIFF_WIDTH)[None, :] // DIFF_HD)).astype(F32)
    slopes = jnp.exp2(-8.0 * jnp.arange(1, DIFF_HEADS + 1, dtype=F32) / DIFF_HEADS) * LOG2E
    eye_h = jnp.eye(GLA_HEADS, dtype=F32)

    outs = {k: [] for k in ("kp", "vp", "cp", "hp", "sp", "ks", "vs", "cs", "hs", "ss")}
    for l in range(depth):
        lam_init = 0.8 - 0.6 * math.exp(-0.3 * l)
        proj = _in_proj(xp, xs, norm1_g[l].reshape(1, d), _pack_w_in(w_in[l]))

        st_all = jnp.concatenate([jnp.zeros((1,) + state_ssd.shape[2:], F32), state_ssd[l]], axis=0)
        st_in = st_all.reshape(nb + 1, SSD_WIDTH, SSD_STATE).transpose(0, 2, 1)
        cprev = jnp.concatenate([jnp.zeros((1, SSD_CONV - 1, SSD_CONV_DIM), F32), state_ssd_conv[l]], axis=0)
        cprev = jnp.pad(cprev, ((0, 0), (8 - (SSD_CONV - 1), 0), (0, 0)))
        cw, cb = ssd_conv_w[l], ssd_conv_b[l].reshape(1, -1)
        y_ssd, st_out = _ssd(
            proj, ncp, st_in, cprev[:, :, :SSD_WIDTH], cprev[:, :, SSD_WIDTH:],
            cw[:, :SSD_WIDTH], cb[:, :SSD_WIDTH], cw[:, SSD_WIDTH:], cb[:, SSD_WIDTH:],
            _pad_lanes(ssd_dt_bias[l]), _pad_lanes(ssd_a_log[l]),
            jnp.repeat(ssd_d[l], SSD_HEAD_DIM).reshape(1, -1), ssd_norm_g[l].reshape(1, -1), emat)
        h_all = st_out.transpose(0, 2, 1).reshape(nb + 1, SSD_HEADS, SSD_HEAD_DIM, SSD_STATE)
        outs["hp"].append(h_all[0:1])
        outs["hs"].append(h_all[1:])
        xbc_raw = proj[:, C_XS:C_XS + SSD_CONV_DIM]
        outs["cp"].append(xbc_raw[lp - (SSD_CONV - 1):lp][None])
        outs["cs"].append(xbc_raw[lp:].reshape(nb, ls, SSD_CONV_DIM)[:, ls - (SSD_CONV - 1):])

        s_all = jnp.concatenate([jnp.zeros((1,) + state_gla.shape[2:], F32), state_gla[l]], axis=0)
        s_in = jnp.einsum('bhkv,hg->bhvgk', s_all, eye_h).reshape(nb + 1, GLA_WIDTH, GLA_HEADS * GLA_DK)
        wa_pad = jnp.zeros((LANE, GLA_HEADS * GLA_DK), F32).at[GLA_RANK:2 * GLA_RANK].set(gla_wa2[l]).astype(BF16)
        y_gla, s_out = _gla(proj, ncp, s_in, wa_pad, gla_ba[l].reshape(1, -1), gla_norm_g[l].reshape(1, -1))
        s5 = s_out.reshape(nb + 1, GLA_HEADS, GLA_DV, GLA_HEADS, GLA_DK)
        s_new = jnp.stack([s5[:, h, :, h, :] for h in range(GLA_HEADS)], axis=1).transpose(0, 1, 3, 2)
        outs["sp"].append(s_new[0:1])
        outs["ss"].append(s_new[1:])

        qg = jnp.tile(diff_qn_g[l], 2 * DIFF_HEADS).reshape(1, -1)
        kg = jnp.tile(diff_kn_g[l], 2 * DIFF_HEADS).reshape(1, -1)
        qa, qb, kn, kb, vt = _attn_prep(proj, qg, kg, segmat)
        og = diff_out_g[l].reshape(1, -1)
        ya_p = _attn_prompt(lp, lam_init, qa, qb, kb, vt, slopes, diff_lambda[l], og)
        ya_s = _attn_sample(lp, nb, l, lam_init, qa, qb, kn, proj, cache_diff_k, cache_diff_v,
                            slopes, diff_lambda[l], og)
        v_rows = proj[:, C_DV:C_DV + DIFF_WIDTH]
        outs["kp"].append(kn[:lp].reshape(1, lp, DIFF_HEADS, 2 * DIFF_HD))
        outs["ks"].append(kn[lp:].reshape(nb, ls, DIFF_HEADS, 2 * DIFF_HD))
        outs["vp"].append(v_rows[:lp].reshape(1, lp, DIFF_HEADS, DIFF_VD))
        outs["vs"].append(v_rows[lp:].reshape(nb, ls, DIFF_HEADS, DIFF_VD))

        x1 = _out_proj(xp, xs, y_ssd, ya_p, ya_s, y_gla, w_out[l].astype(BF16))
        xp, xs = _mlp(x1, lp, norm2_g[l].reshape(1, d), w_mlp1[l].astype(BF16), w_mlp2[l].astype(BF16))

    st = {k: jnp.stack(v) for k, v in outs.items()}
    return (xp.reshape(1, lp, d), xs.reshape(nb, ls, d),
            st["kp"], st["vp"], st["cp"], st["hp"], st["sp"],
            st["ks"], st["vs"], st["cs"], st["hs"], st["ss"])
```

```python
import functools
import math

import numpy as np
import jax
import jax.numpy as jnp
from jax import lax
from jax.experimental import pallas as pl
from jax.experimental.pallas import tpu as pltpu

F32 = jnp.float32
BF16 = jnp.bfloat16
HI = lax.Precision.HIGHEST

D_MODEL = 2048
CHUNK = 64
SSD_HEAD_DIM = 64
SSD_WIDTH = 1024
SSD_HEADS = 16
SSD_GROUPS = 2
SSD_STATE = 128
SSD_CONV = 4
SSD_CONV_DIM = SSD_WIDTH + 2 * SSD_GROUPS * SSD_STATE
DIFF_HD = 64
DIFF_VD = 128
DIFF_WIDTH = 512
DIFF_HEADS = 4
GLA_DK = 64
GLA_DV = 128
GLA_WIDTH = 512
GLA_HEADS = 4
GLA_RANK = 16
GLA_TAU = 16.0
GLA_BLOCK = 16
D_FF = 4 * D_MODEL
EPS = 1e-6

LOG2E = math.log2(math.e)
LANE = 128
SUBLANE = 8
VMEM_CAP = 56 << 20

C_Z, C_XS, C_BC, C_DQ, C_DK, C_DV, C_GV, C_GG, C_GQ, C_GK, C_SM = (
    0, 1024, 2048, 2560, 3072, 3584, 4096, 4608, 5120, 5376, 5632)
PACKED = 5760
PROJ_TN = 1920


def _pick(m, cands):
    for c in cands:
        if m % c == 0:
            return c
    raise ValueError(f"no tile for {m}")


def _cparams(sem, vmem_bytes):
    return pltpu.CompilerParams(dimension_semantics=sem,
                                vmem_limit_bytes=int(min(VMEM_CAP, max(vmem_bytes, 16 << 20))))


def _sigmoid(x):
    return 1.0 / (1.0 + jnp.exp(-x))


def _softplus(x):
    return jnp.maximum(x, 0.0) + jnp.log1p(jnp.exp(-jnp.abs(x)))


def _rms_bf16(x, g):
    ms = jnp.mean(x * x, axis=-1, keepdims=True)
    return (x * lax.rsqrt(ms + EPS) * g).astype(BF16)


def _in_proj_kernel(npt, xp_ref, xs_ref, g_ref, w_ref, o_ref, h_ref):
    i = pl.program_id(0)

    @pl.when(jnp.logical_and(pl.program_id(1) == 0, i < npt))
    def _():
        h_ref[...] = _rms_bf16(xp_ref[...], g_ref[...])

    @pl.when(jnp.logical_and(pl.program_id(1) == 0, i >= npt))
    def _():
        h_ref[...] = _rms_bf16(xs_ref[...], g_ref[...])

    o_ref[...] = jnp.dot(h_ref[...], w_ref[...], preferred_element_type=F32)


def _split_maps(npt):
    return (lambda i, *_: (jnp.minimum(i, npt - 1), 0)), (lambda i, *_: (jnp.maximum(i - npt, 0), 0))


def _in_proj(xp, xs, g, w):
    lp, ms = xp.shape[0], xs.shape[0]
    tm = _pick(math.gcd(lp, ms), (512, 256, 128, 64))
    tn = PROJ_TN
    npt = lp // tm
    pmap, smap = _split_maps(npt)
    vmem = 4 * tm * D_MODEL * 4 + tm * D_MODEL * 2 + 2 * D_MODEL * tn * 2 + 2 * tm * tn * 4 + (4 << 20)
    return pl.pallas_call(
        functools.partial(_in_proj_kernel, npt),
        grid=((lp + ms) // tm, PACKED // tn),
        in_specs=[pl.BlockSpec((tm, D_MODEL), pmap),
                  pl.BlockSpec((tm, D_MODEL), smap),
                  pl.BlockSpec((1, D_MODEL), lambda i, j: (0, 0)),
                  pl.BlockSpec((D_MODEL, tn), lambda i, j: (0, j))],
        out_specs=pl.BlockSpec((tm, tn), lambda i, j: (i, j)),
        out_shape=jax.ShapeDtypeStruct((lp + ms, PACKED), F32),
        scratch_shapes=[pltpu.VMEM((tm, D_MODEL), BF16)],
        compiler_params=_cparams(("parallel", "arbitrary"), vmem),
        name="in_proj",
    )(xp, xs, g, w)


def _ssd_kernel(ncp, z_ref, xs_ref, bc_ref, sm_ref, st_in_ref, cpx_ref, cpb_ref,
                cwx_ref, cbx_ref, cwb_ref, cbb_ref, dtb_ref, alog_ref, dexp_ref, ng_ref, e_ref,
                y_ref, st_out_ref, st_scr, fx_scr, fb_scr):
    c = pl.program_id(0)
    q = CHUNK

    @pl.when(jnp.logical_or(c == 0, c >= ncp))
    def _():
        st_scr[...] = st_in_ref[0]
        fx_scr[0:8, :] = cpx_ref[0]
        fb_scr[0:8, :] = cpb_ref[0]

    def conv(u_ref, f_scr, w_ref, b_ref):
        u = u_ref[...]
        f_scr[8:8 + q, :] = u
        y = b_ref[...] + f_scr[5:5 + q, :] * w_ref[0:1, :]
        for j in range(1, SSD_CONV):
            y = y + f_scr[5 + j:5 + j + q, :] * w_ref[j:j + 1, :]
        f_scr[0:8, :] = u[q - 8:q, :]
        return y * _sigmoid(y)

    xs = conv(xs_ref, fx_scr, cwx_ref, cbx_ref)
    bcv = conv(bc_ref, fb_scr, cwb_ref, cbb_ref)

    dt = _softplus(sm_ref[...] + dtb_ref[...])
    da = dt * (-jnp.exp(alog_ref[...]))
    r64 = lax.broadcasted_iota(jnp.int32, (q, q), 0)
    c64 = lax.broadcasted_iota(jnp.int32, (q, q), 1)
    tril = (c64 <= r64).astype(F32)
    cs = jnp.dot(tril, da, precision=HI, preferred_element_type=F32)
    both = jnp.dot(jnp.concatenate([dt, cs], axis=0), e_ref[...], precision=HI,
                   preferred_element_type=F32)
    dt_e = both[0:q]
    cs_e = both[q:2 * q]

    row = lax.broadcasted_iota(jnp.int32, (q, SSD_WIDTH), 0)
    sidx = jnp.bitwise_and(lax.broadcasted_iota(jnp.int32, (q, SSD_WIDTH), 1), q - 1)
    cs_row = jnp.sum(jnp.where(sidx == row, cs_e, 0.0), axis=0, keepdims=True)
    cs_last = cs_e[q - 1:q, :]
    lmat = jnp.exp(jnp.where(sidx <= row, cs_e - cs_row, -jnp.inf))

    xdt = xs * dt_e
    xdt_end = (xdt * jnp.exp(cs_last - cs_e)).astype(BF16)
    xdt_b = xdt.astype(BF16)
    ecs = jnp.exp(cs_e)
    st = st_scr[...]
    st_b = st.astype(BF16)
    bcb = bcv.astype(BF16)

    r128 = lax.broadcasted_iota(jnp.int32, (2 * q, LANE), 0)
    c128 = lax.broadcasted_iota(jnp.int32, (2 * q, LANE), 1)
    bd2 = (r128 // q) == (c128 // q)

    hg = SSD_HEADS // SSD_GROUPS
    gw = hg * SSD_HEAD_DIM
    y_parts = []
    new_states = []
    for g in range(SSD_GROUPS):
        bm = bcb[:, g * SSD_STATE:(g + 1) * SSD_STATE]
        cm = bcb[:, (SSD_GROUPS + g) * SSD_STATE:(SSD_GROUPS + g + 1) * SSD_STATE]
        bm_rep = jnp.concatenate([bm] * hg, axis=0)
        cbt = lax.dot_general(cm, bm_rep, (((1,), (1,)), ((), ())),
                              preferred_element_type=F32)
        m_all = (cbt * lmat[:, g * gw:(g + 1) * gw]).astype(BF16)
        y_off = jnp.dot(cm, st_b[:, g * gw:(g + 1) * gw], preferred_element_type=F32)
        for j in range(gw // LANE):
            col = g * gw + j * LANE
            xj = xdt_b[:, col:col + LANE]
            xd = jnp.where(bd2, jnp.concatenate([xj, xj], axis=0), jnp.zeros((), BF16))
            y_parts.append(jnp.dot(m_all[:, j * LANE:(j + 1) * LANE], xd, preferred_element_type=F32)
                           + y_off[:, j * LANE:(j + 1) * LANE] * ecs[:, col:col + LANE])
        new_states.append(lax.dot_general(bm, xdt_end[:, g * gw:(g + 1) * gw], (((0,), (0,)), ((), ())),
                                          preferred_element_type=F32))
    y = jnp.concatenate(y_parts, axis=1)
    st_new = st * jnp.exp(cs_last) + jnp.concatenate(new_states, axis=1)
    st_scr[...] = st_new
    st_out_ref[0] = st_new

    y = y + dexp_ref[...] * xs
    zv = z_ref[...]
    y = y * (zv * _sigmoid(zv))
    outs = []
    for g in range(SSD_GROUPS):
        yg = y[:, g * gw:(g + 1) * gw]
        ms = jnp.mean(yg * yg, axis=-1, keepdims=True)
        outs.append(yg * lax.rsqrt(ms + EPS))
    y_ref[...] = jnp.concatenate(outs, axis=1) * ng_ref[...]


def _ssd(proj, ncp, st_in, cpx, cpb, cwx, cbx, cwb, cbb, dtb, alog, dexp, ng, emat):
    m = proj.shape[0]
    nchunks = m // CHUNK
    q = CHUNK
    smap = lambda c: (jnp.maximum(c - (ncp - 1), 0), 0, 0)
    const2 = lambda c: (0, 0)
    return pl.pallas_call(
        functools.partial(_ssd_kernel, ncp),
        grid=(nchunks,),
        in_specs=[pl.BlockSpec((q, 1024), lambda c: (c, C_Z // 1024)),
                  pl.BlockSpec((q, 1024), lambda c: (c, C_XS // 1024)),
                  pl.BlockSpec((q, 512), lambda c: (c, C_BC // 512)),
                  pl.BlockSpec((q, LANE), lambda c: (c, C_SM // LANE)),
                  pl.BlockSpec((1, SSD_STATE, SSD_WIDTH), smap),
                  pl.BlockSpec((1, 8, 1024), smap),
                  pl.BlockSpec((1, 8, 512), smap),
                  pl.BlockSpec((SSD_CONV, 1024), const2),
                  pl.BlockSpec((1, 1024), const2),
                  pl.BlockSpec((SSD_CONV, 512), const2),
                  pl.BlockSpec((1, 512), const2),
                  pl.BlockSpec((1, LANE), const2),
                  pl.BlockSpec((1, LANE), const2),
                  pl.BlockSpec((1, SSD_WIDTH), const2),
                  pl.BlockSpec((1, SSD_WIDTH), const2),
                  pl.BlockSpec((LANE, SSD_WIDTH), const2)],
        out_specs=[pl.BlockSpec((q, SSD_WIDTH), lambda c: (c, 0)),
                   pl.BlockSpec((1, SSD_STATE, SSD_WIDTH), smap)],
        out_shape=[jax.ShapeDtypeStruct((m, SSD_WIDTH), F32),
                   jax.ShapeDtypeStruct(st_in.shape, F32)],
        scratch_shapes=[pltpu.VMEM((SSD_STATE, SSD_WIDTH), F32),
                        pltpu.VMEM((8 + q, 1024), F32),
                        pltpu.VMEM((8 + q, 512), F32)],
        compiler_params=_cparams(("arbitrary",), 32 << 20),
        name="ssd",
    )(proj, proj, proj, proj, st_in, cpx, cpb, cwx, cbx, cwb, cbb, dtb, alog, dexp, ng, emat)


def _gla_kernel(ncp, gq_ref, gk_ref, gv_ref, gg_ref, sm_ref, s_in_ref, wa_ref, ba_ref, ng_ref,
                o_ref, s_out_ref, s_scr):
    c = pl.program_id(0)
    q = CHUNK
    blk = GLA_BLOCK
    kw = GLA_HEADS * GLA_DK

    @pl.when(jnp.logical_or(c == 0, c >= ncp))
    def _():
        s_scr[...] = s_in_ref[0]

    pre = jnp.dot(sm_ref[...].astype(BF16), wa_ref[...], preferred_element_type=F32) + ba_ref[...]
    log_a = -_softplus(-pre) * (1.0 / GLA_TAU)
    r64 = lax.broadcasted_iota(jnp.int32, (q, q), 0)
    c64 = lax.broadcasted_iota(jnp.int32, (q, q), 1)
    same_blk = (r64 // blk) == (c64 // blk)
    causal = jnp.logical_and(same_blk, c64 <= r64)
    b = jnp.dot(causal.astype(F32), log_a, precision=HI, preferred_element_type=F32)
    gk = gk_ref[...]
    qt = gq_ref[...] * (GLA_DK ** -0.5) * jnp.exp(b)
    kt = (gk * jnp.exp(-b)).astype(BF16)
    b_last = jnp.concatenate(
        [jnp.broadcast_to(b[(j + 1) * blk - 1:(j + 1) * blk, :], (blk, kw)) for j in range(q // blk)], axis=0)
    ktil = (gk * jnp.exp(b_last - b)).astype(BF16)
    qt_b = qt.astype(BF16)
    gv_b = gv_ref[...].astype(BF16)

    lane_head = lax.broadcasted_iota(jnp.int32, (q, kw), 1) // GLA_DK
    o_intra = []
    for h in range(GLA_HEADS):
        qm = jnp.where(lane_head == h, qt_b, jnp.zeros((), BF16))
        att = lax.dot_general(qm, kt, (((1,), (1,)), ((), ())), preferred_element_type=F32)
        att = jnp.where(causal, att, 0.0).astype(BF16)
        o_intra.append(jnp.dot(att, gv_b[:, h * GLA_DV:(h + 1) * GLA_DV], preferred_element_type=F32))
    o_intra = jnp.concatenate(o_intra, axis=1)

    rs = lax.broadcasted_iota(jnp.int32, (GLA_WIDTH, kw), 0) // GLA_DV
    cs_ = lax.broadcasted_iota(jnp.int32, (GLA_WIDTH, kw), 1) // GLA_DK
    bd = rs == cs_
    sw = s_scr[...]
    o_inter = []
    for j in range(q // blk):
        sl = slice(j * blk, (j + 1) * blk)
        o_inter.append(lax.dot_general(qt_b[sl], sw.astype(BF16), (((1,), (1,)), ((), ())),
                                       preferred_element_type=F32))
        upd = lax.dot_general(gv_b[sl], ktil[sl], (((0,), (0,)), ((), ())),
                              preferred_element_type=F32)
        decay = jnp.exp(b[(j + 1) * blk - 1:(j + 1) * blk, :])
        sw = sw * decay + jnp.where(bd, upd, 0.0)
    s_scr[...] = sw
    s_out_ref[0] = sw
    o = o_intra + jnp.concatenate(o_inter, axis=0)

    gg = gg_ref[...]
    gate = gg * _sigmoid(gg)
    outs = []
    for h in range(GLA_HEADS):
        oh = o[:, h * GLA_DV:(h + 1) * GLA_DV]
        ms = jnp.mean(oh * oh, axis=-1, keepdims=True)
        outs.append(oh * lax.rsqrt(ms + EPS) * ng_ref[...])
    o_ref[...] = jnp.concatenate(outs, axis=1) * gate


def _gla(proj, ncp, s_in, wa_pad, ba, ng):
    m = proj.shape[0]
    q = CHUNK
    kw = GLA_HEADS * GLA_DK
    smap = lambda c: (jnp.maximum(c - (ncp - 1), 0), 0, 0)
    const2 = lambda c: (0, 0)
    return pl.pallas_call(
        functools.partial(_gla_kernel, ncp),
        grid=(m // q,),
        in_specs=[pl.BlockSpec((q, kw), lambda c: (c, C_GQ // kw)),
                  pl.BlockSpec((q, kw), lambda c: (c, C_GK // kw)),
                  pl.BlockSpec((q, GLA_WIDTH), lambda c: (c, C_GV // GLA_WIDTH)),
                  pl.BlockSpec((q, GLA_WIDTH), lambda c: (c, C_GG // GLA_WIDTH)),
                  pl.BlockSpec((q, LANE), lambda c: (c, C_SM // LANE)),
                  pl.BlockSpec((1, GLA_WIDTH, kw), smap),
                  pl.BlockSpec((LANE, kw), const2),
                  pl.BlockSpec((1, kw), const2),
                  pl.BlockSpec((1, GLA_DV), const2)],
        out_specs=[pl.BlockSpec((q, GLA_WIDTH), lambda c: (c, 0)),
                   pl.BlockSpec((1, GLA_WIDTH, kw), smap)],
        out_shape=[jax.ShapeDtypeStruct((m, GLA_WIDTH), F32),
                   jax.ShapeDtypeStruct(s_in.shape, F32)],
        scratch_shapes=[pltpu.VMEM((GLA_WIDTH, kw), F32)],
        compiler_params=_cparams(("arbitrary",), 24 << 20),
        name="gla",
    )(proj, proj, proj, proj, proj, s_in, wa_pad, ba, ng)


def _attn_prep_kernel(dq_ref, dk_ref, dv_ref, qg_ref, kg_ref, seg_ref,
                      qa_ref, qb_ref, kn_ref, kb_ref, vt_ref):
    seg = seg_ref[...]

    def qknorm(x, g):
        ms = jnp.dot(x * x, seg, precision=HI, preferred_element_type=F32) * (1.0 / DIFF_HD)
        return x * lax.rsqrt(ms + EPS) * g

    qn = qknorm(dq_ref[...], qg_ref[...]) * (DIFF_HD ** -0.5 * LOG2E)
    first = (lax.broadcasted_iota(jnp.int32, qn.shape, 1) // DIFF_HD) % 2 == 0
    qa_ref[...] = jnp.where(first, qn, 0.0).astype(BF16)
    qb_ref[...] = jnp.where(first, 0.0, qn).astype(BF16)
    kn = qknorm(dk_ref[...], kg_ref[...])
    kn_ref[...] = kn
    kb_ref[...] = kn.astype(BF16)
    vt_ref[...] = dv_ref[...].T.astype(BF16)


def _attn_prep(proj, qg, kg, segmat):
    m = proj.shape[0]
    tm = _pick(m, (512, 256, 128))
    w = DIFF_WIDTH
    row = lambda off: pl.BlockSpec((tm, w), lambda i: (i, off // w))
    const2 = lambda i: (0, 0)
    return pl.pallas_call(
        _attn_prep_kernel,
        grid=(m // tm,),
        in_specs=[row(C_DQ), row(C_DK), row(C_DV),
                  pl.BlockSpec((1, w), const2), pl.BlockSpec((1, w), const2),
                  pl.BlockSpec((w, w), const2)],
        out_specs=[pl.BlockSpec((tm, w), lambda i: (i, 0))] * 4 + [pl.BlockSpec((w, tm), lambda i: (0, i))],
        out_shape=[jax.ShapeDtypeStruct((m, w), BF16), jax.ShapeDtypeStruct((m, w), BF16),
                   jax.ShapeDtypeStruct((m, w), F32), jax.ShapeDtypeStruct((m, w), BF16),
                   jax.ShapeDtypeStruct((w, m), BF16)],
        compiler_params=_cparams(("parallel",), 32 << 20),
        name="attn_prep",
    )(proj, proj, proj, qg, kg, segmat)


def _lambda_full(lam_ref, lam_init):
    l = lam_ref[...]
    a = jnp.sum(l[0:1] * l[1:2], axis=-1, keepdims=True)
    b = jnp.sum(l[2:3] * l[3:4], axis=-1, keepdims=True)
    return jnp.exp(a) - jnp.exp(b) + lam_init


ATTN_PEEK = 64
ATTN_MAX_SUM = 2.0 ** 64
ATTN_MIN_SUM = 2.0 ** -60


def _fold8(x, op):
    r = x[0:SUBLANE]
    for i in range(1, x.shape[0] // SUBLANE):
        r = op(r, x[i * SUBLANE:(i + 1) * SUBLANE])
    return r


def _attn_prompt_kernel(t, kbk, lam_init, qi_ref, ki_ref, slope_ref, qa_ref, qb_ref, kb_ref, vt_ref, lam_ref,
                        og_ref, o_ref, m_scr, l_scr, acc_scr, b0_scr, bd_scr):
    h = pl.program_id(0)
    p = pl.program_id(1)
    qi = qi_ref[p]
    ki = ki_ref[p]
    slope = slope_ref[h]
    nb = t // kbk
    w = 2 * t
    nt = (((1,), (1,)), ((), ()))

    @pl.when(p == 0)
    def _():
        b0_scr[...] = slope * lax.broadcasted_iota(jnp.int32, (kbk, LANE), 0).astype(F32)
        kr = lax.broadcasted_iota(jnp.int32, (t, w), 0)
        lane = lax.broadcasted_iota(jnp.int32, (t, w), 1)
        qc = (lane // (2 * kbk)) * kbk + lane % kbk
        corr = slope * jnp.minimum(2 * (qc - kr), 0).astype(F32)
        bd_scr[...] = jnp.where(kr // CHUNK <= qc // CHUNK, corr, -jnp.inf)

    @pl.when(ki == 0)
    def _():
        m_scr[...] = jnp.full(m_scr.shape, -jnp.inf, F32)
        l_scr[...] = jnp.zeros(l_scr.shape, F32)
        acc_scr[...] = jnp.zeros(acc_scr.shape, F32)

    def tile(diag):
        shift = slope * ((ki - qi) * t).astype(F32)
        step = slope * float(kbk)
        qcat = jnp.concatenate([r[g * kbk:(g + 1) * kbk, :] for g in range(nb) for r in (qa_ref, qb_ref)],
                               axis=0)
        b0 = jnp.concatenate([b0_scr[...]] * (w // LANE), axis=1)

        def scores(rows, nrows, lo):
            s = lax.dot_general(kb_ref[rows, :], qcat[lo:, :], nt, preferred_element_type=F32) + b0[0:nrows, lo:]
            if diag:
                s = s + bd_scr[rows, lo:]
            return s

        m_old = m_scr[...]
        peek = scores(slice(0, ATTN_PEEK), ATTN_PEEK, 0)
        m_used = jnp.maximum(m_old, jnp.max(peek, axis=0, keepdims=True) + shift)
        if diag:
            lane = lax.broadcasted_iota(jnp.int32, (1, w), 1)
            qpos = (lane // (2 * kbk)) * kbk + lane % kbk
            ramp = jnp.maximum(qpos - (ATTN_PEEK - 1), 0).astype(F32)
        else:
            ramp = float(t - ATTN_PEEK)
        m_ref = m_used + slope * ramp
        lsum8 = pv = None
        for b in range(nb):
            rows = slice(b * kbk, (b + 1) * kbk)
            lo = 2 * b * kbk if diag else 0
            e = scores(rows, kbk, lo) - (m_ref[:, lo:] - shift - b * step)
            pt = jnp.exp2(e)
            fs = _fold8(pt, jnp.add)
            d = jnp.dot(vt_ref[:, rows], pt.astype(BF16), preferred_element_type=F32)
            if b == 0:
                lsum8, pv = fs, d
            elif lo == 0:
                lsum8, pv = lsum8 + fs, d + pv
            else:
                lsum8 = jnp.concatenate([lsum8[:, :lo], lsum8[:, lo:] + fs], axis=1)
                pv = jnp.concatenate([pv[:, :lo], d + pv[:, lo:]], axis=1)
        lsum = jnp.sum(lsum8, axis=0, keepdims=True)
        ok = jnp.logical_and(jnp.max(lsum) <= ATTN_MAX_SUM, jnp.min(lsum) >= ATTN_MIN_SUM)

        @pl.when(ok)
        def _():
            a_old = jnp.exp2(m_old - m_ref)
            l_scr[...] = a_old * l_scr[...] + lsum
            acc_scr[...] = a_old * acc_scr[...] + pv
            m_scr[...] = m_ref

        @pl.when(jnp.logical_not(ok))
        def _():
            def body(b, carry):
                rows = pl.ds(pl.multiple_of(b * kbk, kbk), kbk)
                s = scores(rows, kbk, 0) + (shift + slope * (b * kbk).astype(F32))
                m_o = m_scr[...]
                m_n = jnp.maximum(m_o, jnp.max(s, axis=0, keepdims=True))
                pt = jnp.exp2(s - m_n)
                a = jnp.exp2(m_o - m_n)
                l_scr[...] = a * l_scr[...] + jnp.sum(pt, axis=0, keepdims=True)
                acc_scr[...] = a * acc_scr[...] + jnp.dot(vt_ref[:, rows], pt.astype(BF16),
                                                          preferred_element_type=F32)
                m_scr[...] = m_n
                return carry

            lax.fori_loop(0, nb, body, 0)

    @pl.when(ki < qi)
    def _():
        tile(False)

    @pl.when(ki == qi)
    def _():
        tile(True)
        lam = _lambda_full(lam_ref, lam_init)
        on = acc_scr[...] / l_scr[...]
        ot = jnp.concatenate([on[:, 2 * g * kbk:(2 * g + 1) * kbk] - lam * on[:, (2 * g + 1) * kbk:(2 * g + 2) * kbk]
                              for g in range(nb)], axis=1)
        ms = jnp.mean(ot * ot, axis=0, keepdims=True)
        o_ref[...] = (ot * lax.rsqrt(ms + EPS)).T * (og_ref[...] * (1.0 - lam_init))


def _attn_prompt(lp, lam_init, qa, qb, kb, vt, slopes, lam_p, og):
    t = _pick(lp, (1024, 512, 256))
    kbk = 256
    nq = lp // t
    pairs = [(i, j) for i in range(nq) for j in range(i + 1)]
    qi_l = jnp.asarray(np.array([a for a, _ in pairs], np.int32))
    ki_l = jnp.asarray(np.array([b for _, b in pairs], np.int32))
    grid_spec = pltpu.PrefetchScalarGridSpec(
        num_scalar_prefetch=3,
        grid=(DIFF_HEADS, len(pairs)),
        in_specs=[pl.BlockSpec((t, LANE), lambda h, p, qi, ki, sl: (qi[p], h)),
                  pl.BlockSpec((t, LANE), lambda h, p, qi, ki, sl: (qi[p], h)),
                  pl.BlockSpec((t, LANE), lambda h, p, qi, ki, sl: (ki[p], h)),
                  pl.BlockSpec((LANE, t), lambda h, p, qi, ki, sl: (h, ki[p])),
                  pl.BlockSpec((4, DIFF_HD), lambda h, p, qi, ki, sl: (0, 0)),
                  pl.BlockSpec((1, DIFF_VD), lambda h, p, qi, ki, sl: (0, 0))],
        out_specs=pl.BlockSpec((t, LANE), lambda h, p, qi, ki, sl: (qi[p], h)),
        scratch_shapes=[pltpu.VMEM((1, 2 * t), F32), pltpu.VMEM((1, 2 * t), F32),
                        pltpu.VMEM((DIFF_VD, 2 * t), F32),
                        pltpu.VMEM((kbk, LANE), F32), pltpu.VMEM((t, 2 * t), F32)])
    return pl.pallas_call(
        functools.partial(_attn_prompt_kernel, t, kbk, lam_init),
        grid_spec=grid_spec,
        out_shape=jax.ShapeDtypeStruct((lp, DIFF_WIDTH), F32),
        compiler_params=_cparams(("arbitrary", "arbitrary"), 40 << 20),
        name="attn_prompt",
    )(qi_l, ki_l, slopes, qa, qb, kb, vt, lam_p, og)


def _attn_sample_kernel(past, lam_init, slope_ref, qa_ref, qb_ref, kn_ref, vn_ref, kc_ref, vc_ref, lam_ref, og_ref,
                        o_ref):
    q = CHUNK
    lam = _lambda_full(lam_ref, lam_init)
    kpos = lax.broadcasted_iota(jnp.int32, (1, past), 1).astype(F32)
    r = lax.broadcasted_iota(jnp.int32, (q, q), 0)
    cc = lax.broadcasted_iota(jnp.int32, (q, q), 1)
    new_bias = (past + r - jnp.abs(r - cc)).astype(F32)
    outs = []
    for h in range(DIFF_HEADS):
        slope = slope_ref[h]
        sl = slice(h * LANE, (h + 1) * LANE)
        kc = kc_ref[pl.ds(h, past, stride=DIFF_HEADS), :].astype(BF16)
        vc = vc_ref[pl.ds(h, past, stride=DIFF_HEADS), :].astype(BF16)
        kn = kn_ref[:, sl].astype(BF16)
        vn = vn_ref[:, sl].astype(BF16)
        res = []
        for q_ref in (qa_ref, qb_ref):
            qv = q_ref[:, sl]
            s_c = lax.dot_general(qv, kc, (((1,), (1,)), ((), ())), preferred_element_type=F32) + slope * kpos
            s_n = lax.dot_general(qv, kn, (((1,), (1,)), ((), ())), preferred_element_type=F32) + slope * new_bias
            mx = jnp.maximum(jnp.max(s_c, axis=-1, keepdims=True), jnp.max(s_n, axis=-1, keepdims=True))
            p_c = jnp.exp2(s_c - mx)
            p_n = jnp.exp2(s_n - mx)
            den = jnp.sum(p_c, axis=-1, keepdims=True) + jnp.sum(p_n, axis=-1, keepdims=True)
            num = (jnp.dot(p_c.astype(BF16), vc, preferred_element_type=F32)
                   + jnp.dot(p_n.astype(BF16), vn, preferred_element_type=F32))
            res.append(num / den)
        o = res[0] - lam * res[1]
        ms = jnp.mean(o * o, axis=-1, keepdims=True)
        outs.append(o * lax.rsqrt(ms + EPS) * (og_ref[...] * (1.0 - lam_init)))
    o_ref[...] = jnp.concatenate(outs, axis=1)


def _attn_sample(lp, nb, layer, lam_init, qa, qb, kn, proj, cache_k, cache_v, slopes, lam_p, og):
    depth, _, past = cache_k.shape[:3]
    q = CHUNK
    w = DIFF_WIDTH
    off = lp // q
    cache_k = cache_k.reshape(depth, nb, past * DIFF_HEADS, DIFF_VD)
    cache_v = cache_v.reshape(depth, nb, past * DIFF_HEADS, DIFF_VD)
    cache_block = (None, None, past * DIFF_HEADS, DIFF_VD)
    grid_spec = pltpu.PrefetchScalarGridSpec(
        num_scalar_prefetch=1,
        grid=(nb,),
        in_specs=[pl.BlockSpec((q, w), lambda b, sl: (off + b, 0)),
                  pl.BlockSpec((q, w), lambda b, sl: (off + b, 0)),
                  pl.BlockSpec((q, w), lambda b, sl: (off + b, 0)),
                  pl.BlockSpec((q, w), lambda b, sl: (off + b, C_DV // w)),
                  pl.BlockSpec(cache_block, lambda b, sl: (layer, b, 0, 0)),
                  pl.BlockSpec(cache_block, lambda b, sl: (layer, b, 0, 0)),
                  pl.BlockSpec((4, DIFF_HD), lambda b, sl: (0, 0)),
                  pl.BlockSpec((1, DIFF_VD), lambda b, sl: (0, 0))],
        out_specs=pl.BlockSpec((q, w), lambda b, sl: (b, 0)))
    return pl.pallas_call(
        functools.partial(_attn_sample_kernel, past, lam_init),
        grid_spec=grid_spec,
        out_shape=jax.ShapeDtypeStruct((nb * q, DIFF_WIDTH), F32),
        compiler_params=_cparams(("arbitrary",), 32 << 20),
        name="attn_sample",
    )(slopes, qa, qb, kn, proj, cache_k, cache_v, lam_p, og)


def _out_proj_kernel(npt, xp_ref, xs_ref, ys_ref, yap_ref, yas_ref, yg_ref, w_ref, o_ref):
    is_prompt = pl.program_id(0) < npt
    ya = jnp.where(is_prompt, yap_ref[...], yas_ref[...])
    acc = jnp.dot(ys_ref[...].astype(BF16), w_ref[0:SSD_WIDTH, :], preferred_element_type=F32)
    acc = acc + jnp.dot(ya.astype(BF16), w_ref[SSD_WIDTH:SSD_WIDTH + DIFF_WIDTH, :],
                        preferred_element_type=F32)
    acc = acc + jnp.dot(yg_ref[...].astype(BF16), w_ref[SSD_WIDTH + DIFF_WIDTH:, :],
                        preferred_element_type=F32)
    o_ref[...] = jnp.where(is_prompt, xp_ref[...], xs_ref[...]) + acc


def _out_proj(xp, xs, ys, yap, yas, yg, w):
    lp, ms = xp.shape[0], xs.shape[0]
    tm = _pick(math.gcd(lp, ms), (512, 256, 128, 64))
    npt = lp // tm
    pmap, smap = _split_maps(npt)
    vmem = 2 * (5 * tm * D_MODEL * 4) + D_MODEL * D_MODEL * 2 + (4 << 20)
    return pl.pallas_call(
        functools.partial(_out_proj_kernel, npt),
        grid=((lp + ms) // tm,),
        in_specs=[pl.BlockSpec((tm, D_MODEL), pmap),
                  pl.BlockSpec((tm, D_MODEL), smap),
                  pl.BlockSpec((tm, SSD_WIDTH), lambda i: (i, 0)),
                  pl.BlockSpec((tm, DIFF_WIDTH), pmap),
                  pl.BlockSpec((tm, DIFF_WIDTH), smap),
                  pl.BlockSpec((tm, GLA_WIDTH), lambda i: (i, 0)),
                  pl.BlockSpec((D_MODEL, D_MODEL), lambda i: (0, 0), pipeline_mode=pl.Buffered(1))],
        out_specs=pl.BlockSpec((tm, D_MODEL), lambda i: (i, 0)),
        out_shape=jax.ShapeDtypeStruct((lp + ms, D_MODEL), F32),
        compiler_params=_cparams(("parallel",), vmem),
        name="out_proj",
    )(xp, xs, ys, yap, yas, yg, w)


def _mlp_kernel(npt, x_ref, g_ref, w1_ref, w2_ref, op_ref, os_ref, h_ref):
    i = pl.program_id(0)
    first = pl.program_id(1) == 0

    @pl.when(first)
    def _():
        h_ref[...] = _rms_bf16(x_ref[...], g_ref[...])

    a = jnp.dot(h_ref[...], w1_ref[...], preferred_element_type=F32)
    a = jnp.square(jnp.maximum(a, 0.0)).astype(BF16)

    def accumulate(o_ref):
        @pl.when(first)
        def _():
            o_ref[...] = x_ref[...]

        o_ref[...] += jnp.dot(a, w2_ref[...], preferred_element_type=F32)

    @pl.when(i < npt)
    def _():
        accumulate(op_ref)

    @pl.when(i >= npt)
    def _():
        accumulate(os_ref)


def _mlp(x, lp, g, w1, w2):
    m = x.shape[0]
    ms = m - lp
    tm = _pick(math.gcd(lp, ms), (512, 256, 128, 64))
    tf = 1024
    npt = lp // tm
    pmap, smap = _split_maps(npt)
    vmem = 6 * tm * D_MODEL * 4 + tm * D_MODEL * 2 + 4 * D_MODEL * tf * 2 + 2 * tm * tf * 4 + (4 << 20)
    return pl.pallas_call(
        functools.partial(_mlp_kernel, npt),
        grid=(m // tm, D_FF // tf),
        in_specs=[pl.BlockSpec((tm, D_MODEL), lambda i, f: (i, 0)),
                  pl.BlockSpec((1, D_MODEL), lambda i, f: (0, 0)),
                  pl.BlockSpec((D_MODEL, tf), lambda i, f: (0, f)),
                  pl.BlockSpec((tf, D_MODEL), lambda i, f: (f, 0))],
        out_specs=[pl.BlockSpec((tm, D_MODEL), pmap), pl.BlockSpec((tm, D_MODEL), smap)],
        out_shape=[jax.ShapeDtypeStruct((lp, D_MODEL), F32), jax.ShapeDtypeStruct((ms, D_MODEL), F32)],
        scratch_shapes=[pltpu.VMEM((tm, D_MODEL), BF16)],
        compiler_params=_cparams(("arbitrary", "arbitrary"), vmem),
        name="mlp",
    )(x, g, w1, w2)


def _pack_w_in(w):
    return jnp.concatenate(
        [w[:, 0:2560], w[:, 2576:4112], w[:, 4624:5136], w[:, 5152:5664], w[:, 4112:4624],
         w[:, 2560:2576], w[:, 5136:5152], jnp.zeros((w.shape[0], PACKED - 5664), w.dtype)],
        axis=1).astype(BF16)


def _pad_lanes(v, width=LANE):
    v = v.reshape(1, -1)
    return jnp.pad(v, ((0, 0), (0, width - v.shape[1])))


def kernel(x_prompt, x_sample, cache_diff_k, cache_diff_v, state_ssd_conv, state_ssd, state_gla, norm1_g, w_in, ssd_conv_w, ssd_conv_b, ssd_dt_bias, ssd_a_log, ssd_d, ssd_norm_g, diff_qn_g, diff_kn_g, diff_lambda, diff_out_g, gla_wa2, gla_ba, gla_norm_g, w_out, norm2_g, w_mlp1, w_mlp2):
    bp, lp, d = x_prompt.shape
    nb, ls, _ = x_sample.shape
    depth = w_in.shape[0]
    past = cache_diff_k.shape[2]
    assert bp == 1 and d == D_MODEL and ls == CHUNK and lp % CHUNK == 0 and past % CHUNK == 0
    ncp = lp // CHUNK
    m = lp + nb * ls

    xp, xs = x_prompt.reshape(lp, d), x_sample.reshape(nb * ls, d)

    emat = (jnp.arange(LANE)[:, None] == (jnp.arange(SSD_WIDTH)[None, :] // SSD_HEAD_DIM)).astype(F32)
    segmat = ((jnp.arange(DIFF_WIDTH)[:, None] // DIFF_HD) == (jnp.arange(DIFF_WIDTH)[None, :] // DIFF_HD)).astype(F32)
    slopes = jnp.exp2(-8.0 * jnp.arange(1, DIFF_HEADS + 1, dtype=F32) / DIFF_HEADS) * LOG2E
    eye_h = jnp.eye(GLA_HEADS, dtype=F32)

    outs = {k: [] for k in ("kp", "vp", "cp", "hp", "sp", "ks", "vs", "cs", "hs", "ss")}
    for l in range(depth):
        lam_init = 0.8 - 0.6 * math.exp(-0.3 * l)
        proj = _in_proj(xp, xs, norm1_g[l].reshape(1, d), _pack_w_in(w_in[l]))

        st_all = jnp.concatenate([jnp.zeros((1,) + state_ssd.shape[2:], F32), state_ssd[l]], axis=0)
        st_in = st_all.reshape(nb + 1, SSD_WIDTH, SSD_STATE).transpose(0, 2, 1)
        cprev = jnp.concatenate([jnp.zeros((1, SSD_CONV - 1, SSD_CONV_DIM), F32), state_ssd_conv[l]], axis=0)
        cprev = jnp.pad(cprev, ((0, 0), (8 - (SSD_CONV - 1), 0), (0, 0)))
        cw, cb = ssd_conv_w[l], ssd_conv_b[l].reshape(1, -1)
        y_ssd, st_out = _ssd(
            proj, ncp, st_in, cprev[:, :, :SSD_WIDTH], cprev[:, :, SSD_WIDTH:],
            cw[:, :SSD_WIDTH], cb[:, :SSD_WIDTH], cw[:, SSD_WIDTH:], cb[:, SSD_WIDTH:],
            _pad_lanes(ssd_dt_bias[l]), _pad_lanes(ssd_a_log[l]),
            jnp.repeat(ssd_d[l], SSD_HEAD_DIM).reshape(1, -1), ssd_norm_g[l].reshape(1, -1), emat)
        h_all = st_out.transpose(0, 2, 1).reshape(nb + 1, SSD_HEADS, SSD_HEAD_DIM, SSD_STATE)
        outs["hp"].append(h_all[0:1])
        outs["hs"].append(h_all[1:])
        xbc_raw = proj[:, C_XS:C_XS + SSD_CONV_DIM]
        outs["cp"].append(xbc_raw[lp - (SSD_CONV - 1):lp][None])
        outs["cs"].append(xbc_raw[lp:].reshape(nb, ls, SSD_CONV_DIM)[:, ls - (SSD_CONV - 1):])

        s_all = jnp.concatenate([jnp.zeros((1,) + state_gla.shape[2:], F32), state_gla[l]], axis=0)
        s_in = jnp.einsum('bhkv,hg->bhvgk', s_all, eye_h).reshape(nb + 1, GLA_WIDTH, GLA_HEADS * GLA_DK)
        wa_pad = jnp.zeros((LANE, GLA_HEADS * GLA_DK), F32).at[GLA_RANK:2 * GLA_RANK].set(gla_wa2[l]).astype(BF16)
        y_gla, s_out = _gla(proj, ncp, s_in, wa_pad, gla_ba[l].reshape(1, -1), gla_norm_g[l].reshape(1, -1))
        s5 = s_out.reshape(nb + 1, GLA_HEADS, GLA_DV, GLA_HEADS, GLA_DK)
        s_new = jnp.stack([s5[:, h, :, h, :] for h in range(GLA_HEADS)], axis=1).transpose(0, 1, 3, 2)
        outs["sp"].append(s_new[0:1])
        outs["ss"].append(s_new[1:])

        qg = jnp.tile(diff_qn_g[l], 2 * DIFF_HEADS).reshape(1, -1)
        kg = jnp.tile(diff_kn_g[l], 2 * DIFF_HEADS).reshape(1, -1)
        qa, qb, kn, kb, vt = _attn_prep(proj, qg, kg, segmat)
        og = diff_out_g[l].reshape(1, -1)
        ya_p = _attn_prompt(lp, lam_init, qa, qb, kb, vt, slopes, diff_lambda[l], og)
        ya_s = _attn_sample(lp, nb, l, lam_init, qa, qb, kn, proj, cache_diff_k, cache_diff_v,
                            slopes, diff_lambda[l], og)
        v_rows = proj[:, C_DV:C_DV + DIFF_WIDTH]
        outs["kp"].append(kn[:lp].reshape(1, lp, DIFF_HEADS, 2 * DIFF_HD))
        outs["ks"].append(kn[lp:].reshape(nb, ls, DIFF_HEADS, 2 * DIFF_HD))
        outs["vp"].append(v_rows[:lp].reshape(1, lp, DIFF_HEADS, DIFF_VD))
        outs["vs"].append(v_rows[lp:].reshape(nb, ls, DIFF_HEADS, DIFF_VD))

        x1 = _out_proj(xp, xs, y_ssd, ya_p, ya_s, y_gla, w_out[l].astype(BF16))
        xp, xs = _mlp(x1, lp, norm2_g[l].reshape(1, d), w_mlp1[l].astype(BF16), w_mlp2[l].astype(BF16))

    st = {k: jnp.stack(v) for k, v in outs.items()}
    return (xp.reshape(1, lp, d), xs.reshape(nb, ls, d),
            st["kp"], st["vp"], st["cp"], st["hp"], st["sp"],
            st["ks"], st["vs"], st["cs"], st["hs"], st["ss"])
```

```python
import functools
import math

import numpy as np
import jax
import jax.numpy as jnp
from jax import lax
from jax.experimental import pallas as pl
from jax.experimental.pallas import tpu as pltpu

F32 = jnp.float32
BF16 = jnp.bfloat16

D_MODEL = 2048
CHUNK = 64
SSD_HEAD_DIM = 64
SSD_WIDTH = 1024
SSD_HEADS = 16
SSD_GROUPS = 2
SSD_STATE = 128
SSD_CONV = 4
SSD_CONV_DIM = SSD_WIDTH + 2 * SSD_GROUPS * SSD_STATE
DIFF_HD = 64
DIFF_VD = 128
DIFF_WIDTH = 512
DIFF_HEADS = 4
GLA_DK = 64
GLA_DV = 128
GLA_WIDTH = 512
GLA_HEADS = 4
GLA_RANK = 16
GLA_TAU = 16.0
GLA_BLOCK = 16
D_FF = 4 * D_MODEL
EPS = 1e-6

LOG2E = math.log2(math.e)
LANE = 128
SUBLANE = 8
VMEM_CAP = 56 << 20

C_Z, C_XS, C_BC, C_DQ, C_DK, C_DV, C_GV, C_GG, C_GQ, C_GK, C_SM = (
    0, 1024, 2048, 2560, 3072, 3584, 4096, 4608, 5120, 5376, 5632)
PACKED = 5760
PROJ_TN = 1920


def _pick(m, cands):
    for c in cands:
        if m % c == 0:
            return c
    raise ValueError(f"no tile for {m}")


def _cparams(sem, vmem_bytes):
    return pltpu.CompilerParams(dimension_semantics=sem,
                                vmem_limit_bytes=int(min(VMEM_CAP, max(vmem_bytes, 16 << 20))))


def _sigmoid(x):
    return 1.0 / (1.0 + jnp.exp(-x))


def _split3(x):
    hi = x.astype(BF16)
    r1 = x - hi.astype(F32)
    mid = r1.astype(BF16)
    lo = (r1 - mid.astype(F32)).astype(BF16)
    return hi, mid, lo


def _select_dot(x, sel):
    return sum(jnp.dot(p, sel, preferred_element_type=F32) for p in _split3(x))


def _dot_select(sel, x):
    return sum(jnp.dot(sel, p, preferred_element_type=F32) for p in _split3(x))


def _softplus(x):
    return jnp.maximum(x, 0.0) + jnp.log1p(jnp.exp(-jnp.abs(x)))


def _rms_bf16(x, g):
    ms = jnp.mean(x * x, axis=-1, keepdims=True)
    return (x * lax.rsqrt(ms + EPS) * g).astype(BF16)


def _in_proj_kernel(npt, xp_ref, xs_ref, g_ref, w_ref, o_ref, h_ref):
    i = pl.program_id(0)

    @pl.when(jnp.logical_and(pl.program_id(1) == 0, i < npt))
    def _():
        h_ref[...] = _rms_bf16(xp_ref[...], g_ref[...])

    @pl.when(jnp.logical_and(pl.program_id(1) == 0, i >= npt))
    def _():
        h_ref[...] = _rms_bf16(xs_ref[...], g_ref[...])

    o_ref[...] = jnp.dot(h_ref[...], w_ref[...], preferred_element_type=F32)


def _split_maps(npt):
    return (lambda i, *_: (jnp.minimum(i, npt - 1), 0)), (lambda i, *_: (jnp.maximum(i - npt, 0), 0))


def _in_proj(xp, xs, g, w):
    lp, ms = xp.shape[0], xs.shape[0]
    tm = _pick(math.gcd(lp, ms), (512, 256, 128, 64))
    tn = PROJ_TN
    npt = lp // tm
    pmap, smap = _split_maps(npt)
    vmem = 4 * tm * D_MODEL * 4 + tm * D_MODEL * 2 + 2 * D_MODEL * tn * 2 + 2 * tm * tn * 4 + (4 << 20)
    return pl.pallas_call(
        functools.partial(_in_proj_kernel, npt),
        grid=((lp + ms) // tm, PACKED // tn),
        in_specs=[pl.BlockSpec((tm, D_MODEL), pmap),
                  pl.BlockSpec((tm, D_MODEL), smap),
                  pl.BlockSpec((1, D_MODEL), lambda i, j: (0, 0)),
                  pl.BlockSpec((D_MODEL, tn), lambda i, j: (0, j))],
        out_specs=pl.BlockSpec((tm, tn), lambda i, j: (i, j)),
        out_shape=jax.ShapeDtypeStruct((lp + ms, PACKED), F32),
        scratch_shapes=[pltpu.VMEM((tm, D_MODEL), BF16)],
        compiler_params=_cparams(("parallel", "arbitrary"), vmem),
        name="in_proj",
    )(xp, xs, g, w)


def _ssd_kernel(ncp, z_ref, xs_ref, bc_ref, sm_ref, st_in_ref, cpx_ref, cpb_ref,
                cwx_ref, cbx_ref, cwb_ref, cbb_ref, dtb_ref, alog_ref, dexp_ref, ng_ref, e_ref,
                y_ref, st_out_ref, st_scr, fx_scr, fb_scr):
    c = pl.program_id(0)
    q = CHUNK

    @pl.when(jnp.logical_or(c == 0, c >= ncp))
    def _():
        st_scr[...] = st_in_ref[0]
        fx_scr[0:8, :] = cpx_ref[0]
        fb_scr[0:8, :] = cpb_ref[0]

    def conv(u_ref, f_scr, w_ref, b_ref):
        u = u_ref[...]
        f_scr[8:8 + q, :] = u
        y = b_ref[...] + f_scr[5:5 + q, :] * w_ref[0:1, :]
        for j in range(1, SSD_CONV):
            y = y + f_scr[5 + j:5 + j + q, :] * w_ref[j:j + 1, :]
        f_scr[0:8, :] = u[q - 8:q, :]
        return y * _sigmoid(y)

    xs = conv(xs_ref, fx_scr, cwx_ref, cbx_ref)
    bcv = conv(bc_ref, fb_scr, cwb_ref, cbb_ref)

    dt = _softplus(sm_ref[...] + dtb_ref[...])
    da = dt * (-jnp.exp(alog_ref[...]))
    r64 = lax.broadcasted_iota(jnp.int32, (q, q), 0)
    c64 = lax.broadcasted_iota(jnp.int32, (q, q), 1)
    tril = (c64 <= r64).astype(BF16)
    cs = _dot_select(tril, da)
    both = _select_dot(jnp.concatenate([dt, cs], axis=0), e_ref[...])
    dt_e = both[0:q]
    cs_e = both[q:2 * q]

    row = lax.broadcasted_iota(jnp.int32, (q, SSD_WIDTH), 0)
    sidx = jnp.bitwise_and(lax.broadcasted_iota(jnp.int32, (q, SSD_WIDTH), 1), q - 1)
    cs_row = jnp.sum(jnp.where(sidx == row, cs_e, 0.0), axis=0, keepdims=True)
    cs_last = cs_e[q - 1:q, :]
    lmat = jnp.exp(jnp.where(sidx <= row, cs_e - cs_row, -jnp.inf))

    xdt = xs * dt_e
    xdt_end = (xdt * jnp.exp(cs_last - cs_e)).astype(BF16)
    xdt_b = xdt.astype(BF16)
    ecs = jnp.exp(cs_e)
    st = st_scr[...]
    st_b = st.astype(BF16)
    bcb = bcv.astype(BF16)

    r128 = lax.broadcasted_iota(jnp.int32, (2 * q, LANE), 0)
    c128 = lax.broadcasted_iota(jnp.int32, (2 * q, LANE), 1)
    bd2 = (r128 // q) == (c128 // q)

    hg = SSD_HEADS // SSD_GROUPS
    gw = hg * SSD_HEAD_DIM
    y_parts = []
    new_states = []
    for g in range(SSD_GROUPS):
        bm = bcb[:, g * SSD_STATE:(g + 1) * SSD_STATE]
        cm = bcb[:, (SSD_GROUPS + g) * SSD_STATE:(SSD_GROUPS + g + 1) * SSD_STATE]
        bm_rep = jnp.concatenate([bm] * hg, axis=0)
        cbt = lax.dot_general(cm, bm_rep, (((1,), (1,)), ((), ())),
                              preferred_element_type=F32)
        m_all = (cbt * lmat[:, g * gw:(g + 1) * gw]).astype(BF16)
        y_off = jnp.dot(cm, st_b[:, g * gw:(g + 1) * gw], preferred_element_type=F32)
        for j in range(gw // LANE):
            col = g * gw + j * LANE
            xj = xdt_b[:, col:col + LANE]
            xd = jnp.where(bd2, jnp.concatenate([xj, xj], axis=0), jnp.zeros((), BF16))
            y_parts.append(jnp.dot(m_all[:, j * LANE:(j + 1) * LANE], xd, preferred_element_type=F32)
                           + y_off[:, j * LANE:(j + 1) * LANE] * ecs[:, col:col + LANE])
        new_states.append(lax.dot_general(bm, xdt_end[:, g * gw:(g + 1) * gw], (((0,), (0,)), ((), ())),
                                          preferred_element_type=F32))
    y = jnp.concatenate(y_parts, axis=1)
    st_new = st * jnp.exp(cs_last) + jnp.concatenate(new_states, axis=1)
    st_scr[...] = st_new
    st_out_ref[0] = st_new

    y = y + dexp_ref[...] * xs
    zv = z_ref[...]
    y = y * (zv * _sigmoid(zv))
    outs = []
    for g in range(SSD_GROUPS):
        yg = y[:, g * gw:(g + 1) * gw]
        ms = jnp.mean(yg * yg, axis=-1, keepdims=True)
        outs.append(yg * lax.rsqrt(ms + EPS))
    y_ref[...] = jnp.concatenate(outs, axis=1) * ng_ref[...]


def _ssd(proj, ncp, st_in, cpx, cpb, cwx, cbx, cwb, cbb, dtb, alog, dexp, ng, emat):
    m = proj.shape[0]
    nchunks = m // CHUNK
    q = CHUNK
    smap = lambda c: (jnp.maximum(c - (ncp - 1), 0), 0, 0)
    const2 = lambda c: (0, 0)
    return pl.pallas_call(
        functools.partial(_ssd_kernel, ncp),
        grid=(nchunks,),
        in_specs=[pl.BlockSpec((q, 1024), lambda c: (c, C_Z // 1024)),
                  pl.BlockSpec((q, 1024), lambda c: (c, C_XS // 1024)),
                  pl.BlockSpec((q, 512), lambda c: (c, C_BC // 512)),
                  pl.BlockSpec((q, LANE), lambda c: (c, C_SM // LANE)),
                  pl.BlockSpec((1, SSD_STATE, SSD_WIDTH), smap),
                  pl.BlockSpec((1, 8, 1024), smap),
                  pl.BlockSpec((1, 8, 512), smap),
                  pl.BlockSpec((SSD_CONV, 1024), const2),
                  pl.BlockSpec((1, 1024), const2),
                  pl.BlockSpec((SSD_CONV, 512), const2),
                  pl.BlockSpec((1, 512), const2),
                  pl.BlockSpec((1, LANE), const2),
                  pl.BlockSpec((1, LANE), const2),
                  pl.BlockSpec((1, SSD_WIDTH), const2),
                  pl.BlockSpec((1, SSD_WIDTH), const2),
                  pl.BlockSpec((LANE, SSD_WIDTH), const2)],
        out_specs=[pl.BlockSpec((q, SSD_WIDTH), lambda c: (c, 0)),
                   pl.BlockSpec((1, SSD_STATE, SSD_WIDTH), smap)],
        out_shape=[jax.ShapeDtypeStruct((m, SSD_WIDTH), F32),
                   jax.ShapeDtypeStruct(st_in.shape, F32)],
        scratch_shapes=[pltpu.VMEM((SSD_STATE, SSD_WIDTH), F32),
                        pltpu.VMEM((8 + q, 1024), F32),
                        pltpu.VMEM((8 + q, 512), F32)],
        compiler_params=_cparams(("arbitrary",), 32 << 20),
        name="ssd",
    )(proj, proj, proj, proj, st_in, cpx, cpb, cwx, cbx, cwb, cbb, dtb, alog, dexp, ng, emat)


def _gla_kernel(ncp, gq_ref, gk_ref, gv_ref, gg_ref, sm_ref, s_in_ref, wa_ref, ba_ref, ng_ref,
                o_ref, s_out_ref, s_scr):
    c = pl.program_id(0)
    q = CHUNK
    blk = GLA_BLOCK
    kw = GLA_HEADS * GLA_DK

    @pl.when(jnp.logical_or(c == 0, c >= ncp))
    def _():
        s_scr[...] = s_in_ref[0]

    pre = jnp.dot(sm_ref[...].astype(BF16), wa_ref[...], preferred_element_type=F32) + ba_ref[...]
    log_a = -_softplus(-pre) * (1.0 / GLA_TAU)
    r64 = lax.broadcasted_iota(jnp.int32, (q, q), 0)
    c64 = lax.broadcasted_iota(jnp.int32, (q, q), 1)
    same_blk = (r64 // blk) == (c64 // blk)
    causal = jnp.logical_and(same_blk, c64 <= r64)
    b = _dot_select(causal.astype(BF16), log_a)
    gk = gk_ref[...]
    qt = gq_ref[...] * (GLA_DK ** -0.5) * jnp.exp(b)
    kt = (gk * jnp.exp(-b)).astype(BF16)
    b_last = jnp.concatenate(
        [jnp.broadcast_to(b[(j + 1) * blk - 1:(j + 1) * blk, :], (blk, kw)) for j in range(q // blk)], axis=0)
    ktil = (gk * jnp.exp(b_last - b)).astype(BF16)
    qt_b = qt.astype(BF16)
    gv_b = gv_ref[...].astype(BF16)

    lane_head = lax.broadcasted_iota(jnp.int32, (q, kw), 1) // GLA_DK
    o_intra = []
    for h in range(GLA_HEADS):
        qm = jnp.where(lane_head == h, qt_b, jnp.zeros((), BF16))
        att = lax.dot_general(qm, kt, (((1,), (1,)), ((), ())), preferred_element_type=F32)
        att = jnp.where(causal, att, 0.0).astype(BF16)
        o_intra.append(jnp.dot(att, gv_b[:, h * GLA_DV:(h + 1) * GLA_DV], preferred_element_type=F32))
    o_intra = jnp.concatenate(o_intra, axis=1)

    rs = lax.broadcasted_iota(jnp.int32, (GLA_WIDTH, kw), 0) // GLA_DV
    cs_ = lax.broadcasted_iota(jnp.int32, (GLA_WIDTH, kw), 1) // GLA_DK
    bd = rs == cs_
    sw = s_scr[...]
    o_inter = []
    for j in range(q // blk):
        sl = slice(j * blk, (j + 1) * blk)
        o_inter.append(lax.dot_general(qt_b[sl], sw.astype(BF16), (((1,), (1,)), ((), ())),
                                       preferred_element_type=F32))
        upd = lax.dot_general(gv_b[sl], ktil[sl], (((0,), (0,)), ((), ())),
                              preferred_element_type=F32)
        decay = jnp.exp(b[(j + 1) * blk - 1:(j + 1) * blk, :])
        sw = sw * decay + jnp.where(bd, upd, 0.0)
    s_scr[...] = sw
    s_out_ref[0] = sw
    o = o_intra + jnp.concatenate(o_inter, axis=0)

    gg = gg_ref[...]
    gate = gg * _sigmoid(gg)
    outs = []
    for h in range(GLA_HEADS):
        oh = o[:, h * GLA_DV:(h + 1) * GLA_DV]
        ms = jnp.mean(oh * oh, axis=-1, keepdims=True)
        outs.append(oh * lax.rsqrt(ms + EPS) * ng_ref[...])
    o_ref[...] = jnp.concatenate(outs, axis=1) * gate


def _gla(proj, ncp, s_in, wa_pad, ba, ng):
    m = proj.shape[0]
    q = CHUNK
    kw = GLA_HEADS * GLA_DK
    smap = lambda c: (jnp.maximum(c - (ncp - 1), 0), 0, 0)
    const2 = lambda c: (0, 0)
    return pl.pallas_call(
        functools.partial(_gla_kernel, ncp),
        grid=(m // q,),
        in_specs=[pl.BlockSpec((q, kw), lambda c: (c, C_GQ // kw)),
                  pl.BlockSpec((q, kw), lambda c: (c, C_GK // kw)),
                  pl.BlockSpec((q, GLA_WIDTH), lambda c: (c, C_GV // GLA_WIDTH)),
                  pl.BlockSpec((q, GLA_WIDTH), lambda c: (c, C_GG // GLA_WIDTH)),
                  pl.BlockSpec((q, LANE), lambda c: (c, C_SM // LANE)),
                  pl.BlockSpec((1, GLA_WIDTH, kw), smap),
                  pl.BlockSpec((LANE, kw), const2),
                  pl.BlockSpec((1, kw), const2),
                  pl.BlockSpec((1, GLA_DV), const2)],
        out_specs=[pl.BlockSpec((q, GLA_WIDTH), lambda c: (c, 0)),
                   pl.BlockSpec((1, GLA_WIDTH, kw), smap)],
        out_shape=[jax.ShapeDtypeStruct((m, GLA_WIDTH), F32),
                   jax.ShapeDtypeStruct(s_in.shape, F32)],
        scratch_shapes=[pltpu.VMEM((GLA_WIDTH, kw), F32)],
        compiler_params=_cparams(("arbitrary",), 24 << 20),
        name="gla",
    )(proj, proj, proj, proj, proj, s_in, wa_pad, ba, ng)


def _attn_prep_kernel(npt, dq_ref, dk_ref, dv_ref, qg_ref, kg_ref, seg_ref,
                      qa_ref, qb_ref, kn_ref, kb_ref, vt_ref, kp_ref, vp_ref, ks_ref, vs_ref):
    seg = seg_ref[...]
    tm = dq_ref.shape[0]

    def qknorm(x, g):
        ms = _select_dot(x * x, seg) * (1.0 / DIFF_HD)
        return x * lax.rsqrt(ms + EPS) * g

    qn = qknorm(dq_ref[...], qg_ref[...]) * (DIFF_HD ** -0.5 * LOG2E)
    first = (lax.broadcasted_iota(jnp.int32, qn.shape, 1) // DIFF_HD) % 2 == 0
    qa_ref[...] = jnp.where(first, qn, 0.0).astype(BF16)
    qb_ref[...] = jnp.where(first, 0.0, qn).astype(BF16)
    kn = qknorm(dk_ref[...], kg_ref[...])
    kn_ref[...] = kn
    kb_ref[...] = kn.astype(BF16)
    dv = dv_ref[...]
    vt_ref[...] = dv.T.astype(BF16)

    def cache_rows(k_ref, v_ref):
        for h in range(DIFF_HEADS):
            k_ref[pl.ds(h, tm, stride=DIFF_HEADS), :] = kn[:, h * LANE:(h + 1) * LANE]
            v_ref[pl.ds(h, tm, stride=DIFF_HEADS), :] = dv[:, h * LANE:(h + 1) * LANE]

    @pl.when(pl.program_id(0) < npt)
    def _():
        cache_rows(kp_ref, vp_ref)

    @pl.when(pl.program_id(0) >= npt)
    def _():
        cache_rows(ks_ref, vs_ref)


def _attn_prep(proj, lp, qg, kg, segmat):
    m = proj.shape[0]
    ms = m - lp
    tm = _pick(math.gcd(lp, ms), (512, 256, 128))
    w = DIFF_WIDTH
    npt = lp // tm
    pmap, smap = _split_maps(npt)
    row = lambda off: pl.BlockSpec((tm, w), lambda i: (i, off // w))
    const2 = lambda i: (0, 0)
    rows4 = tm * DIFF_HEADS
    return pl.pallas_call(
        functools.partial(_attn_prep_kernel, npt),
        grid=(m // tm,),
        in_specs=[row(C_DQ), row(C_DK), row(C_DV),
                  pl.BlockSpec((1, w), const2), pl.BlockSpec((1, w), const2),
                  pl.BlockSpec((w, w), const2)],
        out_specs=[pl.BlockSpec((tm, w), lambda i: (i, 0))] * 4 + [pl.BlockSpec((w, tm), lambda i: (0, i))]
        + [pl.BlockSpec((rows4, LANE), pmap)] * 2 + [pl.BlockSpec((rows4, LANE), smap)] * 2,
        out_shape=[jax.ShapeDtypeStruct((m, w), BF16), jax.ShapeDtypeStruct((m, w), BF16),
                   jax.ShapeDtypeStruct((m, w), F32), jax.ShapeDtypeStruct((m, w), BF16),
                   jax.ShapeDtypeStruct((w, m), BF16)]
        + [jax.ShapeDtypeStruct((lp * DIFF_HEADS, LANE), F32)] * 2
        + [jax.ShapeDtypeStruct((ms * DIFF_HEADS, LANE), F32)] * 2,
        compiler_params=_cparams(("arbitrary",), 40 << 20),
        name="attn_prep",
    )(proj, proj, proj, qg, kg, segmat)


def _lambda_full(lam_ref, lam_init):
    l = lam_ref[...]
    a = jnp.sum(l[0:1] * l[1:2], axis=-1, keepdims=True)
    b = jnp.sum(l[2:3] * l[3:4], axis=-1, keepdims=True)
    return jnp.exp(a) - jnp.exp(b) + lam_init


ATTN_PEEK = 64
ATTN_MAX_SUM = 2.0 ** 64
ATTN_MIN_SUM = 2.0 ** -60


def _fold8(x, op):
    r = x[0:SUBLANE]
    for i in range(1, x.shape[0] // SUBLANE):
        r = op(r, x[i * SUBLANE:(i + 1) * SUBLANE])
    return r


def _attn_prompt_kernel(t, kbk, lam_init, qi_ref, ki_ref, slope_ref, qa_ref, qb_ref, kb_ref, vt_ref, lam_ref,
                        og_ref, o_ref, m_scr, l_scr, acc_scr, b0_scr, bd_scr):
    h = pl.program_id(0)
    p = pl.program_id(1)
    qi = qi_ref[p]
    ki = ki_ref[p]
    slope = slope_ref[h]
    nb = t // kbk
    w = 2 * t
    nt = (((1,), (1,)), ((), ()))

    @pl.when(p == 0)
    def _():
        b0_scr[...] = slope * lax.broadcasted_iota(jnp.int32, (kbk, LANE), 0).astype(F32)
        kr = lax.broadcasted_iota(jnp.int32, (t, w), 0)
        lane = lax.broadcasted_iota(jnp.int32, (t, w), 1)
        qc = (lane // (2 * kbk)) * kbk + lane % kbk
        corr = slope * jnp.minimum(2 * (qc - kr), 0).astype(F32)
        bd_scr[...] = jnp.where(kr // CHUNK <= qc // CHUNK, corr, -jnp.inf)

    @pl.when(ki == 0)
    def _():
        m_scr[...] = jnp.full(m_scr.shape, -jnp.inf, F32)
        l_scr[...] = jnp.zeros(l_scr.shape, F32)
        acc_scr[...] = jnp.zeros(acc_scr.shape, F32)

    def tile(diag):
        shift = slope * ((ki - qi) * t).astype(F32)
        step = slope * float(kbk)
        qcat = jnp.concatenate([r[g * kbk:(g + 1) * kbk, :] for g in range(nb) for r in (qa_ref, qb_ref)],
                               axis=0)
        b0 = jnp.concatenate([b0_scr[...]] * (w // LANE), axis=1)

        def scores(rows, nrows, lo):
            s = lax.dot_general(kb_ref[rows, :], qcat[lo:, :], nt, preferred_element_type=F32) + b0[0:nrows, lo:]
            if diag:
                s = s + bd_scr[rows, lo:]
            return s

        m_old = m_scr[...]
        peek = scores(slice(0, ATTN_PEEK), ATTN_PEEK, 0)
        m_used = jnp.maximum(m_old, jnp.max(peek, axis=0, keepdims=True) + shift)
        if diag:
            lane = lax.broadcasted_iota(jnp.int32, (1, w), 1)
            qpos = (lane // (2 * kbk)) * kbk + lane % kbk
            ramp = jnp.maximum(qpos - (ATTN_PEEK - 1), 0).astype(F32)
        else:
            ramp = float(t - ATTN_PEEK)
        m_ref = m_used + slope * ramp
        lsum8 = pv = None
        for b in range(nb):
            rows = slice(b * kbk, (b + 1) * kbk)
            lo = 2 * b * kbk if diag else 0
            e = scores(rows, kbk, lo) - (m_ref[:, lo:] - shift - b * step)
            pt = jnp.exp2(e)
            fs = _fold8(pt, jnp.add)
            d = jnp.dot(vt_ref[:, rows], pt.astype(BF16), preferred_element_type=F32)
            if b == 0:
                lsum8, pv = fs, d
            elif lo == 0:
                lsum8, pv = lsum8 + fs, d + pv
            else:
                lsum8 = jnp.concatenate([lsum8[:, :lo], lsum8[:, lo:] + fs], axis=1)
                pv = jnp.concatenate([pv[:, :lo], d + pv[:, lo:]], axis=1)
        lsum = jnp.sum(lsum8, axis=0, keepdims=True)
        ok = jnp.logical_and(jnp.max(lsum) <= ATTN_MAX_SUM, jnp.min(lsum) >= ATTN_MIN_SUM)

        @pl.when(ok)
        def _():
            a_old = jnp.exp2(m_old - m_ref)
            l_scr[...] = a_old * l_scr[...] + lsum
            acc_scr[...] = a_old * acc_scr[...] + pv
            m_scr[...] = m_ref

        @pl.when(jnp.logical_not(ok))
        def _():
            def body(b, carry):
                rows = pl.ds(pl.multiple_of(b * kbk, kbk), kbk)
                s = scores(rows, kbk, 0) + (shift + slope * (b * kbk).astype(F32))
                m_o = m_scr[...]
                m_n = jnp.maximum(m_o, jnp.max(s, axis=0, keepdims=True))
                pt = jnp.exp2(s - m_n)
                a = jnp.exp2(m_o - m_n)
                l_scr[...] = a * l_scr[...] + jnp.sum(pt, axis=0, keepdims=True)
                acc_scr[...] = a * acc_scr[...] + jnp.dot(vt_ref[:, rows], pt.astype(BF16),
                                                          preferred_element_type=F32)
                m_scr[...] = m_n
                return carry

            lax.fori_loop(0, nb, body, 0)

    @pl.when(ki < qi)
    def _():
        tile(False)

    @pl.when(ki == qi)
    def _():
        tile(True)
        lam = _lambda_full(lam_ref, lam_init)
        on = acc_scr[...] / l_scr[...]
        ot = jnp.concatenate([on[:, 2 * g * kbk:(2 * g + 1) * kbk] - lam * on[:, (2 * g + 1) * kbk:(2 * g + 2) * kbk]
                              for g in range(nb)], axis=1)
        ms = jnp.mean(ot * ot, axis=0, keepdims=True)
        o_ref[...] = (ot * lax.rsqrt(ms + EPS)).T * (og_ref[...] * (1.0 - lam_init))


def _attn_prompt(lp, lam_init, qa, qb, kb, vt, slopes, lam_p, og):
    t = _pick(lp, (1024, 512, 256))
    kbk = 256
    nq = lp // t
    pairs = [(i, j) for i in range(nq) for j in range(i + 1)]
    qi_l = jnp.asarray(np.array([a for a, _ in pairs], np.int32))
    ki_l = jnp.asarray(np.array([b for _, b in pairs], np.int32))
    grid_spec = pltpu.PrefetchScalarGridSpec(
        num_scalar_prefetch=3,
        grid=(DIFF_HEADS, len(pairs)),
        in_specs=[pl.BlockSpec((t, LANE), lambda h, p, qi, ki, sl: (qi[p], h)),
                  pl.BlockSpec((t, LANE), lambda h, p, qi, ki, sl: (qi[p], h)),
                  pl.BlockSpec((t, LANE), lambda h, p, qi, ki, sl: (ki[p], h)),
                  pl.BlockSpec((LANE, t), lambda h, p, qi, ki, sl: (h, ki[p])),
                  pl.BlockSpec((4, DIFF_HD), lambda h, p, qi, ki, sl: (0, 0)),
                  pl.BlockSpec((1, DIFF_VD), lambda h, p, qi, ki, sl: (0, 0))],
        out_specs=pl.BlockSpec((t, LANE), lambda h, p, qi, ki, sl: (qi[p], h)),
        scratch_shapes=[pltpu.VMEM((1, 2 * t), F32), pltpu.VMEM((1, 2 * t), F32),
                        pltpu.VMEM((DIFF_VD, 2 * t), F32),
                        pltpu.VMEM((kbk, LANE), F32), pltpu.VMEM((t, 2 * t), F32)])
    return pl.pallas_call(
        functools.partial(_attn_prompt_kernel, t, kbk, lam_init),
        grid_spec=grid_spec,
        out_shape=jax.ShapeDtypeStruct((lp, DIFF_WIDTH), F32),
        compiler_params=_cparams(("arbitrary", "arbitrary"), 40 << 20),
        name="attn_prompt",
    )(qi_l, ki_l, slopes, qa, qb, kb, vt, lam_p, og)


def _attn_sample_kernel(past, lam_init, slope_ref, qa_ref, qb_ref, kn_ref, vn_ref, kc_ref, vc_ref, lam_ref, og_ref,
                        o_ref):
    q = CHUNK
    lam = _lambda_full(lam_ref, lam_init)
    kpos = lax.broadcasted_iota(jnp.int32, (1, past), 1).astype(F32)
    r = lax.broadcasted_iota(jnp.int32, (q, q), 0)
    cc = lax.broadcasted_iota(jnp.int32, (q, q), 1)
    new_bias = (past + r - jnp.abs(r - cc)).astype(F32)
    outs = []
    for h in range(DIFF_HEADS):
        slope = slope_ref[h]
        sl = slice(h * LANE, (h + 1) * LANE)
        kc = kc_ref[pl.ds(h, past, stride=DIFF_HEADS), :].astype(BF16)
        vc = vc_ref[pl.ds(h, past, stride=DIFF_HEADS), :].astype(BF16)
        kn = kn_ref[:, sl].astype(BF16)
        vn = vn_ref[:, sl].astype(BF16)
        res = []
        for q_ref in (qa_ref, qb_ref):
            qv = q_ref[:, sl]
            s_c = lax.dot_general(qv, kc, (((1,), (1,)), ((), ())), preferred_element_type=F32) + slope * kpos
            s_n = lax.dot_general(qv, kn, (((1,), (1,)), ((), ())), preferred_element_type=F32) + slope * new_bias
            mx = jnp.maximum(jnp.max(s_c, axis=-1, keepdims=True), jnp.max(s_n, axis=-1, keepdims=True))
            p_c = jnp.exp2(s_c - mx)
            p_n = jnp.exp2(s_n - mx)
            den = jnp.sum(p_c, axis=-1, keepdims=True) + jnp.sum(p_n, axis=-1, keepdims=True)
            num = (jnp.dot(p_c.astype(BF16), vc, preferred_element_type=F32)
                   + jnp.dot(p_n.astype(BF16), vn, preferred_element_type=F32))
            res.append(num / den)
        o = res[0] - lam * res[1]
        ms = jnp.mean(o * o, axis=-1, keepdims=True)
        outs.append(o * lax.rsqrt(ms + EPS) * (og_ref[...] * (1.0 - lam_init)))
    o_ref[...] = jnp.concatenate(outs, axis=1)


def _attn_sample(lp, nb, layer, lam_init, qa, qb, kn, proj, cache_k, cache_v, slopes, lam_p, og):
    depth, _, past = cache_k.shape[:3]
    q = CHUNK
    w = DIFF_WIDTH
    off = lp // q
    cache_k = cache_k.reshape(depth, nb, past * DIFF_HEADS, DIFF_VD)
    cache_v = cache_v.reshape(depth, nb, past * DIFF_HEADS, DIFF_VD)
    cache_block = (None, None, past * DIFF_HEADS, DIFF_VD)
    grid_spec = pltpu.PrefetchScalarGridSpec(
        num_scalar_prefetch=1,
        grid=(nb,),
        in_specs=[pl.BlockSpec((q, w), lambda b, sl: (off + b, 0)),
                  pl.BlockSpec((q, w), lambda b, sl: (off + b, 0)),
                  pl.BlockSpec((q, w), lambda b, sl: (off + b, 0)),
                  pl.BlockSpec((q, w), lambda b, sl: (off + b, C_DV // w)),
                  pl.BlockSpec(cache_block, lambda b, sl: (layer, b, 0, 0)),
                  pl.BlockSpec(cache_block, lambda b, sl: (layer, b, 0, 0)),
                  pl.BlockSpec((4, DIFF_HD), lambda b, sl: (0, 0)),
                  pl.BlockSpec((1, DIFF_VD), lambda b, sl: (0, 0))],
        out_specs=pl.BlockSpec((q, w), lambda b, sl: (b, 0)))
    return pl.pallas_call(
        functools.partial(_attn_sample_kernel, past, lam_init),
        grid_spec=grid_spec,
        out_shape=jax.ShapeDtypeStruct((nb * q, DIFF_WIDTH), F32),
        compiler_params=_cparams(("arbitrary",), 32 << 20),
        name="attn_sample",
    )(slopes, qa, qb, kn, proj, cache_k, cache_v, lam_p, og)


def _out_proj_kernel(npt, xp_ref, xs_ref, ys_ref, yap_ref, yas_ref, yg_ref, w_ref, o_ref):
    is_prompt = pl.program_id(0) < npt
    ya = jnp.where(is_prompt, yap_ref[...], yas_ref[...])
    acc = jnp.dot(ys_ref[...].astype(BF16), w_ref[0:SSD_WIDTH, :], preferred_element_type=F32)
    acc = acc + jnp.dot(ya.astype(BF16), w_ref[SSD_WIDTH:SSD_WIDTH + DIFF_WIDTH, :],
                        preferred_element_type=F32)
    acc = acc + jnp.dot(yg_ref[...].astype(BF16), w_ref[SSD_WIDTH + DIFF_WIDTH:, :],
                        preferred_element_type=F32)
    o_ref[...] = jnp.where(is_prompt, xp_ref[...], xs_ref[...]) + acc


def _out_proj(xp, xs, ys, yap, yas, yg, w):
    lp, ms = xp.shape[0], xs.shape[0]
    tm = _pick(math.gcd(lp, ms), (512, 256, 128, 64))
    npt = lp // tm
    pmap, smap = _split_maps(npt)
    vmem = 2 * (5 * tm * D_MODEL * 4) + D_MODEL * D_MODEL * 2 + (4 << 20)
    return pl.pallas_call(
        functools.partial(_out_proj_kernel, npt),
        grid=((lp + ms) // tm,),
        in_specs=[pl.BlockSpec((tm, D_MODEL), pmap),
                  pl.BlockSpec((tm, D_MODEL), smap),
                  pl.BlockSpec((tm, SSD_WIDTH), lambda i: (i, 0)),
                  pl.BlockSpec((tm, DIFF_WIDTH), pmap),
                  pl.BlockSpec((tm, DIFF_WIDTH), smap),
                  pl.BlockSpec((tm, GLA_WIDTH), lambda i: (i, 0)),
                  pl.BlockSpec((D_MODEL, D_MODEL), lambda i: (0, 0), pipeline_mode=pl.Buffered(1))],
        out_specs=pl.BlockSpec((tm, D_MODEL), lambda i: (i, 0)),
        out_shape=jax.ShapeDtypeStruct((lp + ms, D_MODEL), F32),
        compiler_params=_cparams(("parallel",), vmem),
        name="out_proj",
    )(xp, xs, ys, yap, yas, yg, w)


def _mlp_kernel(npt, x_ref, g_ref, w1_ref, w2_ref, op_ref, os_ref, h_ref):
    i = pl.program_id(0)
    first = pl.program_id(1) == 0

    @pl.when(first)
    def _():
        h_ref[...] = _rms_bf16(x_ref[...], g_ref[...])

    a = jnp.dot(h_ref[...], w1_ref[...], preferred_element_type=F32)
    a = jnp.square(jnp.maximum(a, 0.0)).astype(BF16)

    def accumulate(o_ref):
        @pl.when(first)
        def _():
            o_ref[...] = x_ref[...]

        o_ref[...] += jnp.dot(a, w2_ref[...], preferred_element_type=F32)

    @pl.when(i < npt)
    def _():
        accumulate(op_ref)

    @pl.when(i >= npt)
    def _():
        accumulate(os_ref)


def _mlp(x, lp, g, w1, w2):
    m = x.shape[0]
    ms = m - lp
    tm = _pick(math.gcd(lp, ms), (512, 256, 128, 64))
    tf = 1024
    npt = lp // tm
    pmap, smap = _split_maps(npt)
    vmem = 6 * tm * D_MODEL * 4 + tm * D_MODEL * 2 + 4 * D_MODEL * tf * 2 + 2 * tm * tf * 4 + (4 << 20)
    return pl.pallas_call(
        functools.partial(_mlp_kernel, npt),
        grid=(m // tm, D_FF // tf),
        in_specs=[pl.BlockSpec((tm, D_MODEL), lambda i, f: (i, 0)),
                  pl.BlockSpec((1, D_MODEL), lambda i, f: (0, 0)),
                  pl.BlockSpec((D_MODEL, tf), lambda i, f: (0, f)),
                  pl.BlockSpec((tf, D_MODEL), lambda i, f: (f, 0))],
        out_specs=[pl.BlockSpec((tm, D_MODEL), pmap), pl.BlockSpec((tm, D_MODEL), smap)],
        out_shape=[jax.ShapeDtypeStruct((lp, D_MODEL), F32), jax.ShapeDtypeStruct((ms, D_MODEL), F32)],
        scratch_shapes=[pltpu.VMEM((tm, D_MODEL), BF16)],
        compiler_params=_cparams(("arbitrary", "arbitrary"), vmem),
        name="mlp",
    )(x, g, w1, w2)


def _pack_w_in(w):
    return jnp.concatenate(
        [w[:, 0:2560], w[:, 2576:4112], w[:, 4624:5136], w[:, 5152:5664], w[:, 4112:4624],
         w[:, 2560:2576], w[:, 5136:5152], jnp.zeros((w.shape[0], PACKED - 5664), w.dtype)],
        axis=1).astype(BF16)


def _pad_lanes(v, width=LANE):
    v = v.reshape(1, -1)
    return jnp.pad(v, ((0, 0), (0, width - v.shape[1])))


def kernel(x_prompt, x_sample, cache_diff_k, cache_diff_v, state_ssd_conv, state_ssd, state_gla, norm1_g, w_in, ssd_conv_w, ssd_conv_b, ssd_dt_bias, ssd_a_log, ssd_d, ssd_norm_g, diff_qn_g, diff_kn_g, diff_lambda, diff_out_g, gla_wa2, gla_ba, gla_norm_g, w_out, norm2_g, w_mlp1, w_mlp2):
    bp, lp, d = x_prompt.shape
    nb, ls, _ = x_sample.shape
    depth = w_in.shape[0]
    past = cache_diff_k.shape[2]
    assert bp == 1 and d == D_MODEL and ls == CHUNK and lp % CHUNK == 0 and past % CHUNK == 0
    ncp = lp // CHUNK
    m = lp + nb * ls

    xp, xs = x_prompt.reshape(lp, d), x_sample.reshape(nb * ls, d)

    emat = (jnp.arange(LANE)[:, None] == (jnp.arange(SSD_WIDTH)[None, :] // SSD_HEAD_DIM)).astype(BF16)
    segmat = ((jnp.arange(DIFF_WIDTH)[:, None] // DIFF_HD) == (jnp.arange(DIFF_WIDTH)[None, :] // DIFF_HD)).astype(BF16)
    slopes = jnp.exp2(-8.0 * jnp.arange(1, DIFF_HEADS + 1, dtype=F32) / DIFF_HEADS) * LOG2E
    eye_h = jnp.eye(GLA_HEADS, dtype=F32)

    outs = {k: [] for k in ("kp", "vp", "cp", "hp", "sp", "ks", "vs", "cs", "hs", "ss")}
    for l in range(depth):
        lam_init = 0.8 - 0.6 * math.exp(-0.3 * l)
        proj = _in_proj(xp, xs, norm1_g[l].reshape(1, d), _pack_w_in(w_in[l]))

        st_all = jnp.concatenate([jnp.zeros((1,) + state_ssd.shape[2:], F32), state_ssd[l]], axis=0)
        st_in = st_all.reshape(nb + 1, SSD_WIDTH, SSD_STATE).transpose(0, 2, 1)
        cprev = jnp.concatenate([jnp.zeros((1, SSD_CONV - 1, SSD_CONV_DIM), F32), state_ssd_conv[l]], axis=0)
        cprev = jnp.pad(cprev, ((0, 0), (8 - (SSD_CONV - 1), 0), (0, 0)))
        cw, cb = ssd_conv_w[l], ssd_conv_b[l].reshape(1, -1)
        y_ssd, st_out = _ssd(
            proj, ncp, st_in, cprev[:, :, :SSD_WIDTH], cprev[:, :, SSD_WIDTH:],
            cw[:, :SSD_WIDTH], cb[:, :SSD_WIDTH], cw[:, SSD_WIDTH:], cb[:, SSD_WIDTH:],
            _pad_lanes(ssd_dt_bias[l]), _pad_lanes(ssd_a_log[l]),
            jnp.repeat(ssd_d[l], SSD_HEAD_DIM).reshape(1, -1), ssd_norm_g[l].reshape(1, -1), emat)
        h_all = st_out.transpose(0, 2, 1).reshape(nb + 1, SSD_HEADS, SSD_HEAD_DIM, SSD_STATE)
        outs["hp"].append(h_all[0:1])
        outs["hs"].append(h_all[1:])
        xbc_raw = proj[:, C_XS:C_XS + SSD_CONV_DIM]
        outs["cp"].append(xbc_raw[lp - (SSD_CONV - 1):lp][None])
        outs["cs"].append(xbc_raw[lp:].reshape(nb, ls, SSD_CONV_DIM)[:, ls - (SSD_CONV - 1):])

        s_all = jnp.concatenate([jnp.zeros((1,) + state_gla.shape[2:], F32), state_gla[l]], axis=0)
        s_in = jnp.einsum('bhkv,hg->bhvgk', s_all, eye_h).reshape(nb + 1, GLA_WIDTH, GLA_HEADS * GLA_DK)
        wa_pad = jnp.zeros((LANE, GLA_HEADS * GLA_DK), F32).at[GLA_RANK:2 * GLA_RANK].set(gla_wa2[l]).astype(BF16)
        y_gla, s_out = _gla(proj, ncp, s_in, wa_pad, gla_ba[l].reshape(1, -1), gla_norm_g[l].reshape(1, -1))
        s5 = s_out.reshape(nb + 1, GLA_HEADS, GLA_DV, GLA_HEADS, GLA_DK)
        s_new = jnp.stack([s5[:, h, :, h, :] for h in range(GLA_HEADS)], axis=1).transpose(0, 1, 3, 2)
        outs["sp"].append(s_new[0:1])
        outs["ss"].append(s_new[1:])

        qg = jnp.tile(diff_qn_g[l], 2 * DIFF_HEADS).reshape(1, -1)
        kg = jnp.tile(diff_kn_g[l], 2 * DIFF_HEADS).reshape(1, -1)
        qa, qb, kn, kb, vt, kp4, vp4, ks4, vs4 = _attn_prep(proj, lp, qg, kg, segmat)
        og = diff_out_g[l].reshape(1, -1)
        ya_p = _attn_prompt(lp, lam_init, qa, qb, kb, vt, slopes, diff_lambda[l], og)
        ya_s = _attn_sample(lp, nb, l, lam_init, qa, qb, kn, proj, cache_diff_k, cache_diff_v,
                            slopes, diff_lambda[l], og)
        outs["kp"].append(kp4.reshape(1, lp, DIFF_HEADS, 2 * DIFF_HD))
        outs["ks"].append(ks4.reshape(nb, ls, DIFF_HEADS, 2 * DIFF_HD))
        outs["vp"].append(vp4.reshape(1, lp, DIFF_HEADS, DIFF_VD))
        outs["vs"].append(vs4.reshape(nb, ls, DIFF_HEADS, DIFF_VD))

        x1 = _out_proj(xp, xs, y_ssd, ya_p, ya_s, y_gla, w_out[l].astype(BF16))
        xp, xs = _mlp(x1, lp, norm2_g[l].reshape(1, d), w_mlp1[l].astype(BF16), w_mlp2[l].astype(BF16))

    st = {k: jnp.stack(v) for k, v in outs.items()}
    return (xp.reshape(1, lp, d), xs.reshape(nb, ls, d),
            st["kp"], st["vp"], st["cp"], st["hp"], st["sp"],
            st["ks"], st["vs"], st["cs"], st["hs"], st["ss"])
```

```python
import functools
import math

import numpy as np
import jax
import jax.numpy as jnp
from jax import lax
from jax.experimental import pallas as pl
from jax.experimental.pallas import tpu as pltpu

F32 = jnp.float32
BF16 = jnp.bfloat16

D_MODEL = 2048
CHUNK = 64
SSD_HEAD_DIM = 64
SSD_WIDTH = 1024
SSD_HEADS = 16
SSD_GROUPS = 2
SSD_STATE = 128
SSD_CONV = 4
SSD_CONV_DIM = SSD_WIDTH + 2 * SSD_GROUPS * SSD_STATE
DIFF_HD = 64
DIFF_VD = 128
DIFF_WIDTH = 512
DIFF_HEADS = 4
GLA_DK = 64
GLA_DV = 128
GLA_WIDTH = 512
GLA_HEADS = 4
GLA_RANK = 16
GLA_TAU = 16.0
GLA_BLOCK = 16
D_FF = 4 * D_MODEL
EPS = 1e-6

LOG2E = math.log2(math.e)
LANE = 128
SUBLANE = 8
VMEM_CAP = 56 << 20

C_Z, C_XS, C_BC, C_DQ, C_DK, C_DV, C_GV, C_GG, C_GQ, C_GK, C_SM = (
    0, 1024, 2048, 2560, 3072, 3584, 4096, 4608, 5120, 5376, 5632)
PACKED = 5760
PROJ_TN = 1920


def _pick(m, cands):
    for c in cands:
        if m % c == 0:
            return c
    raise ValueError(f"no tile for {m}")


def _cparams(sem, vmem_bytes):
    return pltpu.CompilerParams(dimension_semantics=sem,
                                vmem_limit_bytes=int(min(VMEM_CAP, max(vmem_bytes, 16 << 20))))


def _sigmoid(x):
    return 1.0 / (1.0 + jnp.exp(-x))


def _split3(x):
    hi = x.astype(BF16)
    r1 = x - hi.astype(F32)
    mid = r1.astype(BF16)
    lo = (r1 - mid.astype(F32)).astype(BF16)
    return hi, mid, lo


def _select_dot(x, sel):
    return sum(jnp.dot(p, sel, preferred_element_type=F32) for p in _split3(x))


def _dot_select(sel, x):
    return sum(jnp.dot(sel, p, preferred_element_type=F32) for p in _split3(x))


def _softplus(x):
    return jnp.maximum(x, 0.0) + jnp.log1p(jnp.exp(-jnp.abs(x)))


def _rms_bf16(x, g):
    ms = jnp.mean(x * x, axis=-1, keepdims=True)
    return (x * lax.rsqrt(ms + EPS) * g).astype(BF16)


def _in_proj_kernel(npt, xp_ref, xs_ref, g_ref, w_ref, o_ref, h_ref):
    i = pl.program_id(0)

    @pl.when(jnp.logical_and(pl.program_id(1) == 0, i < npt))
    def _():
        h_ref[...] = _rms_bf16(xp_ref[...], g_ref[...])

    @pl.when(jnp.logical_and(pl.program_id(1) == 0, i >= npt))
    def _():
        h_ref[...] = _rms_bf16(xs_ref[...], g_ref[...])

    o_ref[...] = jnp.dot(h_ref[...], w_ref[...], preferred_element_type=F32)


def _split_maps(npt):
    return (lambda i, *_: (jnp.minimum(i, npt - 1), 0)), (lambda i, *_: (jnp.maximum(i - npt, 0), 0))


def _in_proj(xp, xs, g, w):
    lp, ms = xp.shape[0], xs.shape[0]
    tm = _pick(math.gcd(lp, ms), (512, 256, 128, 64))
    tn = PROJ_TN
    npt = lp // tm
    pmap, smap = _split_maps(npt)
    vmem = 4 * tm * D_MODEL * 4 + tm * D_MODEL * 2 + 2 * D_MODEL * tn * 2 + 2 * tm * tn * 4 + (4 << 20)
    return pl.pallas_call(
        functools.partial(_in_proj_kernel, npt),
        grid=((lp + ms) // tm, PACKED // tn),
        in_specs=[pl.BlockSpec((tm, D_MODEL), pmap),
                  pl.BlockSpec((tm, D_MODEL), smap),
                  pl.BlockSpec((1, D_MODEL), lambda i, j: (0, 0)),
                  pl.BlockSpec((D_MODEL, tn), lambda i, j: (0, j))],
        out_specs=pl.BlockSpec((tm, tn), lambda i, j: (i, j)),
        out_shape=jax.ShapeDtypeStruct((lp + ms, PACKED), F32),
        scratch_shapes=[pltpu.VMEM((tm, D_MODEL), BF16)],
        compiler_params=_cparams(("parallel", "arbitrary"), vmem),
        name="in_proj",
    )(xp, xs, g, w)


def _ssd_kernel(ncp, z_ref, xs_ref, bc_ref, sm_ref, st_in_ref, cpx_ref, cpb_ref,
                cwx_ref, cbx_ref, cwb_ref, cbb_ref, dtb_ref, alog_ref, dexp_ref, ng_ref, e_ref,
                y_ref, st_out_ref, st_scr, fx_scr, fb_scr):
    c = pl.program_id(0)
    q = CHUNK

    @pl.when(jnp.logical_or(c == 0, c >= ncp))
    def _():
        st_scr[...] = st_in_ref[0]
        fx_scr[0:8, :] = cpx_ref[0]
        fb_scr[0:8, :] = cpb_ref[0]

    def conv(u_ref, f_scr, w_ref, b_ref):
        u = u_ref[...]
        f_scr[8:8 + q, :] = u
        y = b_ref[...] + f_scr[5:5 + q, :] * w_ref[0:1, :]
        for j in range(1, SSD_CONV):
            y = y + f_scr[5 + j:5 + j + q, :] * w_ref[j:j + 1, :]
        f_scr[0:8, :] = u[q - 8:q, :]
        return y * _sigmoid(y)

    xs = conv(xs_ref, fx_scr, cwx_ref, cbx_ref)
    bcv = conv(bc_ref, fb_scr, cwb_ref, cbb_ref)

    dt = _softplus(sm_ref[...] + dtb_ref[...])
    da = dt * (-jnp.exp(alog_ref[...]))
    r64 = lax.broadcasted_iota(jnp.int32, (q, q), 0)
    c64 = lax.broadcasted_iota(jnp.int32, (q, q), 1)
    tril = (c64 <= r64).astype(BF16)
    cs = _dot_select(tril, da)
    both = _select_dot(jnp.concatenate([dt, cs], axis=0), e_ref[...])
    dt_e = both[0:q]
    cs_e = both[q:2 * q]

    row = lax.broadcasted_iota(jnp.int32, (q, SSD_WIDTH), 0)
    sidx = jnp.bitwise_and(lax.broadcasted_iota(jnp.int32, (q, SSD_WIDTH), 1), q - 1)
    cs_row = jnp.sum(jnp.where(sidx == row, cs_e, 0.0), axis=0, keepdims=True)
    cs_last = cs_e[q - 1:q, :]
    lmat = jnp.exp(jnp.where(sidx <= row, cs_e - cs_row, -jnp.inf))

    xdt = xs * dt_e
    xdt_end = (xdt * jnp.exp(cs_last - cs_e)).astype(BF16)
    xdt_b = xdt.astype(BF16)
    ecs = jnp.exp(cs_e)
    st = st_scr[...]
    st_b = st.astype(BF16)
    bcb = bcv.astype(BF16)

    r128 = lax.broadcasted_iota(jnp.int32, (2 * q, LANE), 0)
    c128 = lax.broadcasted_iota(jnp.int32, (2 * q, LANE), 1)
    bd2 = (r128 // q) == (c128 // q)

    hg = SSD_HEADS // SSD_GROUPS
    gw = hg * SSD_HEAD_DIM
    y_parts = []
    new_states = []
    for g in range(SSD_GROUPS):
        bm = bcb[:, g * SSD_STATE:(g + 1) * SSD_STATE]
        cm = bcb[:, (SSD_GROUPS + g) * SSD_STATE:(SSD_GROUPS + g + 1) * SSD_STATE]
        bm_rep = jnp.concatenate([bm] * hg, axis=0)
        cbt = lax.dot_general(cm, bm_rep, (((1,), (1,)), ((), ())),
                              preferred_element_type=F32)
        m_all = (cbt * lmat[:, g * gw:(g + 1) * gw]).astype(BF16)
        y_off = jnp.dot(cm, st_b[:, g * gw:(g + 1) * gw], preferred_element_type=F32)
        for j in range(gw // LANE):
            col = g * gw + j * LANE
            xj = xdt_b[:, col:col + LANE]
            xd = jnp.where(bd2, jnp.concatenate([xj, xj], axis=0), jnp.zeros((), BF16))
            y_parts.append(jnp.dot(m_all[:, j * LANE:(j + 1) * LANE], xd, preferred_element_type=F32)
                           + y_off[:, j * LANE:(j + 1) * LANE] * ecs[:, col:col + LANE])
        new_states.append(lax.dot_general(bm, xdt_end[:, g * gw:(g + 1) * gw], (((0,), (0,)), ((), ())),
                                          preferred_element_type=F32))
    y = jnp.concatenate(y_parts, axis=1)
    st_new = st * jnp.exp(cs_last) + jnp.concatenate(new_states, axis=1)
    st_scr[...] = st_new
    st_out_ref[0] = st_new

    y = y + dexp_ref[...] * xs
    zv = z_ref[...]
    y = y * (zv * _sigmoid(zv))
    outs = []
    for g in range(SSD_GROUPS):
        yg = y[:, g * gw:(g + 1) * gw]
        ms = jnp.mean(yg * yg, axis=-1, keepdims=True)
        outs.append(yg * lax.rsqrt(ms + EPS))
    y_ref[...] = jnp.concatenate(outs, axis=1) * ng_ref[...]


def _ssd(proj, ncp, st_in, cpx, cpb, cwx, cbx, cwb, cbb, dtb, alog, dexp, ng, emat):
    m = proj.shape[0]
    nchunks = m // CHUNK
    q = CHUNK
    smap = lambda c: (jnp.maximum(c - (ncp - 1), 0), 0, 0)
    const2 = lambda c: (0, 0)
    return pl.pallas_call(
        functools.partial(_ssd_kernel, ncp),
        grid=(nchunks,),
        in_specs=[pl.BlockSpec((q, 1024), lambda c: (c, C_Z // 1024)),
                  pl.BlockSpec((q, 1024), lambda c: (c, C_XS // 1024)),
                  pl.BlockSpec((q, 512), lambda c: (c, C_BC // 512)),
                  pl.BlockSpec((q, LANE), lambda c: (c, C_SM // LANE)),
                  pl.BlockSpec((1, SSD_STATE, SSD_WIDTH), smap),
                  pl.BlockSpec((1, 8, 1024), smap),
                  pl.BlockSpec((1, 8, 512), smap),
                  pl.BlockSpec((SSD_CONV, 1024), const2),
                  pl.BlockSpec((1, 1024), const2),
                  pl.BlockSpec((SSD_CONV, 512), const2),
                  pl.BlockSpec((1, 512), const2),
                  pl.BlockSpec((1, LANE), const2),
                  pl.BlockSpec((1, LANE), const2),
                  pl.BlockSpec((1, SSD_WIDTH), const2),
                  pl.BlockSpec((1, SSD_WIDTH), const2),
                  pl.BlockSpec((LANE, SSD_WIDTH), const2)],
        out_specs=[pl.BlockSpec((q, SSD_WIDTH), lambda c: (c, 0)),
                   pl.BlockSpec((1, SSD_STATE, SSD_WIDTH), smap)],
        out_shape=[jax.ShapeDtypeStruct((m, SSD_WIDTH), F32),
                   jax.ShapeDtypeStruct(st_in.shape, F32)],
        scratch_shapes=[pltpu.VMEM((SSD_STATE, SSD_WIDTH), F32),
                        pltpu.VMEM((8 + q, 1024), F32),
                        pltpu.VMEM((8 + q, 512), F32)],
        compiler_params=_cparams(("arbitrary",), 32 << 20),
        name="ssd",
    )(proj, proj, proj, proj, st_in, cpx, cpb, cwx, cbx, cwb, cbb, dtb, alog, dexp, ng, emat)


def _gla_kernel(ncp, gq_ref, gk_ref, gv_ref, gg_ref, sm_ref, s_in_ref, wa_ref, ba_ref, ng_ref,
                o_ref, s_out_ref, s_scr):
    c = pl.program_id(0)
    q = CHUNK
    blk = GLA_BLOCK
    kw = GLA_HEADS * GLA_DK

    @pl.when(jnp.logical_or(c == 0, c >= ncp))
    def _():
        s_scr[...] = s_in_ref[0]

    pre = jnp.dot(sm_ref[...].astype(BF16), wa_ref[...], preferred_element_type=F32) + ba_ref[...]
    log_a = -_softplus(-pre) * (1.0 / GLA_TAU)
    r64 = lax.broadcasted_iota(jnp.int32, (q, q), 0)
    c64 = lax.broadcasted_iota(jnp.int32, (q, q), 1)
    same_blk = (r64 // blk) == (c64 // blk)
    causal = jnp.logical_and(same_blk, c64 <= r64)
    b = _dot_select(causal.astype(BF16), log_a)
    gk = gk_ref[...]
    qt = gq_ref[...] * (GLA_DK ** -0.5) * jnp.exp(b)
    kt = (gk * jnp.exp(-b)).astype(BF16)
    b_last = jnp.concatenate(
        [jnp.broadcast_to(b[(j + 1) * blk - 1:(j + 1) * blk, :], (blk, kw)) for j in range(q // blk)], axis=0)
    ktil = (gk * jnp.exp(b_last - b)).astype(BF16)
    qt_b = qt.astype(BF16)
    gv_b = gv_ref[...].astype(BF16)

    lane_head = lax.broadcasted_iota(jnp.int32, (q, kw), 1) // GLA_DK
    o_intra = []
    for h in range(GLA_HEADS):
        qm = jnp.where(lane_head == h, qt_b, jnp.zeros((), BF16))
        att = lax.dot_general(qm, kt, (((1,), (1,)), ((), ())), preferred_element_type=F32)
        att = jnp.where(causal, att, 0.0).astype(BF16)
        o_intra.append(jnp.dot(att, gv_b[:, h * GLA_DV:(h + 1) * GLA_DV], preferred_element_type=F32))
    o_intra = jnp.concatenate(o_intra, axis=1)

    rs = lax.broadcasted_iota(jnp.int32, (GLA_WIDTH, kw), 0) // GLA_DV
    cs_ = lax.broadcasted_iota(jnp.int32, (GLA_WIDTH, kw), 1) // GLA_DK
    bd = rs == cs_
    sw = s_scr[...]
    o_inter = []
    for j in range(q // blk):
        sl = slice(j * blk, (j + 1) * blk)
        o_inter.append(lax.dot_general(qt_b[sl], sw.astype(BF16), (((1,), (1,)), ((), ())),
                                       preferred_element_type=F32))
        upd = lax.dot_general(gv_b[sl], ktil[sl], (((0,), (0,)), ((), ())),
                              preferred_element_type=F32)
        decay = jnp.exp(b[(j + 1) * blk - 1:(j + 1) * blk, :])
        sw = sw * decay + jnp.where(bd, upd, 0.0)
    s_scr[...] = sw
    s_out_ref[0] = sw
    o = o_intra + jnp.concatenate(o_inter, axis=0)

    gg = gg_ref[...]
    gate = gg * _sigmoid(gg)
    outs = []
    for h in range(GLA_HEADS):
        oh = o[:, h * GLA_DV:(h + 1) * GLA_DV]
        ms = jnp.mean(oh * oh, axis=-1, keepdims=True)
        outs.append(oh * lax.rsqrt(ms + EPS) * ng_ref[...])
    o_ref[...] = jnp.concatenate(outs, axis=1) * gate


def _gla(proj, ncp, s_in, wa_pad, ba, ng):
    m = proj.shape[0]
    q = CHUNK
    kw = GLA_HEADS * GLA_DK
    smap = lambda c: (jnp.maximum(c - (ncp - 1), 0), 0, 0)
    const2 = lambda c: (0, 0)
    return pl.pallas_call(
        functools.partial(_gla_kernel, ncp),
        grid=(m // q,),
        in_specs=[pl.BlockSpec((q, kw), lambda c: (c, C_GQ // kw)),
                  pl.BlockSpec((q, kw), lambda c: (c, C_GK // kw)),
                  pl.BlockSpec((q, GLA_WIDTH), lambda c: (c, C_GV // GLA_WIDTH)),
                  pl.BlockSpec((q, GLA_WIDTH), lambda c: (c, C_GG // GLA_WIDTH)),
                  pl.BlockSpec((q, LANE), lambda c: (c, C_SM // LANE)),
                  pl.BlockSpec((1, GLA_WIDTH, kw), smap),
                  pl.BlockSpec((LANE, kw), const2),
                  pl.BlockSpec((1, kw), const2),
                  pl.BlockSpec((1, GLA_DV), const2)],
        out_specs=[pl.BlockSpec((q, GLA_WIDTH), lambda c: (c, 0)),
                   pl.BlockSpec((1, GLA_WIDTH, kw), smap)],
        out_shape=[jax.ShapeDtypeStruct((m, GLA_WIDTH), F32),
                   jax.ShapeDtypeStruct(s_in.shape, F32)],
        scratch_shapes=[pltpu.VMEM((GLA_WIDTH, kw), F32)],
        compiler_params=_cparams(("arbitrary",), 24 << 20),
        name="gla",
    )(proj, proj, proj, proj, proj, s_in, wa_pad, ba, ng)


def _attn_prep_kernel(npt, dq_ref, dk_ref, dv_ref, qg_ref, kg_ref, seg_ref,
                      qa_ref, qb_ref, kn_ref, kb_ref, vt_ref, kp_ref, vp_ref, ks_ref, vs_ref):
    seg = seg_ref[...]
    tm = dq_ref.shape[0]

    def qknorm(x, g):
        ms = _select_dot(x * x, seg) * (1.0 / DIFF_HD)
        return x * lax.rsqrt(ms + EPS) * g

    qn = qknorm(dq_ref[...], qg_ref[...]) * (DIFF_HD ** -0.5 * LOG2E)
    first = (lax.broadcasted_iota(jnp.int32, qn.shape, 1) // DIFF_HD) % 2 == 0
    qa = jnp.where(first, qn, 0.0).astype(BF16)
    qb = jnp.where(first, 0.0, qn).astype(BF16)
    kn = qknorm(dk_ref[...], kg_ref[...])
    kn_ref[...] = kn
    kb = kn.astype(BF16)
    for h in range(DIFF_HEADS):
        sl = slice(h * LANE, (h + 1) * LANE)
        qa_ref[h] = qa[:, sl]
        qb_ref[h] = qb[:, sl]
        kb_ref[h] = kb[:, sl]
    dv = dv_ref[...]
    vt_ref[...] = dv.T.astype(BF16)

    def cache_rows(k_ref, v_ref):
        for h in range(DIFF_HEADS):
            k_ref[pl.ds(h, tm, stride=DIFF_HEADS), :] = kn[:, h * LANE:(h + 1) * LANE]
            v_ref[pl.ds(h, tm, stride=DIFF_HEADS), :] = dv[:, h * LANE:(h + 1) * LANE]

    @pl.when(pl.program_id(0) < npt)
    def _():
        cache_rows(kp_ref, vp_ref)

    @pl.when(pl.program_id(0) >= npt)
    def _():
        cache_rows(ks_ref, vs_ref)


def _attn_prep(proj, lp, qg, kg, segmat):
    m = proj.shape[0]
    ms = m - lp
    tm = _pick(math.gcd(lp, ms), (512, 256, 128))
    w = DIFF_WIDTH
    npt = lp // tm
    pmap, smap = _split_maps(npt)
    row = lambda off: pl.BlockSpec((tm, w), lambda i: (i, off // w))
    const2 = lambda i: (0, 0)
    rows4 = tm * DIFF_HEADS
    hm = pl.BlockSpec((DIFF_HEADS, tm, LANE), lambda i: (0, i, 0))
    return pl.pallas_call(
        functools.partial(_attn_prep_kernel, npt),
        grid=(m // tm,),
        in_specs=[row(C_DQ), row(C_DK), row(C_DV),
                  pl.BlockSpec((1, w), const2), pl.BlockSpec((1, w), const2),
                  pl.BlockSpec((w, w), const2)],
        out_specs=[hm, hm, pl.BlockSpec((tm, w), lambda i: (i, 0)), hm, pl.BlockSpec((w, tm), lambda i: (0, i))]
        + [pl.BlockSpec((rows4, LANE), pmap)] * 2 + [pl.BlockSpec((rows4, LANE), smap)] * 2,
        out_shape=[jax.ShapeDtypeStruct((DIFF_HEADS, m, LANE), BF16), jax.ShapeDtypeStruct((DIFF_HEADS, m, LANE), BF16),
                   jax.ShapeDtypeStruct((m, w), F32), jax.ShapeDtypeStruct((DIFF_HEADS, m, LANE), BF16),
                   jax.ShapeDtypeStruct((w, m), BF16)]
        + [jax.ShapeDtypeStruct((lp * DIFF_HEADS, LANE), F32)] * 2
        + [jax.ShapeDtypeStruct((ms * DIFF_HEADS, LANE), F32)] * 2,
        compiler_params=_cparams(("arbitrary",), 40 << 20),
        name="attn_prep",
    )(proj, proj, proj, qg, kg, segmat)


def _lambda_full(lam_ref, lam_init):
    l = lam_ref[...]
    a = jnp.sum(l[0:1] * l[1:2], axis=-1, keepdims=True)
    b = jnp.sum(l[2:3] * l[3:4], axis=-1, keepdims=True)
    return jnp.exp(a) - jnp.exp(b) + lam_init


ATTN_PEEK = 64
ATTN_MAX_SUM = 2.0 ** 64
ATTN_MIN_SUM = 2.0 ** -60


def _fold8(x, op):
    r = x[0:SUBLANE]
    for i in range(1, x.shape[0] // SUBLANE):
        r = op(r, x[i * SUBLANE:(i + 1) * SUBLANE])
    return r


def _attn_prompt_kernel(t, kbk, lam_init, qi_ref, ki_ref, slope_ref, qa_ref, qb_ref, kb_ref, vt_ref, lam_ref,
                        og_ref, o_ref, m_scr, l_scr, acc_scr, b0_scr, bd_scr):
    p = pl.program_id(0)
    qi = qi_ref[p]
    ki = ki_ref[p]
    nb = t // kbk
    w = 2 * t
    nt = (((1,), (1,)), ((), ()))

    @pl.when(p == 0)
    def _():
        b0_scr[...] = lax.broadcasted_iota(jnp.int32, (kbk, LANE), 0).astype(F32)
        kr = lax.broadcasted_iota(jnp.int32, (t, w), 0)
        lane = lax.broadcasted_iota(jnp.int32, (t, w), 1)
        qc = (lane // (2 * kbk)) * kbk + lane % kbk
        corr = jnp.minimum(2 * (qc - kr), 0).astype(F32)
        bd_scr[...] = jnp.where(kr // CHUNK <= qc // CHUNK, corr, -jnp.inf)

    @pl.when(ki == 0)
    def _():
        m_scr[...] = jnp.full(m_scr.shape, -jnp.inf, F32)
        l_scr[...] = jnp.zeros(l_scr.shape, F32)
        acc_scr[...] = jnp.zeros(acc_scr.shape, F32)

    def tile(h, diag):
        slope = slope_ref[h]
        shift = slope * ((ki - qi) * t).astype(F32)
        step = slope * float(kbk)
        qcat = jnp.concatenate([r[h, g * kbk:(g + 1) * kbk, :] for g in range(nb) for r in (qa_ref, qb_ref)],
                               axis=0)
        b0 = slope * b0_scr[...]
        b0 = jnp.concatenate([b0] * (w // LANE), axis=1)
        m_h, l_h, acc_h = m_scr.at[h], l_scr.at[h], acc_scr.at[h]

        def scores(rows, nrows, lo):
            s = lax.dot_general(kb_ref[h, rows, :], qcat[lo:, :], nt, preferred_element_type=F32) + b0[0:nrows, lo:]
            if diag:
                s = s + slope * bd_scr[rows, lo:]
            return s

        m_old = m_h[...]
        peek = scores(slice(0, ATTN_PEEK), ATTN_PEEK, 0)
        m_used = jnp.maximum(m_old, jnp.max(peek, axis=0, keepdims=True) + shift)
        if diag:
            lane = lax.broadcasted_iota(jnp.int32, (1, w), 1)
            qpos = (lane // (2 * kbk)) * kbk + lane % kbk
            ramp = jnp.maximum(qpos - (ATTN_PEEK - 1), 0).astype(F32)
        else:
            ramp = float(t - ATTN_PEEK)
        m_ref = m_used + slope * ramp
        ones = jnp.ones((SUBLANE, kbk), BF16)
        pv = None
        for b in range(nb):
            rows = slice(b * kbk, (b + 1) * kbk)
            lo = 2 * b * kbk if diag else 0
            e = scores(rows, kbk, lo) - (m_ref[:, lo:] - shift - b * step)
            pt = jnp.exp2(e).astype(BF16)
            d = jnp.dot(jnp.concatenate([vt_ref[h, :, rows], ones], axis=0), pt,
                        preferred_element_type=F32)
            if b == 0:
                pv = d
            elif lo == 0:
                pv = d + pv
            else:
                pv = jnp.concatenate([pv[:, :lo], d + pv[:, lo:]], axis=1)
        lsum = pv[DIFF_VD:DIFF_VD + 1, :]
        pv = pv[0:DIFF_VD, :]
        ok = jnp.logical_and(jnp.max(lsum) <= ATTN_MAX_SUM, jnp.min(lsum) >= ATTN_MIN_SUM)

        @pl.when(ok)
        def _():
            a_old = jnp.exp2(m_old - m_ref)
            l_h[...] = a_old * l_h[...] + lsum
            acc_h[...] = a_old * acc_h[...] + pv
            m_h[...] = m_ref

        @pl.when(jnp.logical_not(ok))
        def _():
            def body(b, carry):
                rows = pl.ds(pl.multiple_of(b * kbk, kbk), kbk)
                s = scores(rows, kbk, 0) + (shift + slope * (b * kbk).astype(F32))
                m_o = m_h[...]
                m_n = jnp.maximum(m_o, jnp.max(s, axis=0, keepdims=True))
                pt = jnp.exp2(s - m_n)
                a = jnp.exp2(m_o - m_n)
                l_h[...] = a * l_h[...] + jnp.sum(pt, axis=0, keepdims=True)
                acc_h[...] = a * acc_h[...] + jnp.dot(vt_ref[h, :, rows], pt.astype(BF16),
                                                      preferred_element_type=F32)
                m_h[...] = m_n
                return carry

            lax.fori_loop(0, nb, body, 0)

    def head(h, carry):
        @pl.when(ki < qi)
        def _():
            tile(h, False)

        @pl.when(ki == qi)
        def _():
            tile(h, True)
            lam = _lambda_full(lam_ref, lam_init)
            on = acc_scr[h] / l_scr[h]
            ot = jnp.concatenate(
                [on[:, 2 * g * kbk:(2 * g + 1) * kbk] - lam * on[:, (2 * g + 1) * kbk:(2 * g + 2) * kbk]
                 for g in range(nb)], axis=1)
            ms = jnp.mean(ot * ot, axis=0, keepdims=True)
            o_ref[h] = (ot * lax.rsqrt(ms + EPS)).T * (og_ref[...] * (1.0 - lam_init))

        return carry

    lax.fori_loop(0, DIFF_HEADS, head, 0)


def _attn_prompt(lp, lam_init, qa, qb, kb, vt, slopes, lam_p, og):
    t = _pick(lp, (1024, 512, 256))
    kbk = min(t, 512)
    nq = lp // t
    pairs = [(i, j) for i in range(nq) for j in range(i + 1)]
    qi_l = jnp.asarray(np.array([a for a, _ in pairs], np.int32))
    ki_l = jnp.asarray(np.array([b for _, b in pairs], np.int32))
    nh = DIFF_HEADS
    grid_spec = pltpu.PrefetchScalarGridSpec(
        num_scalar_prefetch=3,
        grid=(len(pairs),),
        in_specs=[pl.BlockSpec((nh, t, LANE), lambda p, qi, ki, sl: (0, qi[p], 0)),
                  pl.BlockSpec((nh, t, LANE), lambda p, qi, ki, sl: (0, qi[p], 0)),
                  pl.BlockSpec((nh, t, LANE), lambda p, qi, ki, sl: (0, ki[p], 0)),
                  pl.BlockSpec((nh, DIFF_VD, t), lambda p, qi, ki, sl: (0, 0, ki[p])),
                  pl.BlockSpec((4, DIFF_HD), lambda p, qi, ki, sl: (0, 0)),
                  pl.BlockSpec((1, DIFF_VD), lambda p, qi, ki, sl: (0, 0))],
        out_specs=pl.BlockSpec((nh, t, LANE), lambda p, qi, ki, sl: (0, qi[p], 0)),
        scratch_shapes=[pltpu.VMEM((nh, 1, 2 * t), F32), pltpu.VMEM((nh, 1, 2 * t), F32),
                        pltpu.VMEM((nh, DIFF_VD, 2 * t), F32),
                        pltpu.VMEM((kbk, LANE), F32), pltpu.VMEM((t, 2 * t), F32)])
    return pl.pallas_call(
        functools.partial(_attn_prompt_kernel, t, kbk, lam_init),
        grid_spec=grid_spec,
        out_shape=jax.ShapeDtypeStruct((nh, lp, LANE), F32),
        compiler_params=_cparams(("arbitrary",), 48 << 20),
        name="attn_prompt",
    )(qi_l, ki_l, slopes, qa, qb, kb, vt.reshape(nh, DIFF_VD, vt.shape[1]), lam_p, og)


def _attn_sample_kernel(past, lam_init, slope_ref, qa_ref, qb_ref, kn_ref, vn_ref, kc_ref, vc_ref, lam_ref, og_ref,
                        o_ref):
    q = CHUNK
    lam = _lambda_full(lam_ref, lam_init)
    kpos = lax.broadcasted_iota(jnp.int32, (1, past), 1).astype(F32)
    r = lax.broadcasted_iota(jnp.int32, (q, q), 0)
    cc = lax.broadcasted_iota(jnp.int32, (q, q), 1)
    new_bias = (past + r - jnp.abs(r - cc)).astype(F32)
    outs = []
    for h in range(DIFF_HEADS):
        slope = slope_ref[h]
        sl = slice(h * LANE, (h + 1) * LANE)
        kc = kc_ref[pl.ds(h, past, stride=DIFF_HEADS), :].astype(BF16)
        vc = vc_ref[pl.ds(h, past, stride=DIFF_HEADS), :].astype(BF16)
        kn = kn_ref[:, sl].astype(BF16)
        vn = vn_ref[:, sl].astype(BF16)
        res = []
        for q_ref in (qa_ref, qb_ref):
            qv = q_ref[h]
            s_c = lax.dot_general(qv, kc, (((1,), (1,)), ((), ())), preferred_element_type=F32) + slope * kpos
            s_n = lax.dot_general(qv, kn, (((1,), (1,)), ((), ())), preferred_element_type=F32) + slope * new_bias
            mx = jnp.maximum(jnp.max(s_c, axis=-1, keepdims=True), jnp.max(s_n, axis=-1, keepdims=True))
            p_c = jnp.exp2(s_c - mx)
            p_n = jnp.exp2(s_n - mx)
            den = jnp.sum(p_c, axis=-1, keepdims=True) + jnp.sum(p_n, axis=-1, keepdims=True)
            num = (jnp.dot(p_c.astype(BF16), vc, preferred_element_type=F32)
                   + jnp.dot(p_n.astype(BF16), vn, preferred_element_type=F32))
            res.append(num / den)
        o = res[0] - lam * res[1]
        ms = jnp.mean(o * o, axis=-1, keepdims=True)
        outs.append(o * lax.rsqrt(ms + EPS) * (og_ref[...] * (1.0 - lam_init)))
    o_ref[...] = jnp.concatenate(outs, axis=1)


def _attn_sample(lp, nb, layer, lam_init, qa, qb, kn, proj, cache_k, cache_v, slopes, lam_p, og):
    depth, _, past = cache_k.shape[:3]
    q = CHUNK
    w = DIFF_WIDTH
    off = lp // q
    cache_k = cache_k.reshape(depth, nb, past * DIFF_HEADS, DIFF_VD)
    cache_v = cache_v.reshape(depth, nb, past * DIFF_HEADS, DIFF_VD)
    cache_block = (None, None, past * DIFF_HEADS, DIFF_VD)
    grid_spec = pltpu.PrefetchScalarGridSpec(
        num_scalar_prefetch=1,
        grid=(nb,),
        in_specs=[pl.BlockSpec((DIFF_HEADS, q, LANE), lambda b, sl: (0, off + b, 0)),
                  pl.BlockSpec((DIFF_HEADS, q, LANE), lambda b, sl: (0, off + b, 0)),
                  pl.BlockSpec((q, w), lambda b, sl: (off + b, 0)),
                  pl.BlockSpec((q, w), lambda b, sl: (off + b, C_DV // w)),
                  pl.BlockSpec(cache_block, lambda b, sl: (layer, b, 0, 0)),
                  pl.BlockSpec(cache_block, lambda b, sl: (layer, b, 0, 0)),
                  pl.BlockSpec((4, DIFF_HD), lambda b, sl: (0, 0)),
                  pl.BlockSpec((1, DIFF_VD), lambda b, sl: (0, 0))],
        out_specs=pl.BlockSpec((q, w), lambda b, sl: (b, 0)))
    return pl.pallas_call(
        functools.partial(_attn_sample_kernel, past, lam_init),
        grid_spec=grid_spec,
        out_shape=jax.ShapeDtypeStruct((nb * q, DIFF_WIDTH), F32),
        compiler_params=_cparams(("arbitrary",), 32 << 20),
        name="attn_sample",
    )(slopes, qa, qb, kn, proj, cache_k, cache_v, lam_p, og)


def _out_proj_kernel(npt, xp_ref, xs_ref, ys_ref, yap_ref, yas_ref, yg_ref, w_ref, o_ref):
    is_prompt = pl.program_id(0) < npt
    yap = jnp.concatenate([yap_ref[h] for h in range(DIFF_HEADS)], axis=1)
    ya = jnp.where(is_prompt, yap, yas_ref[...])
    acc = jnp.dot(ys_ref[...].astype(BF16), w_ref[0:SSD_WIDTH, :], preferred_element_type=F32)
    acc = acc + jnp.dot(ya.astype(BF16), w_ref[SSD_WIDTH:SSD_WIDTH + DIFF_WIDTH, :],
                        preferred_element_type=F32)
    acc = acc + jnp.dot(yg_ref[...].astype(BF16), w_ref[SSD_WIDTH + DIFF_WIDTH:, :],
                        preferred_element_type=F32)
    o_ref[...] = jnp.where(is_prompt, xp_ref[...], xs_ref[...]) + acc


def _out_proj(xp, xs, ys, yap, yas, yg, w):
    lp, ms = xp.shape[0], xs.shape[0]
    tm = _pick(math.gcd(lp, ms), (512, 256, 128, 64))
    npt = lp // tm
    pmap, smap = _split_maps(npt)
    vmem = 2 * (5 * tm * D_MODEL * 4) + D_MODEL * D_MODEL * 2 + (4 << 20)
    return pl.pallas_call(
        functools.partial(_out_proj_kernel, npt),
        grid=((lp + ms) // tm,),
        in_specs=[pl.BlockSpec((tm, D_MODEL), pmap),
                  pl.BlockSpec((tm, D_MODEL), smap),
                  pl.BlockSpec((tm, SSD_WIDTH), lambda i: (i, 0)),
                  pl.BlockSpec((DIFF_HEADS, tm, LANE), lambda i: (0, jnp.minimum(i, npt - 1), 0)),
                  pl.BlockSpec((tm, DIFF_WIDTH), smap),
                  pl.BlockSpec((tm, GLA_WIDTH), lambda i: (i, 0)),
                  pl.BlockSpec((D_MODEL, D_MODEL), lambda i: (0, 0), pipeline_mode=pl.Buffered(1))],
        out_specs=pl.BlockSpec((tm, D_MODEL), lambda i: (i, 0)),
        out_shape=jax.ShapeDtypeStruct((lp + ms, D_MODEL), F32),
        compiler_params=_cparams(("parallel",), vmem),
        name="out_proj",
    )(xp, xs, ys, yap, yas, yg, w)


def _mlp_kernel(npt, x_ref, g_ref, w1_ref, w2_ref, op_ref, os_ref, h_ref):
    i = pl.program_id(0)
    first = pl.program_id(1) == 0

    @pl.when(first)
    def _():
        h_ref[...] = _rms_bf16(x_ref[...], g_ref[...])

    a = jnp.dot(h_ref[...], w1_ref[...], preferred_element_type=F32)
    a = jnp.square(jnp.maximum(a, 0.0)).astype(BF16)

    def accumulate(o_ref):
        @pl.when(first)
        def _():
            o_ref[...] = x_ref[...]

        o_ref[...] += jnp.dot(a, w2_ref[...], preferred_element_type=F32)

    @pl.when(i < npt)
    def _():
        accumulate(op_ref)

    @pl.when(i >= npt)
    def _():
        accumulate(os_ref)


def _mlp(x, lp, g, w1, w2):
    m = x.shape[0]
    ms = m - lp
    tm = _pick(math.gcd(lp, ms), (512, 256, 128, 64))
    tf = 1024
    npt = lp // tm
    pmap, smap = _split_maps(npt)
    vmem = 6 * tm * D_MODEL * 4 + tm * D_MODEL * 2 + 4 * D_MODEL * tf * 2 + 2 * tm * tf * 4 + (4 << 20)
    return pl.pallas_call(
        functools.partial(_mlp_kernel, npt),
        grid=(m // tm, D_FF // tf),
        in_specs=[pl.BlockSpec((tm, D_MODEL), lambda i, f: (i, 0)),
                  pl.BlockSpec((1, D_MODEL), lambda i, f: (0, 0)),
                  pl.BlockSpec((D_MODEL, tf), lambda i, f: (0, f)),
                  pl.BlockSpec((tf, D_MODEL), lambda i, f: (f, 0))],
        out_specs=[pl.BlockSpec((tm, D_MODEL), pmap), pl.BlockSpec((tm, D_MODEL), smap)],
        out_shape=[jax.ShapeDtypeStruct((lp, D_MODEL), F32), jax.ShapeDtypeStruct((ms, D_MODEL), F32)],
        scratch_shapes=[pltpu.VMEM((tm, D_MODEL), BF16)],
        compiler_params=_cparams(("arbitrary", "arbitrary"), vmem),
        name="mlp",
    )(x, g, w1, w2)


def _pack_w_in(w):
    return jnp.concatenate(
        [w[:, 0:2560], w[:, 2576:4112], w[:, 4624:5136], w[:, 5152:5664], w[:, 4112:4624],
         w[:, 2560:2576], w[:, 5136:5152], jnp.zeros((w.shape[0], PACKED - 5664), w.dtype)],
        axis=1).astype(BF16)


def _pad_lanes(v, width=LANE):
    v = v.reshape(1, -1)
    return jnp.pad(v, ((0, 0), (0, width - v.shape[1])))


def kernel(x_prompt, x_sample, cache_diff_k, cache_diff_v, state_ssd_conv, state_ssd, state_gla, norm1_g, w_in, ssd_conv_w, ssd_conv_b, ssd_dt_bias, ssd_a_log, ssd_d, ssd_norm_g, diff_qn_g, diff_kn_g, diff_lambda, diff_out_g, gla_wa2, gla_ba, gla_norm_g, w_out, norm2_g, w_mlp1, w_mlp2):
    bp, lp, d = x_prompt.shape
    nb, ls, _ = x_sample.shape
    depth = w_in.shape[0]
    past = cache_diff_k.shape[2]
    assert bp == 1 and d == D_MODEL and ls == CHUNK and lp % CHUNK == 0 and past % CHUNK == 0
    ncp = lp // CHUNK
    m = lp + nb * ls

    xp, xs = x_prompt.reshape(lp, d), x_sample.reshape(nb * ls, d)

    emat = (jnp.arange(LANE)[:, None] == (jnp.arange(SSD_WIDTH)[None, :] // SSD_HEAD_DIM)).astype(BF16)
    segmat = ((jnp.arange(DIFF_WIDTH)[:, None] // DIFF_HD) == (jnp.arange(DIFF_WIDTH)[None, :] // DIFF_HD)).astype(BF16)
    slopes = jnp.exp2(-8.0 * jnp.arange(1, DIFF_HEADS + 1, dtype=F32) / DIFF_HEADS) * LOG2E
    eye_h = jnp.eye(GLA_HEADS, dtype=F32)

    outs = {k: [] for k in ("kp", "vp", "cp", "hp", "sp", "ks", "vs", "cs", "hs", "ss")}
    for l in range(depth):
        lam_init = 0.8 - 0.6 * math.exp(-0.3 * l)
        proj = _in_proj(xp, xs, norm1_g[l].reshape(1, d), _pack_w_in(w_in[l]))

        st_all = jnp.concatenate([jnp.zeros((1,) + state_ssd.shape[2:], F32), state_ssd[l]], axis=0)
        st_in = st_all.reshape(nb + 1, SSD_WIDTH, SSD_STATE).transpose(0, 2, 1)
        cprev = jnp.concatenate([jnp.zeros((1, SSD_CONV - 1, SSD_CONV_DIM), F32), state_ssd_conv[l]], axis=0)
        cprev = jnp.pad(cprev, ((0, 0), (8 - (SSD_CONV - 1), 0), (0, 0)))
        cw, cb = ssd_conv_w[l], ssd_conv_b[l].reshape(1, -1)
        y_ssd, st_out = _ssd(
            proj, ncp, st_in, cprev[:, :, :SSD_WIDTH], cprev[:, :, SSD_WIDTH:],
            cw[:, :SSD_WIDTH], cb[:, :SSD_WIDTH], cw[:, SSD_WIDTH:], cb[:, SSD_WIDTH:],
            _pad_lanes(ssd_dt_bias[l]), _pad_lanes(ssd_a_log[l]),
            jnp.repeat(ssd_d[l], SSD_HEAD_DIM).reshape(1, -1), ssd_norm_g[l].reshape(1, -1), emat)
        h_all = st_out.transpose(0, 2, 1).reshape(nb + 1, SSD_HEADS, SSD_HEAD_DIM, SSD_STATE)
        outs["hp"].append(h_all[0:1])
        outs["hs"].append(h_all[1:])
        xbc_raw = proj[:, C_XS:C_XS + SSD_CONV_DIM]
        outs["cp"].append(xbc_raw[lp - (SSD_CONV - 1):lp][None])
        outs["cs"].append(xbc_raw[lp:].reshape(nb, ls, SSD_CONV_DIM)[:, ls - (SSD_CONV - 1):])

        s_all = jnp.concatenate([jnp.zeros((1,) + state_gla.shape[2:], F32), state_gla[l]], axis=0)
        s_in = jnp.einsum('bhkv,hg->bhvgk', s_all, eye_h).reshape(nb + 1, GLA_WIDTH, GLA_HEADS * GLA_DK)
        wa_pad = jnp.zeros((LANE, GLA_HEADS * GLA_DK), F32).at[GLA_RANK:2 * GLA_RANK].set(gla_wa2[l]).astype(BF16)
        y_gla, s_out = _gla(proj, ncp, s_in, wa_pad, gla_ba[l].reshape(1, -1), gla_norm_g[l].reshape(1, -1))
        s5 = s_out.reshape(nb + 1, GLA_HEADS, GLA_DV, GLA_HEADS, GLA_DK)
        s_new = jnp.stack([s5[:, h, :, h, :] for h in range(GLA_HEADS)], axis=1).transpose(0, 1, 3, 2)
        outs["sp"].append(s_new[0:1])
        outs["ss"].append(s_new[1:])

        qg = jnp.tile(diff_qn_g[l], 2 * DIFF_HEADS).reshape(1, -1)
        kg = jnp.tile(diff_kn_g[l], 2 * DIFF_HEADS).reshape(1, -1)
        qa, qb, kn, kb, vt, kp4, vp4, ks4, vs4 = _attn_prep(proj, lp, qg, kg, segmat)
        og = diff_out_g[l].reshape(1, -1)
        ya_p = _attn_prompt(lp, lam_init, qa, qb, kb, vt, slopes, diff_lambda[l], og)
        ya_s = _attn_sample(lp, nb, l, lam_init, qa, qb, kn, proj, cache_diff_k, cache_diff_v,
                            slopes, diff_lambda[l], og)
        outs["kp"].append(kp4.reshape(1, lp, DIFF_HEADS, 2 * DIFF_HD))
        outs["ks"].append(ks4.reshape(nb, ls, DIFF_HEADS, 2 * DIFF_HD))
        outs["vp"].append(vp4.reshape(1, lp, DIFF_HEADS, DIFF_VD))
        outs["vs"].append(vs4.reshape(nb, ls, DIFF_HEADS, DIFF_VD))

        x1 = _out_proj(xp, xs, y_ssd, ya_p, ya_s, y_gla, w_out[l].astype(BF16))
        xp, xs = _mlp(x1, lp, norm2_g[l].reshape(1, d), w_mlp1[l].astype(BF16), w_mlp2[l].astype(BF16))

    st = {k: jnp.stack(v) for k, v in outs.items()}
    return (xp.reshape(1, lp, d), xs.reshape(nb, ls, d),
            st["kp"], st["vp"], st["cp"], st["hp"], st["sp"],
            st["ks"], st["vs"], st["cs"], st["hs"], st["ss"])
```

```python
import functools
import math

import numpy as np
import jax
import jax.numpy as jnp
from jax import lax
from jax.experimental import pallas as pl
from jax.experimental.pallas import tpu as pltpu

F32 = jnp.float32
BF16 = jnp.bfloat16

D_MODEL = 2048
CHUNK = 64
SSD_HEAD_DIM = 64
SSD_WIDTH = 1024
SSD_HEADS = 16
SSD_GROUPS = 2
SSD_STATE = 128
SSD_CONV = 4
SSD_CONV_DIM = SSD_WIDTH + 2 * SSD_GROUPS * SSD_STATE
DIFF_HD = 64
DIFF_VD = 128
DIFF_WIDTH = 512
DIFF_HEADS = 4
GLA_DK = 64
GLA_DV = 128
GLA_WIDTH = 512
GLA_HEADS = 4
GLA_RANK = 16
GLA_TAU = 16.0
GLA_BLOCK = 16
D_FF = 4 * D_MODEL
EPS = 1e-6

LOG2E = math.log2(math.e)
LANE = 128
SUBLANE = 8
VMEM_CAP = 56 << 20

C_Z, C_XS, C_BC, C_DQ, C_DK, C_DV, C_GV, C_GG, C_GQ, C_GK, C_SM = (
    0, 1024, 2048, 2560, 3072, 3584, 4096, 4608, 5120, 5376, 5632)
PACKED = 5760
PROJ_TN = 1920


def _pick(m, cands):
    for c in cands:
        if m % c == 0:
            return c
    raise ValueError(f"no tile for {m}")


def _cparams(sem, vmem_bytes):
    return pltpu.CompilerParams(dimension_semantics=sem,
                                vmem_limit_bytes=int(min(VMEM_CAP, max(vmem_bytes, 16 << 20))))


def _sigmoid(x):
    return 1.0 / (1.0 + jnp.exp(-x))


def _split3(x):
    hi = x.astype(BF16)
    r1 = x - hi.astype(F32)
    mid = r1.astype(BF16)
    lo = (r1 - mid.astype(F32)).astype(BF16)
    return hi, mid, lo


def _select_dot(x, sel):
    return sum(jnp.dot(p, sel, preferred_element_type=F32) for p in _split3(x))


def _dot_select(sel, x):
    return sum(jnp.dot(sel, p, preferred_element_type=F32) for p in _split3(x))


def _softplus(x):
    return jnp.maximum(x, 0.0) + jnp.log1p(jnp.exp(-jnp.abs(x)))


def _rms_bf16(x, g):
    ms = jnp.mean(x * x, axis=-1, keepdims=True)
    return (x * lax.rsqrt(ms + EPS) * g).astype(BF16)


def _in_proj_kernel(npt, xp_ref, xs_ref, g_ref, w_ref, o_ref, h_ref):
    i = pl.program_id(0)

    @pl.when(jnp.logical_and(pl.program_id(1) == 0, i < npt))
    def _():
        h_ref[...] = _rms_bf16(xp_ref[...], g_ref[...])

    @pl.when(jnp.logical_and(pl.program_id(1) == 0, i >= npt))
    def _():
        h_ref[...] = _rms_bf16(xs_ref[...], g_ref[...])

    o_ref[...] = jnp.dot(h_ref[...], w_ref[...], preferred_element_type=F32)


def _split_maps(npt):
    return (lambda i, *_: (jnp.minimum(i, npt - 1), 0)), (lambda i, *_: (jnp.maximum(i - npt, 0), 0))


def _in_proj(xp, xs, g, w):
    lp, ms = xp.shape[0], xs.shape[0]
    tm = _pick(math.gcd(lp, ms), (512, 256, 128, 64))
    tn = PROJ_TN
    npt = lp // tm
    pmap, smap = _split_maps(npt)
    vmem = 4 * tm * D_MODEL * 4 + tm * D_MODEL * 2 + 2 * D_MODEL * tn * 2 + 2 * tm * tn * 4 + (4 << 20)
    return pl.pallas_call(
        functools.partial(_in_proj_kernel, npt),
        grid=((lp + ms) // tm, PACKED // tn),
        in_specs=[pl.BlockSpec((tm, D_MODEL), pmap),
                  pl.BlockSpec((tm, D_MODEL), smap),
                  pl.BlockSpec((1, D_MODEL), lambda i, j: (0, 0)),
                  pl.BlockSpec((D_MODEL, tn), lambda i, j: (0, j))],
        out_specs=pl.BlockSpec((tm, tn), lambda i, j: (i, j)),
        out_shape=jax.ShapeDtypeStruct((lp + ms, PACKED), F32),
        scratch_shapes=[pltpu.VMEM((tm, D_MODEL), BF16)],
        compiler_params=_cparams(("parallel", "arbitrary"), vmem),
        name="in_proj",
    )(xp, xs, g, w)


def _ssd_kernel(ncp, z_ref, xs_ref, bc_ref, sm_ref, st_in_ref, cpx_ref, cpb_ref,
                cwx_ref, cbx_ref, cwb_ref, cbb_ref, dtb_ref, alog_ref, dexp_ref, ng_ref, e_ref,
                y_ref, st_out_ref, st_scr, fx_scr, fb_scr):
    c = pl.program_id(0)
    q = CHUNK

    @pl.when(jnp.logical_or(c == 0, c >= ncp))
    def _():
        st_scr[...] = st_in_ref[0]
        fx_scr[0:8, :] = cpx_ref[0]
        fb_scr[0:8, :] = cpb_ref[0]

    def conv(u_ref, f_scr, w_ref, b_ref):
        u = u_ref[...]
        f_scr[8:8 + q, :] = u
        y = b_ref[...] + f_scr[5:5 + q, :] * w_ref[0:1, :]
        for j in range(1, SSD_CONV):
            y = y + f_scr[5 + j:5 + j + q, :] * w_ref[j:j + 1, :]
        f_scr[0:8, :] = u[q - 8:q, :]
        return y * _sigmoid(y)

    yield
    dt = _softplus(sm_ref[...] + dtb_ref[...])
    da = dt * (-jnp.exp(alog_ref[...]))
    r64 = lax.broadcasted_iota(jnp.int32, (q, q), 0)
    c64 = lax.broadcasted_iota(jnp.int32, (q, q), 1)
    tril = (c64 <= r64).astype(BF16)
    cs = _dot_select(tril, da)
    yield
    xs = conv(xs_ref, fx_scr, cwx_ref, cbx_ref)
    yield
    both = _select_dot(jnp.concatenate([dt, cs], axis=0), e_ref[...])
    dt_e = both[0:q]
    cs_e = both[q:2 * q]
    yield
    bcv = conv(bc_ref, fb_scr, cwb_ref, cbb_ref)
    yield

    row = lax.broadcasted_iota(jnp.int32, (q, SSD_WIDTH), 0)
    sidx = jnp.bitwise_and(lax.broadcasted_iota(jnp.int32, (q, SSD_WIDTH), 1), q - 1)
    cs_row = jnp.sum(jnp.where(sidx == row, cs_e, 0.0), axis=0, keepdims=True)
    cs_last = cs_e[q - 1:q, :]
    lmat = jnp.exp(jnp.where(sidx <= row, cs_e - cs_row, -jnp.inf))
    yield

    xdt = xs * dt_e
    xdt_end = (xdt * jnp.exp(cs_last - cs_e)).astype(BF16)
    xdt_b = xdt.astype(BF16)
    yield
    ecs = jnp.exp(cs_e)
    st = st_scr[...]
    st_b = st.astype(BF16)
    bcb = bcv.astype(BF16)
    yield

    r128 = lax.broadcasted_iota(jnp.int32, (2 * q, LANE), 0)
    c128 = lax.broadcasted_iota(jnp.int32, (2 * q, LANE), 1)
    bd2 = (r128 // q) == (c128 // q)

    hg = SSD_HEADS // SSD_GROUPS
    gw = hg * SSD_HEAD_DIM
    y_parts = []
    new_states = []
    for g in range(SSD_GROUPS):
        bm = bcb[:, g * SSD_STATE:(g + 1) * SSD_STATE]
        cm = bcb[:, (SSD_GROUPS + g) * SSD_STATE:(SSD_GROUPS + g + 1) * SSD_STATE]
        bm_rep = jnp.concatenate([bm] * hg, axis=0)
        cbt = lax.dot_general(cm, bm_rep, (((1,), (1,)), ((), ())),
                              preferred_element_type=F32)
        m_all = (cbt * lmat[:, g * gw:(g + 1) * gw]).astype(BF16)
        y_off = jnp.dot(cm, st_b[:, g * gw:(g + 1) * gw], preferred_element_type=F32)
        yield
        for j in range(gw // LANE):
            col = g * gw + j * LANE
            xj = xdt_b[:, col:col + LANE]
            xd = jnp.where(bd2, jnp.concatenate([xj, xj], axis=0), jnp.zeros((), BF16))
            y_parts.append(jnp.dot(m_all[:, j * LANE:(j + 1) * LANE], xd, preferred_element_type=F32)
                           + y_off[:, j * LANE:(j + 1) * LANE] * ecs[:, col:col + LANE])
            yield
        new_states.append(lax.dot_general(bm, xdt_end[:, g * gw:(g + 1) * gw], (((0,), (0,)), ((), ())),
                                          preferred_element_type=F32))
        yield
    y = jnp.concatenate(y_parts, axis=1)
    st_new = st * jnp.exp(cs_last) + jnp.concatenate(new_states, axis=1)
    st_scr[...] = st_new
    st_out_ref[0] = st_new
    yield

    y = y + dexp_ref[...] * xs
    zv = z_ref[...]
    y = y * (zv * _sigmoid(zv))
    yield
    outs = []
    for g in range(SSD_GROUPS):
        yg = y[:, g * gw:(g + 1) * gw]
        ms = jnp.mean(yg * yg, axis=-1, keepdims=True)
        outs.append(yg * lax.rsqrt(ms + EPS))
    y_ref[...] = jnp.concatenate(outs, axis=1) * ng_ref[...]


def _ssd_call(proj, ncp, st_in, cpx, cpb, cwx, cbx, cwb, cbb, dtb, alog, dexp, ng, emat):
    m = proj.shape[0]
    q = CHUNK
    smap = lambda c: (jnp.maximum(c - (ncp - 1), 0), 0, 0)
    const2 = lambda c: (0, 0)
    return dict(
        in_specs=[pl.BlockSpec((q, 1024), lambda c: (c, C_Z // 1024)),
                  pl.BlockSpec((q, 1024), lambda c: (c, C_XS // 1024)),
                  pl.BlockSpec((q, 512), lambda c: (c, C_BC // 512)),
                  pl.BlockSpec((q, LANE), lambda c: (c, C_SM // LANE)),
                  pl.BlockSpec((1, SSD_STATE, SSD_WIDTH), smap),
                  pl.BlockSpec((1, 8, 1024), smap),
                  pl.BlockSpec((1, 8, 512), smap),
                  pl.BlockSpec((SSD_CONV, 1024), const2),
                  pl.BlockSpec((1, 1024), const2),
                  pl.BlockSpec((SSD_CONV, 512), const2),
                  pl.BlockSpec((1, 512), const2),
                  pl.BlockSpec((1, LANE), const2),
                  pl.BlockSpec((1, LANE), const2),
                  pl.BlockSpec((1, SSD_WIDTH), const2),
                  pl.BlockSpec((1, SSD_WIDTH), const2),
                  pl.BlockSpec((LANE, SSD_WIDTH), const2)],
        out_specs=[pl.BlockSpec((q, SSD_WIDTH), lambda c: (c, 0)),
                   pl.BlockSpec((1, SSD_STATE, SSD_WIDTH), smap)],
        out_shape=[jax.ShapeDtypeStruct((m, SSD_WIDTH), F32),
                   jax.ShapeDtypeStruct(st_in.shape, F32)],
        scratch_shapes=[pltpu.VMEM((SSD_STATE, SSD_WIDTH), F32),
                        pltpu.VMEM((8 + q, 1024), F32),
                        pltpu.VMEM((8 + q, 512), F32)],
        args=(proj, proj, proj, proj, st_in, cpx, cpb, cwx, cbx, cwb, cbb, dtb, alog, dexp, ng, emat))


def _gla_kernel(ncp, gq_ref, gk_ref, gv_ref, gg_ref, sm_ref, s_in_ref, wa_ref, ba_ref, ng_ref,
                o_ref, s_out_ref, s_scr):
    c = pl.program_id(0)
    q = CHUNK
    blk = GLA_BLOCK
    kw = GLA_HEADS * GLA_DK

    @pl.when(jnp.logical_or(c == 0, c >= ncp))
    def _():
        s_scr[...] = s_in_ref[0]

    yield
    pre = jnp.dot(sm_ref[...].astype(BF16), wa_ref[...], preferred_element_type=F32) + ba_ref[...]
    yield
    log_a = -_softplus(-pre) * (1.0 / GLA_TAU)
    r64 = lax.broadcasted_iota(jnp.int32, (q, q), 0)
    c64 = lax.broadcasted_iota(jnp.int32, (q, q), 1)
    same_blk = (r64 // blk) == (c64 // blk)
    causal = jnp.logical_and(same_blk, c64 <= r64)
    yield
    b = _dot_select(causal.astype(BF16), log_a)
    yield
    gk = gk_ref[...]
    qt = gq_ref[...] * (GLA_DK ** -0.5) * jnp.exp(b)
    kt = (gk * jnp.exp(-b)).astype(BF16)
    yield
    b_last = jnp.concatenate(
        [jnp.broadcast_to(b[(j + 1) * blk - 1:(j + 1) * blk, :], (blk, kw)) for j in range(q // blk)], axis=0)
    ktil = (gk * jnp.exp(b_last - b)).astype(BF16)
    qt_b = qt.astype(BF16)
    gv_b = gv_ref[...].astype(BF16)
    yield

    lane_head = lax.broadcasted_iota(jnp.int32, (q, kw), 1) // GLA_DK
    o_intra = []
    for h in range(GLA_HEADS):
        qm = jnp.where(lane_head == h, qt_b, jnp.zeros((), BF16))
        att = lax.dot_general(qm, kt, (((1,), (1,)), ((), ())), preferred_element_type=F32)
        yield
        att = jnp.where(causal, att, 0.0).astype(BF16)
        o_intra.append(jnp.dot(att, gv_b[:, h * GLA_DV:(h + 1) * GLA_DV], preferred_element_type=F32))
        yield
    o_intra = jnp.concatenate(o_intra, axis=1)

    rs = lax.broadcasted_iota(jnp.int32, (GLA_WIDTH, kw), 0) // GLA_DV
    cs_ = lax.broadcasted_iota(jnp.int32, (GLA_WIDTH, kw), 1) // GLA_DK
    bd = rs == cs_
    sw = s_scr[...]
    o_inter = []
    for j in range(q // blk):
        sl = slice(j * blk, (j + 1) * blk)
        o_inter.append(lax.dot_general(qt_b[sl], sw.astype(BF16), (((1,), (1,)), ((), ())),
                                       preferred_element_type=F32))
        yield
        upd = lax.dot_general(gv_b[sl], ktil[sl], (((0,), (0,)), ((), ())),
                              preferred_element_type=F32)
        yield
        decay = jnp.exp(b[(j + 1) * blk - 1:(j + 1) * blk, :])
        sw = sw * decay + jnp.where(bd, upd, 0.0)
        yield
    s_scr[...] = sw
    s_out_ref[0] = sw
    o = o_intra + jnp.concatenate(o_inter, axis=0)
    yield

    gg = gg_ref[...]
    gate = gg * _sigmoid(gg)
    outs = []
    for h in range(GLA_HEADS):
        oh = o[:, h * GLA_DV:(h + 1) * GLA_DV]
        ms = jnp.mean(oh * oh, axis=-1, keepdims=True)
        outs.append(oh * lax.rsqrt(ms + EPS) * ng_ref[...])
    o_ref[...] = jnp.concatenate(outs, axis=1) * gate


def _gla_call(proj, ncp, s_in, wa_pad, ba, ng):
    m = proj.shape[0]
    q = CHUNK
    kw = GLA_HEADS * GLA_DK
    smap = lambda c: (jnp.maximum(c - (ncp - 1), 0), 0, 0)
    const2 = lambda c: (0, 0)
    return dict(
        in_specs=[pl.BlockSpec((q, kw), lambda c: (c, C_GQ // kw)),
                  pl.BlockSpec((q, kw), lambda c: (c, C_GK // kw)),
                  pl.BlockSpec((q, GLA_WIDTH), lambda c: (c, C_GV // GLA_WIDTH)),
                  pl.BlockSpec((q, GLA_WIDTH), lambda c: (c, C_GG // GLA_WIDTH)),
                  pl.BlockSpec((q, LANE), lambda c: (c, C_SM // LANE)),
                  pl.BlockSpec((1, GLA_WIDTH, kw), smap),
                  pl.BlockSpec((LANE, kw), const2),
                  pl.BlockSpec((1, kw), const2),
                  pl.BlockSpec((1, GLA_DV), const2)],
        out_specs=[pl.BlockSpec((q, GLA_WIDTH), lambda c: (c, 0)),
                   pl.BlockSpec((1, GLA_WIDTH, kw), smap)],
        out_shape=[jax.ShapeDtypeStruct((m, GLA_WIDTH), F32),
                   jax.ShapeDtypeStruct(s_in.shape, F32)],
        scratch_shapes=[pltpu.VMEM((GLA_WIDTH, kw), F32)],
        args=(proj, proj, proj, proj, proj, s_in, wa_pad, ba, ng))


def _mixers_kernel(ncp, n_in, n_out, n_scr, *refs):
    i0, i1 = n_in
    o0, o1 = n_out
    s0, s1 = n_scr
    ins, outs, scr = refs[:i0 + i1], refs[i0 + i1:i0 + i1 + o0 + o1], refs[i0 + i1 + o0 + o1:]
    streams = [_gla_kernel(ncp, *ins[i0:], *outs[o0:], *scr[s0:]),
               _ssd_kernel(ncp, *ins[:i0], *outs[:o0], *scr[:s0])]
    while streams:
        for s in list(streams):
            if next(s, StopIteration) is StopIteration:
                streams.remove(s)


def _mixers(ncp, ssd, gla):
    m = ssd["args"][0].shape[0]
    counts = [(len(ssd[k]), len(gla[k])) for k in ("in_specs", "out_specs", "scratch_shapes")]
    return pl.pallas_call(
        functools.partial(_mixers_kernel, ncp, *counts),
        grid=(m // CHUNK,),
        in_specs=ssd["in_specs"] + gla["in_specs"],
        out_specs=ssd["out_specs"] + gla["out_specs"],
        out_shape=ssd["out_shape"] + gla["out_shape"],
        scratch_shapes=ssd["scratch_shapes"] + gla["scratch_shapes"],
        compiler_params=_cparams(("arbitrary",), 40 << 20),
        name="mixers",
    )(*ssd["args"], *gla["args"])


def _attn_prep_kernel(npt, dq_ref, dk_ref, dv_ref, qg_ref, kg_ref, seg_ref,
                      qa_ref, qb_ref, kn_ref, kb_ref, vt_ref, kp_ref, vp_ref, ks_ref, vs_ref):
    seg = seg_ref[...]
    tm = dq_ref.shape[0]

    def qknorm(x, g):
        ms = _select_dot(x * x, seg) * (1.0 / DIFF_HD)
        return x * lax.rsqrt(ms + EPS) * g

    qn = qknorm(dq_ref[...], qg_ref[...]) * (DIFF_HD ** -0.5 * LOG2E)
    first = (lax.broadcasted_iota(jnp.int32, qn.shape, 1) // DIFF_HD) % 2 == 0
    qa = jnp.where(first, qn, 0.0).astype(BF16)
    qb = jnp.where(first, 0.0, qn).astype(BF16)
    kn = qknorm(dk_ref[...], kg_ref[...])
    kn_ref[...] = kn
    kb = kn.astype(BF16)
    for h in range(DIFF_HEADS):
        sl = slice(h * LANE, (h + 1) * LANE)
        qa_ref[h] = qa[:, sl]
        qb_ref[h] = qb[:, sl]
        kb_ref[h] = kb[:, sl]
    dv = dv_ref[...]
    vt_ref[...] = dv.T.astype(BF16)

    def cache_rows(k_ref, v_ref):
        for h in range(DIFF_HEADS):
            k_ref[pl.ds(h, tm, stride=DIFF_HEADS), :] = kn[:, h * LANE:(h + 1) * LANE]
            v_ref[pl.ds(h, tm, stride=DIFF_HEADS), :] = dv[:, h * LANE:(h + 1) * LANE]

    @pl.when(pl.program_id(0) < npt)
    def _():
        cache_rows(kp_ref, vp_ref)

    @pl.when(pl.program_id(0) >= npt)
    def _():
        cache_rows(ks_ref, vs_ref)


def _attn_prep(proj, lp, qg, kg, segmat):
    m = proj.shape[0]
    ms = m - lp
    tm = _pick(math.gcd(lp, ms), (512, 256, 128))
    w = DIFF_WIDTH
    npt = lp // tm
    pmap, smap = _split_maps(npt)
    row = lambda off: pl.BlockSpec((tm, w), lambda i: (i, off // w))
    const2 = lambda i: (0, 0)
    rows4 = tm * DIFF_HEADS
    hm = pl.BlockSpec((DIFF_HEADS, tm, LANE), lambda i: (0, i, 0))
    return pl.pallas_call(
        functools.partial(_attn_prep_kernel, npt),
        grid=(m // tm,),
        in_specs=[row(C_DQ), row(C_DK), row(C_DV),
                  pl.BlockSpec((1, w), const2), pl.BlockSpec((1, w), const2),
                  pl.BlockSpec((w, w), const2)],
        out_specs=[hm, hm, pl.BlockSpec((tm, w), lambda i: (i, 0)), hm, pl.BlockSpec((w, tm), lambda i: (0, i))]
        + [pl.BlockSpec((rows4, LANE), pmap)] * 2 + [pl.BlockSpec((rows4, LANE), smap)] * 2,
        out_shape=[jax.ShapeDtypeStruct((DIFF_HEADS, m, LANE), BF16), jax.ShapeDtypeStruct((DIFF_HEADS, m, LANE), BF16),
                   jax.ShapeDtypeStruct((m, w), F32), jax.ShapeDtypeStruct((DIFF_HEADS, m, LANE), BF16),
                   jax.ShapeDtypeStruct((w, m), BF16)]
        + [jax.ShapeDtypeStruct((lp * DIFF_HEADS, LANE), F32)] * 2
        + [jax.ShapeDtypeStruct((ms * DIFF_HEADS, LANE), F32)] * 2,
        compiler_params=_cparams(("arbitrary",), 40 << 20),
        name="attn_prep",
    )(proj, proj, proj, qg, kg, segmat)


def _lambda_full(lam_ref, lam_init):
    l = lam_ref[...]
    a = jnp.sum(l[0:1] * l[1:2], axis=-1, keepdims=True)
    b = jnp.sum(l[2:3] * l[3:4], axis=-1, keepdims=True)
    return jnp.exp(a) - jnp.exp(b) + lam_init


ATTN_PEEK = 64
ATTN_MAX_SUM = 2.0 ** 64
ATTN_MIN_SUM = 2.0 ** -60


def _fold8(x, op):
    r = x[0:SUBLANE]
    for i in range(1, x.shape[0] // SUBLANE):
        r = op(r, x[i * SUBLANE:(i + 1) * SUBLANE])
    return r


def _attn_prompt_kernel(t, kbk, lam_init, qi_ref, ki_ref, slope_ref, qa_ref, qb_ref, kb_ref, vt_ref, lam_ref,
                        og_ref, o_ref, m_scr, l_scr, acc_scr, b0_scr, bd_scr):
    p = pl.program_id(0)
    qi = qi_ref[p]
    ki = ki_ref[p]
    nb = t // kbk
    w = 2 * t
    nt = (((1,), (1,)), ((), ()))

    @pl.when(p == 0)
    def _():
        b0_scr[...] = lax.broadcasted_iota(jnp.int32, (kbk, LANE), 0).astype(F32)
        kr = lax.broadcasted_iota(jnp.int32, (t, w), 0)
        lane = lax.broadcasted_iota(jnp.int32, (t, w), 1)
        qc = (lane // (2 * kbk)) * kbk + lane % kbk
        corr = jnp.minimum(2 * (qc - kr), 0).astype(F32)
        bd_scr[...] = jnp.where(kr // CHUNK <= qc // CHUNK, corr, -jnp.inf)

    @pl.when(ki == 0)
    def _():
        m_scr[...] = jnp.full(m_scr.shape, -jnp.inf, F32)
        l_scr[...] = jnp.zeros(l_scr.shape, F32)
        acc_scr[...] = jnp.zeros(acc_scr.shape, F32)

    def tile(h, diag):
        slope = slope_ref[h]
        shift = slope * ((ki - qi) * t).astype(F32)
        step = slope * float(kbk)
        qcat = jnp.concatenate([r[h, g * kbk:(g + 1) * kbk, :] for g in range(nb) for r in (qa_ref, qb_ref)],
                               axis=0)
        b0 = slope * b0_scr[...]
        b0 = jnp.concatenate([b0] * (w // LANE), axis=1)
        m_h, l_h, acc_h = m_scr.at[h], l_scr.at[h], acc_scr.at[h]

        def scores(rows, nrows, lo):
            s = lax.dot_general(kb_ref[h, rows, :], qcat[lo:, :], nt, preferred_element_type=F32) + b0[0:nrows, lo:]
            if diag:
                s = s + slope * bd_scr[rows, lo:]
            return s

        m_old = m_h[...]
        peek = scores(slice(0, ATTN_PEEK), ATTN_PEEK, 0)
        m_used = jnp.maximum(m_old, jnp.max(peek, axis=0, keepdims=True) + shift)
        if diag:
            lane = lax.broadcasted_iota(jnp.int32, (1, w), 1)
            qpos = (lane // (2 * kbk)) * kbk + lane % kbk
            ramp = jnp.maximum(qpos - (ATTN_PEEK - 1), 0).astype(F32)
        else:
            ramp = float(t - ATTN_PEEK)
        m_ref = m_used + slope * ramp
        ones = jnp.ones((SUBLANE, kbk), BF16)
        pv = None
        for b in range(nb):
            rows = slice(b * kbk, (b + 1) * kbk)
            lo = 2 * b * kbk if diag else 0
            e = scores(rows, kbk, lo) - (m_ref[:, lo:] - shift - b * step)
            pt = jnp.exp2(e).astype(BF16)
            d = jnp.dot(jnp.concatenate([vt_ref[h, :, rows], ones], axis=0), pt,
                        preferred_element_type=F32)
            if b == 0:
                pv = d
            elif lo == 0:
                pv = d + pv
            else:
                pv = jnp.concatenate([pv[:, :lo], d + pv[:, lo:]], axis=1)
        lsum = pv[DIFF_VD:DIFF_VD + 1, :]
        pv = pv[0:DIFF_VD, :]
        ok = jnp.logical_and(jnp.max(lsum) <= ATTN_MAX_SUM, jnp.min(lsum) >= ATTN_MIN_SUM)

        @pl.when(ok)
        def _():
            a_old = jnp.exp2(m_old - m_ref)
            l_h[...] = a_old * l_h[...] + lsum
            acc_h[...] = a_old * acc_h[...] + pv
            m_h[...] = m_ref

        @pl.when(jnp.logical_not(ok))
        def _():
            def body(b, carry):
                rows = pl.ds(pl.multiple_of(b * kbk, kbk), kbk)
                s = scores(rows, kbk, 0) + (shift + slope * (b * kbk).astype(F32))
                m_o = m_h[...]
                m_n = jnp.maximum(m_o, jnp.max(s, axis=0, keepdims=True))
                pt = jnp.exp2(s - m_n)
                a = jnp.exp2(m_o - m_n)
                l_h[...] = a * l_h[...] + jnp.sum(pt, axis=0, keepdims=True)
                acc_h[...] = a * acc_h[...] + jnp.dot(vt_ref[h, :, rows], pt.astype(BF16),
                                                      preferred_element_type=F32)
                m_h[...] = m_n
                return carry

            lax.fori_loop(0, nb, body, 0)

    def head(h, carry):
        @pl.when(ki < qi)
        def _():
            tile(h, False)

        @pl.when(ki == qi)
        def _():
            tile(h, True)
            lam = _lambda_full(lam_ref, lam_init)
            on = acc_scr[h] / l_scr[h]
            ot = jnp.concatenate(
                [on[:, 2 * g * kbk:(2 * g + 1) * kbk] - lam * on[:, (2 * g + 1) * kbk:(2 * g + 2) * kbk]
                 for g in range(nb)], axis=1)
            ms = jnp.mean(ot * ot, axis=0, keepdims=True)
            o_ref[h] = (ot * lax.rsqrt(ms + EPS)).T * (og_ref[...] * (1.0 - lam_init))

        return carry

    lax.fori_loop(0, DIFF_HEADS, head, 0)


def _attn_prompt(lp, lam_init, qa, qb, kb, vt, slopes, lam_p, og):
    t = _pick(lp, (1024, 512, 256))
    kbk = min(t, 512)
    nq = lp // t
    pairs = [(i, j) for i in range(nq) for j in range(i + 1)]
    qi_l = jnp.asarray(np.array([a for a, _ in pairs], np.int32))
    ki_l = jnp.asarray(np.array([b for _, b in pairs], np.int32))
    nh = DIFF_HEADS
    grid_spec = pltpu.PrefetchScalarGridSpec(
        num_scalar_prefetch=3,
        grid=(len(pairs),),
        in_specs=[pl.BlockSpec((nh, t, LANE), lambda p, qi, ki, sl: (0, qi[p], 0)),
                  pl.BlockSpec((nh, t, LANE), lambda p, qi, ki, sl: (0, qi[p], 0)),
                  pl.BlockSpec((nh, t, LANE), lambda p, qi, ki, sl: (0, ki[p], 0)),
                  pl.BlockSpec((nh, DIFF_VD, t), lambda p, qi, ki, sl: (0, 0, ki[p])),
                  pl.BlockSpec((4, DIFF_HD), lambda p, qi, ki, sl: (0, 0)),
                  pl.BlockSpec((1, DIFF_VD), lambda p, qi, ki, sl: (0, 0))],
        out_specs=pl.BlockSpec((nh, t, LANE), lambda p, qi, ki, sl: (0, qi[p], 0)),
        scratch_shapes=[pltpu.VMEM((nh, 1, 2 * t), F32), pltpu.VMEM((nh, 1, 2 * t), F32),
                        pltpu.VMEM((nh, DIFF_VD, 2 * t), F32),
                        pltpu.VMEM((kbk, LANE), F32), pltpu.VMEM((t, 2 * t), F32)])
    return pl.pallas_call(
        functools.partial(_attn_prompt_kernel, t, kbk, lam_init),
        grid_spec=grid_spec,
        out_shape=jax.ShapeDtypeStruct((nh, lp, LANE), F32),
        compiler_params=_cparams(("arbitrary",), 48 << 20),
        name="attn_prompt",
    )(qi_l, ki_l, slopes, qa, qb, kb, vt.reshape(nh, DIFF_VD, vt.shape[1]), lam_p, og)


def _attn_sample_kernel(past, lam_init, slope_ref, qa_ref, qb_ref, kn_ref, vn_ref, kc_ref, vc_ref, lam_ref, og_ref,
                        o_ref):
    q = CHUNK
    lam = _lambda_full(lam_ref, lam_init)
    kpos = lax.broadcasted_iota(jnp.int32, (1, past), 1).astype(F32)
    r = lax.broadcasted_iota(jnp.int32, (q, q), 0)
    cc = lax.broadcasted_iota(jnp.int32, (q, q), 1)
    new_bias = (past + r - jnp.abs(r - cc)).astype(F32)
    outs = []
    for h in range(DIFF_HEADS):
        slope = slope_ref[h]
        sl = slice(h * LANE, (h + 1) * LANE)
        kc = kc_ref[pl.ds(h, past, stride=DIFF_HEADS), :].astype(BF16)
        vc = vc_ref[pl.ds(h, past, stride=DIFF_HEADS), :].astype(BF16)
        kn = kn_ref[:, sl].astype(BF16)
        vn = vn_ref[:, sl].astype(BF16)
        res = []
        for q_ref in (qa_ref, qb_ref):
            qv = q_ref[h]
            s_c = lax.dot_general(qv, kc, (((1,), (1,)), ((), ())), preferred_element_type=F32) + slope * kpos
            s_n = lax.dot_general(qv, kn, (((1,), (1,)), ((), ())), preferred_element_type=F32) + slope * new_bias
            mx = jnp.maximum(jnp.max(s_c, axis=-1, keepdims=True), jnp.max(s_n, axis=-1, keepdims=True))
            p_c = jnp.exp2(s_c - mx)
            p_n = jnp.exp2(s_n - mx)
            den = jnp.sum(p_c, axis=-1, keepdims=True) + jnp.sum(p_n, axis=-1, keepdims=True)
            num = (jnp.dot(p_c.astype(BF16), vc, preferred_element_type=F32)
                   + jnp.dot(p_n.astype(BF16), vn, preferred_element_type=F32))
            res.append(num / den)
        o = res[0] - lam * res[1]
        ms = jnp.mean(o * o, axis=-1, keepdims=True)
        outs.append(o * lax.rsqrt(ms + EPS) * (og_ref[...] * (1.0 - lam_init)))
    o_ref[...] = jnp.concatenate(outs, axis=1)


def _attn_sample(lp, nb, layer, lam_init, qa, qb, kn, proj, cache_k, cache_v, slopes, lam_p, og):
    depth, _, past = cache_k.shape[:3]
    q = CHUNK
    w = DIFF_WIDTH
    off = lp // q
    cache_k = cache_k.reshape(depth, nb, past * DIFF_HEADS, DIFF_VD)
    cache_v = cache_v.reshape(depth, nb, past * DIFF_HEADS, DIFF_VD)
    cache_block = (None, None, past * DIFF_HEADS, DIFF_VD)
    grid_spec = pltpu.PrefetchScalarGridSpec(
        num_scalar_prefetch=1,
        grid=(nb,),
        in_specs=[pl.BlockSpec((DIFF_HEADS, q, LANE), lambda b, sl: (0, off + b, 0)),
                  pl.BlockSpec((DIFF_HEADS, q, LANE), lambda b, sl: (0, off + b, 0)),
                  pl.BlockSpec((q, w), lambda b, sl: (off + b, 0)),
                  pl.BlockSpec((q, w), lambda b, sl: (off + b, C_DV // w)),
                  pl.BlockSpec(cache_block, lambda b, sl: (layer, b, 0, 0)),
                  pl.BlockSpec(cache_block, lambda b, sl: (layer, b, 0, 0)),
                  pl.BlockSpec((4, DIFF_HD), lambda b, sl: (0, 0)),
                  pl.BlockSpec((1, DIFF_VD), lambda b, sl: (0, 0))],
        out_specs=pl.BlockSpec((q, w), lambda b, sl: (b, 0)))
    return pl.pallas_call(
        functools.partial(_attn_sample_kernel, past, lam_init),
        grid_spec=grid_spec,
        out_shape=jax.ShapeDtypeStruct((nb * q, DIFF_WIDTH), F32),
        compiler_params=_cparams(("arbitrary",), 32 << 20),
        name="attn_sample",
    )(slopes, qa, qb, kn, proj, cache_k, cache_v, lam_p, og)


def _out_proj_kernel(npt, xp_ref, xs_ref, ys_ref, yap_ref, yas_ref, yg_ref, w_ref, o_ref):
    is_prompt = pl.program_id(0) < npt
    yap = jnp.concatenate([yap_ref[h] for h in range(DIFF_HEADS)], axis=1)
    ya = jnp.where(is_prompt, yap, yas_ref[...])
    acc = jnp.dot(ys_ref[...].astype(BF16), w_ref[0:SSD_WIDTH, :], preferred_element_type=F32)
    acc = acc + jnp.dot(ya.astype(BF16), w_ref[SSD_WIDTH:SSD_WIDTH + DIFF_WIDTH, :],
                        preferred_element_type=F32)
    acc = acc + jnp.dot(yg_ref[...].astype(BF16), w_ref[SSD_WIDTH + DIFF_WIDTH:, :],
                        preferred_element_type=F32)
    o_ref[...] = jnp.where(is_prompt, xp_ref[...], xs_ref[...]) + acc


def _out_proj(xp, xs, ys, yap, yas, yg, w):
    lp, ms = xp.shape[0], xs.shape[0]
    tm = _pick(math.gcd(lp, ms), (512, 256, 128, 64))
    npt = lp // tm
    pmap, smap = _split_maps(npt)
    vmem = 2 * (5 * tm * D_MODEL * 4) + D_MODEL * D_MODEL * 2 + (4 << 20)
    return pl.pallas_call(
        functools.partial(_out_proj_kernel, npt),
        grid=((lp + ms) // tm,),
        in_specs=[pl.BlockSpec((tm, D_MODEL), pmap),
                  pl.BlockSpec((tm, D_MODEL), smap),
                  pl.BlockSpec((tm, SSD_WIDTH), lambda i: (i, 0)),
                  pl.BlockSpec((DIFF_HEADS, tm, LANE), lambda i: (0, jnp.minimum(i, npt - 1), 0)),
                  pl.BlockSpec((tm, DIFF_WIDTH), smap),
                  pl.BlockSpec((tm, GLA_WIDTH), lambda i: (i, 0)),
                  pl.BlockSpec((D_MODEL, D_MODEL), lambda i: (0, 0), pipeline_mode=pl.Buffered(1))],
        out_specs=pl.BlockSpec((tm, D_MODEL), lambda i: (i, 0)),
        out_shape=jax.ShapeDtypeStruct((lp + ms, D_MODEL), F32),
        compiler_params=_cparams(("parallel",), vmem),
        name="out_proj",
    )(xp, xs, ys, yap, yas, yg, w)


def _mlp_kernel(npt, x_ref, g_ref, w1_ref, w2_ref, op_ref, os_ref, h_ref):
    i = pl.program_id(0)
    first = pl.program_id(1) == 0

    @pl.when(first)
    def _():
        h_ref[...] = _rms_bf16(x_ref[...], g_ref[...])

    a = jnp.dot(h_ref[...], w1_ref[...], preferred_element_type=F32)
    a = jnp.square(jnp.maximum(a, 0.0)).astype(BF16)

    def accumulate(o_ref):
        @pl.when(first)
        def _():
            o_ref[...] = x_ref[...]

        o_ref[...] += jnp.dot(a, w2_ref[...], preferred_element_type=F32)

    @pl.when(i < npt)
    def _():
        accumulate(op_ref)

    @pl.when(i >= npt)
    def _():
        accumulate(os_ref)


def _mlp(x, lp, g, w1, w2):
    m = x.shape[0]
    ms = m - lp
    tm = _pick(math.gcd(lp, ms), (512, 256, 128, 64))
    tf = 1024
    npt = lp // tm
    pmap, smap = _split_maps(npt)
    vmem = 6 * tm * D_MODEL * 4 + tm * D_MODEL * 2 + 4 * D_MODEL * tf * 2 + 2 * tm * tf * 4 + (4 << 20)
    return pl.pallas_call(
        functools.partial(_mlp_kernel, npt),
        grid=(m // tm, D_FF // tf),
        in_specs=[pl.BlockSpec((tm, D_MODEL), lambda i, f: (i, 0)),
                  pl.BlockSpec((1, D_MODEL), lambda i, f: (0, 0)),
                  pl.BlockSpec((D_MODEL, tf), lambda i, f: (0, f)),
                  pl.BlockSpec((tf, D_MODEL), lambda i, f: (f, 0))],
        out_specs=[pl.BlockSpec((tm, D_MODEL), pmap), pl.BlockSpec((tm, D_MODEL), smap)],
        out_shape=[jax.ShapeDtypeStruct((lp, D_MODEL), F32), jax.ShapeDtypeStruct((ms, D_MODEL), F32)],
        scratch_shapes=[pltpu.VMEM((tm, D_MODEL), BF16)],
        compiler_params=_cparams(("arbitrary", "arbitrary"), vmem),
        name="mlp",
    )(x, g, w1, w2)


def _pack_w_in(w):
    return jnp.concatenate(
        [w[:, 0:2560], w[:, 2576:4112], w[:, 4624:5136], w[:, 5152:5664], w[:, 4112:4624],
         w[:, 2560:2576], w[:, 5136:5152], jnp.zeros((w.shape[0], PACKED - 5664), w.dtype)],
        axis=1).astype(BF16)


def _pad_lanes(v, width=LANE):
    v = v.reshape(1, -1)
    return jnp.pad(v, ((0, 0), (0, width - v.shape[1])))


def kernel(x_prompt, x_sample, cache_diff_k, cache_diff_v, state_ssd_conv, state_ssd, state_gla, norm1_g, w_in, ssd_conv_w, ssd_conv_b, ssd_dt_bias, ssd_a_log, ssd_d, ssd_norm_g, diff_qn_g, diff_kn_g, diff_lambda, diff_out_g, gla_wa2, gla_ba, gla_norm_g, w_out, norm2_g, w_mlp1, w_mlp2):
    bp, lp, d = x_prompt.shape
    nb, ls, _ = x_sample.shape
    depth = w_in.shape[0]
    past = cache_diff_k.shape[2]
    assert bp == 1 and d == D_MODEL and ls == CHUNK and lp % CHUNK == 0 and past % CHUNK == 0
    ncp = lp // CHUNK
    m = lp + nb * ls

    xp, xs = x_prompt.reshape(lp, d), x_sample.reshape(nb * ls, d)

    emat = (jnp.arange(LANE)[:, None] == (jnp.arange(SSD_WIDTH)[None, :] // SSD_HEAD_DIM)).astype(BF16)
    segmat = ((jnp.arange(DIFF_WIDTH)[:, None] // DIFF_HD) == (jnp.arange(DIFF_WIDTH)[None, :] // DIFF_HD)).astype(BF16)
    slopes = jnp.exp2(-8.0 * jnp.arange(1, DIFF_HEADS + 1, dtype=F32) / DIFF_HEADS) * LOG2E
    eye_h = jnp.eye(GLA_HEADS, dtype=F32)

    outs = {k: [] for k in ("kp", "vp", "cp", "hp", "sp", "ks", "vs", "cs", "hs", "ss")}
    for l in range(depth):
        lam_init = 0.8 - 0.6 * math.exp(-0.3 * l)
        proj = _in_proj(xp, xs, norm1_g[l].reshape(1, d), _pack_w_in(w_in[l]))

        st_all = jnp.concatenate([jnp.zeros((1,) + state_ssd.shape[2:], F32), state_ssd[l]], axis=0)
        st_in = st_all.reshape(nb + 1, SSD_WIDTH, SSD_STATE).transpose(0, 2, 1)
        cprev = jnp.concatenate([jnp.zeros((1, SSD_CONV - 1, SSD_CONV_DIM), F32), state_ssd_conv[l]], axis=0)
        cprev = jnp.pad(cprev, ((0, 0), (8 - (SSD_CONV - 1), 0), (0, 0)))
        cw, cb = ssd_conv_w[l], ssd_conv_b[l].reshape(1, -1)
        ssd_call = _ssd_call(
            proj, ncp, st_in, cprev[:, :, :SSD_WIDTH], cprev[:, :, SSD_WIDTH:],
            cw[:, :SSD_WIDTH], cb[:, :SSD_WIDTH], cw[:, SSD_WIDTH:], cb[:, SSD_WIDTH:],
            _pad_lanes(ssd_dt_bias[l]), _pad_lanes(ssd_a_log[l]),
            jnp.repeat(ssd_d[l], SSD_HEAD_DIM).reshape(1, -1), ssd_norm_g[l].reshape(1, -1), emat)
        xbc_raw = proj[:, C_XS:C_XS + SSD_CONV_DIM]
        outs["cp"].append(xbc_raw[lp - (SSD_CONV - 1):lp][None])
        outs["cs"].append(xbc_raw[lp:].reshape(nb, ls, SSD_CONV_DIM)[:, ls - (SSD_CONV - 1):])

        s_all = jnp.concatenate([jnp.zeros((1,) + state_gla.shape[2:], F32), state_gla[l]], axis=0)
        s_in = jnp.einsum('bhkv,hg->bhvgk', s_all, eye_h).reshape(nb + 1, GLA_WIDTH, GLA_HEADS * GLA_DK)
        wa_pad = jnp.zeros((LANE, GLA_HEADS * GLA_DK), F32).at[GLA_RANK:2 * GLA_RANK].set(gla_wa2[l]).astype(BF16)
        gla_call = _gla_call(proj, ncp, s_in, wa_pad, gla_ba[l].reshape(1, -1), gla_norm_g[l].reshape(1, -1))
        y_ssd, st_out, y_gla, s_out = _mixers(ncp, ssd_call, gla_call)
        h_all = st_out.transpose(0, 2, 1).reshape(nb + 1, SSD_HEADS, SSD_HEAD_DIM, SSD_STATE)
        outs["hp"].append(h_all[0:1])
        outs["hs"].append(h_all[1:])
        s5 = s_out.reshape(nb + 1, GLA_HEADS, GLA_DV, GLA_HEADS, GLA_DK)
        s_new = jnp.stack([s5[:, h, :, h, :] for h in range(GLA_HEADS)], axis=1).transpose(0, 1, 3, 2)
        outs["sp"].append(s_new[0:1])
        outs["ss"].append(s_new[1:])

        qg = jnp.tile(diff_qn_g[l], 2 * DIFF_HEADS).reshape(1, -1)
        kg = jnp.tile(diff_kn_g[l], 2 * DIFF_HEADS).reshape(1, -1)
        qa, qb, kn, kb, vt, kp4, vp4, ks4, vs4 = _attn_prep(proj, lp, qg, kg, segmat)
        og = diff_out_g[l].reshape(1, -1)
        ya_p = _attn_prompt(lp, lam_init, qa, qb, kb, vt, slopes, diff_lambda[l], og)
        ya_s = _attn_sample(lp, nb, l, lam_init, qa, qb, kn, proj, cache_diff_k, cache_diff_v,
                            slopes, diff_lambda[l], og)
        outs["kp"].append(kp4.reshape(1, lp, DIFF_HEADS, 2 * DIFF_HD))
        outs["ks"].append(ks4.reshape(nb, ls, DIFF_HEADS, 2 * DIFF_HD))
        outs["vp"].append(vp4.reshape(1, lp, DIFF_HEADS, DIFF_VD))
        outs["vs"].append(vs4.reshape(nb, ls, DIFF_HEADS, DIFF_VD))

        x1 = _out_proj(xp, xs, y_ssd, ya_p, ya_s, y_gla, w_out[l].astype(BF16))
        xp, xs = _mlp(x1, lp, norm2_g[l].reshape(1, d), w_mlp1[l].astype(BF16), w_mlp2[l].astype(BF16))

    st = {k: jnp.stack(v) for k, v in outs.items()}
    return (xp.reshape(1, lp, d), xs.reshape(nb, ls, d),
            st["kp"], st["vp"], st["cp"], st["hp"], st["sp"],
            st["ks"], st["vs"], st["cs"], st["hs"], st["ss"])
```

```python
import functools
import math

import numpy as np
import jax
import jax.numpy as jnp
from jax import lax
from jax.experimental import pallas as pl
from jax.experimental.pallas import tpu as pltpu

F32 = jnp.float32
BF16 = jnp.bfloat16

D_MODEL = 2048
CHUNK = 64
SSD_HEAD_DIM = 64
SSD_WIDTH = 1024
SSD_HEADS = 16
SSD_GROUPS = 2
SSD_STATE = 128
SSD_CONV = 4
SSD_CONV_DIM = SSD_WIDTH + 2 * SSD_GROUPS * SSD_STATE
DIFF_HD = 64
DIFF_VD = 128
DIFF_WIDTH = 512
DIFF_HEADS = 4
GLA_DK = 64
GLA_DV = 128
GLA_WIDTH = 512
GLA_HEADS = 4
GLA_RANK = 16
GLA_TAU = 16.0
GLA_BLOCK = 16
D_FF = 4 * D_MODEL
EPS = 1e-6

LOG2E = math.log2(math.e)
LANE = 128
SUBLANE = 8
VMEM_CAP = 56 << 20

C_Z, C_XS, C_BC, C_DQ, C_DK, C_DV, C_GV, C_GG, C_GQ, C_GK, C_SM = (
    0, 1024, 2048, 2560, 3072, 3584, 4096, 4608, 5120, 5376, 5632)
PACKED = 5760
PROJ_TN = 1920


def _pick(m, cands):
    for c in cands:
        if m % c == 0:
            return c
    raise ValueError(f"no tile for {m}")


def _cparams(sem, vmem_bytes):
    return pltpu.CompilerParams(dimension_semantics=sem,
                                vmem_limit_bytes=int(min(VMEM_CAP, max(vmem_bytes, 16 << 20))))


def _sigmoid(x):
    return 1.0 / (1.0 + jnp.exp(-x))


def _split3(x):
    hi = x.astype(BF16)
    r1 = x - hi.astype(F32)
    mid = r1.astype(BF16)
    lo = (r1 - mid.astype(F32)).astype(BF16)
    return hi, mid, lo


def _select_dot(x, sel):
    return sum(jnp.dot(p, sel, preferred_element_type=F32) for p in _split3(x))


def _dot_select(sel, x):
    return sum(jnp.dot(sel, p, preferred_element_type=F32) for p in _split3(x))


def _softplus(x):
    return jnp.maximum(x, 0.0) + jnp.log1p(jnp.exp(-jnp.abs(x)))


def _rms_bf16(x, g):
    ms = jnp.mean(x * x, axis=-1, keepdims=True)
    return (x * lax.rsqrt(ms + EPS) * g).astype(BF16)


def _in_proj_kernel(npt, xp_ref, xs_ref, g_ref, w_ref, o_ref, h_ref):
    i = pl.program_id(0)

    @pl.when(jnp.logical_and(pl.program_id(1) == 0, i < npt))
    def _():
        h_ref[...] = _rms_bf16(xp_ref[...], g_ref[...])

    @pl.when(jnp.logical_and(pl.program_id(1) == 0, i >= npt))
    def _():
        h_ref[...] = _rms_bf16(xs_ref[...], g_ref[...])

    o_ref[...] = jnp.dot(h_ref[...], w_ref[...], preferred_element_type=F32)


def _split_maps(npt):
    return (lambda i, *_: (jnp.minimum(i, npt - 1), 0)), (lambda i, *_: (jnp.maximum(i - npt, 0), 0))


def _row_split(xp, xs, lp_rows):
    if xs is None:
        return xp, xp, lp_rows, xp.shape[0] - lp_rows, xp.shape[0]
    return xp, xs, xp.shape[0], xs.shape[0], xp.shape[0]


def _in_proj(xp, xs, lp_rows, g, w, layer):
    xp, xs, lp, ms, first_rows = _row_split(xp, xs, lp_rows)
    tm = _pick(math.gcd(lp, ms), (512, 256, 128, 64))
    tn = PROJ_TN
    npt = first_rows // tm
    pmap, smap = _split_maps(npt)
    vmem = 4 * tm * D_MODEL * 4 + tm * D_MODEL * 2 + 2 * D_MODEL * tn * 2 + 2 * tm * tn * 4 + (4 << 20)
    return pl.pallas_call(
        functools.partial(_in_proj_kernel, npt),
        grid=((lp + ms) // tm, PACKED // tn),
        in_specs=[pl.BlockSpec((tm, D_MODEL), pmap),
                  pl.BlockSpec((tm, D_MODEL), smap),
                  pl.BlockSpec((1, D_MODEL), lambda i, j: (0, 0)),
                  pl.BlockSpec((None, D_MODEL, tn), lambda i, j: (layer, 0, j))],
        out_specs=pl.BlockSpec((tm, tn), lambda i, j: (i, j)),
        out_shape=jax.ShapeDtypeStruct((lp + ms, PACKED), F32),
        scratch_shapes=[pltpu.VMEM((tm, D_MODEL), BF16)],
        compiler_params=_cparams(("parallel", "arbitrary"), vmem),
        name="in_proj",
    )(xp, xs, g, w)


def _ssd_kernel(ncp, z_ref, xs_ref, bc_ref, sm_ref, st_in_ref, cpx_ref, cpb_ref,
                cwx_ref, cbx_ref, cwb_ref, cbb_ref, dtb_ref, alog_ref, dexp_ref, ng_ref, e_ref,
                y_ref, st_out_ref, st_scr, fx_scr, fb_scr):
    c = pl.program_id(0)
    q = CHUNK

    @pl.when(jnp.logical_or(c == 0, c >= ncp))
    def _():
        st_scr[...] = st_in_ref[0]
        fx_scr[0:8, :] = cpx_ref[0]
        fb_scr[0:8, :] = cpb_ref[0]

    def conv(u_ref, f_scr, w_ref, b_ref):
        u = u_ref[...]
        f_scr[8:8 + q, :] = u
        y = b_ref[...] + f_scr[5:5 + q, :] * w_ref[0:1, :]
        for j in range(1, SSD_CONV):
            y = y + f_scr[5 + j:5 + j + q, :] * w_ref[j:j + 1, :]
        f_scr[0:8, :] = u[q - 8:q, :]
        return y * _sigmoid(y)

    yield
    dt = _softplus(sm_ref[...] + dtb_ref[...])
    da = dt * (-jnp.exp(alog_ref[...]))
    r64 = lax.broadcasted_iota(jnp.int32, (q, q), 0)
    c64 = lax.broadcasted_iota(jnp.int32, (q, q), 1)
    tril = (c64 <= r64).astype(BF16)
    cs = _dot_select(tril, da)
    yield
    xs = conv(xs_ref, fx_scr, cwx_ref, cbx_ref)
    yield
    both = _select_dot(jnp.concatenate([dt, cs], axis=0), e_ref[...])
    dt_e = both[0:q]
    cs_e = both[q:2 * q]
    yield
    bcv = conv(bc_ref, fb_scr, cwb_ref, cbb_ref)
    yield

    row = lax.broadcasted_iota(jnp.int32, (q, SSD_WIDTH), 0)
    sidx = jnp.bitwise_and(lax.broadcasted_iota(jnp.int32, (q, SSD_WIDTH), 1), q - 1)
    cs_row = jnp.sum(jnp.where(sidx == row, cs_e, 0.0), axis=0, keepdims=True)
    cs_last = cs_e[q - 1:q, :]
    lmat = jnp.exp(jnp.where(sidx <= row, cs_e - cs_row, -jnp.inf))
    yield

    xdt = xs * dt_e
    xdt_end = (xdt * jnp.exp(cs_last - cs_e)).astype(BF16)
    xdt_b = xdt.astype(BF16)
    yield
    ecs = jnp.exp(cs_e)
    st = st_scr[...]
    st_b = st.astype(BF16)
    bcb = bcv.astype(BF16)
    yield

    r128 = lax.broadcasted_iota(jnp.int32, (2 * q, LANE), 0)
    c128 = lax.broadcasted_iota(jnp.int32, (2 * q, LANE), 1)
    bd2 = (r128 // q) == (c128 // q)

    hg = SSD_HEADS // SSD_GROUPS
    gw = hg * SSD_HEAD_DIM
    y_parts = []
    new_states = []
    for g in range(SSD_GROUPS):
        bm = bcb[:, g * SSD_STATE:(g + 1) * SSD_STATE]
        cm = bcb[:, (SSD_GROUPS + g) * SSD_STATE:(SSD_GROUPS + g + 1) * SSD_STATE]
        bm_rep = jnp.concatenate([bm] * hg, axis=0)
        cbt = lax.dot_general(cm, bm_rep, (((1,), (1,)), ((), ())),
                              preferred_element_type=F32)
        m_all = (cbt * lmat[:, g * gw:(g + 1) * gw]).astype(BF16)
        y_off = jnp.dot(cm, st_b[:, g * gw:(g + 1) * gw], preferred_element_type=F32)
        yield
        for j in range(gw // LANE):
            col = g * gw + j * LANE
            xj = xdt_b[:, col:col + LANE]
            xd = jnp.where(bd2, jnp.concatenate([xj, xj], axis=0), jnp.zeros((), BF16))
            y_parts.append(jnp.dot(m_all[:, j * LANE:(j + 1) * LANE], xd, preferred_element_type=F32)
                           + y_off[:, j * LANE:(j + 1) * LANE] * ecs[:, col:col + LANE])
            yield
        new_states.append(lax.dot_general(bm, xdt_end[:, g * gw:(g + 1) * gw], (((0,), (0,)), ((), ())),
                                          preferred_element_type=F32))
        yield
    y = jnp.concatenate(y_parts, axis=1)
    st_new = st * jnp.exp(cs_last) + jnp.concatenate(new_states, axis=1)
    st_scr[...] = st_new
    st_out_ref[0] = st_new
    yield

    y = y + dexp_ref[...] * xs
    zv = z_ref[...]
    y = y * (zv * _sigmoid(zv))
    yield
    outs = []
    for g in range(SSD_GROUPS):
        yg = y[:, g * gw:(g + 1) * gw]
        ms = jnp.mean(yg * yg, axis=-1, keepdims=True)
        outs.append(yg * lax.rsqrt(ms + EPS))
    y_ref[...] = jnp.concatenate(outs, axis=1) * ng_ref[...]


def _ssd_call(proj, ncp, st_in, cpx, cpb, cwx, cbx, cwb, cbb, dtb, alog, dexp, ng, emat):
    m = proj.shape[0]
    q = CHUNK
    smap = lambda c: (jnp.maximum(c - (ncp - 1), 0), 0, 0)
    const2 = lambda c: (0, 0)
    return dict(
        in_specs=[pl.BlockSpec((q, 1024), lambda c: (c, C_Z // 1024)),
                  pl.BlockSpec((q, 1024), lambda c: (c, C_XS // 1024)),
                  pl.BlockSpec((q, 512), lambda c: (c, C_BC // 512)),
                  pl.BlockSpec((q, LANE), lambda c: (c, C_SM // LANE)),
                  pl.BlockSpec((1, SSD_STATE, SSD_WIDTH), smap),
                  pl.BlockSpec((1, 8, 1024), smap),
                  pl.BlockSpec((1, 8, 512), smap),
                  pl.BlockSpec((SSD_CONV, 1024), const2),
                  pl.BlockSpec((1, 1024), const2),
                  pl.BlockSpec((SSD_CONV, 512), const2),
                  pl.BlockSpec((1, 512), const2),
                  pl.BlockSpec((1, LANE), const2),
                  pl.BlockSpec((1, LANE), const2),
                  pl.BlockSpec((1, SSD_WIDTH), const2),
                  pl.BlockSpec((1, SSD_WIDTH), const2),
                  pl.BlockSpec((LANE, SSD_WIDTH), const2)],
        out_specs=[pl.BlockSpec((q, SSD_WIDTH), lambda c: (c, 0)),
                   pl.BlockSpec((1, SSD_STATE, SSD_WIDTH), smap)],
        out_shape=[jax.ShapeDtypeStruct((m, SSD_WIDTH), F32),
                   jax.ShapeDtypeStruct(st_in.shape, F32)],
        scratch_shapes=[pltpu.VMEM((SSD_STATE, SSD_WIDTH), F32),
                        pltpu.VMEM((8 + q, 1024), F32),
                        pltpu.VMEM((8 + q, 512), F32)],
        args=(proj, proj, proj, proj, st_in, cpx, cpb, cwx, cbx, cwb, cbb, dtb, alog, dexp, ng, emat))


def _gla_kernel(ncp, gq_ref, gk_ref, gv_ref, gg_ref, sm_ref, s_in_ref, wa_ref, ba_ref, ng_ref,
                o_ref, s_out_ref, s_scr):
    c = pl.program_id(0)
    q = CHUNK
    blk = GLA_BLOCK
    kw = GLA_HEADS * GLA_DK

    @pl.when(jnp.logical_or(c == 0, c >= ncp))
    def _():
        s_scr[...] = s_in_ref[0]

    yield
    pre = jnp.dot(sm_ref[...].astype(BF16), wa_ref[...], preferred_element_type=F32) + ba_ref[...]
    yield
    log_a = -_softplus(-pre) * (1.0 / GLA_TAU)
    r64 = lax.broadcasted_iota(jnp.int32, (q, q), 0)
    c64 = lax.broadcasted_iota(jnp.int32, (q, q), 1)
    same_blk = (r64 // blk) == (c64 // blk)
    causal = jnp.logical_and(same_blk, c64 <= r64)
    yield
    b = _dot_select(causal.astype(BF16), log_a)
    yield
    gk = gk_ref[...]
    qt = gq_ref[...] * (GLA_DK ** -0.5) * jnp.exp(b)
    kt = (gk * jnp.exp(-b)).astype(BF16)
    yield
    b_last = jnp.concatenate(
        [jnp.broadcast_to(b[(j + 1) * blk - 1:(j + 1) * blk, :], (blk, kw)) for j in range(q // blk)], axis=0)
    ktil = (gk * jnp.exp(b_last - b)).astype(BF16)
    qt_b = qt.astype(BF16)
    gv_b = gv_ref[...].astype(BF16)
    yield

    lane_head = lax.broadcasted_iota(jnp.int32, (q, kw), 1) // GLA_DK
    o_intra = []
    for h in range(GLA_HEADS):
        qm = jnp.where(lane_head == h, qt_b, jnp.zeros((), BF16))
        att = lax.dot_general(qm, kt, (((1,), (1,)), ((), ())), preferred_element_type=F32)
        yield
        att = jnp.where(causal, att, 0.0).astype(BF16)
        o_intra.append(jnp.dot(att, gv_b[:, h * GLA_DV:(h + 1) * GLA_DV], preferred_element_type=F32))
        yield

    rs = lax.broadcasted_iota(jnp.int32, (GLA_WIDTH, kw), 0) // GLA_DV
    cs_ = lax.broadcasted_iota(jnp.int32, (GLA_WIDTH, kw), 1) // GLA_DK
    bd = rs == cs_
    sw = s_scr[...]
    o_inter = []
    for j in range(q // blk):
        sl = slice(j * blk, (j + 1) * blk)
        o_inter.append(lax.dot_general(qt_b[sl], sw.astype(BF16), (((1,), (1,)), ((), ())),
                                       preferred_element_type=F32))
        yield
        upd = lax.dot_general(gv_b[sl], ktil[sl], (((0,), (0,)), ((), ())),
                              preferred_element_type=F32)
        yield
        decay = jnp.exp(b[(j + 1) * blk - 1:(j + 1) * blk, :])
        sw = sw * decay + jnp.where(bd, upd, 0.0)
        yield
    s_scr[...] = sw
    s_out_ref[0] = sw
    o = jnp.concatenate(o_intra, axis=1) + jnp.concatenate(o_inter, axis=0)
    yield

    gg = gg_ref[...]
    gate = gg * _sigmoid(gg)
    outs = []
    for h in range(GLA_HEADS):
        oh = o[:, h * GLA_DV:(h + 1) * GLA_DV]
        ms = jnp.mean(oh * oh, axis=-1, keepdims=True)
        outs.append(oh * lax.rsqrt(ms + EPS) * ng_ref[...])
    o_ref[...] = jnp.concatenate(outs, axis=1) * gate


def _gla_call(proj, ncp, s_in, wa_pad, ba, ng):
    m = proj.shape[0]
    q = CHUNK
    kw = GLA_HEADS * GLA_DK
    smap = lambda c: (jnp.maximum(c - (ncp - 1), 0), 0, 0)
    const2 = lambda c: (0, 0)
    return dict(
        in_specs=[pl.BlockSpec((q, kw), lambda c: (c, C_GQ // kw)),
                  pl.BlockSpec((q, kw), lambda c: (c, C_GK // kw)),
                  pl.BlockSpec((q, GLA_WIDTH), lambda c: (c, C_GV // GLA_WIDTH)),
                  pl.BlockSpec((q, GLA_WIDTH), lambda c: (c, C_GG // GLA_WIDTH)),
                  pl.BlockSpec((q, LANE), lambda c: (c, C_SM // LANE)),
                  pl.BlockSpec((1, GLA_WIDTH, kw), smap),
                  pl.BlockSpec((LANE, kw), const2),
                  pl.BlockSpec((1, kw), const2),
                  pl.BlockSpec((1, GLA_DV), const2)],
        out_specs=[pl.BlockSpec((q, GLA_WIDTH), lambda c: (c, 0)),
                   pl.BlockSpec((1, GLA_WIDTH, kw), smap)],
        out_shape=[jax.ShapeDtypeStruct((m, GLA_WIDTH), F32),
                   jax.ShapeDtypeStruct(s_in.shape, F32)],
        scratch_shapes=[pltpu.VMEM((GLA_WIDTH, kw), F32)],
        args=(proj, proj, proj, proj, proj, s_in, wa_pad, ba, ng))


def _mixers_kernel(ncp, n_in, n_out, n_scr, *refs):
    i0, i1 = n_in
    o0, o1 = n_out
    s0, s1 = n_scr
    ins, outs, scr = refs[:i0 + i1], refs[i0 + i1:i0 + i1 + o0 + o1], refs[i0 + i1 + o0 + o1:]
    streams = [(_gla_kernel(ncp, *ins[i0:], *outs[o0:], *scr[s0:]), 1),
               (_ssd_kernel(ncp, *ins[:i0], *outs[:o0], *scr[:s0]), 2)]
    while streams:
        for entry in list(streams):
            for _ in range(entry[1]):
                if next(entry[0], StopIteration) is StopIteration:
                    streams.remove(entry)
                    break


def _mixers(ncp, ssd, gla):
    m = ssd["args"][0].shape[0]
    counts = [(len(ssd[k]), len(gla[k])) for k in ("in_specs", "out_specs", "scratch_shapes")]
    return pl.pallas_call(
        functools.partial(_mixers_kernel, ncp, *counts),
        grid=(m // CHUNK,),
        in_specs=ssd["in_specs"] + gla["in_specs"],
        out_specs=ssd["out_specs"] + gla["out_specs"],
        out_shape=ssd["out_shape"] + gla["out_shape"],
        scratch_shapes=ssd["scratch_shapes"] + gla["scratch_shapes"],
        compiler_params=_cparams(("arbitrary",), 40 << 20),
        name="mixers",
    )(*ssd["args"], *gla["args"])


def _attn_prep_kernel(npt, dq_ref, dk_ref, dv_ref, qg_ref, kg_ref, seg_ref,
                      qa_ref, qb_ref, kn_ref, kb_ref, vt_ref, kp_ref, vp_ref, ks_ref, vs_ref):
    seg = seg_ref[...]
    tm = dq_ref.shape[0]

    def qknorm(x, g):
        ms = _select_dot(x * x, seg) * (1.0 / DIFF_HD)
        return x * lax.rsqrt(ms + EPS) * g

    qn = qknorm(dq_ref[...], qg_ref[...]) * (DIFF_HD ** -0.5 * LOG2E)
    first = (lax.broadcasted_iota(jnp.int32, qn.shape, 1) // DIFF_HD) % 2 == 0
    qa = jnp.where(first, qn, 0.0).astype(BF16)
    qb = jnp.where(first, 0.0, qn).astype(BF16)
    kn = qknorm(dk_ref[...], kg_ref[...])
    kn_ref[...] = kn
    kb = kn.astype(BF16)
    for h in range(DIFF_HEADS):
        sl = slice(h * LANE, (h + 1) * LANE)
        qa_ref[h] = qa[:, sl]
        qb_ref[h] = qb[:, sl]
        kb_ref[h] = kb[:, sl]
    dv = dv_ref[...]
    vt_ref[...] = dv.T.astype(BF16)

    def cache_rows(k_ref, v_ref):
        for h in range(DIFF_HEADS):
            k_ref[pl.ds(h, tm, stride=DIFF_HEADS), :] = kn[:, h * LANE:(h + 1) * LANE]
            v_ref[pl.ds(h, tm, stride=DIFF_HEADS), :] = dv[:, h * LANE:(h + 1) * LANE]

    @pl.when(pl.program_id(0) < npt)
    def _():
        cache_rows(kp_ref, vp_ref)

    @pl.when(pl.program_id(0) >= npt)
    def _():
        cache_rows(ks_ref, vs_ref)


def _attn_prep(proj, lp, qg, kg, segmat):
    m = proj.shape[0]
    ms = m - lp
    tm = _pick(math.gcd(lp, ms), (512, 256, 128))
    w = DIFF_WIDTH
    npt = lp // tm
    pmap, smap = _split_maps(npt)
    row = lambda off: pl.BlockSpec((tm, w), lambda i: (i, off // w))
    const2 = lambda i: (0, 0)
    rows4 = tm * DIFF_HEADS
    hm = pl.BlockSpec((DIFF_HEADS, tm, LANE), lambda i: (0, i, 0))
    return pl.pallas_call(
        functools.partial(_attn_prep_kernel, npt),
        grid=(m // tm,),
        in_specs=[row(C_DQ), row(C_DK), row(C_DV),
                  pl.BlockSpec((1, w), const2), pl.BlockSpec((1, w), const2),
                  pl.BlockSpec((w, w), const2)],
        out_specs=[hm, hm, pl.BlockSpec((tm, w), lambda i: (i, 0)), hm, pl.BlockSpec((w, tm), lambda i: (0, i))]
        + [pl.BlockSpec((rows4, LANE), pmap)] * 2 + [pl.BlockSpec((rows4, LANE), smap)] * 2,
        out_shape=[jax.ShapeDtypeStruct((DIFF_HEADS, m, LANE), BF16), jax.ShapeDtypeStruct((DIFF_HEADS, m, LANE), BF16),
                   jax.ShapeDtypeStruct((m, w), F32), jax.ShapeDtypeStruct((DIFF_HEADS, m, LANE), BF16),
                   jax.ShapeDtypeStruct((w, m), BF16)]
        + [jax.ShapeDtypeStruct((lp * DIFF_HEADS, LANE), F32)] * 2
        + [jax.ShapeDtypeStruct((ms * DIFF_HEADS, LANE), F32)] * 2,
        compiler_params=_cparams(("arbitrary",), 40 << 20),
        name="attn_prep",
    )(proj, proj, proj, qg, kg, segmat)


def _lambda_full(lam_ref, lam_init):
    l = lam_ref[...]
    a = jnp.sum(l[0:1] * l[1:2], axis=-1, keepdims=True)
    b = jnp.sum(l[2:3] * l[3:4], axis=-1, keepdims=True)
    return jnp.exp(a) - jnp.exp(b) + lam_init


ATTN_PEEK = 64
ATTN_MAX_SUM = 2.0 ** 64
ATTN_MIN_SUM = 2.0 ** -60


def _fold8(x, op):
    r = x[0:SUBLANE]
    for i in range(1, x.shape[0] // SUBLANE):
        r = op(r, x[i * SUBLANE:(i + 1) * SUBLANE])
    return r


def _attn_prompt_kernel(t, kbk, lam_init, qi_ref, ki_ref, slope_ref, qa_ref, qb_ref, kb_ref, vt_ref, lam_ref,
                        og_ref, o_ref, m_scr, l_scr, acc_scr, b0_scr, bd_scr):
    p = pl.program_id(0)
    qi = qi_ref[p]
    ki = ki_ref[p]
    nb = t // kbk
    w = 2 * t
    nt = (((1,), (1,)), ((), ()))

    @pl.when(p == 0)
    def _():
        b0_scr[...] = lax.broadcasted_iota(jnp.int32, (kbk, LANE), 0).astype(F32)
        kr = lax.broadcasted_iota(jnp.int32, (t, w), 0)
        lane = lax.broadcasted_iota(jnp.int32, (t, w), 1)
        qc = (lane // (2 * kbk)) * kbk + lane % kbk
        corr = jnp.minimum(2 * (qc - kr), 0).astype(F32)
        bd_scr[...] = jnp.where(kr // CHUNK <= qc // CHUNK, corr, -jnp.inf)

    @pl.when(ki == 0)
    def _():
        m_scr[...] = jnp.full(m_scr.shape, -jnp.inf, F32)
        l_scr[...] = jnp.zeros(l_scr.shape, F32)
        acc_scr[...] = jnp.zeros(acc_scr.shape, F32)

    def tile(h, diag):
        slope = slope_ref[h]
        shift = slope * ((ki - qi) * t).astype(F32)
        step = slope * float(kbk)
        qcat = jnp.concatenate([r[h, g * kbk:(g + 1) * kbk, :] for g in range(nb) for r in (qa_ref, qb_ref)],
                               axis=0)
        b0 = slope * b0_scr[...]
        b0 = jnp.concatenate([b0] * (w // LANE), axis=1)
        m_h, l_h, acc_h = m_scr.at[h], l_scr.at[h], acc_scr.at[h]

        def scores(rows, nrows, lo):
            s = lax.dot_general(kb_ref[h, rows, :], qcat[lo:, :], nt, preferred_element_type=F32) + b0[0:nrows, lo:]
            if diag:
                s = s + slope * bd_scr[rows, lo:]
            return s

        m_old = m_h[...]
        peek = scores(slice(0, ATTN_PEEK), ATTN_PEEK, 0)
        m_used = jnp.maximum(m_old, jnp.max(peek, axis=0, keepdims=True) + shift)
        if diag:
            lane = lax.broadcasted_iota(jnp.int32, (1, w), 1)
            qpos = (lane // (2 * kbk)) * kbk + lane % kbk
            ramp = jnp.maximum(qpos - (ATTN_PEEK - 1), 0).astype(F32)
        else:
            ramp = float(t - ATTN_PEEK)
        m_ref = m_used + slope * ramp
        ones = jnp.ones((SUBLANE, kbk), BF16)
        pv = None
        los = [2 * b * kbk if diag else 0 for b in range(nb)]
        sc = [scores(slice(b * kbk, (b + 1) * kbk), kbk, los[b]) for b in range(nb)]
        for b in range(nb):
            rows = slice(b * kbk, (b + 1) * kbk)
            lo = los[b]
            e = sc[b] - (m_ref[:, lo:] - shift - b * step)
            pt = jnp.exp2(e).astype(BF16)
            d = jnp.dot(jnp.concatenate([vt_ref[h, :, rows], ones], axis=0), pt,
                        preferred_element_type=F32)
            if b == 0:
                pv = d
            elif lo == 0:
                pv = d + pv
            else:
                pv = jnp.concatenate([pv[:, :lo], d + pv[:, lo:]], axis=1)
        lsum = pv[DIFF_VD:DIFF_VD + 1, :]
        pv = pv[0:DIFF_VD, :]
        ok = jnp.logical_and(jnp.max(lsum) <= ATTN_MAX_SUM, jnp.min(lsum) >= ATTN_MIN_SUM)

        @pl.when(ok)
        def _():
            a_old = jnp.exp2(m_old - m_ref)
            l_h[...] = a_old * l_h[...] + lsum
            acc_h[...] = a_old * acc_h[...] + pv
            m_h[...] = m_ref

        @pl.when(jnp.logical_not(ok))
        def _():
            def body(b, carry):
                rows = pl.ds(pl.multiple_of(b * kbk, kbk), kbk)
                s = scores(rows, kbk, 0) + (shift + slope * (b * kbk).astype(F32))
                m_o = m_h[...]
                m_n = jnp.maximum(m_o, jnp.max(s, axis=0, keepdims=True))
                pt = jnp.exp2(s - m_n)
                a = jnp.exp2(m_o - m_n)
                l_h[...] = a * l_h[...] + jnp.sum(pt, axis=0, keepdims=True)
                acc_h[...] = a * acc_h[...] + jnp.dot(vt_ref[h, :, rows], pt.astype(BF16),
                                                      preferred_element_type=F32)
                m_h[...] = m_n
                return carry

            lax.fori_loop(0, nb, body, 0)

    def head(h, carry):
        @pl.when(ki < qi)
        def _():
            tile(h, False)

        @pl.when(ki == qi)
        def _():
            tile(h, True)
            lam = _lambda_full(lam_ref, lam_init)
            on = acc_scr[h] / l_scr[h]
            ot = jnp.concatenate(
                [on[:, 2 * g * kbk:(2 * g + 1) * kbk] - lam * on[:, (2 * g + 1) * kbk:(2 * g + 2) * kbk]
                 for g in range(nb)], axis=1)
            ms = jnp.mean(ot * ot, axis=0, keepdims=True)
            o_ref[h] = (ot * lax.rsqrt(ms + EPS)).T * (og_ref[...] * (1.0 - lam_init))

        return carry

    lax.fori_loop(0, DIFF_HEADS, head, 0)


def _attn_prompt(lp, lam_init, qa, qb, kb, vt, slopes, lam_p, og):
    t = _pick(lp, (1024, 512, 256))
    kbk = min(t, 512)
    nq = lp // t
    pairs = [(i, j) for i in range(nq) for j in range(i + 1)]
    qi_l = jnp.asarray(np.array([a for a, _ in pairs], np.int32))
    ki_l = jnp.asarray(np.array([b for _, b in pairs], np.int32))
    nh = DIFF_HEADS
    grid_spec = pltpu.PrefetchScalarGridSpec(
        num_scalar_prefetch=3,
        grid=(len(pairs),),
        in_specs=[pl.BlockSpec((nh, t, LANE), lambda p, qi, ki, sl: (0, qi[p], 0)),
                  pl.BlockSpec((nh, t, LANE), lambda p, qi, ki, sl: (0, qi[p], 0)),
                  pl.BlockSpec((nh, t, LANE), lambda p, qi, ki, sl: (0, ki[p], 0)),
                  pl.BlockSpec((nh, DIFF_VD, t), lambda p, qi, ki, sl: (0, 0, ki[p])),
                  pl.BlockSpec((4, DIFF_HD), lambda p, qi, ki, sl: (0, 0)),
                  pl.BlockSpec((1, DIFF_VD), lambda p, qi, ki, sl: (0, 0))],
        out_specs=pl.BlockSpec((nh, t, LANE), lambda p, qi, ki, sl: (0, qi[p], 0)),
        scratch_shapes=[pltpu.VMEM((nh, 1, 2 * t), F32), pltpu.VMEM((nh, 1, 2 * t), F32),
                        pltpu.VMEM((nh, DIFF_VD, 2 * t), F32),
                        pltpu.VMEM((kbk, LANE), F32), pltpu.VMEM((t, 2 * t), F32)])
    return pl.pallas_call(
        functools.partial(_attn_prompt_kernel, t, kbk, lam_init),
        grid_spec=grid_spec,
        out_shape=jax.ShapeDtypeStruct((nh, lp, LANE), F32),
        compiler_params=_cparams(("arbitrary",), 48 << 20),
        name="attn_prompt",
    )(qi_l, ki_l, slopes, qa, qb, kb, vt.reshape(nh, DIFF_VD, vt.shape[1]), lam_p, og)


def _attn_sample_kernel(past, lam_init, slope_ref, qa_ref, qb_ref, kn_ref, vn_ref, kc_ref, vc_ref, lam_ref, og_ref,
                        o_ref):
    q = CHUNK
    lam = _lambda_full(lam_ref, lam_init)
    kpos = lax.broadcasted_iota(jnp.int32, (1, past), 1).astype(F32)
    r = lax.broadcasted_iota(jnp.int32, (q, q), 0)
    cc = lax.broadcasted_iota(jnp.int32, (q, q), 1)
    new_bias = (past + r - jnp.abs(r - cc)).astype(F32)
    outs = []
    for h in range(DIFF_HEADS):
        slope = slope_ref[h]
        sl = slice(h * LANE, (h + 1) * LANE)
        kc = kc_ref[pl.ds(h, past, stride=DIFF_HEADS), :].astype(BF16)
        vc = vc_ref[pl.ds(h, past, stride=DIFF_HEADS), :].astype(BF16)
        kn = kn_ref[:, sl].astype(BF16)
        vn = vn_ref[:, sl].astype(BF16)
        res = []
        for q_ref in (qa_ref, qb_ref):
            qv = q_ref[h]
            s_c = lax.dot_general(qv, kc, (((1,), (1,)), ((), ())), preferred_element_type=F32) + slope * kpos
            s_n = lax.dot_general(qv, kn, (((1,), (1,)), ((), ())), preferred_element_type=F32) + slope * new_bias
            mx = jnp.maximum(jnp.max(s_c, axis=-1, keepdims=True), jnp.max(s_n, axis=-1, keepdims=True))
            p_c = jnp.exp2(s_c - mx)
            p_n = jnp.exp2(s_n - mx)
            den = jnp.sum(p_c, axis=-1, keepdims=True) + jnp.sum(p_n, axis=-1, keepdims=True)
            num = (jnp.dot(p_c.astype(BF16), vc, preferred_element_type=F32)
                   + jnp.dot(p_n.astype(BF16), vn, preferred_element_type=F32))
            res.append(num / den)
        o = res[0] - lam * res[1]
        ms = jnp.mean(o * o, axis=-1, keepdims=True)
        outs.append(o * lax.rsqrt(ms + EPS) * (og_ref[...] * (1.0 - lam_init)))
    o_ref[...] = jnp.concatenate(outs, axis=1)


def _attn_sample(lp, nb, layer, lam_init, qa, qb, kn, proj, cache_k, cache_v, slopes, lam_p, og):
    depth, _, past = cache_k.shape[:3]
    q = CHUNK
    w = DIFF_WIDTH
    off = lp // q
    cache_k = cache_k.reshape(depth, nb, past * DIFF_HEADS, DIFF_VD)
    cache_v = cache_v.reshape(depth, nb, past * DIFF_HEADS, DIFF_VD)
    cache_block = (None, None, past * DIFF_HEADS, DIFF_VD)
    grid_spec = pltpu.PrefetchScalarGridSpec(
        num_scalar_prefetch=1,
        grid=(nb,),
        in_specs=[pl.BlockSpec((DIFF_HEADS, q, LANE), lambda b, sl: (0, off + b, 0)),
                  pl.BlockSpec((DIFF_HEADS, q, LANE), lambda b, sl: (0, off + b, 0)),
                  pl.BlockSpec((q, w), lambda b, sl: (off + b, 0)),
                  pl.BlockSpec((q, w), lambda b, sl: (off + b, C_DV // w)),
                  pl.BlockSpec(cache_block, lambda b, sl: (layer, b, 0, 0)),
                  pl.BlockSpec(cache_block, lambda b, sl: (layer, b, 0, 0)),
                  pl.BlockSpec((4, DIFF_HD), lambda b, sl: (0, 0)),
                  pl.BlockSpec((1, DIFF_VD), lambda b, sl: (0, 0))],
        out_specs=pl.BlockSpec((q, w), lambda b, sl: (b, 0)))
    return pl.pallas_call(
        functools.partial(_attn_sample_kernel, past, lam_init),
        grid_spec=grid_spec,
        out_shape=jax.ShapeDtypeStruct((nb * q, DIFF_WIDTH), F32),
        compiler_params=_cparams(("arbitrary",), 32 << 20),
        name="attn_sample",
    )(slopes, qa, qb, kn, proj, cache_k, cache_v, lam_p, og)


def _out_proj_kernel(npt_x, npt, xp_ref, xs_ref, ys_ref, yap_ref, yas_ref, yg_ref, w_ref, o_ref):
    is_prompt = pl.program_id(0) < npt
    yap = jnp.concatenate([yap_ref[h] for h in range(DIFF_HEADS)], axis=1)
    ya = jnp.where(is_prompt, yap, yas_ref[...])
    acc = jnp.dot(ys_ref[...].astype(BF16), w_ref[0:SSD_WIDTH, :], preferred_element_type=F32)
    acc = acc + jnp.dot(ya.astype(BF16), w_ref[SSD_WIDTH:SSD_WIDTH + DIFF_WIDTH, :],
                        preferred_element_type=F32)
    acc = acc + jnp.dot(yg_ref[...].astype(BF16), w_ref[SSD_WIDTH + DIFF_WIDTH:, :],
                        preferred_element_type=F32)
    o_ref[...] = jnp.where(pl.program_id(0) < npt_x, xp_ref[...], xs_ref[...]) + acc


def _out_proj(xp, xs, lp_rows, ys, yap, yas, yg, w, layer):
    xp, xs, lp, ms, first_rows = _row_split(xp, xs, lp_rows)
    tm = _pick(math.gcd(lp, ms), (512, 256, 128, 64))
    npt = lp // tm
    pmap, smap = _split_maps(npt)
    xpmap, xsmap = _split_maps(first_rows // tm)
    vmem = 2 * (5 * tm * D_MODEL * 4) + D_MODEL * D_MODEL * 2 + (4 << 20)
    return pl.pallas_call(
        functools.partial(_out_proj_kernel, first_rows // tm, npt),
        grid=((lp + ms) // tm,),
        in_specs=[pl.BlockSpec((tm, D_MODEL), xpmap),
                  pl.BlockSpec((tm, D_MODEL), xsmap),
                  pl.BlockSpec((tm, SSD_WIDTH), lambda i: (i, 0)),
                  pl.BlockSpec((DIFF_HEADS, tm, LANE), lambda i: (0, jnp.minimum(i, npt - 1), 0)),
                  pl.BlockSpec((tm, DIFF_WIDTH), smap),
                  pl.BlockSpec((tm, GLA_WIDTH), lambda i: (i, 0)),
                  pl.BlockSpec((None, D_MODEL, D_MODEL), lambda i: (layer, 0, 0), pipeline_mode=pl.Buffered(1))],
        out_specs=pl.BlockSpec((tm, D_MODEL), lambda i: (i, 0)),
        out_shape=jax.ShapeDtypeStruct((lp + ms, D_MODEL), F32),
        compiler_params=_cparams(("parallel",), vmem),
        name="out_proj",
    )(xp, xs, ys, yap, yas, yg, w)


def _mlp_kernel(x_ref, g_ref, w1_ref, w2_ref, o_ref, h_ref):
    @pl.when(pl.program_id(1) == 0)
    def _():
        x = x_ref[...]
        h_ref[...] = _rms_bf16(x, g_ref[...])
        o_ref[...] = x

    h = h_ref[...]
    half = w1_ref.shape[1] // 2
    a0 = jnp.dot(h, w1_ref[:, 0:half], preferred_element_type=F32)
    a1 = jnp.dot(h, w1_ref[:, half:], preferred_element_type=F32)
    a0 = jnp.square(jnp.maximum(a0, 0.0)).astype(BF16)
    upd = jnp.dot(a0, w2_ref[0:half, :], preferred_element_type=F32)
    a1 = jnp.square(jnp.maximum(a1, 0.0)).astype(BF16)
    upd = jnp.dot(a1, w2_ref[half:, :], preferred_element_type=F32) + upd
    o_ref[...] += upd


def _mlp(x, g, w1, w2, layer):
    m = x.shape[0]
    tm = _pick(m, (512, 256, 128, 64))
    tf = 1024
    vmem = 4 * tm * D_MODEL * 4 + tm * D_MODEL * 2 + 4 * D_MODEL * tf * 2 + 3 * tm * tf * 4 + tm * D_MODEL * 4 + (4 << 20)
    return pl.pallas_call(
        _mlp_kernel,
        grid=(m // tm, D_FF // tf),
        in_specs=[pl.BlockSpec((tm, D_MODEL), lambda i, f: (i, 0)),
                  pl.BlockSpec((1, D_MODEL), lambda i, f: (0, 0)),
                  pl.BlockSpec((None, D_MODEL, tf), lambda i, f: (layer, 0, f)),
                  pl.BlockSpec((None, tf, D_MODEL), lambda i, f: (layer, f, 0))],
        out_specs=pl.BlockSpec((tm, D_MODEL), lambda i, f: (i, 0)),
        out_shape=jax.ShapeDtypeStruct((m, D_MODEL), F32),
        scratch_shapes=[pltpu.VMEM((tm, D_MODEL), BF16)],
        compiler_params=_cparams(("parallel", "arbitrary"), vmem),
        name="mlp",
    )(x, g, w1, w2)


def _pack_w_in(w):
    return jnp.concatenate(
        [w[..., 0:2560], w[..., 2576:4112], w[..., 4624:5136], w[..., 5152:5664], w[..., 4112:4624],
         w[..., 2560:2576], w[..., 5136:5152], jnp.zeros(w.shape[:-1] + (PACKED - 5664,), w.dtype)],
        axis=-1).astype(BF16)


def _pad_lanes(v, width=LANE):
    v = v.reshape(1, -1)
    return jnp.pad(v, ((0, 0), (0, width - v.shape[1])))


def kernel(x_prompt, x_sample, cache_diff_k, cache_diff_v, state_ssd_conv, state_ssd, state_gla, norm1_g, w_in, ssd_conv_w, ssd_conv_b, ssd_dt_bias, ssd_a_log, ssd_d, ssd_norm_g, diff_qn_g, diff_kn_g, diff_lambda, diff_out_g, gla_wa2, gla_ba, gla_norm_g, w_out, norm2_g, w_mlp1, w_mlp2):
    bp, lp, d = x_prompt.shape
    nb, ls, _ = x_sample.shape
    depth = w_in.shape[0]
    past = cache_diff_k.shape[2]
    assert bp == 1 and d == D_MODEL and ls == CHUNK and lp % CHUNK == 0 and past % CHUNK == 0
    ncp = lp // CHUNK
    m = lp + nb * ls

    xp, xs = x_prompt.reshape(lp, d), x_sample.reshape(nb * ls, d)

    emat = (jnp.arange(LANE)[:, None] == (jnp.arange(SSD_WIDTH)[None, :] // SSD_HEAD_DIM)).astype(BF16)
    segmat = ((jnp.arange(DIFF_WIDTH)[:, None] // DIFF_HD) == (jnp.arange(DIFF_WIDTH)[None, :] // DIFF_HD)).astype(BF16)
    slopes = jnp.exp2(-8.0 * jnp.arange(1, DIFF_HEADS + 1, dtype=F32) / DIFF_HEADS) * LOG2E
    eye_h = jnp.eye(GLA_HEADS, dtype=F32)
    w_in_b, w_out_b = _pack_w_in(w_in), w_out.astype(BF16)
    w_mlp1_b, w_mlp2_b = w_mlp1.astype(BF16), w_mlp2.astype(BF16)

    outs = {k: [] for k in ("kp", "vp", "cp", "hp", "sp", "ks", "vs", "cs", "hs", "ss")}
    for l in range(depth):
        lam_init = 0.8 - 0.6 * math.exp(-0.3 * l)
        proj = _in_proj(xp, xs, lp, norm1_g[l].reshape(1, d), w_in_b, l)

        st_all = jnp.concatenate([jnp.zeros((1,) + state_ssd.shape[2:], F32), state_ssd[l]], axis=0)
        st_in = st_all.reshape(nb + 1, SSD_WIDTH, SSD_STATE).transpose(0, 2, 1)
        cprev = jnp.concatenate([jnp.zeros((1, SSD_CONV - 1, SSD_CONV_DIM), F32), state_ssd_conv[l]], axis=0)
        cprev = jnp.pad(cprev, ((0, 0), (8 - (SSD_CONV - 1), 0), (0, 0)))
        cw, cb = ssd_conv_w[l], ssd_conv_b[l].reshape(1, -1)
        ssd_call = _ssd_call(
            proj, ncp, st_in, cprev[:, :, :SSD_WIDTH], cprev[:, :, SSD_WIDTH:],
            cw[:, :SSD_WIDTH], cb[:, :SSD_WIDTH], cw[:, SSD_WIDTH:], cb[:, SSD_WIDTH:],
            _pad_lanes(ssd_dt_bias[l]), _pad_lanes(ssd_a_log[l]),
            jnp.repeat(ssd_d[l], SSD_HEAD_DIM).reshape(1, -1), ssd_norm_g[l].reshape(1, -1), emat)
        xbc_raw = proj[:, C_XS:C_XS + SSD_CONV_DIM]
        outs["cp"].append(xbc_raw[lp - (SSD_CONV - 1):lp][None])
        outs["cs"].append(xbc_raw[lp:].reshape(nb, ls, SSD_CONV_DIM)[:, ls - (SSD_CONV - 1):])

        s_all = jnp.concatenate([jnp.zeros((1,) + state_gla.shape[2:], F32), state_gla[l]], axis=0)
        s_in = jnp.einsum('bhkv,hg->bhvgk', s_all, eye_h).reshape(nb + 1, GLA_WIDTH, GLA_HEADS * GLA_DK)
        wa_pad = jnp.zeros((LANE, GLA_HEADS * GLA_DK), F32).at[GLA_RANK:2 * GLA_RANK].set(gla_wa2[l]).astype(BF16)
        gla_call = _gla_call(proj, ncp, s_in, wa_pad, gla_ba[l].reshape(1, -1), gla_norm_g[l].reshape(1, -1))
        y_ssd, st_out, y_gla, s_out = _mixers(ncp, ssd_call, gla_call)
        h_all = st_out.transpose(0, 2, 1).reshape(nb + 1, SSD_HEADS, SSD_HEAD_DIM, SSD_STATE)
        outs["hp"].append(h_all[0:1])
        outs["hs"].append(h_all[1:])
        s5 = s_out.reshape(nb + 1, GLA_HEADS, GLA_DV, GLA_HEADS, GLA_DK)
        s_new = jnp.stack([s5[:, h, :, h, :] for h in range(GLA_HEADS)], axis=1).transpose(0, 1, 3, 2)
        outs["sp"].append(s_new[0:1])
        outs["ss"].append(s_new[1:])

        qg = jnp.tile(diff_qn_g[l], 2 * DIFF_HEADS).reshape(1, -1)
        kg = jnp.tile(diff_kn_g[l], 2 * DIFF_HEADS).reshape(1, -1)
        qa, qb, kn, kb, vt, kp4, vp4, ks4, vs4 = _attn_prep(proj, lp, qg, kg, segmat)
        og = diff_out_g[l].reshape(1, -1)
        ya_p = _attn_prompt(lp, lam_init, qa, qb, kb, vt, slopes, diff_lambda[l], og)
        ya_s = _attn_sample(lp, nb, l, lam_init, qa, qb, kn, proj, cache_diff_k, cache_diff_v,
                            slopes, diff_lambda[l], og)
        outs["kp"].append(kp4.reshape(1, lp, DIFF_HEADS, 2 * DIFF_HD))
        outs["ks"].append(ks4.reshape(nb, ls, DIFF_HEADS, 2 * DIFF_HD))
        outs["vp"].append(vp4.reshape(1, lp, DIFF_HEADS, DIFF_VD))
        outs["vs"].append(vs4.reshape(nb, ls, DIFF_HEADS, DIFF_VD))

        x1 = _out_proj(xp, xs, lp, y_ssd, ya_p, ya_s, y_gla, w_out_b, l)
        xp, xs = _mlp(x1, norm2_g[l].reshape(1, d), w_mlp1_b, w_mlp2_b, l), None

    st = {k: jnp.stack(v) for k, v in outs.items()}
    return (xp[:lp].reshape(1, lp, d), xp[lp:].reshape(nb, ls, d),
            st["kp"], st["vp"], st["cp"], st["hp"], st["sp"],
            st["ks"], st["vs"], st["cs"], st["hs"], st["ss"])
```

```python
import functools
import math

import numpy as np
import jax
import jax.numpy as jnp
from jax import lax
from jax.experimental import pallas as pl
from jax.experimental.pallas import tpu as pltpu

F32 = jnp.float32
BF16 = jnp.bfloat16

D_MODEL = 2048
CHUNK = 64
SSD_HEAD_DIM = 64
SSD_WIDTH = 1024
SSD_HEADS = 16
SSD_GROUPS = 2
SSD_STATE = 128
SSD_CONV = 4
SSD_CONV_DIM = SSD_WIDTH + 2 * SSD_GROUPS * SSD_STATE
DIFF_HD = 64
DIFF_VD = 128
DIFF_WIDTH = 512
DIFF_HEADS = 4
GLA_DK = 64
GLA_DV = 128
GLA_WIDTH = 512
GLA_HEADS = 4
GLA_RANK = 16
GLA_TAU = 16.0
GLA_BLOCK = 16
D_FF = 4 * D_MODEL
EPS = 1e-6

LOG2E = math.log2(math.e)
LANE = 128
SUBLANE = 8
VMEM_CAP = 56 << 20

C_Z, C_XS, C_BC, C_DQ, C_DK, C_DV, C_GV, C_GG, C_GQ, C_GK, C_SM = (
    0, 1024, 2048, 2560, 3072, 3584, 4096, 4608, 5120, 5376, 5632)
PACKED = 5760
PROJ_TN = 1920


def _pick(m, cands):
    for c in cands:
        if m % c == 0:
            return c
    raise ValueError(f"no tile for {m}")


def _cparams(sem, vmem_bytes):
    return pltpu.CompilerParams(dimension_semantics=sem,
                                vmem_limit_bytes=int(min(VMEM_CAP, max(vmem_bytes, 16 << 20))))


def _sigmoid(x):
    return 1.0 / (1.0 + jnp.exp(-x))


def _split3(x):
    hi = x.astype(BF16)
    r1 = x - hi.astype(F32)
    mid = r1.astype(BF16)
    lo = (r1 - mid.astype(F32)).astype(BF16)
    return hi, mid, lo


def _select_dot(x, sel):
    return sum(jnp.dot(p, sel, preferred_element_type=F32) for p in _split3(x))


def _dot_select(sel, x):
    return sum(jnp.dot(sel, p, preferred_element_type=F32) for p in _split3(x))


def _softplus(x):
    return jnp.maximum(x, 0.0) + jnp.log1p(jnp.exp(-jnp.abs(x)))


def _rms_bf16(x, g):
    ms = jnp.mean(x * x, axis=-1, keepdims=True)
    return (x * lax.rsqrt(ms + EPS) * g).astype(BF16)


def _in_proj_kernel(npt, xp_ref, xs_ref, g_ref, w_ref, o_ref, h_ref):
    i = pl.program_id(0)

    @pl.when(jnp.logical_and(pl.program_id(1) == 0, i < npt))
    def _():
        h_ref[...] = _rms_bf16(xp_ref[...], g_ref[...])

    @pl.when(jnp.logical_and(pl.program_id(1) == 0, i >= npt))
    def _():
        h_ref[...] = _rms_bf16(xs_ref[...], g_ref[...])

    o_ref[...] = jnp.dot(h_ref[...], w_ref[...], preferred_element_type=F32)


def _split_maps(npt):
    return (lambda i, *_: (jnp.minimum(i, npt - 1), 0)), (lambda i, *_: (jnp.maximum(i - npt, 0), 0))


def _row_split(xp, xs, lp_rows):
    if xs is None:
        return xp, xp, lp_rows, xp.shape[0] - lp_rows, xp.shape[0]
    return xp, xs, xp.shape[0], xs.shape[0], xp.shape[0]


def _in_proj(xp, xs, lp_rows, g, w, layer):
    xp, xs, lp, ms, first_rows = _row_split(xp, xs, lp_rows)
    tm = _pick(math.gcd(lp, ms), (512, 256, 128, 64))
    tn = PROJ_TN
    npt = first_rows // tm
    pmap, smap = _split_maps(npt)
    vmem = 4 * tm * D_MODEL * 4 + tm * D_MODEL * 2 + 2 * D_MODEL * tn * 2 + 2 * tm * tn * 4 + (4 << 20)
    return pl.pallas_call(
        functools.partial(_in_proj_kernel, npt),
        grid=((lp + ms) // tm, PACKED // tn),
        in_specs=[pl.BlockSpec((tm, D_MODEL), pmap),
                  pl.BlockSpec((tm, D_MODEL), smap),
                  pl.BlockSpec((1, D_MODEL), lambda i, j: (0, 0)),
                  pl.BlockSpec((None, D_MODEL, tn), lambda i, j: (layer, 0, j))],
        out_specs=pl.BlockSpec((tm, tn), lambda i, j: (i, j)),
        out_shape=jax.ShapeDtypeStruct((lp + ms, PACKED), F32),
        scratch_shapes=[pltpu.VMEM((tm, D_MODEL), BF16)],
        compiler_params=_cparams(("parallel", "arbitrary"), vmem),
        name="in_proj",
    )(xp, xs, g, w)


def _ssd_kernel(ncp, z_ref, xs_ref, bc_ref, sm_ref, st_in_ref, cpx_ref, cpb_ref,
                cwx_ref, cbx_ref, cwb_ref, cbb_ref, dtb_ref, alog_ref, dexp_ref, ng_ref, e_ref,
                y_ref, st_out_ref, st_scr, fx_scr, fb_scr):
    c = pl.program_id(0)
    q = CHUNK

    @pl.when(jnp.logical_or(c == 0, c >= ncp))
    def _():
        st_scr[...] = st_in_ref[0]
        fx_scr[0:8, :] = cpx_ref[0]
        fb_scr[0:8, :] = cpb_ref[0]

    def conv(u_ref, f_scr, w_ref, b_ref):
        u = u_ref[...]
        f_scr[8:8 + q, :] = u
        y = b_ref[...] + f_scr[5:5 + q, :] * w_ref[0:1, :]
        for j in range(1, SSD_CONV):
            y = y + f_scr[5 + j:5 + j + q, :] * w_ref[j:j + 1, :]
        f_scr[0:8, :] = u[q - 8:q, :]
        return y * _sigmoid(y)

    yield
    dt = _softplus(sm_ref[...] + dtb_ref[...])
    da = dt * (-jnp.exp(alog_ref[...]))
    r64 = lax.broadcasted_iota(jnp.int32, (q, q), 0)
    c64 = lax.broadcasted_iota(jnp.int32, (q, q), 1)
    tril = (c64 <= r64).astype(BF16)
    cs = _dot_select(tril, da)
    yield
    xs = conv(xs_ref, fx_scr, cwx_ref, cbx_ref)
    yield
    both = _select_dot(jnp.concatenate([dt, cs], axis=0), e_ref[...])
    dt_e = both[0:q]
    cs_e = both[q:2 * q]
    yield
    bcv = conv(bc_ref, fb_scr, cwb_ref, cbb_ref)
    yield

    row = lax.broadcasted_iota(jnp.int32, (q, SSD_WIDTH), 0)
    sidx = jnp.bitwise_and(lax.broadcasted_iota(jnp.int32, (q, SSD_WIDTH), 1), q - 1)
    cs_row = jnp.sum(jnp.where(sidx == row, cs_e, 0.0), axis=0, keepdims=True)
    cs_last = cs_e[q - 1:q, :]
    lmat = jnp.exp(jnp.where(sidx <= row, cs_e - cs_row, -jnp.inf))
    yield

    xdt = xs * dt_e
    xdt_end = (xdt * jnp.exp(cs_last - cs_e)).astype(BF16)
    xdt_b = xdt.astype(BF16)
    yield
    ecs = jnp.exp(cs_e)
    st = st_scr[...]
    st_b = st.astype(BF16)
    bcb = bcv.astype(BF16)
    yield

    r128 = lax.broadcasted_iota(jnp.int32, (2 * q, LANE), 0)
    c128 = lax.broadcasted_iota(jnp.int32, (2 * q, LANE), 1)
    bd2 = (r128 // q) == (c128 // q)

    hg = SSD_HEADS // SSD_GROUPS
    gw = hg * SSD_HEAD_DIM
    y_parts = []
    new_states = []
    for g in range(SSD_GROUPS):
        bm = bcb[:, g * SSD_STATE:(g + 1) * SSD_STATE]
        cm = bcb[:, (SSD_GROUPS + g) * SSD_STATE:(SSD_GROUPS + g + 1) * SSD_STATE]
        bm_rep = jnp.concatenate([bm] * hg, axis=0)
        cbt = lax.dot_general(cm, bm_rep, (((1,), (1,)), ((), ())),
                              preferred_element_type=F32)
        m_all = (cbt * lmat[:, g * gw:(g + 1) * gw]).astype(BF16)
        y_off = jnp.dot(cm, st_b[:, g * gw:(g + 1) * gw], preferred_element_type=F32)
        yield
        for j in range(gw // LANE):
            col = g * gw + j * LANE
            xj = xdt_b[:, col:col + LANE]
            xd = jnp.where(bd2, jnp.concatenate([xj, xj], axis=0), jnp.zeros((), BF16))
            y_parts.append(jnp.dot(m_all[:, j * LANE:(j + 1) * LANE], xd, preferred_element_type=F32)
                           + y_off[:, j * LANE:(j + 1) * LANE] * ecs[:, col:col + LANE])
            yield
        new_states.append(lax.dot_general(bm, xdt_end[:, g * gw:(g + 1) * gw], (((0,), (0,)), ((), ())),
                                          preferred_element_type=F32))
        yield
    y = jnp.concatenate(y_parts, axis=1)
    st_new = st * jnp.exp(cs_last) + jnp.concatenate(new_states, axis=1)
    st_scr[...] = st_new
    st_out_ref[0] = st_new
    yield

    y = y + dexp_ref[...] * xs
    zv = z_ref[...]
    y = y * (zv * _sigmoid(zv))
    yield
    outs = []
    for g in range(SSD_GROUPS):
        yg = y[:, g * gw:(g + 1) * gw]
        ms = jnp.mean(yg * yg, axis=-1, keepdims=True)
        outs.append(yg * lax.rsqrt(ms + EPS))
    y_ref[...] = jnp.concatenate(outs, axis=1) * ng_ref[...]


def _ssd_call(proj, ncp, st_in, cpx, cpb, cwx, cbx, cwb, cbb, dtb, alog, dexp, ng, emat):
    m = proj.shape[0]
    q = CHUNK
    smap = lambda c: (jnp.maximum(c - (ncp - 1), 0), 0, 0)
    const2 = lambda c: (0, 0)
    return dict(
        in_specs=[pl.BlockSpec((q, 1024), lambda c: (c, C_Z // 1024)),
                  pl.BlockSpec((q, 1024), lambda c: (c, C_XS // 1024)),
                  pl.BlockSpec((q, 512), lambda c: (c, C_BC // 512)),
                  pl.BlockSpec((q, LANE), lambda c: (c, C_SM // LANE)),
                  pl.BlockSpec((1, SSD_STATE, SSD_WIDTH), smap),
                  pl.BlockSpec((1, 8, 1024), smap),
                  pl.BlockSpec((1, 8, 512), smap),
                  pl.BlockSpec((SSD_CONV, 1024), const2),
                  pl.BlockSpec((1, 1024), const2),
                  pl.BlockSpec((SSD_CONV, 512), const2),
                  pl.BlockSpec((1, 512), const2),
                  pl.BlockSpec((1, LANE), const2),
                  pl.BlockSpec((1, LANE), const2),
                  pl.BlockSpec((1, SSD_WIDTH), const2),
                  pl.BlockSpec((1, SSD_WIDTH), const2),
                  pl.BlockSpec((LANE, SSD_WIDTH), const2)],
        out_specs=[pl.BlockSpec((q, SSD_WIDTH), lambda c: (c, 0)),
                   pl.BlockSpec((1, SSD_STATE, SSD_WIDTH), smap)],
        out_shape=[jax.ShapeDtypeStruct((m, SSD_WIDTH), F32),
                   jax.ShapeDtypeStruct(st_in.shape, F32)],
        scratch_shapes=[pltpu.VMEM((SSD_STATE, SSD_WIDTH), F32),
                        pltpu.VMEM((8 + q, 1024), F32),
                        pltpu.VMEM((8 + q, 512), F32)],
        args=(proj, proj, proj, proj, st_in, cpx, cpb, cwx, cbx, cwb, cbb, dtb, alog, dexp, ng, emat))


def _gla_kernel(ncp, gq_ref, gk_ref, gv_ref, gg_ref, sm_ref, s_in_ref, wa_ref, ba_ref, ng_ref,
                o_ref, s_out_ref, s_scr):
    c = pl.program_id(0)
    q = CHUNK
    blk = GLA_BLOCK
    kw = GLA_HEADS * GLA_DK

    @pl.when(jnp.logical_or(c == 0, c >= ncp))
    def _():
        s_scr[...] = s_in_ref[0]

    yield
    pre = jnp.dot(sm_ref[...].astype(BF16), wa_ref[...], preferred_element_type=F32) + ba_ref[...]
    yield
    log_a = -_softplus(-pre) * (1.0 / GLA_TAU)
    r64 = lax.broadcasted_iota(jnp.int32, (q, q), 0)
    c64 = lax.broadcasted_iota(jnp.int32, (q, q), 1)
    same_blk = (r64 // blk) == (c64 // blk)
    causal = jnp.logical_and(same_blk, c64 <= r64)
    yield
    b = _dot_select(causal.astype(BF16), log_a)
    yield
    gk = gk_ref[...]
    qt = gq_ref[...] * (GLA_DK ** -0.5) * jnp.exp(b)
    kt = (gk * jnp.exp(-b)).astype(BF16)
    yield
    b_last = jnp.concatenate(
        [jnp.broadcast_to(b[(j + 1) * blk - 1:(j + 1) * blk, :], (blk, kw)) for j in range(q // blk)], axis=0)
    ktil = (gk * jnp.exp(b_last - b)).astype(BF16)
    qt_b = qt.astype(BF16)
    gv_b = gv_ref[...].astype(BF16)
    yield

    lane_head = lax.broadcasted_iota(jnp.int32, (q, kw), 1) // GLA_DK
    o_intra = []
    for h in range(GLA_HEADS):
        qm = jnp.where(lane_head == h, qt_b, jnp.zeros((), BF16))
        att = lax.dot_general(qm, kt, (((1,), (1,)), ((), ())), preferred_element_type=F32)
        yield
        att = jnp.where(causal, att, 0.0).astype(BF16)
        o_intra.append(jnp.dot(att, gv_b[:, h * GLA_DV:(h + 1) * GLA_DV], preferred_element_type=F32))
        yield

    rs = lax.broadcasted_iota(jnp.int32, (GLA_WIDTH, kw), 0) // GLA_DV
    cs_ = lax.broadcasted_iota(jnp.int32, (GLA_WIDTH, kw), 1) // GLA_DK
    bd = rs == cs_
    sw = s_scr[...]
    o_inter = []
    for j in range(q // blk):
        sl = slice(j * blk, (j + 1) * blk)
        o_inter.append(lax.dot_general(qt_b[sl], sw.astype(BF16), (((1,), (1,)), ((), ())),
                                       preferred_element_type=F32))
        yield
        upd = lax.dot_general(gv_b[sl], ktil[sl], (((0,), (0,)), ((), ())),
                              preferred_element_type=F32)
        yield
        decay = jnp.exp(b[(j + 1) * blk - 1:(j + 1) * blk, :])
        sw = sw * decay + jnp.where(bd, upd, 0.0)
        yield
    s_scr[...] = sw
    s_out_ref[0] = sw
    o = jnp.concatenate(o_intra, axis=1) + jnp.concatenate(o_inter, axis=0)
    yield

    gg = gg_ref[...]
    gate = gg * _sigmoid(gg)
    outs = []
    for h in range(GLA_HEADS):
        oh = o[:, h * GLA_DV:(h + 1) * GLA_DV]
        ms = jnp.mean(oh * oh, axis=-1, keepdims=True)
        outs.append(oh * lax.rsqrt(ms + EPS) * ng_ref[...])
    o_ref[...] = jnp.concatenate(outs, axis=1) * gate


def _gla_call(proj, ncp, s_in, wa_pad, ba, ng):
    m = proj.shape[0]
    q = CHUNK
    kw = GLA_HEADS * GLA_DK
    smap = lambda c: (jnp.maximum(c - (ncp - 1), 0), 0, 0)
    const2 = lambda c: (0, 0)
    return dict(
        in_specs=[pl.BlockSpec((q, kw), lambda c: (c, C_GQ // kw)),
                  pl.BlockSpec((q, kw), lambda c: (c, C_GK // kw)),
                  pl.BlockSpec((q, GLA_WIDTH), lambda c: (c, C_GV // GLA_WIDTH)),
                  pl.BlockSpec((q, GLA_WIDTH), lambda c: (c, C_GG // GLA_WIDTH)),
                  pl.BlockSpec((q, LANE), lambda c: (c, C_SM // LANE)),
                  pl.BlockSpec((1, GLA_WIDTH, kw), smap),
                  pl.BlockSpec((LANE, kw), const2),
                  pl.BlockSpec((1, kw), const2),
                  pl.BlockSpec((1, GLA_DV), const2)],
        out_specs=[pl.BlockSpec((q, GLA_WIDTH), lambda c: (c, 0)),
                   pl.BlockSpec((1, GLA_WIDTH, kw), smap)],
        out_shape=[jax.ShapeDtypeStruct((m, GLA_WIDTH), F32),
                   jax.ShapeDtypeStruct(s_in.shape, F32)],
        scratch_shapes=[pltpu.VMEM((GLA_WIDTH, kw), F32)],
        args=(proj, proj, proj, proj, proj, s_in, wa_pad, ba, ng))


def _mixers_kernel(ncp, n_in, n_out, n_scr, *refs):
    i0, i1 = n_in
    o0, o1 = n_out
    s0, s1 = n_scr
    ins, outs, scr = refs[:i0 + i1], refs[i0 + i1:i0 + i1 + o0 + o1], refs[i0 + i1 + o0 + o1:]
    streams = [(_gla_kernel(ncp, *ins[i0:], *outs[o0:], *scr[s0:]), 1),
               (_ssd_kernel(ncp, *ins[:i0], *outs[:o0], *scr[:s0]), 2)]
    while streams:
        for entry in list(streams):
            for _ in range(entry[1]):
                if next(entry[0], StopIteration) is StopIteration:
                    streams.remove(entry)
                    break


def _mixers(ncp, ssd, gla):
    m = ssd["args"][0].shape[0]
    counts = [(len(ssd[k]), len(gla[k])) for k in ("in_specs", "out_specs", "scratch_shapes")]
    return pl.pallas_call(
        functools.partial(_mixers_kernel, ncp, *counts),
        grid=(m // CHUNK,),
        in_specs=ssd["in_specs"] + gla["in_specs"],
        out_specs=ssd["out_specs"] + gla["out_specs"],
        out_shape=ssd["out_shape"] + gla["out_shape"],
        scratch_shapes=ssd["scratch_shapes"] + gla["scratch_shapes"],
        compiler_params=_cparams(("arbitrary",), 40 << 20),
        name="mixers",
    )(*ssd["args"], *gla["args"])


def _attn_prep_kernel(npt, dq_ref, dk_ref, dv_ref, qg_ref, kg_ref, seg_ref,
                      qa_ref, qb_ref, kn_ref, kb_ref, vt_ref, kp_ref, vp_ref, ks_ref, vs_ref):
    seg = seg_ref[...]
    tm = dq_ref.shape[0]

    def qknorm(x, g):
        ms = _select_dot(x * x, seg) * (1.0 / DIFF_HD)
        return x * lax.rsqrt(ms + EPS) * g

    qn = qknorm(dq_ref[...], qg_ref[...]) * (DIFF_HD ** -0.5 * LOG2E)
    first = (lax.broadcasted_iota(jnp.int32, qn.shape, 1) // DIFF_HD) % 2 == 0
    qa = jnp.where(first, qn, 0.0).astype(BF16)
    qb = jnp.where(first, 0.0, qn).astype(BF16)
    kn = qknorm(dk_ref[...], kg_ref[...])
    kn_ref[...] = kn
    kb = kn.astype(BF16)
    for h in range(DIFF_HEADS):
        sl = slice(h * LANE, (h + 1) * LANE)
        qa_ref[h] = qa[:, sl]
        qb_ref[h] = qb[:, sl]
        kb_ref[h] = kb[:, sl]
    dv = dv_ref[...]
    vt_ref[...] = dv.T.astype(BF16)

    def cache_rows(k_ref, v_ref):
        for h in range(DIFF_HEADS):
            k_ref[pl.ds(h, tm, stride=DIFF_HEADS), :] = kn[:, h * LANE:(h + 1) * LANE]
            v_ref[pl.ds(h, tm, stride=DIFF_HEADS), :] = dv[:, h * LANE:(h + 1) * LANE]

    @pl.when(pl.program_id(0) < npt)
    def _():
        cache_rows(kp_ref, vp_ref)

    @pl.when(pl.program_id(0) >= npt)
    def _():
        cache_rows(ks_ref, vs_ref)


def _attn_prep(proj, lp, qg, kg, segmat):
    m = proj.shape[0]
    ms = m - lp
    tm = _pick(math.gcd(lp, ms), (512, 256, 128))
    w = DIFF_WIDTH
    npt = lp // tm
    pmap, smap = _split_maps(npt)
    row = lambda off: pl.BlockSpec((tm, w), lambda i: (i, off // w))
    const2 = lambda i: (0, 0)
    rows4 = tm * DIFF_HEADS
    hm = pl.BlockSpec((DIFF_HEADS, tm, LANE), lambda i: (0, i, 0))
    return pl.pallas_call(
        functools.partial(_attn_prep_kernel, npt),
        grid=(m // tm,),
        in_specs=[row(C_DQ), row(C_DK), row(C_DV),
                  pl.BlockSpec((1, w), const2), pl.BlockSpec((1, w), const2),
                  pl.BlockSpec((w, w), const2)],
        out_specs=[hm, hm, pl.BlockSpec((tm, w), lambda i: (i, 0)), hm, pl.BlockSpec((w, tm), lambda i: (0, i))]
        + [pl.BlockSpec((rows4, LANE), pmap)] * 2 + [pl.BlockSpec((rows4, LANE), smap)] * 2,
        out_shape=[jax.ShapeDtypeStruct((DIFF_HEADS, m, LANE), BF16), jax.ShapeDtypeStruct((DIFF_HEADS, m, LANE), BF16),
                   jax.ShapeDtypeStruct((m, w), F32), jax.ShapeDtypeStruct((DIFF_HEADS, m, LANE), BF16),
                   jax.ShapeDtypeStruct((w, m), BF16)]
        + [jax.ShapeDtypeStruct((lp * DIFF_HEADS, LANE), F32)] * 2
        + [jax.ShapeDtypeStruct((ms * DIFF_HEADS, LANE), F32)] * 2,
        compiler_params=_cparams(("arbitrary",), 40 << 20),
        name="attn_prep",
    )(proj, proj, proj, qg, kg, segmat)


def _lambda_full(lam_ref, lam_init):
    l = lam_ref[...]
    a = jnp.sum(l[0:1] * l[1:2], axis=-1, keepdims=True)
    b = jnp.sum(l[2:3] * l[3:4], axis=-1, keepdims=True)
    return jnp.exp(a) - jnp.exp(b) + lam_init


ATTN_PEEK = 64
ATTN_HEADS_PER_ITER = 2
ATTN_MAX_SUM = 2.0 ** 64
ATTN_MIN_SUM = 2.0 ** -60


def _fold8(x, op):
    r = x[0:SUBLANE]
    for i in range(1, x.shape[0] // SUBLANE):
        r = op(r, x[i * SUBLANE:(i + 1) * SUBLANE])
    return r


def _attn_prompt_kernel(t, kbk, lam_init, qi_ref, ki_ref, slope_ref, qa_ref, qb_ref, kb_ref, vt_ref, lam_ref,
                        og_ref, o_ref, m_scr, l_scr, acc_scr, b0_scr, bd_scr):
    p = pl.program_id(0)
    qi = qi_ref[p]
    ki = ki_ref[p]
    nb = t // kbk
    w = 2 * t
    nt = (((1,), (1,)), ((), ()))

    @pl.when(p == 0)
    def _():
        b0_scr[...] = lax.broadcasted_iota(jnp.int32, (kbk, LANE), 0).astype(F32)
        kr = lax.broadcasted_iota(jnp.int32, (t, w), 0)
        lane = lax.broadcasted_iota(jnp.int32, (t, w), 1)
        qc = (lane // (2 * kbk)) * kbk + lane % kbk
        corr = jnp.minimum(2 * (qc - kr), 0).astype(F32)
        bd_scr[...] = jnp.where(kr // CHUNK <= qc // CHUNK, corr, -jnp.inf)

    @pl.when(ki == 0)
    def _():
        m_scr[...] = jnp.full(m_scr.shape, -jnp.inf, F32)
        l_scr[...] = jnp.zeros(l_scr.shape, F32)
        acc_scr[...] = jnp.zeros(acc_scr.shape, F32)

    def tile(h, diag):
        slope = slope_ref[h]
        shift = slope * ((ki - qi) * t).astype(F32)
        step = slope * float(kbk)
        qcat = jnp.concatenate([r[h, g * kbk:(g + 1) * kbk, :] for g in range(nb) for r in (qa_ref, qb_ref)],
                               axis=0)
        b0 = slope * b0_scr[...]
        b0 = jnp.concatenate([b0] * (w // LANE), axis=1)
        m_h, l_h, acc_h = m_scr.at[h], l_scr.at[h], acc_scr.at[h]

        def scores(rows, nrows, lo):
            s = lax.dot_general(kb_ref[h, rows, :], qcat[lo:, :], nt, preferred_element_type=F32) + b0[0:nrows, lo:]
            if diag:
                s = s + slope * bd_scr[rows, lo:]
            return s

        m_old = m_h[...]
        peek = scores(slice(0, ATTN_PEEK), ATTN_PEEK, 0)
        yield
        m_used = jnp.maximum(m_old, jnp.max(peek, axis=0, keepdims=True) + shift)
        if diag:
            lane = lax.broadcasted_iota(jnp.int32, (1, w), 1)
            qpos = (lane // (2 * kbk)) * kbk + lane % kbk
            ramp = jnp.maximum(qpos - (ATTN_PEEK - 1), 0).astype(F32)
        else:
            ramp = float(t - ATTN_PEEK)
        m_ref = m_used + slope * ramp
        ones = jnp.ones((SUBLANE, kbk), BF16)
        pv = None
        los = [2 * b * kbk if diag else 0 for b in range(nb)]
        sc = []
        for b in range(nb):
            sc.append(scores(slice(b * kbk, (b + 1) * kbk), kbk, los[b]))
            yield
        for b in range(nb):
            rows = slice(b * kbk, (b + 1) * kbk)
            lo = los[b]
            e = sc[b] - (m_ref[:, lo:] - shift - b * step)
            pt = jnp.exp2(e).astype(BF16)
            yield
            d = jnp.dot(jnp.concatenate([vt_ref[h, :, rows], ones], axis=0), pt,
                        preferred_element_type=F32)
            if b == 0:
                pv = d
            elif lo == 0:
                pv = d + pv
            else:
                pv = jnp.concatenate([pv[:, :lo], d + pv[:, lo:]], axis=1)
            yield
        lsum = pv[DIFF_VD:DIFF_VD + 1, :]
        pv = pv[0:DIFF_VD, :]

        def finish():
            ok = jnp.logical_and(jnp.max(lsum) <= ATTN_MAX_SUM, jnp.min(lsum) >= ATTN_MIN_SUM)

            @pl.when(ok)
            def _():
                a_old = jnp.exp2(m_old - m_ref)
                l_h[...] = a_old * l_h[...] + lsum
                acc_h[...] = a_old * acc_h[...] + pv
                m_h[...] = m_ref

            @pl.when(jnp.logical_not(ok))
            def _():
                def body(b, carry):
                    rows = pl.ds(pl.multiple_of(b * kbk, kbk), kbk)
                    s = scores(rows, kbk, 0) + (shift + slope * (b * kbk).astype(F32))
                    m_o = m_h[...]
                    m_n = jnp.maximum(m_o, jnp.max(s, axis=0, keepdims=True))
                    pt = jnp.exp2(s - m_n)
                    a = jnp.exp2(m_o - m_n)
                    l_h[...] = a * l_h[...] + jnp.sum(pt, axis=0, keepdims=True)
                    acc_h[...] = a * acc_h[...] + jnp.dot(vt_ref[h, :, rows], pt.astype(BF16),
                                                          preferred_element_type=F32)
                    m_h[...] = m_n
                    return carry

                lax.fori_loop(0, nb, body, 0)

        return finish

    def tiles(heads, diag):
        gens = [tile(h, diag) for h in heads]
        finishers = [None] * len(gens)
        live = list(range(len(gens)))
        while live:
            for n in list(live):
                try:
                    next(gens[n])
                except StopIteration as done:
                    finishers[n] = done.value
                    live.remove(n)
        for finish in finishers:
            finish()

    def head_group(hp, carry):
        heads = [hp * ATTN_HEADS_PER_ITER + n for n in range(ATTN_HEADS_PER_ITER)]

        @pl.when(ki < qi)
        def _():
            tiles(heads, False)

        @pl.when(ki == qi)
        def _():
            tiles(heads, True)
            lam = _lambda_full(lam_ref, lam_init)
            for h in heads:
                on = acc_scr[h] / l_scr[h]
                ot = jnp.concatenate(
                    [on[:, 2 * g * kbk:(2 * g + 1) * kbk] - lam * on[:, (2 * g + 1) * kbk:(2 * g + 2) * kbk]
                     for g in range(nb)], axis=1)
                ms = jnp.mean(ot * ot, axis=0, keepdims=True)
                o_ref[h] = (ot * lax.rsqrt(ms + EPS)).T * (og_ref[...] * (1.0 - lam_init))

        return carry

    lax.fori_loop(0, DIFF_HEADS // ATTN_HEADS_PER_ITER, head_group, 0)


def _attn_prompt(lp, lam_init, qa, qb, kb, vt, slopes, lam_p, og):
    t = _pick(lp, (1024, 512, 256))
    kbk = min(t, 512)
    nq = lp // t
    pairs = [(i, j) for i in range(nq) for j in range(i + 1)]
    qi_l = jnp.asarray(np.array([a for a, _ in pairs], np.int32))
    ki_l = jnp.asarray(np.array([b for _, b in pairs], np.int32))
    nh = DIFF_HEADS
    grid_spec = pltpu.PrefetchScalarGridSpec(
        num_scalar_prefetch=3,
        grid=(len(pairs),),
        in_specs=[pl.BlockSpec((nh, t, LANE), lambda p, qi, ki, sl: (0, qi[p], 0)),
                  pl.BlockSpec((nh, t, LANE), lambda p, qi, ki, sl: (0, qi[p], 0)),
                  pl.BlockSpec((nh, t, LANE), lambda p, qi, ki, sl: (0, ki[p], 0)),
                  pl.BlockSpec((nh, DIFF_VD, t), lambda p, qi, ki, sl: (0, 0, ki[p])),
                  pl.BlockSpec((4, DIFF_HD), lambda p, qi, ki, sl: (0, 0)),
                  pl.BlockSpec((1, DIFF_VD), lambda p, qi, ki, sl: (0, 0))],
        out_specs=pl.BlockSpec((nh, t, LANE), lambda p, qi, ki, sl: (0, qi[p], 0)),
        scratch_shapes=[pltpu.VMEM((nh, 1, 2 * t), F32), pltpu.VMEM((nh, 1, 2 * t), F32),
                        pltpu.VMEM((nh, DIFF_VD, 2 * t), F32),
                        pltpu.VMEM((kbk, LANE), F32), pltpu.VMEM((t, 2 * t), F32)])
    return pl.pallas_call(
        functools.partial(_attn_prompt_kernel, t, kbk, lam_init),
        grid_spec=grid_spec,
        out_shape=jax.ShapeDtypeStruct((nh, lp, LANE), F32),
        compiler_params=_cparams(("arbitrary",), 48 << 20),
        name="attn_prompt",
    )(qi_l, ki_l, slopes, qa, qb, kb, vt.reshape(nh, DIFF_VD, vt.shape[1]), lam_p, og)


def _attn_sample_kernel(past, lam_init, slope_ref, qa_ref, qb_ref, kn_ref, vn_ref, kc_ref, vc_ref, lam_ref, og_ref,
                        o_ref):
    q = CHUNK
    lam = _lambda_full(lam_ref, lam_init)
    kpos = lax.broadcasted_iota(jnp.int32, (1, past), 1).astype(F32)
    r = lax.broadcasted_iota(jnp.int32, (q, q), 0)
    cc = lax.broadcasted_iota(jnp.int32, (q, q), 1)
    new_bias = (past + r - jnp.abs(r - cc)).astype(F32)
    outs = []
    for h in range(DIFF_HEADS):
        slope = slope_ref[h]
        sl = slice(h * LANE, (h + 1) * LANE)
        kc = kc_ref[pl.ds(h, past, stride=DIFF_HEADS), :].astype(BF16)
        vc = vc_ref[pl.ds(h, past, stride=DIFF_HEADS), :].astype(BF16)
        kn = kn_ref[:, sl].astype(BF16)
        vn = vn_ref[:, sl].astype(BF16)
        res = []
        for q_ref in (qa_ref, qb_ref):
            qv = q_ref[h]
            s_c = lax.dot_general(qv, kc, (((1,), (1,)), ((), ())), preferred_element_type=F32) + slope * kpos
            s_n = lax.dot_general(qv, kn, (((1,), (1,)), ((), ())), preferred_element_type=F32) + slope * new_bias
            mx = jnp.maximum(jnp.max(s_c, axis=-1, keepdims=True), jnp.max(s_n, axis=-1, keepdims=True))
            p_c = jnp.exp2(s_c - mx)
            p_n = jnp.exp2(s_n - mx)
            den = jnp.sum(p_c, axis=-1, keepdims=True) + jnp.sum(p_n, axis=-1, keepdims=True)
            num = (jnp.dot(p_c.astype(BF16), vc, preferred_element_type=F32)
                   + jnp.dot(p_n.astype(BF16), vn, preferred_element_type=F32))
            res.append(num / den)
        o = res[0] - lam * res[1]
        ms = jnp.mean(o * o, axis=-1, keepdims=True)
        outs.append(o * lax.rsqrt(ms + EPS) * (og_ref[...] * (1.0 - lam_init)))
    o_ref[...] = jnp.concatenate(outs, axis=1)


def _attn_sample(lp, nb, layer, lam_init, qa, qb, kn, proj, cache_k, cache_v, slopes, lam_p, og):
    depth, _, past = cache_k.shape[:3]
    q = CHUNK
    w = DIFF_WIDTH
    off = lp // q
    cache_k = cache_k.reshape(depth, nb, past * DIFF_HEADS, DIFF_VD)
    cache_v = cache_v.reshape(depth, nb, past * DIFF_HEADS, DIFF_VD)
    cache_block = (None, None, past * DIFF_HEADS, DIFF_VD)
    grid_spec = pltpu.PrefetchScalarGridSpec(
        num_scalar_prefetch=1,
        grid=(nb,),
        in_specs=[pl.BlockSpec((DIFF_HEADS, q, LANE), lambda b, sl: (0, off + b, 0)),
                  pl.BlockSpec((DIFF_HEADS, q, LANE), lambda b, sl: (0, off + b, 0)),
                  pl.BlockSpec((q, w), lambda b, sl: (off + b, 0)),
                  pl.BlockSpec((q, w), lambda b, sl: (off + b, C_DV // w)),
                  pl.BlockSpec(cache_block, lambda b, sl: (layer, b, 0, 0)),
                  pl.BlockSpec(cache_block, lambda b, sl: (layer, b, 0, 0)),
                  pl.BlockSpec((4, DIFF_HD), lambda b, sl: (0, 0)),
                  pl.BlockSpec((1, DIFF_VD), lambda b, sl: (0, 0))],
        out_specs=pl.BlockSpec((q, w), lambda b, sl: (b, 0)))
    return pl.pallas_call(
        functools.partial(_attn_sample_kernel, past, lam_init),
        grid_spec=grid_spec,
        out_shape=jax.ShapeDtypeStruct((nb * q, DIFF_WIDTH), F32),
        compiler_params=_cparams(("arbitrary",), 32 << 20),
        name="attn_sample",
    )(slopes, qa, qb, kn, proj, cache_k, cache_v, lam_p, og)


def _out_proj_kernel(npt_x, npt, xp_ref, xs_ref, ys_ref, yap_ref, yas_ref, yg_ref, w_ref, o_ref):
    is_prompt = pl.program_id(0) < npt
    yap = jnp.concatenate([yap_ref[h] for h in range(DIFF_HEADS)], axis=1)
    ya = jnp.where(is_prompt, yap, yas_ref[...])
    acc = jnp.dot(ys_ref[...].astype(BF16), w_ref[0:SSD_WIDTH, :], preferred_element_type=F32)
    acc = acc + jnp.dot(ya.astype(BF16), w_ref[SSD_WIDTH:SSD_WIDTH + DIFF_WIDTH, :],
                        preferred_element_type=F32)
    acc = acc + jnp.dot(yg_ref[...].astype(BF16), w_ref[SSD_WIDTH + DIFF_WIDTH:, :],
                        preferred_element_type=F32)
    o_ref[...] = jnp.where(pl.program_id(0) < npt_x, xp_ref[...], xs_ref[...]) + acc


def _out_proj(xp, xs, lp_rows, ys, yap, yas, yg, w, layer):
    xp, xs, lp, ms, first_rows = _row_split(xp, xs, lp_rows)
    tm = _pick(math.gcd(lp, ms), (512, 256, 128, 64))
    npt = lp // tm
    pmap, smap = _split_maps(npt)
    xpmap, xsmap = _split_maps(first_rows // tm)
    vmem = 2 * (5 * tm * D_MODEL * 4) + D_MODEL * D_MODEL * 2 + (4 << 20)
    return pl.pallas_call(
        functools.partial(_out_proj_kernel, first_rows // tm, npt),
        grid=((lp + ms) // tm,),
        in_specs=[pl.BlockSpec((tm, D_MODEL), xpmap),
                  pl.BlockSpec((tm, D_MODEL), xsmap),
                  pl.BlockSpec((tm, SSD_WIDTH), lambda i: (i, 0)),
                  pl.BlockSpec((DIFF_HEADS, tm, LANE), lambda i: (0, jnp.minimum(i, npt - 1), 0)),
                  pl.BlockSpec((tm, DIFF_WIDTH), smap),
                  pl.BlockSpec((tm, GLA_WIDTH), lambda i: (i, 0)),
                  pl.BlockSpec((None, D_MODEL, D_MODEL), lambda i: (layer, 0, 0), pipeline_mode=pl.Buffered(1))],
        out_specs=pl.BlockSpec((tm, D_MODEL), lambda i: (i, 0)),
        out_shape=jax.ShapeDtypeStruct((lp + ms, D_MODEL), F32),
        compiler_params=_cparams(("parallel",), vmem),
        name="out_proj",
    )(xp, xs, ys, yap, yas, yg, w)


def _mlp_kernel(x_ref, g_ref, w1_ref, w2_ref, o_ref, h_ref):
    @pl.when(pl.program_id(1) == 0)
    def _():
        x = x_ref[...]
        h_ref[...] = _rms_bf16(x, g_ref[...])
        o_ref[...] = x

    h = h_ref[...]
    half = w1_ref.shape[1] // 2
    a0 = jnp.dot(h, w1_ref[:, 0:half], preferred_element_type=F32)
    a1 = jnp.dot(h, w1_ref[:, half:], preferred_element_type=F32)
    a0 = jnp.square(jnp.maximum(a0, 0.0)).astype(BF16)
    upd = jnp.dot(a0, w2_ref[0:half, :], preferred_element_type=F32)
    a1 = jnp.square(jnp.maximum(a1, 0.0)).astype(BF16)
    upd = jnp.dot(a1, w2_ref[half:, :], preferred_element_type=F32) + upd
    o_ref[...] += upd


def _mlp(x, g, w1, w2, layer):
    m = x.shape[0]
    tm = _pick(m, (512, 256, 128, 64))
    tf = 1024
    vmem = 4 * tm * D_MODEL * 4 + tm * D_MODEL * 2 + 4 * D_MODEL * tf * 2 + 3 * tm * tf * 4 + tm * D_MODEL * 4 + (4 << 20)
    return pl.pallas_call(
        _mlp_kernel,
        grid=(m // tm, D_FF // tf),
        in_specs=[pl.BlockSpec((tm, D_MODEL), lambda i, f: (i, 0)),
                  pl.BlockSpec((1, D_MODEL), lambda i, f: (0, 0)),
                  pl.BlockSpec((None, D_MODEL, tf), lambda i, f: (layer, 0, f)),
                  pl.BlockSpec((None, tf, D_MODEL), lambda i, f: (layer, f, 0))],
        out_specs=pl.BlockSpec((tm, D_MODEL), lambda i, f: (i, 0)),
        out_shape=jax.ShapeDtypeStruct((m, D_MODEL), F32),
        scratch_shapes=[pltpu.VMEM((tm, D_MODEL), BF16)],
        compiler_params=_cparams(("parallel", "arbitrary"), vmem),
        name="mlp",
    )(x, g, w1, w2)


def _pack_w_in_kernel(w_ref, o_ref):
    x = w_ref[...]
    o_ref[...] = jnp.concatenate(
        [x[:, 0:2560], x[:, 2576:4112], x[:, 4624:5136], x[:, 5152:5664], x[:, 4112:4624],
         x[:, 2560:2576], x[:, 5136:5152], jnp.zeros((x.shape[0], PACKED - 5664), x.dtype)],
        axis=-1).astype(BF16)


def _pack_w_in(w):
    depth, rows, cols = w.shape
    tr = _pick(rows, (256, 128, 64))
    return pl.pallas_call(
        _pack_w_in_kernel,
        grid=(depth, rows // tr),
        in_specs=[pl.BlockSpec((None, tr, cols), lambda l, i: (l, i, 0))],
        out_specs=pl.BlockSpec((None, tr, PACKED), lambda l, i: (l, i, 0)),
        out_shape=jax.ShapeDtypeStruct((depth, rows, PACKED), BF16),
        compiler_params=_cparams(("parallel", "parallel"), 32 << 20),
        name="pack_w_in",
    )(w)


def _pad_lanes(v, width=LANE):
    v = v.reshape(1, -1)
    return jnp.pad(v, ((0, 0), (0, width - v.shape[1])))


def kernel(x_prompt, x_sample, cache_diff_k, cache_diff_v, state_ssd_conv, state_ssd, state_gla, norm1_g, w_in, ssd_conv_w, ssd_conv_b, ssd_dt_bias, ssd_a_log, ssd_d, ssd_norm_g, diff_qn_g, diff_kn_g, diff_lambda, diff_out_g, gla_wa2, gla_ba, gla_norm_g, w_out, norm2_g, w_mlp1, w_mlp2):
    bp, lp, d = x_prompt.shape
    nb, ls, _ = x_sample.shape
    depth = w_in.shape[0]
    past = cache_diff_k.shape[2]
    assert bp == 1 and d == D_MODEL and ls == CHUNK and lp % CHUNK == 0 and past % CHUNK == 0
    ncp = lp // CHUNK
    m = lp + nb * ls

    xp, xs = x_prompt.reshape(lp, d), x_sample.reshape(nb * ls, d)

    emat = (jnp.arange(LANE)[:, None] == (jnp.arange(SSD_WIDTH)[None, :] // SSD_HEAD_DIM)).astype(BF16)
    segmat = ((jnp.arange(DIFF_WIDTH)[:, None] // DIFF_HD) == (jnp.arange(DIFF_WIDTH)[None, :] // DIFF_HD)).astype(BF16)
    slopes = jnp.exp2(-8.0 * jnp.arange(1, DIFF_HEADS + 1, dtype=F32) / DIFF_HEADS) * LOG2E
    eye_h = jnp.eye(GLA_HEADS, dtype=F32)
    w_in_b, w_out_b = _pack_w_in(w_in), w_out.astype(BF16)
    w_mlp1_b, w_mlp2_b = w_mlp1.astype(BF16), w_mlp2.astype(BF16)

    outs = {k: [] for k in ("kp", "vp", "cp", "hp", "sp", "ks", "vs", "cs", "hs", "ss")}
    for l in range(depth):
        lam_init = 0.8 - 0.6 * math.exp(-0.3 * l)
        proj = _in_proj(xp, xs, lp, norm1_g[l].reshape(1, d), w_in_b, l)

        st_all = jnp.concatenate([jnp.zeros((1,) + state_ssd.shape[2:], F32), state_ssd[l]], axis=0)
        st_in = st_all.reshape(nb + 1, SSD_WIDTH, SSD_STATE).transpose(0, 2, 1)
        cprev = jnp.concatenate([jnp.zeros((1, SSD_CONV - 1, SSD_CONV_DIM), F32), state_ssd_conv[l]], axis=0)
        cprev = jnp.pad(cprev, ((0, 0), (8 - (SSD_CONV - 1), 0), (0, 0)))
        cw, cb = ssd_conv_w[l], ssd_conv_b[l].reshape(1, -1)
        ssd_call = _ssd_call(
            proj, ncp, st_in, cprev[:, :, :SSD_WIDTH], cprev[:, :, SSD_WIDTH:],
            cw[:, :SSD_WIDTH], cb[:, :SSD_WIDTH], cw[:, SSD_WIDTH:], cb[:, SSD_WIDTH:],
            _pad_lanes(ssd_dt_bias[l]), _pad_lanes(ssd_a_log[l]),
            jnp.repeat(ssd_d[l], SSD_HEAD_DIM).reshape(1, -1), ssd_norm_g[l].reshape(1, -1), emat)
        xbc_raw = proj[:, C_XS:C_XS + SSD_CONV_DIM]
        outs["cp"].append(xbc_raw[lp - (SSD_CONV - 1):lp][None])
        outs["cs"].append(xbc_raw[lp:].reshape(nb, ls, SSD_CONV_DIM)[:, ls - (SSD_CONV - 1):])

        s_all = jnp.concatenate([jnp.zeros((1,) + state_gla.shape[2:], F32), state_gla[l]], axis=0)
        s_in = jnp.einsum('bhkv,hg->bhvgk', s_all, eye_h).reshape(nb + 1, GLA_WIDTH, GLA_HEADS * GLA_DK)
        wa_pad = jnp.zeros((LANE, GLA_HEADS * GLA_DK), F32).at[GLA_RANK:2 * GLA_RANK].set(gla_wa2[l]).astype(BF16)
        gla_call = _gla_call(proj, ncp, s_in, wa_pad, gla_ba[l].reshape(1, -1), gla_norm_g[l].reshape(1, -1))
        y_ssd, st_out, y_gla, s_out = _mixers(ncp, ssd_call, gla_call)
        h_all = st_out.transpose(0, 2, 1).reshape(nb + 1, SSD_HEADS, SSD_HEAD_DIM, SSD_STATE)
        outs["hp"].append(h_all[0:1])
        outs["hs"].append(h_all[1:])
        s5 = s_out.reshape(nb + 1, GLA_HEADS, GLA_DV, GLA_HEADS, GLA_DK)
        s_new = jnp.stack([s5[:, h, :, h, :] for h in range(GLA_HEADS)], axis=1).transpose(0, 1, 3, 2)
        outs["sp"].append(s_new[0:1])
        outs["ss"].append(s_new[1:])

        qg = jnp.tile(diff_qn_g[l], 2 * DIFF_HEADS).reshape(1, -1)
        kg = jnp.tile(diff_kn_g[l], 2 * DIFF_HEADS).reshape(1, -1)
        qa, qb, kn, kb, vt, kp4, vp4, ks4, vs4 = _attn_prep(proj, lp, qg, kg, segmat)
        og = diff_out_g[l].reshape(1, -1)
        ya_p = _attn_prompt(lp, lam_init, qa, qb, kb, vt, slopes, diff_lambda[l], og)
        ya_s = _attn_sample(lp, nb, l, lam_init, qa, qb, kn, proj, cache_diff_k, cache_diff_v,
                            slopes, diff_lambda[l], og)
        outs["kp"].append(kp4.reshape(1, lp, DIFF_HEADS, 2 * DIFF_HD))
        outs["ks"].append(ks4.reshape(nb, ls, DIFF_HEADS, 2 * DIFF_HD))
        outs["vp"].append(vp4.reshape(1, lp, DIFF_HEADS, DIFF_VD))
        outs["vs"].append(vs4.reshape(nb, ls, DIFF_HEADS, DIFF_VD))

        x1 = _out_proj(xp, xs, lp, y_ssd, ya_p, ya_s, y_gla, w_out_b, l)
        xp, xs = _mlp(x1, norm2_g[l].reshape(1, d), w_mlp1_b, w_mlp2_b, l), None

    st = {k: jnp.stack(v) for k, v in outs.items()}
    return (xp[:lp].reshape(1, lp, d), xp[lp:].reshape(nb, ls, d),
            st["kp"], st["vp"], st["cp"], st["hp"], st["sp"],
            st["ks"], st["vs"], st["cs"], st["hs"], st["ss"])
```

```python
import functools
import math

import numpy as np
import jax
import jax.numpy as jnp
from jax import lax
from jax.experimental import pallas as pl
from jax.experimental.pallas import tpu as pltpu

F32 = jnp.float32
BF16 = jnp.bfloat16

D_MODEL = 2048
CHUNK = 64
SSD_HEAD_DIM = 64
SSD_WIDTH = 1024
SSD_HEADS = 16
SSD_GROUPS = 2
SSD_STATE = 128
SSD_CONV = 4
SSD_CONV_DIM = SSD_WIDTH + 2 * SSD_GROUPS * SSD_STATE
DIFF_HD = 64
DIFF_VD = 128
DIFF_WIDTH = 512
DIFF_HEADS = 4
GLA_DK = 64
GLA_DV = 128
GLA_WIDTH = 512
GLA_HEADS = 4
GLA_RANK = 16
GLA_TAU = 16.0
GLA_BLOCK = 16
D_FF = 4 * D_MODEL
EPS = 1e-6

LOG2E = math.log2(math.e)
LANE = 128
SUBLANE = 8
VMEM_CAP = 56 << 20

C_Z, C_XS, C_BC, C_DQ, C_DK, C_DV, C_GV, C_GG, C_GQ, C_GK, C_SM = (
    0, 1024, 2048, 2560, 3072, 3584, 4096, 4608, 5120, 5376, 5632)
PACKED = 5760
PROJ_TN = 1920


def _pick(m, cands):
    for c in cands:
        if m % c == 0:
            return c
    raise ValueError(f"no tile for {m}")


def _cparams(sem, vmem_bytes):
    return pltpu.CompilerParams(dimension_semantics=sem,
                                vmem_limit_bytes=int(min(VMEM_CAP, max(vmem_bytes, 16 << 20))))


def _sigmoid(x):
    return 1.0 / (1.0 + jnp.exp(-x))


def _split3(x):
    hi = x.astype(BF16)
    r1 = x - hi.astype(F32)
    mid = r1.astype(BF16)
    lo = (r1 - mid.astype(F32)).astype(BF16)
    return hi, mid, lo


def _select_dot(x, sel):
    return sum(jnp.dot(p, sel, preferred_element_type=F32) for p in _split3(x))


def _dot_select(sel, x):
    return sum(jnp.dot(sel, p, preferred_element_type=F32) for p in _split3(x))


def _softplus(x):
    return jnp.maximum(x, 0.0) + jnp.log1p(jnp.exp(-jnp.abs(x)))


def _rms_bf16(x, g):
    ms = jnp.mean(x * x, axis=-1, keepdims=True)
    return (x * lax.rsqrt(ms + EPS) * g).astype(BF16)


def _in_proj_kernel(npt, xp_ref, xs_ref, g_ref, w_ref, o_ref, h_ref):
    i = pl.program_id(0)

    @pl.when(jnp.logical_and(pl.program_id(1) == 0, i < npt))
    def _():
        h_ref[...] = _rms_bf16(xp_ref[...], g_ref[...])

    @pl.when(jnp.logical_and(pl.program_id(1) == 0, i >= npt))
    def _():
        h_ref[...] = _rms_bf16(xs_ref[...], g_ref[...])

    o_ref[...] = lax.dot_general(h_ref[...], w_ref[...], (((1,), (1,)), ((), ())), preferred_element_type=F32)


def _split_maps(npt):
    return (lambda i, *_: (jnp.minimum(i, npt - 1), 0)), (lambda i, *_: (jnp.maximum(i - npt, 0), 0))


def _row_split(xp, xs, lp_rows):
    if xs is None:
        return xp, xp, lp_rows, xp.shape[0] - lp_rows, xp.shape[0]
    return xp, xs, xp.shape[0], xs.shape[0], xp.shape[0]


def _in_proj(xp, xs, lp_rows, g, w, layer):
    xp, xs, lp, ms, first_rows = _row_split(xp, xs, lp_rows)
    tm = _pick(math.gcd(lp, ms), (512, 256, 128, 64))
    tn = PROJ_TN
    npt = first_rows // tm
    pmap, smap = _split_maps(npt)
    vmem = 4 * tm * D_MODEL * 4 + tm * D_MODEL * 2 + 2 * D_MODEL * tn * 2 + 2 * tm * tn * 4 + (4 << 20)
    return pl.pallas_call(
        functools.partial(_in_proj_kernel, npt),
        grid=((lp + ms) // tm, PACKED // tn),
        in_specs=[pl.BlockSpec((tm, D_MODEL), pmap),
                  pl.BlockSpec((tm, D_MODEL), smap),
                  pl.BlockSpec((1, D_MODEL), lambda i, j: (0, 0)),
                  pl.BlockSpec((None, tn, D_MODEL), lambda i, j: (layer, j, 0))],
        out_specs=pl.BlockSpec((tm, tn), lambda i, j: (i, j)),
        out_shape=jax.ShapeDtypeStruct((lp + ms, PACKED), F32),
        scratch_shapes=[pltpu.VMEM((tm, D_MODEL), BF16)],
        compiler_params=_cparams(("parallel", "arbitrary"), vmem),
        name="in_proj",
    )(xp, xs, g, w)


def _ssd_kernel(ncp, z_ref, xs_ref, bc_ref, sm_ref, st_in_ref, cpx_ref, cpb_ref,
                cwx_ref, cbx_ref, cwb_ref, cbb_ref, dtb_ref, alog_ref, dexp_ref, ng_ref, e_ref,
                y_ref, st_out_ref, st_scr, fx_scr, fb_scr):
    c = pl.program_id(0)
    q = CHUNK

    @pl.when(jnp.logical_or(c == 0, c >= ncp))
    def _():
        st_scr[...] = st_in_ref[0]
        fx_scr[0:8, :] = cpx_ref[0]
        fb_scr[0:8, :] = cpb_ref[0]

    def conv(u_ref, f_scr, w_ref, b_ref):
        u = u_ref[...]
        f_scr[8:8 + q, :] = u
        y = b_ref[...] + f_scr[5:5 + q, :] * w_ref[0:1, :]
        for j in range(1, SSD_CONV):
            y = y + f_scr[5 + j:5 + j + q, :] * w_ref[j:j + 1, :]
        f_scr[0:8, :] = u[q - 8:q, :]
        return y * _sigmoid(y)

    yield
    dt = _softplus(sm_ref[...] + dtb_ref[...])
    da = dt * (-jnp.exp(alog_ref[...]))
    r64 = lax.broadcasted_iota(jnp.int32, (q, q), 0)
    c64 = lax.broadcasted_iota(jnp.int32, (q, q), 1)
    tril = (c64 <= r64).astype(BF16)
    cs = _dot_select(tril, da)
    yield
    xs = conv(xs_ref, fx_scr, cwx_ref, cbx_ref)
    yield
    both = _select_dot(jnp.concatenate([dt, cs], axis=0), e_ref[...])
    dt_e = both[0:q]
    cs_e = both[q:2 * q]
    yield
    bcv = conv(bc_ref, fb_scr, cwb_ref, cbb_ref)
    yield

    row = lax.broadcasted_iota(jnp.int32, (q, SSD_WIDTH), 0)
    sidx = jnp.bitwise_and(lax.broadcasted_iota(jnp.int32, (q, SSD_WIDTH), 1), q - 1)
    cs_row = jnp.sum(jnp.where(sidx == row, cs_e, 0.0), axis=0, keepdims=True)
    cs_last = cs_e[q - 1:q, :]
    lmat = jnp.exp(jnp.where(sidx <= row, cs_e - cs_row, -jnp.inf))
    yield

    xdt = xs * dt_e
    xdt_end = (xdt * jnp.exp(cs_last - cs_e)).astype(BF16)
    xdt_b = xdt.astype(BF16)
    yield
    ecs = jnp.exp(cs_e)
    st = st_scr[...]
    st_b = st.astype(BF16)
    bcb = bcv.astype(BF16)
    yield

    r128 = lax.broadcasted_iota(jnp.int32, (2 * q, LANE), 0)
    c128 = lax.broadcasted_iota(jnp.int32, (2 * q, LANE), 1)
    bd2 = (r128 // q) == (c128 // q)

    hg = SSD_HEADS // SSD_GROUPS
    gw = hg * SSD_HEAD_DIM
    y_parts = []
    new_states = []
    for g in range(SSD_GROUPS):
        bm = bcb[:, g * SSD_STATE:(g + 1) * SSD_STATE]
        cm = bcb[:, (SSD_GROUPS + g) * SSD_STATE:(SSD_GROUPS + g + 1) * SSD_STATE]
        bm_rep = jnp.concatenate([bm] * hg, axis=0)
        cbt = lax.dot_general(cm, bm_rep, (((1,), (1,)), ((), ())),
                              preferred_element_type=F32)
        m_all = (cbt * lmat[:, g * gw:(g + 1) * gw]).astype(BF16)
        y_off = jnp.dot(cm, st_b[:, g * gw:(g + 1) * gw], preferred_element_type=F32)
        yield
        for j in range(gw // LANE):
            col = g * gw + j * LANE
            xj = xdt_b[:, col:col + LANE]
            xd = jnp.where(bd2, jnp.concatenate([xj, xj], axis=0), jnp.zeros((), BF16))
            y_parts.append(jnp.dot(m_all[:, j * LANE:(j + 1) * LANE], xd, preferred_element_type=F32)
                           + y_off[:, j * LANE:(j + 1) * LANE] * ecs[:, col:col + LANE])
            yield
        new_states.append(lax.dot_general(bm, xdt_end[:, g * gw:(g + 1) * gw], (((0,), (0,)), ((), ())),
                                          preferred_element_type=F32))
        yield
    y = jnp.concatenate(y_parts, axis=1)
    st_new = st * jnp.exp(cs_last) + jnp.concatenate(new_states, axis=1)
    st_scr[...] = st_new
    st_out_ref[0] = st_new
    yield

    y = y + dexp_ref[...] * xs
    zv = z_ref[...]
    y = y * (zv * _sigmoid(zv))
    yield
    outs = []
    for g in range(SSD_GROUPS):
        yg = y[:, g * gw:(g + 1) * gw]
        ms = jnp.mean(yg * yg, axis=-1, keepdims=True)
        outs.append(yg * lax.rsqrt(ms + EPS))
    y_ref[...] = jnp.concatenate(outs, axis=1) * ng_ref[...]


def _ssd_call(proj, ncp, st_in, cpx, cpb, cwx, cbx, cwb, cbb, dtb, alog, dexp, ng, emat):
    m = proj.shape[0]
    q = CHUNK
    smap = lambda c: (jnp.maximum(c - (ncp - 1), 0), 0, 0)
    const2 = lambda c: (0, 0)
    return dict(
        in_specs=[pl.BlockSpec((q, 1024), lambda c: (c, C_Z // 1024)),
                  pl.BlockSpec((q, 1024), lambda c: (c, C_XS // 1024)),
                  pl.BlockSpec((q, 512), lambda c: (c, C_BC // 512)),
                  pl.BlockSpec((q, LANE), lambda c: (c, C_SM // LANE)),
                  pl.BlockSpec((1, SSD_STATE, SSD_WIDTH), smap),
                  pl.BlockSpec((1, 8, 1024), smap),
                  pl.BlockSpec((1, 8, 512), smap),
                  pl.BlockSpec((SSD_CONV, 1024), const2),
                  pl.BlockSpec((1, 1024), const2),
                  pl.BlockSpec((SSD_CONV, 512), const2),
                  pl.BlockSpec((1, 512), const2),
                  pl.BlockSpec((1, LANE), const2),
                  pl.BlockSpec((1, LANE), const2),
                  pl.BlockSpec((1, SSD_WIDTH), const2),
                  pl.BlockSpec((1, SSD_WIDTH), const2),
                  pl.BlockSpec((LANE, SSD_WIDTH), const2)],
        out_specs=[pl.BlockSpec((q, SSD_WIDTH), lambda c: (c, 0)),
                   pl.BlockSpec((1, SSD_STATE, SSD_WIDTH), smap)],
        out_shape=[jax.ShapeDtypeStruct((m, SSD_WIDTH), F32),
                   jax.ShapeDtypeStruct(st_in.shape, F32)],
        scratch_shapes=[pltpu.VMEM((SSD_STATE, SSD_WIDTH), F32),
                        pltpu.VMEM((8 + q, 1024), F32),
                        pltpu.VMEM((8 + q, 512), F32)],
        args=(proj, proj, proj, proj, st_in, cpx, cpb, cwx, cbx, cwb, cbb, dtb, alog, dexp, ng, emat))


def _gla_kernel(ncp, gq_ref, gk_ref, gv_ref, gg_ref, sm_ref, s_in_ref, wa_ref, ba_ref, ng_ref,
                o_ref, s_out_ref, s_scr):
    c = pl.program_id(0)
    q = CHUNK
    blk = GLA_BLOCK
    kw = GLA_HEADS * GLA_DK

    @pl.when(jnp.logical_or(c == 0, c >= ncp))
    def _():
        s_scr[...] = s_in_ref[0]

    yield
    pre = jnp.dot(sm_ref[...].astype(BF16), wa_ref[...], preferred_element_type=F32) + ba_ref[...]
    yield
    log_a = -_softplus(-pre) * (1.0 / GLA_TAU)
    r64 = lax.broadcasted_iota(jnp.int32, (q, q), 0)
    c64 = lax.broadcasted_iota(jnp.int32, (q, q), 1)
    same_blk = (r64 // blk) == (c64 // blk)
    causal = jnp.logical_and(same_blk, c64 <= r64)
    yield
    b = _dot_select(causal.astype(BF16), log_a)
    yield
    gk = gk_ref[...]
    qt = gq_ref[...] * (GLA_DK ** -0.5) * jnp.exp(b)
    kt = (gk * jnp.exp(-b)).astype(BF16)
    yield
    b_last = jnp.concatenate(
        [jnp.broadcast_to(b[(j + 1) * blk - 1:(j + 1) * blk, :], (blk, kw)) for j in range(q // blk)], axis=0)
    ktil = (gk * jnp.exp(b_last - b)).astype(BF16)
    qt_b = qt.astype(BF16)
    gv_b = gv_ref[...].astype(BF16)
    yield

    lane_head = lax.broadcasted_iota(jnp.int32, (q, kw), 1) // GLA_DK
    o_intra = []
    for h in range(GLA_HEADS):
        qm = jnp.where(lane_head == h, qt_b, jnp.zeros((), BF16))
        att = lax.dot_general(qm, kt, (((1,), (1,)), ((), ())), preferred_element_type=F32)
        yield
        att = jnp.where(causal, att, 0.0).astype(BF16)
        o_intra.append(jnp.dot(att, gv_b[:, h * GLA_DV:(h + 1) * GLA_DV], preferred_element_type=F32))
        yield

    rs = lax.broadcasted_iota(jnp.int32, (GLA_WIDTH, kw), 0) // GLA_DV
    cs_ = lax.broadcasted_iota(jnp.int32, (GLA_WIDTH, kw), 1) // GLA_DK
    bd = rs == cs_
    sw = s_scr[...]
    o_inter = []
    for j in range(q // blk):
        sl = slice(j * blk, (j + 1) * blk)
        o_inter.append(lax.dot_general(qt_b[sl], sw.astype(BF16), (((1,), (1,)), ((), ())),
                                       preferred_element_type=F32))
        yield
        upd = lax.dot_general(gv_b[sl], ktil[sl], (((0,), (0,)), ((), ())),
                              preferred_element_type=F32)
        yield
        decay = jnp.exp(b[(j + 1) * blk - 1:(j + 1) * blk, :])
        sw = sw * decay + jnp.where(bd, upd, 0.0)
        yield
    s_scr[...] = sw
    s_out_ref[0] = sw
    o = jnp.concatenate(o_intra, axis=1) + jnp.concatenate(o_inter, axis=0)
    yield

    gg = gg_ref[...]
    gate = gg * _sigmoid(gg)
    outs = []
    for h in range(GLA_HEADS):
        oh = o[:, h * GLA_DV:(h + 1) * GLA_DV]
        ms = jnp.mean(oh * oh, axis=-1, keepdims=True)
        outs.append(oh * lax.rsqrt(ms + EPS) * ng_ref[...])
    o_ref[...] = jnp.concatenate(outs, axis=1) * gate


def _gla_call(proj, ncp, s_in, wa_pad, ba, ng):
    m = proj.shape[0]
    q = CHUNK
    kw = GLA_HEADS * GLA_DK
    smap = lambda c: (jnp.maximum(c - (ncp - 1), 0), 0, 0)
    const2 = lambda c: (0, 0)
    return dict(
        in_specs=[pl.BlockSpec((q, kw), lambda c: (c, C_GQ // kw)),
                  pl.BlockSpec((q, kw), lambda c: (c, C_GK // kw)),
                  pl.BlockSpec((q, GLA_WIDTH), lambda c: (c, C_GV // GLA_WIDTH)),
                  pl.BlockSpec((q, GLA_WIDTH), lambda c: (c, C_GG // GLA_WIDTH)),
                  pl.BlockSpec((q, LANE), lambda c: (c, C_SM // LANE)),
                  pl.BlockSpec((1, GLA_WIDTH, kw), smap),
                  pl.BlockSpec((LANE, kw), const2),
                  pl.BlockSpec((1, kw), const2),
                  pl.BlockSpec((1, GLA_DV), const2)],
        out_specs=[pl.BlockSpec((q, GLA_WIDTH), lambda c: (c, 0)),
                   pl.BlockSpec((1, GLA_WIDTH, kw), smap)],
        out_shape=[jax.ShapeDtypeStruct((m, GLA_WIDTH), F32),
                   jax.ShapeDtypeStruct(s_in.shape, F32)],
        scratch_shapes=[pltpu.VMEM((GLA_WIDTH, kw), F32)],
        args=(proj, proj, proj, proj, proj, s_in, wa_pad, ba, ng))


def _mixers_kernel(ncp, n_in, n_out, n_scr, *refs):
    i0, i1 = n_in
    o0, o1 = n_out
    s0, s1 = n_scr
    ins, outs, scr = refs[:i0 + i1], refs[i0 + i1:i0 + i1 + o0 + o1], refs[i0 + i1 + o0 + o1:]
    streams = [(_gla_kernel(ncp, *ins[i0:], *outs[o0:], *scr[s0:]), 1),
               (_ssd_kernel(ncp, *ins[:i0], *outs[:o0], *scr[:s0]), 2)]
    while streams:
        for entry in list(streams):
            for _ in range(entry[1]):
                if next(entry[0], StopIteration) is StopIteration:
                    streams.remove(entry)
                    break


def _mixers(ncp, ssd, gla):
    m = ssd["args"][0].shape[0]
    counts = [(len(ssd[k]), len(gla[k])) for k in ("in_specs", "out_specs", "scratch_shapes")]
    return pl.pallas_call(
        functools.partial(_mixers_kernel, ncp, *counts),
        grid=(m // CHUNK,),
        in_specs=ssd["in_specs"] + gla["in_specs"],
        out_specs=ssd["out_specs"] + gla["out_specs"],
        out_shape=ssd["out_shape"] + gla["out_shape"],
        scratch_shapes=ssd["scratch_shapes"] + gla["scratch_shapes"],
        compiler_params=_cparams(("arbitrary",), 40 << 20),
        name="mixers",
    )(*ssd["args"], *gla["args"])


def _attn_prep_kernel(npt, dq_ref, dk_ref, dv_ref, qg_ref, kg_ref, seg_ref,
                      qa_ref, qb_ref, kn_ref, kb_ref, vt_ref, kp_ref, vp_ref, ks_ref, vs_ref):
    seg = seg_ref[...]
    tm = dq_ref.shape[0]

    def qknorm(x, g):
        ms = _select_dot(x * x, seg) * (1.0 / DIFF_HD)
        return x * lax.rsqrt(ms + EPS) * g

    qn = qknorm(dq_ref[...], qg_ref[...]) * (DIFF_HD ** -0.5 * LOG2E)
    first = (lax.broadcasted_iota(jnp.int32, qn.shape, 1) // DIFF_HD) % 2 == 0
    qa = jnp.where(first, qn, 0.0).astype(BF16)
    qb = jnp.where(first, 0.0, qn).astype(BF16)
    kn = qknorm(dk_ref[...], kg_ref[...])
    kn_ref[...] = kn
    kb = kn.astype(BF16)
    for h in range(DIFF_HEADS):
        sl = slice(h * LANE, (h + 1) * LANE)
        qa_ref[h] = qa[:, sl]
        qb_ref[h] = qb[:, sl]
        kb_ref[h] = kb[:, sl]
    dv = dv_ref[...]
    vt_ref[...] = dv.T.astype(BF16)

    def cache_rows(k_ref, v_ref):
        for h in range(DIFF_HEADS):
            k_ref[pl.ds(h, tm, stride=DIFF_HEADS), :] = kn[:, h * LANE:(h + 1) * LANE]
            v_ref[pl.ds(h, tm, stride=DIFF_HEADS), :] = dv[:, h * LANE:(h + 1) * LANE]

    @pl.when(pl.program_id(0) < npt)
    def _():
        cache_rows(kp_ref, vp_ref)

    @pl.when(pl.program_id(0) >= npt)
    def _():
        cache_rows(ks_ref, vs_ref)


def _attn_prep(proj, lp, qg, kg, segmat):
    m = proj.shape[0]
    ms = m - lp
    tm = _pick(math.gcd(lp, ms), (512, 256, 128))
    w = DIFF_WIDTH
    npt = lp // tm
    pmap, smap = _split_maps(npt)
    row = lambda off: pl.BlockSpec((tm, w), lambda i: (i, off // w))
    const2 = lambda i: (0, 0)
    rows4 = tm * DIFF_HEADS
    hm = pl.BlockSpec((DIFF_HEADS, tm, LANE), lambda i: (0, i, 0))
    return pl.pallas_call(
        functools.partial(_attn_prep_kernel, npt),
        grid=(m // tm,),
        in_specs=[row(C_DQ), row(C_DK), row(C_DV),
                  pl.BlockSpec((1, w), const2), pl.BlockSpec((1, w), const2),
                  pl.BlockSpec((w, w), const2)],
        out_specs=[hm, hm, pl.BlockSpec((tm, w), lambda i: (i, 0)), hm, pl.BlockSpec((w, tm), lambda i: (0, i))]
        + [pl.BlockSpec((rows4, LANE), pmap)] * 2 + [pl.BlockSpec((rows4, LANE), smap)] * 2,
        out_shape=[jax.ShapeDtypeStruct((DIFF_HEADS, m, LANE), BF16), jax.ShapeDtypeStruct((DIFF_HEADS, m, LANE), BF16),
                   jax.ShapeDtypeStruct((m, w), F32), jax.ShapeDtypeStruct((DIFF_HEADS, m, LANE), BF16),
                   jax.ShapeDtypeStruct((w, m), BF16)]
        + [jax.ShapeDtypeStruct((lp * DIFF_HEADS, LANE), F32)] * 2
        + [jax.ShapeDtypeStruct((ms * DIFF_HEADS, LANE), F32)] * 2,
        compiler_params=_cparams(("arbitrary",), 40 << 20),
        name="attn_prep",
    )(proj, proj, proj, qg, kg, segmat)


def _lambda_full(lam_ref, lam_init):
    l = lam_ref[...]
    a = jnp.sum(l[0:1] * l[1:2], axis=-1, keepdims=True)
    b = jnp.sum(l[2:3] * l[3:4], axis=-1, keepdims=True)
    return jnp.exp(a) - jnp.exp(b) + lam_init


ATTN_PEEK = 64
ATTN_HEADS_PER_ITER = 2
ATTN_MAX_SUM = 2.0 ** 64
ATTN_MIN_SUM = 2.0 ** -60


def _fold8(x, op):
    r = x[0:SUBLANE]
    for i in range(1, x.shape[0] // SUBLANE):
        r = op(r, x[i * SUBLANE:(i + 1) * SUBLANE])
    return r


def _attn_prompt_kernel(t, kbk, lam_init, qi_ref, ki_ref, slope_ref, qa_ref, qb_ref, kb_ref, vt_ref, lam_ref,
                        og_ref, o_ref, m_scr, l_scr, acc_scr, b0_scr, bd_scr):
    p = pl.program_id(0)
    qi = qi_ref[p]
    ki = ki_ref[p]
    nb = t // kbk
    w = 2 * t
    nt = (((1,), (1,)), ((), ()))

    @pl.when(p == 0)
    def _():
        b0_scr[...] = lax.broadcasted_iota(jnp.int32, (kbk, LANE), 0).astype(F32)
        kr = lax.broadcasted_iota(jnp.int32, (t, w), 0)
        lane = lax.broadcasted_iota(jnp.int32, (t, w), 1)
        qc = (lane // (2 * kbk)) * kbk + lane % kbk
        corr = jnp.minimum(2 * (qc - kr), 0).astype(F32)
        bd_scr[...] = jnp.where(kr // CHUNK <= qc // CHUNK, corr, -jnp.inf)

    @pl.when(ki == 0)
    def _():
        m_scr[...] = jnp.full(m_scr.shape, -jnp.inf, F32)
        l_scr[...] = jnp.zeros(l_scr.shape, F32)
        acc_scr[...] = jnp.zeros(acc_scr.shape, F32)

    def tile(h, diag):
        slope = slope_ref[h]
        shift = slope * ((ki - qi) * t).astype(F32)
        step = slope * float(kbk)
        qcat = jnp.concatenate([r[h, g * kbk:(g + 1) * kbk, :] for g in range(nb) for r in (qa_ref, qb_ref)],
                               axis=0)
        b0 = slope * b0_scr[...]
        b0 = jnp.concatenate([b0] * (w // LANE), axis=1)
        m_h, l_h, acc_h = m_scr.at[h], l_scr.at[h], acc_scr.at[h]

        def scores(rows, nrows, lo):
            s = lax.dot_general(kb_ref[h, rows, :], qcat[lo:, :], nt, preferred_element_type=F32) + b0[0:nrows, lo:]
            if diag:
                s = s + slope * bd_scr[rows, lo:]
            return s

        m_old = m_h[...]
        peek = scores(slice(0, ATTN_PEEK), ATTN_PEEK, 0)
        yield
        m_used = jnp.maximum(m_old, jnp.max(peek, axis=0, keepdims=True) + shift)
        if diag:
            lane = lax.broadcasted_iota(jnp.int32, (1, w), 1)
            qpos = (lane // (2 * kbk)) * kbk + lane % kbk
            ramp = jnp.maximum(qpos - (ATTN_PEEK - 1), 0).astype(F32)
        else:
            ramp = float(t - ATTN_PEEK)
        m_ref = m_used + slope * ramp
        ones = jnp.ones((SUBLANE, kbk), BF16)
        pv = None
        los = [2 * b * kbk if diag else 0 for b in range(nb)]
        sc = []
        for b in range(nb):
            sc.append(scores(slice(b * kbk, (b + 1) * kbk), kbk, los[b]))
            yield
        for b in range(nb):
            rows = slice(b * kbk, (b + 1) * kbk)
            lo = los[b]
            e = sc[b] - (m_ref[:, lo:] - shift - b * step)
            pt = jnp.exp2(e).astype(BF16)
            yield
            d = jnp.dot(jnp.concatenate([vt_ref[h, :, rows], ones], axis=0), pt,
                        preferred_element_type=F32)
            if b == 0:
                pv = d
            elif lo == 0:
                pv = d + pv
            else:
                pv = jnp.concatenate([pv[:, :lo], d + pv[:, lo:]], axis=1)
            yield
        lsum = pv[DIFF_VD:DIFF_VD + 1, :]
        pv = pv[0:DIFF_VD, :]

        def finish():
            ok = jnp.logical_and(jnp.max(lsum) <= ATTN_MAX_SUM, jnp.min(lsum) >= ATTN_MIN_SUM)

            @pl.when(ok)
            def _():
                a_old = jnp.exp2(m_old - m_ref)
                l_h[...] = a_old * l_h[...] + lsum
                acc_h[...] = a_old * acc_h[...] + pv
                m_h[...] = m_ref

            @pl.when(jnp.logical_not(ok))
            def _():
                def body(b, carry):
                    rows = pl.ds(pl.multiple_of(b * kbk, kbk), kbk)
                    s = scores(rows, kbk, 0) + (shift + slope * (b * kbk).astype(F32))
                    m_o = m_h[...]
                    m_n = jnp.maximum(m_o, jnp.max(s, axis=0, keepdims=True))
                    pt = jnp.exp2(s - m_n)
                    a = jnp.exp2(m_o - m_n)
                    l_h[...] = a * l_h[...] + jnp.sum(pt, axis=0, keepdims=True)
                    acc_h[...] = a * acc_h[...] + jnp.dot(vt_ref[h, :, rows], pt.astype(BF16),
                                                          preferred_element_type=F32)
                    m_h[...] = m_n
                    return carry

                lax.fori_loop(0, nb, body, 0)

        return finish

    def tiles(heads, diag):
        gens = [tile(h, diag) for h in heads]
        finishers = [None] * len(gens)
        live = list(range(len(gens)))
        while live:
            for n in list(live):
                try:
                    next(gens[n])
                except StopIteration as done:
                    finishers[n] = done.value
                    live.remove(n)
        for finish in finishers:
            finish()

    def head_group(hp, carry):
        heads = [hp * ATTN_HEADS_PER_ITER + n for n in range(ATTN_HEADS_PER_ITER)]

        @pl.when(ki < qi)
        def _():
            tiles(heads, False)

        @pl.when(ki == qi)
        def _():
            tiles(heads, True)
            lam = _lambda_full(lam_ref, lam_init)
            for h in heads:
                on = acc_scr[h] / l_scr[h]
                ot = jnp.concatenate(
                    [on[:, 2 * g * kbk:(2 * g + 1) * kbk] - lam * on[:, (2 * g + 1) * kbk:(2 * g + 2) * kbk]
                     for g in range(nb)], axis=1)
                ms = jnp.mean(ot * ot, axis=0, keepdims=True)
                o_ref[h] = (ot * lax.rsqrt(ms + EPS)).T * (og_ref[...] * (1.0 - lam_init))

        return carry

    lax.fori_loop(0, DIFF_HEADS // ATTN_HEADS_PER_ITER, head_group, 0)


def _attn_prompt(lp, lam_init, qa, qb, kb, vt, slopes, lam_p, og):
    t = _pick(lp, (1024, 512, 256))
    kbk = min(t, 512)
    nq = lp // t
    pairs = [(i, j) for i in range(nq) for j in range(i + 1)]
    qi_l = jnp.asarray(np.array([a for a, _ in pairs], np.int32))
    ki_l = jnp.asarray(np.array([b for _, b in pairs], np.int32))
    nh = DIFF_HEADS
    grid_spec = pltpu.PrefetchScalarGridSpec(
        num_scalar_prefetch=3,
        grid=(len(pairs),),
        in_specs=[pl.BlockSpec((nh, t, LANE), lambda p, qi, ki, sl: (0, qi[p], 0)),
                  pl.BlockSpec((nh, t, LANE), lambda p, qi, ki, sl: (0, qi[p], 0)),
                  pl.BlockSpec((nh, t, LANE), lambda p, qi, ki, sl: (0, ki[p], 0)),
                  pl.BlockSpec((nh, DIFF_VD, t), lambda p, qi, ki, sl: (0, 0, ki[p])),
                  pl.BlockSpec((4, DIFF_HD), lambda p, qi, ki, sl: (0, 0)),
                  pl.BlockSpec((1, DIFF_VD), lambda p, qi, ki, sl: (0, 0))],
        out_specs=pl.BlockSpec((nh, t, LANE), lambda p, qi, ki, sl: (0, qi[p], 0)),
        scratch_shapes=[pltpu.VMEM((nh, 1, 2 * t), F32), pltpu.VMEM((nh, 1, 2 * t), F32),
                        pltpu.VMEM((nh, DIFF_VD, 2 * t), F32),
                        pltpu.VMEM((kbk, LANE), F32), pltpu.VMEM((t, 2 * t), F32)])
    return pl.pallas_call(
        functools.partial(_attn_prompt_kernel, t, kbk, lam_init),
        grid_spec=grid_spec,
        out_shape=jax.ShapeDtypeStruct((nh, lp, LANE), F32),
        compiler_params=_cparams(("arbitrary",), 48 << 20),
        name="attn_prompt",
    )(qi_l, ki_l, slopes, qa, qb, kb, vt.reshape(nh, DIFF_VD, vt.shape[1]), lam_p, og)


def _attn_sample_kernel(past, lam_init, slope_ref, qa_ref, qb_ref, kn_ref, vn_ref, kc_ref, vc_ref, lam_ref, og_ref,
                        o_ref):
    q = CHUNK
    lam = _lambda_full(lam_ref, lam_init)
    kpos = lax.broadcasted_iota(jnp.int32, (1, past), 1).astype(F32)
    r = lax.broadcasted_iota(jnp.int32, (q, q), 0)
    cc = lax.broadcasted_iota(jnp.int32, (q, q), 1)
    new_bias = (past + r - jnp.abs(r - cc)).astype(F32)
    new_bias = jnp.concatenate([new_bias, new_bias], axis=0)
    outs = []
    for h in range(DIFF_HEADS):
        slope = slope_ref[h]
        sl = slice(h * LANE, (h + 1) * LANE)
        kc = kc_ref[pl.ds(h, past, stride=DIFF_HEADS), :].astype(BF16)
        vc = vc_ref[pl.ds(h, past, stride=DIFF_HEADS), :].astype(BF16)
        kn = kn_ref[:, sl].astype(BF16)
        vn = vn_ref[:, sl].astype(BF16)
        qv = jnp.concatenate([qa_ref[h], qb_ref[h]], axis=0)
        s_c = lax.dot_general(qv, kc, (((1,), (1,)), ((), ())), preferred_element_type=F32) + slope * kpos
        s_n = lax.dot_general(qv, kn, (((1,), (1,)), ((), ())), preferred_element_type=F32) + slope * new_bias
        mx = jnp.maximum(jnp.max(s_c, axis=-1, keepdims=True), jnp.max(s_n, axis=-1, keepdims=True))
        p_c = jnp.exp2(s_c - mx)
        p_n = jnp.exp2(s_n - mx)
        den = jnp.sum(p_c, axis=-1, keepdims=True) + jnp.sum(p_n, axis=-1, keepdims=True)
        num = (jnp.dot(p_c.astype(BF16), vc, preferred_element_type=F32)
               + jnp.dot(p_n.astype(BF16), vn, preferred_element_type=F32))
        res = num / den
        o = res[0:q] - lam * res[q:2 * q]
        ms = jnp.mean(o * o, axis=-1, keepdims=True)
        outs.append(o * lax.rsqrt(ms + EPS) * (og_ref[...] * (1.0 - lam_init)))
    o_ref[...] = jnp.concatenate(outs, axis=1)


def _attn_sample(lp, nb, layer, lam_init, qa, qb, kn, proj, cache_k, cache_v, slopes, lam_p, og):
    depth, _, past = cache_k.shape[:3]
    q = CHUNK
    w = DIFF_WIDTH
    off = lp // q
    cache_k = cache_k.reshape(depth, nb, past * DIFF_HEADS, DIFF_VD)
    cache_v = cache_v.reshape(depth, nb, past * DIFF_HEADS, DIFF_VD)
    cache_block = (None, None, past * DIFF_HEADS, DIFF_VD)
    grid_spec = pltpu.PrefetchScalarGridSpec(
        num_scalar_prefetch=1,
        grid=(nb,),
        in_specs=[pl.BlockSpec((DIFF_HEADS, q, LANE), lambda b, sl: (0, off + b, 0)),
                  pl.BlockSpec((DIFF_HEADS, q, LANE), lambda b, sl: (0, off + b, 0)),
                  pl.BlockSpec((q, w), lambda b, sl: (off + b, 0)),
                  pl.BlockSpec((q, w), lambda b, sl: (off + b, C_DV // w)),
                  pl.BlockSpec(cache_block, lambda b, sl: (layer, b, 0, 0)),
                  pl.BlockSpec(cache_block, lambda b, sl: (layer, b, 0, 0)),
                  pl.BlockSpec((4, DIFF_HD), lambda b, sl: (0, 0)),
                  pl.BlockSpec((1, DIFF_VD), lambda b, sl: (0, 0))],
        out_specs=pl.BlockSpec((q, w), lambda b, sl: (b, 0)))
    return pl.pallas_call(
        functools.partial(_attn_sample_kernel, past, lam_init),
        grid_spec=grid_spec,
        out_shape=jax.ShapeDtypeStruct((nb * q, DIFF_WIDTH), F32),
        compiler_params=_cparams(("arbitrary",), 32 << 20),
        name="attn_sample",
    )(slopes, qa, qb, kn, proj, cache_k, cache_v, lam_p, og)


def _out_proj_kernel(npt_x, npt, xp_ref, xs_ref, ys_ref, yap_ref, yas_ref, yg_ref, w_ref, o_ref):
    is_prompt = pl.program_id(0) < npt
    yap = jnp.concatenate([yap_ref[h] for h in range(DIFF_HEADS)], axis=1)
    ya = jnp.where(is_prompt, yap, yas_ref[...])
    acc = jnp.dot(ys_ref[...].astype(BF16), w_ref[0:SSD_WIDTH, :], preferred_element_type=F32)
    acc = acc + jnp.dot(ya.astype(BF16), w_ref[SSD_WIDTH:SSD_WIDTH + DIFF_WIDTH, :],
                        preferred_element_type=F32)
    acc = acc + jnp.dot(yg_ref[...].astype(BF16), w_ref[SSD_WIDTH + DIFF_WIDTH:, :],
                        preferred_element_type=F32)
    o_ref[...] = jnp.where(pl.program_id(0) < npt_x, xp_ref[...], xs_ref[...]) + acc


def _out_proj(xp, xs, lp_rows, ys, yap, yas, yg, w, layer):
    xp, xs, lp, ms, first_rows = _row_split(xp, xs, lp_rows)
    tm = _pick(math.gcd(lp, ms), (512, 256, 128, 64))
    npt = lp // tm
    pmap, smap = _split_maps(npt)
    xpmap, xsmap = _split_maps(first_rows // tm)
    vmem = 2 * (5 * tm * D_MODEL * 4) + D_MODEL * D_MODEL * 2 + (4 << 20)
    return pl.pallas_call(
        functools.partial(_out_proj_kernel, first_rows // tm, npt),
        grid=((lp + ms) // tm,),
        in_specs=[pl.BlockSpec((tm, D_MODEL), xpmap),
                  pl.BlockSpec((tm, D_MODEL), xsmap),
                  pl.BlockSpec((tm, SSD_WIDTH), lambda i: (i, 0)),
                  pl.BlockSpec((DIFF_HEADS, tm, LANE), lambda i: (0, jnp.minimum(i, npt - 1), 0)),
                  pl.BlockSpec((tm, DIFF_WIDTH), smap),
                  pl.BlockSpec((tm, GLA_WIDTH), lambda i: (i, 0)),
                  pl.BlockSpec((None, D_MODEL, D_MODEL), lambda i: (layer, 0, 0), pipeline_mode=pl.Buffered(1))],
        out_specs=pl.BlockSpec((tm, D_MODEL), lambda i: (i, 0)),
        out_shape=jax.ShapeDtypeStruct((lp + ms, D_MODEL), F32),
        compiler_params=_cparams(("parallel",), vmem),
        name="out_proj",
    )(xp, xs, ys, yap, yas, yg, w)


def _mlp_kernel(x_ref, g_ref, w1_ref, w2_ref, o_ref, h_ref):
    @pl.when(pl.program_id(1) == 0)
    def _():
        x = x_ref[...]
        h_ref[...] = _rms_bf16(x, g_ref[...])
        o_ref[...] = x

    h = h_ref[...]
    half = w1_ref.shape[1] // 2
    a0 = jnp.dot(h, w1_ref[:, 0:half], preferred_element_type=F32)
    a1 = jnp.dot(h, w1_ref[:, half:], preferred_element_type=F32)
    a0 = jnp.square(jnp.maximum(a0, 0.0)).astype(BF16)
    upd = jnp.dot(a0, w2_ref[0:half, :], preferred_element_type=F32)
    a1 = jnp.square(jnp.maximum(a1, 0.0)).astype(BF16)
    upd = jnp.dot(a1, w2_ref[half:, :], preferred_element_type=F32) + upd
    o_ref[...] += upd


def _mlp(x, g, w1, w2, layer):
    m = x.shape[0]
    tm = _pick(m, (512, 256, 128, 64))
    tf = 1024
    vmem = 4 * tm * D_MODEL * 4 + tm * D_MODEL * 2 + 4 * D_MODEL * tf * 2 + 3 * tm * tf * 4 + tm * D_MODEL * 4 + (4 << 20)
    return pl.pallas_call(
        _mlp_kernel,
        grid=(m // tm, D_FF // tf),
        in_specs=[pl.BlockSpec((tm, D_MODEL), lambda i, f: (i, 0)),
                  pl.BlockSpec((1, D_MODEL), lambda i, f: (0, 0)),
                  pl.BlockSpec((None, D_MODEL, tf), lambda i, f: (layer, 0, f)),
                  pl.BlockSpec((None, tf, D_MODEL), lambda i, f: (layer, f, 0))],
        out_specs=pl.BlockSpec((tm, D_MODEL), lambda i, f: (i, 0)),
        out_shape=jax.ShapeDtypeStruct((m, D_MODEL), F32),
        scratch_shapes=[pltpu.VMEM((tm, D_MODEL), BF16)],
        compiler_params=_cparams(("parallel", "arbitrary"), vmem),
        name="mlp",
    )(x, g, w1, w2)


def _pack_w_in_kernel(w_ref, o_ref):
    x = w_ref[...]
    o_ref[...] = jnp.concatenate(
        [x[0:2560], x[2576:4112], x[4624:5136], x[5152:5664], x[4112:4624],
         x[2560:2576], x[5136:5152], jnp.zeros((PACKED - 5664, x.shape[1]), x.dtype)],
        axis=0).astype(BF16)


def _pack_w_in(w):
    depth, rows, cols = w.shape
    wt = jnp.swapaxes(w, 1, 2)
    tc = _pick(rows, (256, 128))
    return pl.pallas_call(
        _pack_w_in_kernel,
        grid=(depth, rows // tc),
        in_specs=[pl.BlockSpec((None, cols, tc), lambda l, i: (l, 0, i))],
        out_specs=pl.BlockSpec((None, PACKED, tc), lambda l, i: (l, 0, i)),
        out_shape=jax.ShapeDtypeStruct((depth, PACKED, rows), BF16),
        compiler_params=_cparams(("parallel", "parallel"), 32 << 20),
        name="pack_w_in",
    )(wt)


def _pad_lanes(v, width=LANE):
    v = v.reshape(1, -1)
    return jnp.pad(v, ((0, 0), (0, width - v.shape[1])))


def kernel(x_prompt, x_sample, cache_diff_k, cache_diff_v, state_ssd_conv, state_ssd, state_gla, norm1_g, w_in, ssd_conv_w, ssd_conv_b, ssd_dt_bias, ssd_a_log, ssd_d, ssd_norm_g, diff_qn_g, diff_kn_g, diff_lambda, diff_out_g, gla_wa2, gla_ba, gla_norm_g, w_out, norm2_g, w_mlp1, w_mlp2):
    bp, lp, d = x_prompt.shape
    nb, ls, _ = x_sample.shape
    depth = w_in.shape[0]
    past = cache_diff_k.shape[2]
    assert bp == 1 and d == D_MODEL and ls == CHUNK and lp % CHUNK == 0 and past % CHUNK == 0
    ncp = lp // CHUNK
    m = lp + nb * ls

    xp, xs = x_prompt.reshape(lp, d), x_sample.reshape(nb * ls, d)

    emat = (jnp.arange(LANE)[:, None] == (jnp.arange(SSD_WIDTH)[None, :] // SSD_HEAD_DIM)).astype(BF16)
    segmat = ((jnp.arange(DIFF_WIDTH)[:, None] // DIFF_HD) == (jnp.arange(DIFF_WIDTH)[None, :] // DIFF_HD)).astype(BF16)
    slopes = jnp.exp2(-8.0 * jnp.arange(1, DIFF_HEADS + 1, dtype=F32) / DIFF_HEADS) * LOG2E
    eye_h = jnp.eye(GLA_HEADS, dtype=F32)
    w_in_b, w_out_b = _pack_w_in(w_in), w_out.astype(BF16)
    w_mlp1_b, w_mlp2_b = w_mlp1.astype(BF16), w_mlp2.astype(BF16)

    outs = {k: [] for k in ("kp", "vp", "cp", "hp", "sp", "ks", "vs", "cs", "hs", "ss")}
    for l in range(depth):
        lam_init = 0.8 - 0.6 * math.exp(-0.3 * l)
        proj = _in_proj(xp, xs, lp, norm1_g[l].reshape(1, d), w_in_b, l)

        st_all = jnp.concatenate([jnp.zeros((1,) + state_ssd.shape[2:], F32), state_ssd[l]], axis=0)
        st_in = st_all.reshape(nb + 1, SSD_WIDTH, SSD_STATE).transpose(0, 2, 1)
        cprev = jnp.concatenate([jnp.zeros((1, SSD_CONV - 1, SSD_CONV_DIM), F32), state_ssd_conv[l]], axis=0)
        cprev = jnp.pad(cprev, ((0, 0), (8 - (SSD_CONV - 1), 0), (0, 0)))
        cw, cb = ssd_conv_w[l], ssd_conv_b[l].reshape(1, -1)
        ssd_call = _ssd_call(
            proj, ncp, st_in, cprev[:, :, :SSD_WIDTH], cprev[:, :, SSD_WIDTH:],
            cw[:, :SSD_WIDTH], cb[:, :SSD_WIDTH], cw[:, SSD_WIDTH:], cb[:, SSD_WIDTH:],
            _pad_lanes(ssd_dt_bias[l]), _pad_lanes(ssd_a_log[l]),
            jnp.repeat(ssd_d[l], SSD_HEAD_DIM).reshape(1, -1), ssd_norm_g[l].reshape(1, -1), emat)
        xbc_raw = proj[:, C_XS:C_XS + SSD_CONV_DIM]
        outs["cp"].append(xbc_raw[lp - (SSD_CONV - 1):lp][None])
        outs["cs"].append(xbc_raw[lp:].reshape(nb, ls, SSD_CONV_DIM)[:, ls - (SSD_CONV - 1):])

        s_all = jnp.concatenate([jnp.zeros((1,) + state_gla.shape[2:], F32), state_gla[l]], axis=0)
        s_in = jnp.einsum('bhkv,hg->bhvgk', s_all, eye_h).reshape(nb + 1, GLA_WIDTH, GLA_HEADS * GLA_DK)
        wa_pad = jnp.zeros((LANE, GLA_HEADS * GLA_DK), F32).at[GLA_RANK:2 * GLA_RANK].set(gla_wa2[l]).astype(BF16)
        gla_call = _gla_call(proj, ncp, s_in, wa_pad, gla_ba[l].reshape(1, -1), gla_norm_g[l].reshape(1, -1))
        y_ssd, st_out, y_gla, s_out = _mixers(ncp, ssd_call, gla_call)
        h_all = st_out.transpose(0, 2, 1).reshape(nb + 1, SSD_HEADS, SSD_HEAD_DIM, SSD_STATE)
        outs["hp"].append(h_all[0:1])
        outs["hs"].append(h_all[1:])
        s5 = s_out.reshape(nb + 1, GLA_HEADS, GLA_DV, GLA_HEADS, GLA_DK)
        s_new = jnp.stack([s5[:, h, :, h, :] for h in range(GLA_HEADS)], axis=1).transpose(0, 1, 3, 2)
        outs["sp"].append(s_new[0:1])
        outs["ss"].append(s_new[1:])

        qg = jnp.tile(diff_qn_g[l], 2 * DIFF_HEADS).reshape(1, -1)
        kg = jnp.tile(diff_kn_g[l], 2 * DIFF_HEADS).reshape(1, -1)
        qa, qb, kn, kb, vt, kp4, vp4, ks4, vs4 = _attn_prep(proj, lp, qg, kg, segmat)
        og = diff_out_g[l].reshape(1, -1)
        ya_p = _attn_prompt(lp, lam_init, qa, qb, kb, vt, slopes, diff_lambda[l], og)
        ya_s = _attn_sample(lp, nb, l, lam_init, qa, qb, kn, proj, cache_diff_k, cache_diff_v,
                            slopes, diff_lambda[l], og)
        outs["kp"].append(kp4.reshape(1, lp, DIFF_HEADS, 2 * DIFF_HD))
        outs["ks"].append(ks4.reshape(nb, ls, DIFF_HEADS, 2 * DIFF_HD))
        outs["vp"].append(vp4.reshape(1, lp, DIFF_HEADS, DIFF_VD))
        outs["vs"].append(vs4.reshape(nb, ls, DIFF_HEADS, DIFF_VD))

        x1 = _out_proj(xp, xs, lp, y_ssd, ya_p, ya_s, y_gla, w_out_b, l)
        xp, xs = _mlp(x1, norm2_g[l].reshape(1, d), w_mlp1_b, w_mlp2_b, l), None

    st = {k: jnp.stack(v) for k, v in outs.items()}
    return (xp[:lp].reshape(1, lp, d), xp[lp:].reshape(nb, ls, d),
            st["kp"], st["vp"], st["cp"], st["hp"], st["sp"],
            st["ks"], st["vs"], st["cs"], st["hs"], st["ss"])
```

```python
import functools
import math

import numpy as np
import jax
import jax.numpy as jnp
from jax import lax
from jax.experimental import pallas as pl
from jax.experimental.pallas import tpu as pltpu

F32 = jnp.float32
BF16 = jnp.bfloat16

D_MODEL = 2048
CHUNK = 64
SSD_HEAD_DIM = 64
SSD_WIDTH = 1024
SSD_HEADS = 16
SSD_GROUPS = 2
SSD_STATE = 128
SSD_CONV = 4
SSD_CONV_DIM = SSD_WIDTH + 2 * SSD_GROUPS * SSD_STATE
DIFF_HD = 64
DIFF_VD = 128
DIFF_WIDTH = 512
DIFF_HEADS = 4
GLA_DK = 64
GLA_DV = 128
GLA_WIDTH = 512
GLA_HEADS = 4
GLA_RANK = 16
GLA_TAU = 16.0
GLA_BLOCK = 16
D_FF = 4 * D_MODEL
EPS = 1e-6

LOG2E = math.log2(math.e)
LANE = 128
SUBLANE = 8
VMEM_CAP = 56 << 20

C_Z, C_XS, C_BC, C_DQ, C_DK, C_DV, C_GV, C_GG, C_GQ, C_GK, C_SM = (
    0, 1024, 2048, 2560, 3072, 3584, 4096, 4608, 5120, 5376, 5632)
PACKED = 5760
PROJ_TN = 1920


def _pick(m, cands):
    for c in cands:
        if m % c == 0:
            return c
    raise ValueError(f"no tile for {m}")


def _cparams(sem, vmem_bytes):
    return pltpu.CompilerParams(dimension_semantics=sem,
                                vmem_limit_bytes=int(min(VMEM_CAP, max(vmem_bytes, 16 << 20))))


def _sigmoid(x):
    return 1.0 / (1.0 + jnp.exp(-x))


def _split3(x):
    hi = x.astype(BF16)
    r1 = x - hi.astype(F32)
    mid = r1.astype(BF16)
    lo = (r1 - mid.astype(F32)).astype(BF16)
    return hi, mid, lo


def _select_dot(x, sel):
    return sum(jnp.dot(p, sel, preferred_element_type=F32) for p in _split3(x))


def _dot_select(sel, x):
    return sum(jnp.dot(sel, p, preferred_element_type=F32) for p in _split3(x))


def _softplus(x):
    return jnp.maximum(x, 0.0) + jnp.log1p(jnp.exp(-jnp.abs(x)))


def _rms_bf16(x, g):
    ms = jnp.mean(x * x, axis=-1, keepdims=True)
    return (x * lax.rsqrt(ms + EPS) * g).astype(BF16)


def _in_proj_kernel(npt, xp_ref, xs_ref, g_ref, w_ref, o_ref, h_ref):
    i = pl.program_id(0)

    @pl.when(jnp.logical_and(pl.program_id(1) == 0, i < npt))
    def _():
        h_ref[...] = _rms_bf16(xp_ref[...], g_ref[...])

    @pl.when(jnp.logical_and(pl.program_id(1) == 0, i >= npt))
    def _():
        h_ref[...] = _rms_bf16(xs_ref[...], g_ref[...])

    o_ref[...] = lax.dot_general(h_ref[...], w_ref[...], (((1,), (1,)), ((), ())), preferred_element_type=F32)


def _split_maps(npt):
    return (lambda i, *_: (jnp.minimum(i, npt - 1), 0)), (lambda i, *_: (jnp.maximum(i - npt, 0), 0))


def _row_split(xp, xs, lp_rows):
    if xs is None:
        return xp, xp, lp_rows, xp.shape[0] - lp_rows, xp.shape[0]
    return xp, xs, xp.shape[0], xs.shape[0], xp.shape[0]


def _in_proj(xp, xs, lp_rows, g, w, layer):
    xp, xs, lp, ms, first_rows = _row_split(xp, xs, lp_rows)
    tm = _pick(math.gcd(lp, ms), (512, 256, 128, 64))
    tn = PROJ_TN
    npt = first_rows // tm
    pmap, smap = _split_maps(npt)
    vmem = 4 * tm * D_MODEL * 4 + tm * D_MODEL * 2 + 2 * D_MODEL * tn * 2 + 2 * tm * tn * 4 + (4 << 20)
    return pl.pallas_call(
        functools.partial(_in_proj_kernel, npt),
        grid=((lp + ms) // tm, PACKED // tn),
        in_specs=[pl.BlockSpec((tm, D_MODEL), pmap),
                  pl.BlockSpec((tm, D_MODEL), smap),
                  pl.BlockSpec((1, D_MODEL), lambda i, j: (0, 0)),
                  pl.BlockSpec((None, tn, D_MODEL), lambda i, j: (layer, j, 0))],
        out_specs=pl.BlockSpec((tm, tn), lambda i, j: (i, j)),
        out_shape=jax.ShapeDtypeStruct((lp + ms, PACKED), F32),
        scratch_shapes=[pltpu.VMEM((tm, D_MODEL), BF16)],
        compiler_params=_cparams(("parallel", "arbitrary"), vmem),
        name="in_proj",
    )(xp, xs, g, w)


def _ssd_kernel(ncp, z_ref, xs_ref, bc_ref, sm_ref, st_in_ref, cpx_ref, cpb_ref,
                cwx_ref, cbx_ref, cwb_ref, cbb_ref, dtb_ref, alog_ref, dexp_ref, ng_ref, e_ref,
                y_ref, hp_ref, hs_ref, st_scr, fx_scr, fb_scr):
    c = pl.program_id(0)
    q = CHUNK

    @pl.when(c == 0)
    def _():
        st_scr[...] = jnp.zeros(st_scr.shape, F32)

    @pl.when(c >= ncp)
    def _():
        st_scr[...] = st_in_ref[...].T

    @pl.when(jnp.logical_or(c == 0, c >= ncp))
    def _():
        fx_scr[0:8, :] = cpx_ref[0]
        fb_scr[0:8, :] = cpb_ref[0]

    def conv(u_ref, f_scr, w_ref, b_ref):
        u = u_ref[...]
        f_scr[8:8 + q, :] = u
        y = b_ref[...] + f_scr[5:5 + q, :] * w_ref[0:1, :]
        for j in range(1, SSD_CONV):
            y = y + f_scr[5 + j:5 + j + q, :] * w_ref[j:j + 1, :]
        f_scr[0:8, :] = u[q - 8:q, :]
        return y * _sigmoid(y)

    yield
    dt = _softplus(sm_ref[...] + dtb_ref[...])
    da = dt * (-jnp.exp(alog_ref[...]))
    r64 = lax.broadcasted_iota(jnp.int32, (q, q), 0)
    c64 = lax.broadcasted_iota(jnp.int32, (q, q), 1)
    tril = (c64 <= r64).astype(BF16)
    cs = _dot_select(tril, da)
    yield
    xs = conv(xs_ref, fx_scr, cwx_ref, cbx_ref)
    yield
    both = _select_dot(jnp.concatenate([dt, cs], axis=0), e_ref[...])
    dt_e = both[0:q]
    cs_e = both[q:2 * q]
    yield
    bcv = conv(bc_ref, fb_scr, cwb_ref, cbb_ref)
    yield

    row = lax.broadcasted_iota(jnp.int32, (q, SSD_WIDTH), 0)
    sidx = jnp.bitwise_and(lax.broadcasted_iota(jnp.int32, (q, SSD_WIDTH), 1), q - 1)
    cs_row = jnp.sum(jnp.where(sidx == row, cs_e, 0.0), axis=0, keepdims=True)
    cs_last = cs_e[q - 1:q, :]
    lmat = jnp.exp(jnp.where(sidx <= row, cs_e - cs_row, -jnp.inf))
    yield

    xdt = xs * dt_e
    xdt_end = (xdt * jnp.exp(cs_last - cs_e)).astype(BF16)
    xdt_b = xdt.astype(BF16)
    yield
    ecs = jnp.exp(cs_e)
    st = st_scr[...]
    st_b = st.astype(BF16)
    bcb = bcv.astype(BF16)
    yield

    r128 = lax.broadcasted_iota(jnp.int32, (2 * q, LANE), 0)
    c128 = lax.broadcasted_iota(jnp.int32, (2 * q, LANE), 1)
    bd2 = (r128 // q) == (c128 // q)

    hg = SSD_HEADS // SSD_GROUPS
    gw = hg * SSD_HEAD_DIM
    y_parts = []
    new_states = []
    for g in range(SSD_GROUPS):
        bm = bcb[:, g * SSD_STATE:(g + 1) * SSD_STATE]
        cm = bcb[:, (SSD_GROUPS + g) * SSD_STATE:(SSD_GROUPS + g + 1) * SSD_STATE]
        bm_rep = jnp.concatenate([bm] * hg, axis=0)
        cbt = lax.dot_general(cm, bm_rep, (((1,), (1,)), ((), ())),
                              preferred_element_type=F32)
        m_all = (cbt * lmat[:, g * gw:(g + 1) * gw]).astype(BF16)
        y_off = jnp.dot(cm, st_b[:, g * gw:(g + 1) * gw], preferred_element_type=F32)
        yield
        for j in range(gw // LANE):
            col = g * gw + j * LANE
            xj = xdt_b[:, col:col + LANE]
            xd = jnp.where(bd2, jnp.concatenate([xj, xj], axis=0), jnp.zeros((), BF16))
            y_parts.append(jnp.dot(m_all[:, j * LANE:(j + 1) * LANE], xd, preferred_element_type=F32)
                           + y_off[:, j * LANE:(j + 1) * LANE] * ecs[:, col:col + LANE])
            yield
        new_states.append(lax.dot_general(bm, xdt_end[:, g * gw:(g + 1) * gw], (((0,), (0,)), ((), ())),
                                          preferred_element_type=F32))
        yield
    y = jnp.concatenate(y_parts, axis=1)
    st_new = st * jnp.exp(cs_last) + jnp.concatenate(new_states, axis=1)
    st_scr[...] = st_new
    yield

    y = y + dexp_ref[...] * xs
    zv = z_ref[...]
    y = y * (zv * _sigmoid(zv))
    yield
    outs = []
    for g in range(SSD_GROUPS):
        yg = y[:, g * gw:(g + 1) * gw]
        ms = jnp.mean(yg * yg, axis=-1, keepdims=True)
        outs.append(yg * lax.rsqrt(ms + EPS))
    y_ref[...] = jnp.concatenate(outs, axis=1) * ng_ref[...]

    def finish():
        @pl.when(c == ncp - 1)
        def _():
            hp_ref[...] = st_scr[...].T

        @pl.when(c >= ncp)
        def _():
            hs_ref[...] = st_scr[...].T

    return finish


def _ssd_call(proj, ncp, layer, state, cpx, cpb, cwx, cbx, cwb, cbb, dtb, alog, dexp, ng, emat):
    m = proj.shape[0]
    q = CHUNK
    nstreams = state.shape[1]
    smap = lambda c: (jnp.maximum(c - (ncp - 1), 0), 0, 0)
    stream = lambda c: jnp.maximum(c - ncp, 0)
    const2 = lambda c: (0, 0)
    return dict(
        in_specs=[pl.BlockSpec((q, 1024), lambda c: (c, C_Z // 1024)),
                  pl.BlockSpec((q, 1024), lambda c: (c, C_XS // 1024)),
                  pl.BlockSpec((q, 512), lambda c: (c, C_BC // 512)),
                  pl.BlockSpec((q, LANE), lambda c: (c, C_SM // LANE)),
                  pl.BlockSpec((None, None, SSD_WIDTH, SSD_STATE), lambda c: (layer, stream(c), 0, 0)),
                  pl.BlockSpec((1, 8, 1024), smap),
                  pl.BlockSpec((1, 8, 512), smap),
                  pl.BlockSpec((SSD_CONV, 1024), const2),
                  pl.BlockSpec((1, 1024), const2),
                  pl.BlockSpec((SSD_CONV, 512), const2),
                  pl.BlockSpec((1, 512), const2),
                  pl.BlockSpec((1, LANE), const2),
                  pl.BlockSpec((1, LANE), const2),
                  pl.BlockSpec((1, SSD_WIDTH), const2),
                  pl.BlockSpec((1, SSD_WIDTH), const2),
                  pl.BlockSpec((LANE, SSD_WIDTH), const2)],
        out_specs=[pl.BlockSpec((q, SSD_WIDTH), lambda c: (c, 0)),
                   pl.BlockSpec((SSD_WIDTH, SSD_STATE), const2),
                   pl.BlockSpec((None, SSD_WIDTH, SSD_STATE), lambda c: (stream(c), 0, 0))],
        out_shape=[jax.ShapeDtypeStruct((m, SSD_WIDTH), F32),
                   jax.ShapeDtypeStruct((SSD_WIDTH, SSD_STATE), F32),
                   jax.ShapeDtypeStruct((nstreams, SSD_WIDTH, SSD_STATE), F32)],
        scratch_shapes=[pltpu.VMEM((SSD_STATE, SSD_WIDTH), F32),
                        pltpu.VMEM((8 + q, 1024), F32),
                        pltpu.VMEM((8 + q, 512), F32)],
        args=(proj, proj, proj, proj, state, cpx, cpb, cwx, cbx, cwb, cbb, dtb, alog, dexp, ng, emat))


def _gla_kernel(ncp, gq_ref, gk_ref, gv_ref, gg_ref, sm_ref, s_in_ref, wa_ref, ba_ref, ng_ref,
                o_ref, s_out_ref, s_scr):
    c = pl.program_id(0)
    q = CHUNK
    blk = GLA_BLOCK
    kw = GLA_HEADS * GLA_DK

    @pl.when(jnp.logical_or(c == 0, c >= ncp))
    def _():
        s_scr[...] = s_in_ref[0]

    yield
    pre = jnp.dot(sm_ref[...].astype(BF16), wa_ref[...], preferred_element_type=F32) + ba_ref[...]
    yield
    log_a = -_softplus(-pre) * (1.0 / GLA_TAU)
    r64 = lax.broadcasted_iota(jnp.int32, (q, q), 0)
    c64 = lax.broadcasted_iota(jnp.int32, (q, q), 1)
    same_blk = (r64 // blk) == (c64 // blk)
    causal = jnp.logical_and(same_blk, c64 <= r64)
    yield
    b = _dot_select(causal.astype(BF16), log_a)
    yield
    gk = gk_ref[...]
    qt = gq_ref[...] * (GLA_DK ** -0.5) * jnp.exp(b)
    kt = (gk * jnp.exp(-b)).astype(BF16)
    yield
    b_last = jnp.concatenate(
        [jnp.broadcast_to(b[(j + 1) * blk - 1:(j + 1) * blk, :], (blk, kw)) for j in range(q // blk)], axis=0)
    ktil = (gk * jnp.exp(b_last - b)).astype(BF16)
    qt_b = qt.astype(BF16)
    gv_b = gv_ref[...].astype(BF16)
    yield

    lane_head = lax.broadcasted_iota(jnp.int32, (q, kw), 1) // GLA_DK
    o_intra = []
    for h in range(GLA_HEADS):
        qm = jnp.where(lane_head == h, qt_b, jnp.zeros((), BF16))
        att = lax.dot_general(qm, kt, (((1,), (1,)), ((), ())), preferred_element_type=F32)
        yield
        att = jnp.where(causal, att, 0.0).astype(BF16)
        o_intra.append(jnp.dot(att, gv_b[:, h * GLA_DV:(h + 1) * GLA_DV], preferred_element_type=F32))
        yield

    rs = lax.broadcasted_iota(jnp.int32, (GLA_WIDTH, kw), 0) // GLA_DV
    cs_ = lax.broadcasted_iota(jnp.int32, (GLA_WIDTH, kw), 1) // GLA_DK
    bd = rs == cs_
    sw = s_scr[...]
    o_inter = []
    for j in range(q // blk):
        sl = slice(j * blk, (j + 1) * blk)
        o_inter.append(lax.dot_general(qt_b[sl], sw.astype(BF16), (((1,), (1,)), ((), ())),
                                       preferred_element_type=F32))
        yield
        upd = lax.dot_general(gv_b[sl], ktil[sl], (((0,), (0,)), ((), ())),
                              preferred_element_type=F32)
        yield
        decay = jnp.exp(b[(j + 1) * blk - 1:(j + 1) * blk, :])
        sw = sw * decay + jnp.where(bd, upd, 0.0)
        yield
    s_scr[...] = sw
    s_out_ref[0] = sw
    o = jnp.concatenate(o_intra, axis=1) + jnp.concatenate(o_inter, axis=0)
    yield

    gg = gg_ref[...]
    gate = gg * _sigmoid(gg)
    outs = []
    for h in range(GLA_HEADS):
        oh = o[:, h * GLA_DV:(h + 1) * GLA_DV]
        ms = jnp.mean(oh * oh, axis=-1, keepdims=True)
        outs.append(oh * lax.rsqrt(ms + EPS) * ng_ref[...])
    o_ref[...] = jnp.concatenate(outs, axis=1) * gate


def _gla_call(proj, ncp, s_in, wa_pad, ba, ng):
    m = proj.shape[0]
    q = CHUNK
    kw = GLA_HEADS * GLA_DK
    smap = lambda c: (jnp.maximum(c - (ncp - 1), 0), 0, 0)
    const2 = lambda c: (0, 0)
    return dict(
        in_specs=[pl.BlockSpec((q, kw), lambda c: (c, C_GQ // kw)),
                  pl.BlockSpec((q, kw), lambda c: (c, C_GK // kw)),
                  pl.BlockSpec((q, GLA_WIDTH), lambda c: (c, C_GV // GLA_WIDTH)),
                  pl.BlockSpec((q, GLA_WIDTH), lambda c: (c, C_GG // GLA_WIDTH)),
                  pl.BlockSpec((q, LANE), lambda c: (c, C_SM // LANE)),
                  pl.BlockSpec((1, GLA_WIDTH, kw), smap),
                  pl.BlockSpec((LANE, kw), const2),
                  pl.BlockSpec((1, kw), const2),
                  pl.BlockSpec((1, GLA_DV), const2)],
        out_specs=[pl.BlockSpec((q, GLA_WIDTH), lambda c: (c, 0)),
                   pl.BlockSpec((1, GLA_WIDTH, kw), smap)],
        out_shape=[jax.ShapeDtypeStruct((m, GLA_WIDTH), F32),
                   jax.ShapeDtypeStruct(s_in.shape, F32)],
        scratch_shapes=[pltpu.VMEM((GLA_WIDTH, kw), F32)],
        args=(proj, proj, proj, proj, proj, s_in, wa_pad, ba, ng))


def _mixers_kernel(ncp, n_in, n_out, n_scr, *refs):
    i0, i1 = n_in
    o0, o1 = n_out
    s0, s1 = n_scr
    ins, outs, scr = refs[:i0 + i1], refs[i0 + i1:i0 + i1 + o0 + o1], refs[i0 + i1 + o0 + o1:]
    streams = [(_gla_kernel(ncp, *ins[i0:], *outs[o0:], *scr[s0:]), 1),
               (_ssd_kernel(ncp, *ins[:i0], *outs[:o0], *scr[:s0]), 2)]
    finishers = []
    while streams:
        for entry in list(streams):
            for _ in range(entry[1]):
                try:
                    next(entry[0])
                except StopIteration as done:
                    finishers.append(done.value)
                    streams.remove(entry)
                    break
    for finish in finishers:
        if finish is not None:
            finish()


def _mixers(ncp, ssd, gla):
    m = ssd["args"][0].shape[0]
    counts = [(len(ssd[k]), len(gla[k])) for k in ("in_specs", "out_specs", "scratch_shapes")]
    return pl.pallas_call(
        functools.partial(_mixers_kernel, ncp, *counts),
        grid=(m // CHUNK,),
        in_specs=ssd["in_specs"] + gla["in_specs"],
        out_specs=ssd["out_specs"] + gla["out_specs"],
        out_shape=ssd["out_shape"] + gla["out_shape"],
        scratch_shapes=ssd["scratch_shapes"] + gla["scratch_shapes"],
        compiler_params=_cparams(("arbitrary",), 40 << 20),
        name="mixers",
    )(*ssd["args"], *gla["args"])


def _attn_prep_kernel(npt, dq_ref, dk_ref, dv_ref, qg_ref, kg_ref, seg_ref,
                      qa_ref, qb_ref, kn_ref, kb_ref, vt_ref, kp_ref, vp_ref, ks_ref, vs_ref):
    seg = seg_ref[...]
    tm = dq_ref.shape[0]

    def qknorm(x, g):
        ms = _select_dot(x * x, seg) * (1.0 / DIFF_HD)
        return x * lax.rsqrt(ms + EPS) * g

    qn = qknorm(dq_ref[...], qg_ref[...]) * (DIFF_HD ** -0.5 * LOG2E)
    first = (lax.broadcasted_iota(jnp.int32, qn.shape, 1) // DIFF_HD) % 2 == 0
    qa = jnp.where(first, qn, 0.0).astype(BF16)
    qb = jnp.where(first, 0.0, qn).astype(BF16)
    kn = qknorm(dk_ref[...], kg_ref[...])
    kn_ref[...] = kn
    kb = kn.astype(BF16)
    for h in range(DIFF_HEADS):
        sl = slice(h * LANE, (h + 1) * LANE)
        qa_ref[h] = qa[:, sl]
        qb_ref[h] = qb[:, sl]
        kb_ref[h] = kb[:, sl]
    dv = dv_ref[...]
    vt_ref[...] = dv.T.astype(BF16)

    def cache_rows(k_ref, v_ref):
        for h in range(DIFF_HEADS):
            k_ref[pl.ds(h, tm, stride=DIFF_HEADS), :] = kn[:, h * LANE:(h + 1) * LANE]
            v_ref[pl.ds(h, tm, stride=DIFF_HEADS), :] = dv[:, h * LANE:(h + 1) * LANE]

    @pl.when(pl.program_id(0) < npt)
    def _():
        cache_rows(kp_ref, vp_ref)

    @pl.when(pl.program_id(0) >= npt)
    def _():
        cache_rows(ks_ref, vs_ref)


def _attn_prep(proj, lp, qg, kg, segmat):
    m = proj.shape[0]
    ms = m - lp
    tm = _pick(math.gcd(lp, ms), (512, 256, 128))
    w = DIFF_WIDTH
    npt = lp // tm
    pmap, smap = _split_maps(npt)
    row = lambda off: pl.BlockSpec((tm, w), lambda i: (i, off // w))
    const2 = lambda i: (0, 0)
    rows4 = tm * DIFF_HEADS
    hm = pl.BlockSpec((DIFF_HEADS, tm, LANE), lambda i: (0, i, 0))
    return pl.pallas_call(
        functools.partial(_attn_prep_kernel, npt),
        grid=(m // tm,),
        in_specs=[row(C_DQ), row(C_DK), row(C_DV),
                  pl.BlockSpec((1, w), const2), pl.BlockSpec((1, w), const2),
                  pl.BlockSpec((w, w), const2)],
        out_specs=[hm, hm, pl.BlockSpec((tm, w), lambda i: (i, 0)), hm, pl.BlockSpec((w, tm), lambda i: (0, i))]
        + [pl.BlockSpec((rows4, LANE), pmap)] * 2 + [pl.BlockSpec((rows4, LANE), smap)] * 2,
        out_shape=[jax.ShapeDtypeStruct((DIFF_HEADS, m, LANE), BF16), jax.ShapeDtypeStruct((DIFF_HEADS, m, LANE), BF16),
                   jax.ShapeDtypeStruct((m, w), F32), jax.ShapeDtypeStruct((DIFF_HEADS, m, LANE), BF16),
                   jax.ShapeDtypeStruct((w, m), BF16)]
        + [jax.ShapeDtypeStruct((lp * DIFF_HEADS, LANE), F32)] * 2
        + [jax.ShapeDtypeStruct((ms * DIFF_HEADS, LANE), F32)] * 2,
        compiler_params=_cparams(("arbitrary",), 40 << 20),
        name="attn_prep",
    )(proj, proj, proj, qg, kg, segmat)


def _lambda_full(lam_ref, lam_init):
    l = lam_ref[...]
    a = jnp.sum(l[0:1] * l[1:2], axis=-1, keepdims=True)
    b = jnp.sum(l[2:3] * l[3:4], axis=-1, keepdims=True)
    return jnp.exp(a) - jnp.exp(b) + lam_init


ATTN_PEEK = 64
ATTN_HEADS_PER_ITER = 2
ATTN_MAX_SUM = 2.0 ** 64
ATTN_MIN_SUM = 2.0 ** -60


def _fold8(x, op):
    r = x[0:SUBLANE]
    for i in range(1, x.shape[0] // SUBLANE):
        r = op(r, x[i * SUBLANE:(i + 1) * SUBLANE])
    return r


def _attn_prompt_kernel(t, kbk, lam_init, qi_ref, ki_ref, slope_ref, qa_ref, qb_ref, kb_ref, vt_ref, lam_ref,
                        og_ref, o_ref, m_scr, l_scr, acc_scr, b0_scr, bd_scr):
    p = pl.program_id(0)
    qi = qi_ref[p]
    ki = ki_ref[p]
    nb = t // kbk
    w = 2 * t
    nt = (((1,), (1,)), ((), ()))

    @pl.when(p == 0)
    def _():
        b0_scr[...] = lax.broadcasted_iota(jnp.int32, (kbk, LANE), 0).astype(F32)
        kr = lax.broadcasted_iota(jnp.int32, (t, w), 0)
        lane = lax.broadcasted_iota(jnp.int32, (t, w), 1)
        qc = (lane // (2 * kbk)) * kbk + lane % kbk
        corr = jnp.minimum(2 * (qc - kr), 0).astype(F32)
        bd_scr[...] = jnp.where(kr // CHUNK <= qc // CHUNK, corr, -jnp.inf)

    @pl.when(ki == 0)
    def _():
        m_scr[...] = jnp.full(m_scr.shape, -jnp.inf, F32)
        l_scr[...] = jnp.zeros(l_scr.shape, F32)
        acc_scr[...] = jnp.zeros(acc_scr.shape, F32)

    def tile(h, diag):
        slope = slope_ref[h]
        shift = slope * ((ki - qi) * t).astype(F32)
        step = slope * float(kbk)
        qcat = jnp.concatenate([r[h, g * kbk:(g + 1) * kbk, :] for g in range(nb) for r in (qa_ref, qb_ref)],
                               axis=0)
        b0 = slope * b0_scr[...]
        b0 = jnp.concatenate([b0] * (w // LANE), axis=1)
        m_h, l_h, acc_h = m_scr.at[h], l_scr.at[h], acc_scr.at[h]

        def scores(rows, nrows, lo):
            s = lax.dot_general(kb_ref[h, rows, :], qcat[lo:, :], nt, preferred_element_type=F32) + b0[0:nrows, lo:]
            if diag:
                s = s + slope * bd_scr[rows, lo:]
            return s

        m_old = m_h[...]
        peek = scores(slice(0, ATTN_PEEK), ATTN_PEEK, 0)
        yield
        m_used = jnp.maximum(m_old, jnp.max(peek, axis=0, keepdims=True) + shift)
        if diag:
            lane = lax.broadcasted_iota(jnp.int32, (1, w), 1)
            qpos = (lane // (2 * kbk)) * kbk + lane % kbk
            ramp = jnp.maximum(qpos - (ATTN_PEEK - 1), 0).astype(F32)
        else:
            ramp = float(t - ATTN_PEEK)
        m_ref = m_used + slope * ramp
        ones = jnp.ones((SUBLANE, kbk), BF16)
        pv = None
        los = [2 * b * kbk if diag else 0 for b in range(nb)]
        sc = []
        for b in range(nb):
            sc.append(scores(slice(b * kbk, (b + 1) * kbk), kbk, los[b]))
            yield
        for b in range(nb):
            rows = slice(b * kbk, (b + 1) * kbk)
            lo = los[b]
            e = sc[b] - (m_ref[:, lo:] - shift - b * step)
            pt = jnp.exp2(e).astype(BF16)
            yield
            d = jnp.dot(jnp.concatenate([vt_ref[h, :, rows], ones], axis=0), pt,
                        preferred_element_type=F32)
            if b == 0:
                pv = d
            elif lo == 0:
                pv = d + pv
            else:
                pv = jnp.concatenate([pv[:, :lo], d + pv[:, lo:]], axis=1)
            yield
        lsum = pv[DIFF_VD:DIFF_VD + 1, :]
        pv = pv[0:DIFF_VD, :]

        def finish():
            ok = jnp.logical_and(jnp.max(lsum) <= ATTN_MAX_SUM, jnp.min(lsum) >= ATTN_MIN_SUM)

            @pl.when(ok)
            def _():
                a_old = jnp.exp2(m_old - m_ref)
                l_h[...] = a_old * l_h[...] + lsum
                acc_h[...] = a_old * acc_h[...] + pv
                m_h[...] = m_ref

            @pl.when(jnp.logical_not(ok))
            def _():
                def body(b, carry):
                    rows = pl.ds(pl.multiple_of(b * kbk, kbk), kbk)
                    s = scores(rows, kbk, 0) + (shift + slope * (b * kbk).astype(F32))
                    m_o = m_h[...]
                    m_n = jnp.maximum(m_o, jnp.max(s, axis=0, keepdims=True))
                    pt = jnp.exp2(s - m_n)
                    a = jnp.exp2(m_o - m_n)
                    l_h[...] = a * l_h[...] + jnp.sum(pt, axis=0, keepdims=True)
                    acc_h[...] = a * acc_h[...] + jnp.dot(vt_ref[h, :, rows], pt.astype(BF16),
                                                          preferred_element_type=F32)
                    m_h[...] = m_n
                    return carry

                lax.fori_loop(0, nb, body, 0)

        return finish

    def tiles(heads, diag):
        gens = [tile(h, diag) for h in heads]
        finishers = [None] * len(gens)
        live = list(range(len(gens)))
        while live:
            for n in list(live):
                try:
                    next(gens[n])
                except StopIteration as done:
                    finishers[n] = done.value
                    live.remove(n)
        for finish in finishers:
            finish()

    def head_group(hp, carry):
        heads = [hp * ATTN_HEADS_PER_ITER + n for n in range(ATTN_HEADS_PER_ITER)]

        @pl.when(ki < qi)
        def _():
            tiles(heads, False)

        @pl.when(ki == qi)
        def _():
            tiles(heads, True)
            lam = _lambda_full(lam_ref, lam_init)
            for h in heads:
                on = acc_scr[h] / l_scr[h]
                ot = jnp.concatenate(
                    [on[:, 2 * g * kbk:(2 * g + 1) * kbk] - lam * on[:, (2 * g + 1) * kbk:(2 * g + 2) * kbk]
                     for g in range(nb)], axis=1)
                ms = jnp.mean(ot * ot, axis=0, keepdims=True)
                o_ref[h] = (ot * lax.rsqrt(ms + EPS)).T * (og_ref[...] * (1.0 - lam_init))

        return carry

    lax.fori_loop(0, DIFF_HEADS // ATTN_HEADS_PER_ITER, head_group, 0)


def _attn_prompt(lp, lam_init, qa, qb, kb, vt, slopes, lam_p, og):
    t = _pick(lp, (1024, 512, 256))
    kbk = min(t, 512)
    nq = lp // t
    pairs = [(i, j) for i in range(nq) for j in range(i + 1)]
    qi_l = jnp.asarray(np.array([a for a, _ in pairs], np.int32))
    ki_l = jnp.asarray(np.array([b for _, b in pairs], np.int32))
    nh = DIFF_HEADS
    grid_spec = pltpu.PrefetchScalarGridSpec(
        num_scalar_prefetch=3,
        grid=(len(pairs),),
        in_specs=[pl.BlockSpec((nh, t, LANE), lambda p, qi, ki, sl: (0, qi[p], 0)),
                  pl.BlockSpec((nh, t, LANE), lambda p, qi, ki, sl: (0, qi[p], 0)),
                  pl.BlockSpec((nh, t, LANE), lambda p, qi, ki, sl: (0, ki[p], 0)),
                  pl.BlockSpec((nh, DIFF_VD, t), lambda p, qi, ki, sl: (0, 0, ki[p])),
                  pl.BlockSpec((4, DIFF_HD), lambda p, qi, ki, sl: (0, 0)),
                  pl.BlockSpec((1, DIFF_VD), lambda p, qi, ki, sl: (0, 0))],
        out_specs=pl.BlockSpec((nh, t, LANE), lambda p, qi, ki, sl: (0, qi[p], 0)),
        scratch_shapes=[pltpu.VMEM((nh, 1, 2 * t), F32), pltpu.VMEM((nh, 1, 2 * t), F32),
                        pltpu.VMEM((nh, DIFF_VD, 2 * t), F32),
                        pltpu.VMEM((kbk, LANE), F32), pltpu.VMEM((t, 2 * t), F32)])
    return pl.pallas_call(
        functools.partial(_attn_prompt_kernel, t, kbk, lam_init),
        grid_spec=grid_spec,
        out_shape=jax.ShapeDtypeStruct((nh, lp, LANE), F32),
        compiler_params=_cparams(("arbitrary",), 48 << 20),
        name="attn_prompt",
    )(qi_l, ki_l, slopes, qa, qb, kb, vt.reshape(nh, DIFF_VD, vt.shape[1]), lam_p, og)


def _attn_sample_kernel(past, lam_init, slope_ref, qa_ref, qb_ref, kn_ref, vn_ref, kc_ref, vc_ref, lam_ref, og_ref,
                        o_ref):
    q = CHUNK
    lam = _lambda_full(lam_ref, lam_init)
    kpos = lax.broadcasted_iota(jnp.int32, (1, past), 1).astype(F32)
    r = lax.broadcasted_iota(jnp.int32, (q, q), 0)
    cc = lax.broadcasted_iota(jnp.int32, (q, q), 1)
    new_bias = (past + r - jnp.abs(r - cc)).astype(F32)
    new_bias = jnp.concatenate([new_bias, new_bias], axis=0)
    outs = []
    for h in range(DIFF_HEADS):
        slope = slope_ref[h]
        sl = slice(h * LANE, (h + 1) * LANE)
        kc = kc_ref[pl.ds(h, past, stride=DIFF_HEADS), :].astype(BF16)
        vc = vc_ref[pl.ds(h, past, stride=DIFF_HEADS), :].astype(BF16)
        kn = kn_ref[:, sl].astype(BF16)
        vn = vn_ref[:, sl].astype(BF16)
        qv = jnp.concatenate([qa_ref[h], qb_ref[h]], axis=0)
        s_c = lax.dot_general(qv, kc, (((1,), (1,)), ((), ())), preferred_element_type=F32) + slope * kpos
        s_n = lax.dot_general(qv, kn, (((1,), (1,)), ((), ())), preferred_element_type=F32) + slope * new_bias
        mx = jnp.maximum(jnp.max(s_c, axis=-1, keepdims=True), jnp.max(s_n, axis=-1, keepdims=True))
        p_c = jnp.exp2(s_c - mx)
        p_n = jnp.exp2(s_n - mx)
        den = jnp.sum(p_c, axis=-1, keepdims=True) + jnp.sum(p_n, axis=-1, keepdims=True)
        num = (jnp.dot(p_c.astype(BF16), vc, preferred_element_type=F32)
               + jnp.dot(p_n.astype(BF16), vn, preferred_element_type=F32))
        res = num / den
        o = res[0:q] - lam * res[q:2 * q]
        ms = jnp.mean(o * o, axis=-1, keepdims=True)
        outs.append(o * lax.rsqrt(ms + EPS) * (og_ref[...] * (1.0 - lam_init)))
    o_ref[...] = jnp.concatenate(outs, axis=1)


def _attn_sample(lp, nb, layer, lam_init, qa, qb, kn, proj, cache_k, cache_v, slopes, lam_p, og):
    depth, _, past = cache_k.shape[:3]
    q = CHUNK
    w = DIFF_WIDTH
    off = lp // q
    cache_k = cache_k.reshape(depth, nb, past * DIFF_HEADS, DIFF_VD)
    cache_v = cache_v.reshape(depth, nb, past * DIFF_HEADS, DIFF_VD)
    cache_block = (None, None, past * DIFF_HEADS, DIFF_VD)
    grid_spec = pltpu.PrefetchScalarGridSpec(
        num_scalar_prefetch=1,
        grid=(nb,),
        in_specs=[pl.BlockSpec((DIFF_HEADS, q, LANE), lambda b, sl: (0, off + b, 0)),
                  pl.BlockSpec((DIFF_HEADS, q, LANE), lambda b, sl: (0, off + b, 0)),
                  pl.BlockSpec((q, w), lambda b, sl: (off + b, 0)),
                  pl.BlockSpec((q, w), lambda b, sl: (off + b, C_DV // w)),
                  pl.BlockSpec(cache_block, lambda b, sl: (layer, b, 0, 0)),
                  pl.BlockSpec(cache_block, lambda b, sl: (layer, b, 0, 0)),
                  pl.BlockSpec((4, DIFF_HD), lambda b, sl: (0, 0)),
                  pl.BlockSpec((1, DIFF_VD), lambda b, sl: (0, 0))],
        out_specs=pl.BlockSpec((q, w), lambda b, sl: (b, 0)))
    return pl.pallas_call(
        functools.partial(_attn_sample_kernel, past, lam_init),
        grid_spec=grid_spec,
        out_shape=jax.ShapeDtypeStruct((nb * q, DIFF_WIDTH), F32),
        compiler_params=_cparams(("arbitrary",), 32 << 20),
        name="attn_sample",
    )(slopes, qa, qb, kn, proj, cache_k, cache_v, lam_p, og)


def _out_proj_kernel(npt_x, npt, xp_ref, xs_ref, ys_ref, yap_ref, yas_ref, yg_ref, w_ref, o_ref):
    is_prompt = pl.program_id(0) < npt
    yap = jnp.concatenate([yap_ref[h] for h in range(DIFF_HEADS)], axis=1)
    ya = jnp.where(is_prompt, yap, yas_ref[...])
    acc = jnp.dot(ys_ref[...].astype(BF16), w_ref[0:SSD_WIDTH, :], preferred_element_type=F32)
    acc = acc + jnp.dot(ya.astype(BF16), w_ref[SSD_WIDTH:SSD_WIDTH + DIFF_WIDTH, :],
                        preferred_element_type=F32)
    acc = acc + jnp.dot(yg_ref[...].astype(BF16), w_ref[SSD_WIDTH + DIFF_WIDTH:, :],
                        preferred_element_type=F32)
    o_ref[...] = jnp.where(pl.program_id(0) < npt_x, xp_ref[...], xs_ref[...]) + acc


def _out_proj(xp, xs, lp_rows, ys, yap, yas, yg, w, layer):
    xp, xs, lp, ms, first_rows = _row_split(xp, xs, lp_rows)
    tm = _pick(math.gcd(lp, ms), (512, 256, 128, 64))
    npt = lp // tm
    pmap, smap = _split_maps(npt)
    xpmap, xsmap = _split_maps(first_rows // tm)
    vmem = 2 * (5 * tm * D_MODEL * 4) + D_MODEL * D_MODEL * 2 + (4 << 20)
    return pl.pallas_call(
        functools.partial(_out_proj_kernel, first_rows // tm, npt),
        grid=((lp + ms) // tm,),
        in_specs=[pl.BlockSpec((tm, D_MODEL), xpmap),
                  pl.BlockSpec((tm, D_MODEL), xsmap),
                  pl.BlockSpec((tm, SSD_WIDTH), lambda i: (i, 0)),
                  pl.BlockSpec((DIFF_HEADS, tm, LANE), lambda i: (0, jnp.minimum(i, npt - 1), 0)),
                  pl.BlockSpec((tm, DIFF_WIDTH), smap),
                  pl.BlockSpec((tm, GLA_WIDTH), lambda i: (i, 0)),
                  pl.BlockSpec((None, D_MODEL, D_MODEL), lambda i: (layer, 0, 0), pipeline_mode=pl.Buffered(1))],
        out_specs=pl.BlockSpec((tm, D_MODEL), lambda i: (i, 0)),
        out_shape=jax.ShapeDtypeStruct((lp + ms, D_MODEL), F32),
        compiler_params=_cparams(("parallel",), vmem),
        name="out_proj",
    )(xp, xs, ys, yap, yas, yg, w)


def _mlp_kernel(x_ref, g_ref, w1_ref, w2_ref, o_ref, h_ref):
    @pl.when(pl.program_id(1) == 0)
    def _():
        x = x_ref[...]
        h_ref[...] = _rms_bf16(x, g_ref[...])
        o_ref[...] = x

    h = h_ref[...]
    half = w1_ref.shape[1] // 2
    a0 = jnp.dot(h, w1_ref[:, 0:half], preferred_element_type=F32)
    a1 = jnp.dot(h, w1_ref[:, half:], preferred_element_type=F32)
    a0 = jnp.square(jnp.maximum(a0, 0.0)).astype(BF16)
    upd = jnp.dot(a0, w2_ref[0:half, :], preferred_element_type=F32)
    a1 = jnp.square(jnp.maximum(a1, 0.0)).astype(BF16)
    upd = jnp.dot(a1, w2_ref[half:, :], preferred_element_type=F32) + upd
    o_ref[...] += upd


def _mlp(x, g, w1, w2, layer):
    m = x.shape[0]
    tm = _pick(m, (512, 256, 128, 64))
    tf = 1024
    vmem = 4 * tm * D_MODEL * 4 + tm * D_MODEL * 2 + 4 * D_MODEL * tf * 2 + 3 * tm * tf * 4 + tm * D_MODEL * 4 + (4 << 20)
    return pl.pallas_call(
        _mlp_kernel,
        grid=(m // tm, D_FF // tf),
        in_specs=[pl.BlockSpec((tm, D_MODEL), lambda i, f: (i, 0)),
                  pl.BlockSpec((1, D_MODEL), lambda i, f: (0, 0)),
                  pl.BlockSpec((None, D_MODEL, tf), lambda i, f: (layer, 0, f)),
                  pl.BlockSpec((None, tf, D_MODEL), lambda i, f: (layer, f, 0))],
        out_specs=pl.BlockSpec((tm, D_MODEL), lambda i, f: (i, 0)),
        out_shape=jax.ShapeDtypeStruct((m, D_MODEL), F32),
        scratch_shapes=[pltpu.VMEM((tm, D_MODEL), BF16)],
        compiler_params=_cparams(("parallel", "arbitrary"), vmem),
        name="mlp",
    )(x, g, w1, w2)


def _pack_w_in_kernel(w_ref, o_ref):
    x = w_ref[...]
    o_ref[...] = jnp.concatenate(
        [x[0:2560], x[2576:4112], x[4624:5136], x[5152:5664], x[4112:4624],
         x[2560:2576], x[5136:5152], jnp.zeros((PACKED - 5664, x.shape[1]), x.dtype)],
        axis=0).astype(BF16)


def _pack_w_in(w):
    depth, rows, cols = w.shape
    wt = jnp.swapaxes(w, 1, 2)
    tc = _pick(rows, (256, 128))
    return pl.pallas_call(
        _pack_w_in_kernel,
        grid=(depth, rows // tc),
        in_specs=[pl.BlockSpec((None, cols, tc), lambda l, i: (l, 0, i))],
        out_specs=pl.BlockSpec((None, PACKED, tc), lambda l, i: (l, 0, i)),
        out_shape=jax.ShapeDtypeStruct((depth, PACKED, rows), BF16),
        compiler_params=_cparams(("parallel", "parallel"), 32 << 20),
        name="pack_w_in",
    )(wt)


def _pad_lanes(v, width=LANE):
    v = v.reshape(1, -1)
    return jnp.pad(v, ((0, 0), (0, width - v.shape[1])))


def kernel(x_prompt, x_sample, cache_diff_k, cache_diff_v, state_ssd_conv, state_ssd, state_gla, norm1_g, w_in, ssd_conv_w, ssd_conv_b, ssd_dt_bias, ssd_a_log, ssd_d, ssd_norm_g, diff_qn_g, diff_kn_g, diff_lambda, diff_out_g, gla_wa2, gla_ba, gla_norm_g, w_out, norm2_g, w_mlp1, w_mlp2):
    bp, lp, d = x_prompt.shape
    nb, ls, _ = x_sample.shape
    depth = w_in.shape[0]
    past = cache_diff_k.shape[2]
    assert bp == 1 and d == D_MODEL and ls == CHUNK and lp % CHUNK == 0 and past % CHUNK == 0
    ncp = lp // CHUNK
    m = lp + nb * ls

    xp, xs = x_prompt.reshape(lp, d), x_sample.reshape(nb * ls, d)

    emat = (jnp.arange(LANE)[:, None] == (jnp.arange(SSD_WIDTH)[None, :] // SSD_HEAD_DIM)).astype(BF16)
    segmat = ((jnp.arange(DIFF_WIDTH)[:, None] // DIFF_HD) == (jnp.arange(DIFF_WIDTH)[None, :] // DIFF_HD)).astype(BF16)
    slopes = jnp.exp2(-8.0 * jnp.arange(1, DIFF_HEADS + 1, dtype=F32) / DIFF_HEADS) * LOG2E
    eye_h = jnp.eye(GLA_HEADS, dtype=F32)
    w_in_b, w_out_b = _pack_w_in(w_in), w_out.astype(BF16)
    w_mlp1_b, w_mlp2_b = w_mlp1.astype(BF16), w_mlp2.astype(BF16)

    outs = {k: [] for k in ("kp", "vp", "cp", "hp", "sp", "ks", "vs", "cs", "hs", "ss")}
    for l in range(depth):
        lam_init = 0.8 - 0.6 * math.exp(-0.3 * l)
        proj = _in_proj(xp, xs, lp, norm1_g[l].reshape(1, d), w_in_b, l)

        cprev = jnp.concatenate([jnp.zeros((1, SSD_CONV - 1, SSD_CONV_DIM), F32), state_ssd_conv[l]], axis=0)
        cprev = jnp.pad(cprev, ((0, 0), (8 - (SSD_CONV - 1), 0), (0, 0)))
        cw, cb = ssd_conv_w[l], ssd_conv_b[l].reshape(1, -1)
        ssd_call = _ssd_call(
            proj, ncp, l, state_ssd.reshape(depth, nb, SSD_WIDTH, SSD_STATE),
            cprev[:, :, :SSD_WIDTH], cprev[:, :, SSD_WIDTH:],
            cw[:, :SSD_WIDTH], cb[:, :SSD_WIDTH], cw[:, SSD_WIDTH:], cb[:, SSD_WIDTH:],
            _pad_lanes(ssd_dt_bias[l]), _pad_lanes(ssd_a_log[l]),
            jnp.repeat(ssd_d[l], SSD_HEAD_DIM).reshape(1, -1), ssd_norm_g[l].reshape(1, -1), emat)
        xbc_raw = proj[:, C_XS:C_XS + SSD_CONV_DIM]
        outs["cp"].append(xbc_raw[lp - (SSD_CONV - 1):lp][None])
        outs["cs"].append(xbc_raw[lp:].reshape(nb, ls, SSD_CONV_DIM)[:, ls - (SSD_CONV - 1):])

        s_all = jnp.concatenate([jnp.zeros((1,) + state_gla.shape[2:], F32), state_gla[l]], axis=0)
        s_in = jnp.einsum('bhkv,hg->bhvgk', s_all, eye_h).reshape(nb + 1, GLA_WIDTH, GLA_HEADS * GLA_DK)
        wa_pad = jnp.zeros((LANE, GLA_HEADS * GLA_DK), F32).at[GLA_RANK:2 * GLA_RANK].set(gla_wa2[l]).astype(BF16)
        gla_call = _gla_call(proj, ncp, s_in, wa_pad, gla_ba[l].reshape(1, -1), gla_norm_g[l].reshape(1, -1))
        y_ssd, h_p, h_s, y_gla, s_out = _mixers(ncp, ssd_call, gla_call)
        outs["hp"].append(h_p.reshape(1, SSD_HEADS, SSD_HEAD_DIM, SSD_STATE))
        outs["hs"].append(h_s.reshape(nb, SSD_HEADS, SSD_HEAD_DIM, SSD_STATE))
        s5 = s_out.reshape(nb + 1, GLA_HEADS, GLA_DV, GLA_HEADS, GLA_DK)
        s_new = jnp.stack([s5[:, h, :, h, :] for h in range(GLA_HEADS)], axis=1).transpose(0, 1, 3, 2)
        outs["sp"].append(s_new[0:1])
        outs["ss"].append(s_new[1:])

        qg = jnp.tile(diff_qn_g[l], 2 * DIFF_HEADS).reshape(1, -1)
        kg = jnp.tile(diff_kn_g[l], 2 * DIFF_HEADS).reshape(1, -1)
        qa, qb, kn, kb, vt, kp4, vp4, ks4, vs4 = _attn_prep(proj, lp, qg, kg, segmat)
        og = diff_out_g[l].reshape(1, -1)
        ya_p = _attn_prompt(lp, lam_init, qa, qb, kb, vt, slopes, diff_lambda[l], og)
        ya_s = _attn_sample(lp, nb, l, lam_init, qa, qb, kn, proj, cache_diff_k, cache_diff_v,
                            slopes, diff_lambda[l], og)
        outs["kp"].append(kp4.reshape(1, lp, DIFF_HEADS, 2 * DIFF_HD))
        outs["ks"].append(ks4.reshape(nb, ls, DIFF_HEADS, 2 * DIFF_HD))
        outs["vp"].append(vp4.reshape(1, lp, DIFF_HEADS, DIFF_VD))
        outs["vs"].append(vs4.reshape(nb, ls, DIFF_HEADS, DIFF_VD))

        x1 = _out_proj(xp, xs, lp, y_ssd, ya_p, ya_s, y_gla, w_out_b, l)
        xp, xs = _mlp(x1, norm2_g[l].reshape(1, d), w_mlp1_b, w_mlp2_b, l), None

    st = {k: jnp.stack(v) for k, v in outs.items()}
    return (xp[:lp].reshape(1, lp, d), xp[lp:].reshape(nb, ls, d),
            st["kp"], st["vp"], st["cp"], st["hp"], st["sp"],
            st["ks"], st["vs"], st["cs"], st["hs"], st["ss"])
```

```python
import functools
import math

import numpy as np
import jax
import jax.numpy as jnp
from jax import lax
from jax.experimental import pallas as pl
from jax.experimental.pallas import tpu as pltpu

F32 = jnp.float32
BF16 = jnp.bfloat16

D_MODEL = 2048
CHUNK = 64
SSD_HEAD_DIM = 64
SSD_WIDTH = 1024
SSD_HEADS = 16
SSD_GROUPS = 2
SSD_STATE = 128
SSD_CONV = 4
SSD_CONV_DIM = SSD_WIDTH + 2 * SSD_GROUPS * SSD_STATE
DIFF_HD = 64
DIFF_VD = 128
DIFF_WIDTH = 512
DIFF_HEADS = 4
GLA_DK = 64
GLA_DV = 128
GLA_WIDTH = 512
GLA_HEADS = 4
GLA_RANK = 16
GLA_TAU = 16.0
GLA_BLOCK = 16
D_FF = 4 * D_MODEL
EPS = 1e-6

LOG2E = math.log2(math.e)
LANE = 128
SUBLANE = 8
VMEM_CAP = 56 << 20

_IN_WIDTHS = (("z", SSD_WIDTH), ("xbc", SSD_CONV_DIM), ("dt", SSD_HEADS),
              ("dq", DIFF_WIDTH), ("dk", DIFF_WIDTH), ("dv", DIFF_WIDTH),
              ("gq", GLA_HEADS * GLA_DK), ("gk", GLA_HEADS * GLA_DK), ("gv", GLA_WIDTH),
              ("ga", GLA_RANK), ("gg", GLA_WIDTH))
IN_RANGE = {}
IN_COLS = 0
for _name, _width in _IN_WIDTHS:
    IN_RANGE[_name] = (IN_COLS, IN_COLS + _width)
    IN_COLS += _width
PACK_ORDER = ("z", "xbc", "dq", "dk", "dv", "gv", "gg", "gq", "gk", "dt", "ga")
C_Z, C_XS, C_BC, C_DQ, C_DK, C_DV, C_GV, C_GG, C_GQ, C_GK, C_SM = (
    0, 1024, 2048, 2560, 3072, 3584, 4096, 4608, 5120, 5376, 5632)
PACKED = 5760
PROJ_TN = 1920


def _pick(m, cands):
    for c in cands:
        if m % c == 0:
            return c
    raise ValueError(f"no tile for {m}")


def _cparams(sem, vmem_bytes):
    return pltpu.CompilerParams(dimension_semantics=sem,
                                vmem_limit_bytes=int(min(VMEM_CAP, max(vmem_bytes, 16 << 20))))


def _sigmoid(x):
    return 1.0 / (1.0 + jnp.exp(-x))


def _split3(x):
    hi = x.astype(BF16)
    r1 = x - hi.astype(F32)
    mid = r1.astype(BF16)
    lo = (r1 - mid.astype(F32)).astype(BF16)
    return hi, mid, lo


def _select_dot(x, sel):
    return sum(jnp.dot(p, sel, preferred_element_type=F32) for p in _split3(x))


def _dot_select(sel, x):
    return sum(jnp.dot(sel, p, preferred_element_type=F32) for p in _split3(x))


def _softplus(x):
    return jnp.maximum(x, 0.0) + jnp.log1p(jnp.exp(-jnp.abs(x)))


def _rms_bf16(x, g):
    ms = jnp.mean(x * x, axis=-1, keepdims=True)
    return (x * lax.rsqrt(ms + EPS) * g).astype(BF16)


def _in_proj_kernel(npt, xp_ref, xs_ref, g_ref, w_ref, o_ref, h_ref):
    i = pl.program_id(0)

    @pl.when(jnp.logical_and(pl.program_id(1) == 0, i < npt))
    def _():
        h_ref[...] = _rms_bf16(xp_ref[...], g_ref[...])

    @pl.when(jnp.logical_and(pl.program_id(1) == 0, i >= npt))
    def _():
        h_ref[...] = _rms_bf16(xs_ref[...], g_ref[...])

    o_ref[...] = lax.dot_general(h_ref[...], w_ref[...], (((1,), (1,)), ((), ())), preferred_element_type=F32)


def _split_maps(npt):
    return (lambda i, *_: (jnp.minimum(i, npt - 1), 0)), (lambda i, *_: (jnp.maximum(i - npt, 0), 0))


def _row_split(xp, xs, lp_rows):
    if xs is None:
        return xp, xp, lp_rows, xp.shape[0] - lp_rows, xp.shape[0]
    return xp, xs, xp.shape[0], xs.shape[0], xp.shape[0]


def _in_proj(xp, xs, lp_rows, g, w, layer):
    xp, xs, lp, ms, first_rows = _row_split(xp, xs, lp_rows)
    tm = _pick(math.gcd(lp, ms), (512, 256, 128, 64))
    tn = PROJ_TN
    npt = first_rows // tm
    pmap, smap = _split_maps(npt)
    vmem = 4 * tm * D_MODEL * 4 + tm * D_MODEL * 2 + 2 * D_MODEL * tn * 2 + 2 * tm * tn * 4 + (4 << 20)
    return pl.pallas_call(
        functools.partial(_in_proj_kernel, npt),
        grid=((lp + ms) // tm, PACKED // tn),
        in_specs=[pl.BlockSpec((tm, D_MODEL), pmap),
                  pl.BlockSpec((tm, D_MODEL), smap),
                  pl.BlockSpec((1, D_MODEL), lambda i, j: (0, 0)),
                  pl.BlockSpec((None, tn, D_MODEL), lambda i, j: (layer, j, 0))],
        out_specs=pl.BlockSpec((tm, tn), lambda i, j: (i, j)),
        out_shape=jax.ShapeDtypeStruct((lp + ms, PACKED), F32),
        scratch_shapes=[pltpu.VMEM((tm, D_MODEL), BF16)],
        compiler_params=_cparams(("parallel", "arbitrary"), vmem),
        name="in_proj",
    )(xp, xs, g, w)


def _ssd_kernel(ncp, z_ref, xs_ref, bc_ref, sm_ref, st_in_ref, cpx_ref, cpb_ref,
                cwx_ref, cbx_ref, cwb_ref, cbb_ref, dtb_ref, alog_ref, dexp_ref, ng_ref, e_ref,
                y_ref, hp_ref, hs_ref, st_scr, fx_scr, fb_scr):
    c = pl.program_id(0)
    q = CHUNK

    @pl.when(c == 0)
    def _():
        st_scr[...] = jnp.zeros(st_scr.shape, F32)

    @pl.when(c >= ncp)
    def _():
        st_scr[...] = st_in_ref[...].T

    @pl.when(jnp.logical_or(c == 0, c >= ncp))
    def _():
        fx_scr[0:8, :] = cpx_ref[0]
        fb_scr[0:8, :] = cpb_ref[0]

    def conv(u_ref, f_scr, w_ref, b_ref):
        u = u_ref[...]
        f_scr[8:8 + q, :] = u
        y = b_ref[...] + f_scr[5:5 + q, :] * w_ref[0:1, :]
        for j in range(1, SSD_CONV):
            y = y + f_scr[5 + j:5 + j + q, :] * w_ref[j:j + 1, :]
        f_scr[0:8, :] = u[q - 8:q, :]
        return y * _sigmoid(y)

    yield
    dt = _softplus(sm_ref[...] + dtb_ref[...])
    da = dt * (-jnp.exp(alog_ref[...]))
    r64 = lax.broadcasted_iota(jnp.int32, (q, q), 0)
    c64 = lax.broadcasted_iota(jnp.int32, (q, q), 1)
    tril = (c64 <= r64).astype(BF16)
    cs = _dot_select(tril, da)
    yield
    xs = conv(xs_ref, fx_scr, cwx_ref, cbx_ref)
    yield
    both = _select_dot(jnp.concatenate([dt, cs], axis=0), e_ref[...])
    dt_e = both[0:q]
    cs_e = both[q:2 * q]
    yield
    bcv = conv(bc_ref, fb_scr, cwb_ref, cbb_ref)
    yield

    row = lax.broadcasted_iota(jnp.int32, (q, SSD_WIDTH), 0)
    sidx = jnp.bitwise_and(lax.broadcasted_iota(jnp.int32, (q, SSD_WIDTH), 1), q - 1)
    cs_row = jnp.sum(jnp.where(sidx == row, cs_e, 0.0), axis=0, keepdims=True)
    cs_last = cs_e[q - 1:q, :]
    lmat = jnp.exp(jnp.where(sidx <= row, cs_e - cs_row, -jnp.inf))
    yield

    xdt = xs * dt_e
    xdt_end = (xdt * jnp.exp(cs_last - cs_e)).astype(BF16)
    xdt_b = xdt.astype(BF16)
    yield
    ecs = jnp.exp(cs_e)
    st = st_scr[...]
    st_b = st.astype(BF16)
    bcb = bcv.astype(BF16)
    yield

    r128 = lax.broadcasted_iota(jnp.int32, (2 * q, LANE), 0)
    c128 = lax.broadcasted_iota(jnp.int32, (2 * q, LANE), 1)
    bd2 = (r128 // q) == (c128 // q)

    hg = SSD_HEADS // SSD_GROUPS
    gw = hg * SSD_HEAD_DIM
    y_parts = []
    new_states = []
    for g in range(SSD_GROUPS):
        bm = bcb[:, g * SSD_STATE:(g + 1) * SSD_STATE]
        cm = bcb[:, (SSD_GROUPS + g) * SSD_STATE:(SSD_GROUPS + g + 1) * SSD_STATE]
        bm_rep = jnp.concatenate([bm] * hg, axis=0)
        cbt = lax.dot_general(cm, bm_rep, (((1,), (1,)), ((), ())),
                              preferred_element_type=F32)
        m_all = (cbt * lmat[:, g * gw:(g + 1) * gw]).astype(BF16)
        y_off = jnp.dot(cm, st_b[:, g * gw:(g + 1) * gw], preferred_element_type=F32)
        yield
        for j in range(gw // LANE):
            col = g * gw + j * LANE
            xj = xdt_b[:, col:col + LANE]
            xd = jnp.where(bd2, jnp.concatenate([xj, xj], axis=0), jnp.zeros((), BF16))
            y_parts.append(jnp.dot(m_all[:, j * LANE:(j + 1) * LANE], xd, preferred_element_type=F32)
                           + y_off[:, j * LANE:(j + 1) * LANE] * ecs[:, col:col + LANE])
            yield
        new_states.append(lax.dot_general(bm, xdt_end[:, g * gw:(g + 1) * gw], (((0,), (0,)), ((), ())),
                                          preferred_element_type=F32))
        yield
    y = jnp.concatenate(y_parts, axis=1)
    st_new = st * jnp.exp(cs_last) + jnp.concatenate(new_states, axis=1)
    st_scr[...] = st_new
    yield

    y = y + dexp_ref[...] * xs
    zv = z_ref[...]
    y = y * (zv * _sigmoid(zv))
    yield
    outs = []
    for g in range(SSD_GROUPS):
        yg = y[:, g * gw:(g + 1) * gw]
        ms = jnp.mean(yg * yg, axis=-1, keepdims=True)
        outs.append(yg * lax.rsqrt(ms + EPS))
    y_ref[...] = jnp.concatenate(outs, axis=1) * ng_ref[...]

    def finish():
        @pl.when(c == ncp - 1)
        def _():
            hp_ref[...] = st_scr[...].T

        @pl.when(c >= ncp)
        def _():
            hs_ref[...] = st_scr[...].T

    return finish


def _ssd_call(proj, ncp, layer, state, cpx, cpb, cwx, cbx, cwb, cbb, dtb, alog, dexp, ng, emat):
    m = proj.shape[0]
    q = CHUNK
    nstreams = state.shape[1]
    smap = lambda c: (jnp.maximum(c - (ncp - 1), 0), 0, 0)
    stream = lambda c: jnp.maximum(c - ncp, 0)
    const2 = lambda c: (0, 0)
    return dict(
        in_specs=[pl.BlockSpec((q, 1024), lambda c: (c, C_Z // 1024)),
                  pl.BlockSpec((q, 1024), lambda c: (c, C_XS // 1024)),
                  pl.BlockSpec((q, 512), lambda c: (c, C_BC // 512)),
                  pl.BlockSpec((q, LANE), lambda c: (c, C_SM // LANE)),
                  pl.BlockSpec((None, None, SSD_WIDTH, SSD_STATE), lambda c: (layer, stream(c), 0, 0)),
                  pl.BlockSpec((1, 8, 1024), smap),
                  pl.BlockSpec((1, 8, 512), smap),
                  pl.BlockSpec((SSD_CONV, 1024), const2),
                  pl.BlockSpec((1, 1024), const2),
                  pl.BlockSpec((SSD_CONV, 512), const2),
                  pl.BlockSpec((1, 512), const2),
                  pl.BlockSpec((1, LANE), const2),
                  pl.BlockSpec((1, LANE), const2),
                  pl.BlockSpec((1, SSD_WIDTH), const2),
                  pl.BlockSpec((1, SSD_WIDTH), const2),
                  pl.BlockSpec((LANE, SSD_WIDTH), const2)],
        out_specs=[pl.BlockSpec((q, SSD_WIDTH), lambda c: (c, 0)),
                   pl.BlockSpec((SSD_WIDTH, SSD_STATE), const2),
                   pl.BlockSpec((None, SSD_WIDTH, SSD_STATE), lambda c: (stream(c), 0, 0))],
        out_shape=[jax.ShapeDtypeStruct((m, SSD_WIDTH), F32),
                   jax.ShapeDtypeStruct((SSD_WIDTH, SSD_STATE), F32),
                   jax.ShapeDtypeStruct((nstreams, SSD_WIDTH, SSD_STATE), F32)],
        scratch_shapes=[pltpu.VMEM((SSD_STATE, SSD_WIDTH), F32),
                        pltpu.VMEM((8 + q, 1024), F32),
                        pltpu.VMEM((8 + q, 512), F32)],
        args=(proj, proj, proj, proj, state, cpx, cpb, cwx, cbx, cwb, cbb, dtb, alog, dexp, ng, emat))


def _gla_kernel(ncp, gq_ref, gk_ref, gv_ref, gg_ref, sm_ref, s_in_ref, wa_ref, ba_ref, ng_ref,
                o_ref, s_out_ref, s_scr):
    c = pl.program_id(0)
    q = CHUNK
    blk = GLA_BLOCK
    kw = GLA_HEADS * GLA_DK

    @pl.when(jnp.logical_or(c == 0, c >= ncp))
    def _():
        s_scr[...] = s_in_ref[0]

    yield
    pre = jnp.dot(sm_ref[...].astype(BF16), wa_ref[...], preferred_element_type=F32) + ba_ref[...]
    yield
    log_a = -_softplus(-pre) * (1.0 / GLA_TAU)
    r64 = lax.broadcasted_iota(jnp.int32, (q, q), 0)
    c64 = lax.broadcasted_iota(jnp.int32, (q, q), 1)
    same_blk = (r64 // blk) == (c64 // blk)
    causal = jnp.logical_and(same_blk, c64 <= r64)
    yield
    b = _dot_select(causal.astype(BF16), log_a)
    yield
    gk = gk_ref[...]
    qt = gq_ref[...] * (GLA_DK ** -0.5) * jnp.exp(b)
    kt = (gk * jnp.exp(-b)).astype(BF16)
    yield
    b_last = jnp.concatenate(
        [jnp.broadcast_to(b[(j + 1) * blk - 1:(j + 1) * blk, :], (blk, kw)) for j in range(q // blk)], axis=0)
    ktil = (gk * jnp.exp(b_last - b)).astype(BF16)
    qt_b = qt.astype(BF16)
    gv_b = gv_ref[...].astype(BF16)
    yield

    lane_head = lax.broadcasted_iota(jnp.int32, (q, kw), 1) // GLA_DK
    o_intra = []
    for h in range(GLA_HEADS):
        qm = jnp.where(lane_head == h, qt_b, jnp.zeros((), BF16))
        att = lax.dot_general(qm, kt, (((1,), (1,)), ((), ())), preferred_element_type=F32)
        yield
        att = jnp.where(causal, att, 0.0).astype(BF16)
        o_intra.append(jnp.dot(att, gv_b[:, h * GLA_DV:(h + 1) * GLA_DV], preferred_element_type=F32))
        yield

    rs = lax.broadcasted_iota(jnp.int32, (GLA_WIDTH, kw), 0) // GLA_DV
    cs_ = lax.broadcasted_iota(jnp.int32, (GLA_WIDTH, kw), 1) // GLA_DK
    bd = rs == cs_
    sw = s_scr[...]
    o_inter = []
    for j in range(q // blk):
        sl = slice(j * blk, (j + 1) * blk)
        o_inter.append(lax.dot_general(qt_b[sl], sw.astype(BF16), (((1,), (1,)), ((), ())),
                                       preferred_element_type=F32))
        yield
        upd = lax.dot_general(gv_b[sl], ktil[sl], (((0,), (0,)), ((), ())),
                              preferred_element_type=F32)
        yield
        decay = jnp.exp(b[(j + 1) * blk - 1:(j + 1) * blk, :])
        sw = sw * decay + jnp.where(bd, upd, 0.0)
        yield
    s_scr[...] = sw
    s_out_ref[0] = sw
    o = jnp.concatenate(o_intra, axis=1) + jnp.concatenate(o_inter, axis=0)
    yield

    gg = gg_ref[...]
    gate = gg * _sigmoid(gg)
    outs = []
    for h in range(GLA_HEADS):
        oh = o[:, h * GLA_DV:(h + 1) * GLA_DV]
        ms = jnp.mean(oh * oh, axis=-1, keepdims=True)
        outs.append(oh * lax.rsqrt(ms + EPS) * ng_ref[...])
    o_ref[...] = jnp.concatenate(outs, axis=1) * gate


def _gla_call(proj, ncp, s_in, wa_pad, ba, ng):
    m = proj.shape[0]
    q = CHUNK
    kw = GLA_HEADS * GLA_DK
    smap = lambda c: (jnp.maximum(c - (ncp - 1), 0), 0, 0)
    const2 = lambda c: (0, 0)
    return dict(
        in_specs=[pl.BlockSpec((q, kw), lambda c: (c, C_GQ // kw)),
                  pl.BlockSpec((q, kw), lambda c: (c, C_GK // kw)),
                  pl.BlockSpec((q, GLA_WIDTH), lambda c: (c, C_GV // GLA_WIDTH)),
                  pl.BlockSpec((q, GLA_WIDTH), lambda c: (c, C_GG // GLA_WIDTH)),
                  pl.BlockSpec((q, LANE), lambda c: (c, C_SM // LANE)),
                  pl.BlockSpec((1, GLA_WIDTH, kw), smap),
                  pl.BlockSpec((LANE, kw), const2),
                  pl.BlockSpec((1, kw), const2),
                  pl.BlockSpec((1, GLA_DV), const2)],
        out_specs=[pl.BlockSpec((q, GLA_WIDTH), lambda c: (c, 0)),
                   pl.BlockSpec((1, GLA_WIDTH, kw), smap)],
        out_shape=[jax.ShapeDtypeStruct((m, GLA_WIDTH), F32),
                   jax.ShapeDtypeStruct(s_in.shape, F32)],
        scratch_shapes=[pltpu.VMEM((GLA_WIDTH, kw), F32)],
        args=(proj, proj, proj, proj, proj, s_in, wa_pad, ba, ng))


def _mixers_kernel(ncp, n_in, n_out, n_scr, *refs):
    i0, i1 = n_in
    o0, o1 = n_out
    s0, s1 = n_scr
    ins, outs, scr = refs[:i0 + i1], refs[i0 + i1:i0 + i1 + o0 + o1], refs[i0 + i1 + o0 + o1:]
    streams = [(_gla_kernel(ncp, *ins[i0:], *outs[o0:], *scr[s0:]), 1),
               (_ssd_kernel(ncp, *ins[:i0], *outs[:o0], *scr[:s0]), 2)]
    finishers = []
    while streams:
        for entry in list(streams):
            for _ in range(entry[1]):
                try:
                    next(entry[0])
                except StopIteration as done:
                    finishers.append(done.value)
                    streams.remove(entry)
                    break
    for finish in finishers:
        if finish is not None:
            finish()


def _mixers(ncp, ssd, gla):
    m = ssd["args"][0].shape[0]
    counts = [(len(ssd[k]), len(gla[k])) for k in ("in_specs", "out_specs", "scratch_shapes")]
    return pl.pallas_call(
        functools.partial(_mixers_kernel, ncp, *counts),
        grid=(m // CHUNK,),
        in_specs=ssd["in_specs"] + gla["in_specs"],
        out_specs=ssd["out_specs"] + gla["out_specs"],
        out_shape=ssd["out_shape"] + gla["out_shape"],
        scratch_shapes=ssd["scratch_shapes"] + gla["scratch_shapes"],
        compiler_params=_cparams(("arbitrary",), 40 << 20),
        name="mixers",
    )(*ssd["args"], *gla["args"])


def _attn_prep_kernel(npt, dq_ref, dk_ref, dv_ref, qg_ref, kg_ref, seg_ref,
                      qa_ref, qb_ref, kn_ref, kb_ref, vt_ref, kp_ref, vp_ref, ks_ref, vs_ref):
    seg = seg_ref[...]
    tm = dq_ref.shape[0]

    def qknorm(x, g):
        ms = _select_dot(x * x, seg) * (1.0 / DIFF_HD)
        return x * lax.rsqrt(ms + EPS) * g

    qn = qknorm(dq_ref[...], qg_ref[...]) * (DIFF_HD ** -0.5 * LOG2E)
    first = (lax.broadcasted_iota(jnp.int32, qn.shape, 1) // DIFF_HD) % 2 == 0
    qa = jnp.where(first, qn, 0.0).astype(BF16)
    qb = jnp.where(first, 0.0, qn).astype(BF16)
    kn = qknorm(dk_ref[...], kg_ref[...])
    kn_ref[...] = kn
    kb = kn.astype(BF16)
    for h in range(DIFF_HEADS):
        sl = slice(h * LANE, (h + 1) * LANE)
        qa_ref[h] = qa[:, sl]
        qb_ref[h] = qb[:, sl]
        kb_ref[h] = kb[:, sl]
    dv = dv_ref[...]
    vt_ref[...] = dv.T.astype(BF16)

    def cache_rows(k_ref, v_ref):
        for h in range(DIFF_HEADS):
            k_ref[pl.ds(h, tm, stride=DIFF_HEADS), :] = kn[:, h * LANE:(h + 1) * LANE]
            v_ref[pl.ds(h, tm, stride=DIFF_HEADS), :] = dv[:, h * LANE:(h + 1) * LANE]

    @pl.when(pl.program_id(0) < npt)
    def _():
        cache_rows(kp_ref, vp_ref)

    @pl.when(pl.program_id(0) >= npt)
    def _():
        cache_rows(ks_ref, vs_ref)


def _attn_prep(proj, lp, qg, kg, segmat):
    m = proj.shape[0]
    ms = m - lp
    tm = _pick(math.gcd(lp, ms), (512, 256, 128))
    w = DIFF_WIDTH
    npt = lp // tm
    pmap, smap = _split_maps(npt)
    row = lambda off: pl.BlockSpec((tm, w), lambda i: (i, off // w))
    const2 = lambda i: (0, 0)
    rows4 = tm * DIFF_HEADS
    hm = pl.BlockSpec((DIFF_HEADS, tm, LANE), lambda i: (0, i, 0))
    return pl.pallas_call(
        functools.partial(_attn_prep_kernel, npt),
        grid=(m // tm,),
        in_specs=[row(C_DQ), row(C_DK), row(C_DV),
                  pl.BlockSpec((1, w), const2), pl.BlockSpec((1, w), const2),
                  pl.BlockSpec((w, w), const2)],
        out_specs=[hm, hm, pl.BlockSpec((tm, w), lambda i: (i, 0)), hm, pl.BlockSpec((w, tm), lambda i: (0, i))]
        + [pl.BlockSpec((rows4, LANE), pmap)] * 2 + [pl.BlockSpec((rows4, LANE), smap)] * 2,
        out_shape=[jax.ShapeDtypeStruct((DIFF_HEADS, m, LANE), BF16), jax.ShapeDtypeStruct((DIFF_HEADS, m, LANE), BF16),
                   jax.ShapeDtypeStruct((m, w), F32), jax.ShapeDtypeStruct((DIFF_HEADS, m, LANE), BF16),
                   jax.ShapeDtypeStruct((w, m), BF16)]
        + [jax.ShapeDtypeStruct((lp * DIFF_HEADS, LANE), F32)] * 2
        + [jax.ShapeDtypeStruct((ms * DIFF_HEADS, LANE), F32)] * 2,
        compiler_params=_cparams(("arbitrary",), 40 << 20),
        name="attn_prep",
    )(proj, proj, proj, qg, kg, segmat)


def _lambda_full(lam_ref, lam_init):
    l = lam_ref[...]
    a = jnp.sum(l[0:1] * l[1:2], axis=-1, keepdims=True)
    b = jnp.sum(l[2:3] * l[3:4], axis=-1, keepdims=True)
    return jnp.exp(a) - jnp.exp(b) + lam_init


ATTN_PEEK = 64
ATTN_HEADS_PER_ITER = 2
ATTN_MAX_SUM = 2.0 ** 64
ATTN_MIN_SUM = 2.0 ** -60


def _fold8(x, op):
    r = x[0:SUBLANE]
    for i in range(1, x.shape[0] // SUBLANE):
        r = op(r, x[i * SUBLANE:(i + 1) * SUBLANE])
    return r


def _attn_prompt_kernel(t, kbk, lam_init, qi_ref, ki_ref, slope_ref, qa_ref, qb_ref, kb_ref, vt_ref, lam_ref,
                        og_ref, o_ref, m_scr, l_scr, acc_scr, b0_scr, bd_scr):
    p = pl.program_id(0)
    qi = qi_ref[p]
    ki = ki_ref[p]
    nb = t // kbk
    w = 2 * t
    nt = (((1,), (1,)), ((), ()))

    @pl.when(p == 0)
    def _():
        b0_scr[...] = lax.broadcasted_iota(jnp.int32, (kbk, LANE), 0).astype(F32)
        kr = lax.broadcasted_iota(jnp.int32, (t, w), 0)
        lane = lax.broadcasted_iota(jnp.int32, (t, w), 1)
        qc = (lane // (2 * kbk)) * kbk + lane % kbk
        corr = jnp.minimum(2 * (qc - kr), 0).astype(F32)
        bd_scr[...] = jnp.where(kr // CHUNK <= qc // CHUNK, corr, -jnp.inf)

    @pl.when(ki == 0)
    def _():
        m_scr[...] = jnp.full(m_scr.shape, -jnp.inf, F32)
        l_scr[...] = jnp.zeros(l_scr.shape, F32)
        acc_scr[...] = jnp.zeros(acc_scr.shape, F32)

    def tile(h, diag):
        slope = slope_ref[h]
        shift = slope * jnp.asarray((ki - qi) * t, F32)
        step = slope * float(kbk)
        qcat = jnp.concatenate([r[h, g * kbk:(g + 1) * kbk, :] for g in range(nb) for r in (qa_ref, qb_ref)],
                               axis=0)
        b0 = slope * b0_scr[...]
        b0 = jnp.concatenate([b0] * (w // LANE), axis=1)
        m_h, l_h, acc_h = m_scr.at[h], l_scr.at[h], acc_scr.at[h]

        def scores(rows, nrows, lo):
            s = lax.dot_general(kb_ref[h, rows, :], qcat[lo:, :], nt, preferred_element_type=F32) + b0[0:nrows, lo:]
            if diag:
                s = s + slope * bd_scr[rows, lo:]
            return s

        m_old = m_h[...]
        peek = scores(slice(0, ATTN_PEEK), ATTN_PEEK, 0)
        yield
        m_used = jnp.maximum(m_old, jnp.max(peek, axis=0, keepdims=True) + shift)
        if diag:
            lane = lax.broadcasted_iota(jnp.int32, (1, w), 1)
            qpos = (lane // (2 * kbk)) * kbk + lane % kbk
            ramp = jnp.maximum(qpos - (ATTN_PEEK - 1), 0).astype(F32)
        else:
            ramp = float(t - ATTN_PEEK)
        m_ref = m_used + slope * ramp
        ones = jnp.ones((SUBLANE, kbk), BF16)
        pv = None
        los = [2 * b * kbk if diag else 0 for b in range(nb)]
        sc = []
        for b in range(nb):
            sc.append(scores(slice(b * kbk, (b + 1) * kbk), kbk, los[b]))
            yield
        for b in range(nb):
            rows = slice(b * kbk, (b + 1) * kbk)
            lo = los[b]
            e = sc[b] - (m_ref[:, lo:] - shift - b * step)
            pt = jnp.exp2(e).astype(BF16)
            yield
            d = jnp.dot(jnp.concatenate([vt_ref[h, :, rows], ones], axis=0), pt,
                        preferred_element_type=F32)
            if b == 0:
                pv = d
            elif lo == 0:
                pv = d + pv
            else:
                pv = jnp.concatenate([pv[:, :lo], d + pv[:, lo:]], axis=1)
            yield
        lsum = pv[DIFF_VD:DIFF_VD + 1, :]
        pv = pv[0:DIFF_VD, :]

        def finish():
            ok = jnp.logical_and(jnp.max(lsum) <= ATTN_MAX_SUM, jnp.min(lsum) >= ATTN_MIN_SUM)

            @pl.when(ok)
            def _():
                a_old = jnp.exp2(m_old - m_ref)
                l_h[...] = a_old * l_h[...] + lsum
                acc_h[...] = a_old * acc_h[...] + pv
                m_h[...] = m_ref

            @pl.when(jnp.logical_not(ok))
            def _():
                def body(b, carry):
                    rows = pl.ds(pl.multiple_of(b * kbk, kbk), kbk)
                    s = scores(rows, kbk, 0) + (shift + slope * jnp.asarray(b * kbk, F32))
                    m_o = m_h[...]
                    m_n = jnp.maximum(m_o, jnp.max(s, axis=0, keepdims=True))
                    pt = jnp.exp2(s - m_n)
                    a = jnp.exp2(m_o - m_n)
                    l_h[...] = a * l_h[...] + jnp.sum(pt, axis=0, keepdims=True)
                    acc_h[...] = a * acc_h[...] + jnp.dot(vt_ref[h, :, rows], pt.astype(BF16),
                                                          preferred_element_type=F32)
                    m_h[...] = m_n
                    return carry

                lax.fori_loop(0, nb, body, 0)

        return finish

    def tiles(heads, diag):
        gens = [tile(h, diag) for h in heads]
        finishers = [None] * len(gens)
        live = list(range(len(gens)))
        while live:
            for n in list(live):
                try:
                    next(gens[n])
                except StopIteration as done:
                    finishers[n] = done.value
                    live.remove(n)
        for finish in finishers:
            finish()

    def head_group(hp, carry):
        heads = [hp * ATTN_HEADS_PER_ITER + n for n in range(ATTN_HEADS_PER_ITER)]

        @pl.when(ki < qi)
        def _():
            tiles(heads, False)

        @pl.when(ki == qi)
        def _():
            tiles(heads, True)
            lam = _lambda_full(lam_ref, lam_init)
            for h in heads:
                on = acc_scr[h] / l_scr[h]
                ot = jnp.concatenate(
                    [on[:, 2 * g * kbk:(2 * g + 1) * kbk] - lam * on[:, (2 * g + 1) * kbk:(2 * g + 2) * kbk]
                     for g in range(nb)], axis=1)
                ms = jnp.mean(ot * ot, axis=0, keepdims=True)
                o_ref[h] = (ot * lax.rsqrt(ms + EPS)).T * (og_ref[...] * (1.0 - lam_init))

        return carry

    lax.fori_loop(0, DIFF_HEADS // ATTN_HEADS_PER_ITER, head_group, 0)


def _attn_prompt(lp, lam_init, qa, qb, kb, vt, slopes, lam_p, og):
    t = _pick(lp, (1024, 512, 256))
    kbk = min(t, 512)
    nq = lp // t
    pairs = [(i, j) for i in range(nq) for j in range(i + 1)]
    qi_l = jnp.asarray(np.array([a for a, _ in pairs], np.int32))
    ki_l = jnp.asarray(np.array([b for _, b in pairs], np.int32))
    nh = DIFF_HEADS
    grid_spec = pltpu.PrefetchScalarGridSpec(
        num_scalar_prefetch=3,
        grid=(len(pairs),),
        in_specs=[pl.BlockSpec((nh, t, LANE), lambda p, qi, ki, sl: (0, qi[p], 0)),
                  pl.BlockSpec((nh, t, LANE), lambda p, qi, ki, sl: (0, qi[p], 0)),
                  pl.BlockSpec((nh, t, LANE), lambda p, qi, ki, sl: (0, ki[p], 0)),
                  pl.BlockSpec((nh, DIFF_VD, t), lambda p, qi, ki, sl: (0, 0, ki[p])),
                  pl.BlockSpec((4, DIFF_HD), lambda p, qi, ki, sl: (0, 0)),
                  pl.BlockSpec((1, DIFF_VD), lambda p, qi, ki, sl: (0, 0))],
        out_specs=pl.BlockSpec((nh, t, LANE), lambda p, qi, ki, sl: (0, qi[p], 0)),
        scratch_shapes=[pltpu.VMEM((nh, 1, 2 * t), F32), pltpu.VMEM((nh, 1, 2 * t), F32),
                        pltpu.VMEM((nh, DIFF_VD, 2 * t), F32),
                        pltpu.VMEM((kbk, LANE), F32), pltpu.VMEM((t, 2 * t), F32)])
    return pl.pallas_call(
        functools.partial(_attn_prompt_kernel, t, kbk, lam_init),
        grid_spec=grid_spec,
        out_shape=jax.ShapeDtypeStruct((nh, lp, LANE), F32),
        compiler_params=_cparams(("arbitrary",), 48 << 20),
        name="attn_prompt",
    )(qi_l, ki_l, slopes, qa, qb, kb, vt.reshape(nh, DIFF_VD, vt.shape[1]), lam_p, og)


def _attn_sample_kernel(past, lam_init, slope_ref, qa_ref, qb_ref, kn_ref, vn_ref, kc_ref, vc_ref, lam_ref, og_ref,
                        o_ref):
    q = CHUNK
    lam = _lambda_full(lam_ref, lam_init)
    kpos = lax.broadcasted_iota(jnp.int32, (1, past), 1).astype(F32)
    r = lax.broadcasted_iota(jnp.int32, (q, q), 0)
    cc = lax.broadcasted_iota(jnp.int32, (q, q), 1)
    new_bias = (past + r - jnp.abs(r - cc)).astype(F32)
    new_bias = jnp.concatenate([new_bias, new_bias], axis=0)
    outs = []
    for h in range(DIFF_HEADS):
        slope = slope_ref[h]
        sl = slice(h * LANE, (h + 1) * LANE)
        kc = kc_ref[pl.ds(h, past, stride=DIFF_HEADS), :].astype(BF16)
        vc = vc_ref[pl.ds(h, past, stride=DIFF_HEADS), :].astype(BF16)
        kn = kn_ref[:, sl].astype(BF16)
        vn = vn_ref[:, sl].astype(BF16)
        qv = jnp.concatenate([qa_ref[h], qb_ref[h]], axis=0)
        s_c = lax.dot_general(qv, kc, (((1,), (1,)), ((), ())), preferred_element_type=F32) + slope * kpos
        s_n = lax.dot_general(qv, kn, (((1,), (1,)), ((), ())), preferred_element_type=F32) + slope * new_bias
        mx = jnp.maximum(jnp.max(s_c, axis=-1, keepdims=True), jnp.max(s_n, axis=-1, keepdims=True))
        p_c = jnp.exp2(s_c - mx)
        p_n = jnp.exp2(s_n - mx)
        den = jnp.sum(p_c, axis=-1, keepdims=True) + jnp.sum(p_n, axis=-1, keepdims=True)
        num = (jnp.dot(p_c.astype(BF16), vc, preferred_element_type=F32)
               + jnp.dot(p_n.astype(BF16), vn, preferred_element_type=F32))
        res = num / den
        o = res[0:q] - lam * res[q:2 * q]
        ms = jnp.mean(o * o, axis=-1, keepdims=True)
        outs.append(o * lax.rsqrt(ms + EPS) * (og_ref[...] * (1.0 - lam_init)))
    o_ref[...] = jnp.concatenate(outs, axis=1)


def _attn_sample(lp, nb, layer, lam_init, qa, qb, kn, proj, cache_k, cache_v, slopes, lam_p, og):
    depth, _, past = cache_k.shape[:3]
    q = CHUNK
    w = DIFF_WIDTH
    off = lp // q
    cache_k = cache_k.reshape(depth, nb, past * DIFF_HEADS, DIFF_VD)
    cache_v = cache_v.reshape(depth, nb, past * DIFF_HEADS, DIFF_VD)
    cache_block = (None, None, past * DIFF_HEADS, DIFF_VD)
    grid_spec = pltpu.PrefetchScalarGridSpec(
        num_scalar_prefetch=1,
        grid=(nb,),
        in_specs=[pl.BlockSpec((DIFF_HEADS, q, LANE), lambda b, sl: (0, off + b, 0)),
                  pl.BlockSpec((DIFF_HEADS, q, LANE), lambda b, sl: (0, off + b, 0)),
                  pl.BlockSpec((q, w), lambda b, sl: (off + b, 0)),
                  pl.BlockSpec((q, w), lambda b, sl: (off + b, C_DV // w)),
                  pl.BlockSpec(cache_block, lambda b, sl: (layer, b, 0, 0)),
                  pl.BlockSpec(cache_block, lambda b, sl: (layer, b, 0, 0)),
                  pl.BlockSpec((4, DIFF_HD), lambda b, sl: (0, 0)),
                  pl.BlockSpec((1, DIFF_VD), lambda b, sl: (0, 0))],
        out_specs=pl.BlockSpec((q, w), lambda b, sl: (b, 0)))
    return pl.pallas_call(
        functools.partial(_attn_sample_kernel, past, lam_init),
        grid_spec=grid_spec,
        out_shape=jax.ShapeDtypeStruct((nb * q, DIFF_WIDTH), F32),
        compiler_params=_cparams(("arbitrary",), 32 << 20),
        name="attn_sample",
    )(slopes, qa, qb, kn, proj, cache_k, cache_v, lam_p, og)


def _out_proj_kernel(npt_x, npt, xp_ref, xs_ref, ys_ref, yap_ref, yas_ref, yg_ref, w_ref, o_ref):
    is_prompt = pl.program_id(0) < npt
    yap = jnp.concatenate([yap_ref[h] for h in range(DIFF_HEADS)], axis=1)
    ya = jnp.where(is_prompt, yap, yas_ref[...])
    acc = jnp.dot(ys_ref[...].astype(BF16), w_ref[0:SSD_WIDTH, :], preferred_element_type=F32)
    acc = acc + jnp.dot(ya.astype(BF16), w_ref[SSD_WIDTH:SSD_WIDTH + DIFF_WIDTH, :],
                        preferred_element_type=F32)
    acc = acc + jnp.dot(yg_ref[...].astype(BF16), w_ref[SSD_WIDTH + DIFF_WIDTH:, :],
                        preferred_element_type=F32)
    o_ref[...] = jnp.where(pl.program_id(0) < npt_x, xp_ref[...], xs_ref[...]) + acc


def _out_proj(xp, xs, lp_rows, ys, yap, yas, yg, w, layer):
    xp, xs, lp, ms, first_rows = _row_split(xp, xs, lp_rows)
    tm = _pick(math.gcd(lp, ms), (512, 256, 128, 64))
    npt = lp // tm
    pmap, smap = _split_maps(npt)
    xpmap, xsmap = _split_maps(first_rows // tm)
    vmem = 2 * (5 * tm * D_MODEL * 4) + D_MODEL * D_MODEL * 2 + (4 << 20)
    return pl.pallas_call(
        functools.partial(_out_proj_kernel, first_rows // tm, npt),
        grid=((lp + ms) // tm,),
        in_specs=[pl.BlockSpec((tm, D_MODEL), xpmap),
                  pl.BlockSpec((tm, D_MODEL), xsmap),
                  pl.BlockSpec((tm, SSD_WIDTH), lambda i: (i, 0)),
                  pl.BlockSpec((DIFF_HEADS, tm, LANE), lambda i: (0, jnp.minimum(i, npt - 1), 0)),
                  pl.BlockSpec((tm, DIFF_WIDTH), smap),
                  pl.BlockSpec((tm, GLA_WIDTH), lambda i: (i, 0)),
                  pl.BlockSpec((None, D_MODEL, D_MODEL), lambda i: (layer, 0, 0), pipeline_mode=pl.Buffered(1))],
        out_specs=pl.BlockSpec((tm, D_MODEL), lambda i: (i, 0)),
        out_shape=jax.ShapeDtypeStruct((lp + ms, D_MODEL), F32),
        compiler_params=_cparams(("parallel",), vmem),
        name="out_proj",
    )(xp, xs, ys, yap, yas, yg, w)


def _mlp_kernel(x_ref, g_ref, w1_ref, w2_ref, o_ref, h_ref):
    @pl.when(pl.program_id(1) == 0)
    def _():
        x = x_ref[...]
        h_ref[...] = _rms_bf16(x, g_ref[...])
        o_ref[...] = x

    h = h_ref[...]
    half = w1_ref.shape[1] // 2
    a0 = jnp.dot(h, w1_ref[:, 0:half], preferred_element_type=F32)
    a1 = jnp.dot(h, w1_ref[:, half:], preferred_element_type=F32)
    a0 = jnp.square(jnp.maximum(a0, 0.0)).astype(BF16)
    upd = jnp.dot(a0, w2_ref[0:half, :], preferred_element_type=F32)
    a1 = jnp.square(jnp.maximum(a1, 0.0)).astype(BF16)
    upd = jnp.dot(a1, w2_ref[half:, :], preferred_element_type=F32) + upd
    o_ref[...] += upd


def _mlp(x, g, w1, w2, layer):
    m = x.shape[0]
    tm = _pick(m, (512, 256, 128, 64))
    tf = 1024
    vmem = 4 * tm * D_MODEL * 4 + tm * D_MODEL * 2 + 4 * D_MODEL * tf * 2 + 3 * tm * tf * 4 + tm * D_MODEL * 4 + (4 << 20)
    return pl.pallas_call(
        _mlp_kernel,
        grid=(m // tm, D_FF // tf),
        in_specs=[pl.BlockSpec((tm, D_MODEL), lambda i, f: (i, 0)),
                  pl.BlockSpec((1, D_MODEL), lambda i, f: (0, 0)),
                  pl.BlockSpec((None, D_MODEL, tf), lambda i, f: (layer, 0, f)),
                  pl.BlockSpec((None, tf, D_MODEL), lambda i, f: (layer, f, 0))],
        out_specs=pl.BlockSpec((tm, D_MODEL), lambda i, f: (i, 0)),
        out_shape=jax.ShapeDtypeStruct((m, D_MODEL), F32),
        scratch_shapes=[pltpu.VMEM((tm, D_MODEL), BF16)],
        compiler_params=_cparams(("parallel", "arbitrary"), vmem),
        name="mlp",
    )(x, g, w1, w2)


def _pack_w_in_kernel(w_ref, o_ref):
    x = w_ref[...]
    o_ref[...] = jnp.concatenate(
        [x[IN_RANGE[n][0]:IN_RANGE[n][1]] for n in PACK_ORDER]
        + [jnp.zeros((PACKED - IN_COLS, x.shape[1]), x.dtype)], axis=0).astype(BF16)


def _pack_w_in(w):
    depth, rows, cols = w.shape
    wt = jnp.swapaxes(w, 1, 2)
    tc = _pick(rows, (256, 128))
    return pl.pallas_call(
        _pack_w_in_kernel,
        grid=(depth, rows // tc),
        in_specs=[pl.BlockSpec((None, cols, tc), lambda l, i: (l, 0, i))],
        out_specs=pl.BlockSpec((None, PACKED, tc), lambda l, i: (l, 0, i)),
        out_shape=jax.ShapeDtypeStruct((depth, PACKED, rows), BF16),
        compiler_params=_cparams(("parallel", "parallel"), 32 << 20),
        name="pack_w_in",
    )(wt)


def _pad_lanes(v, width=LANE):
    v = v.reshape(1, -1)
    return jnp.pad(v, ((0, 0), (0, width - v.shape[1])))


def kernel(x_prompt, x_sample, cache_diff_k, cache_diff_v, state_ssd_conv, state_ssd, state_gla, norm1_g, w_in, ssd_conv_w, ssd_conv_b, ssd_dt_bias, ssd_a_log, ssd_d, ssd_norm_g, diff_qn_g, diff_kn_g, diff_lambda, diff_out_g, gla_wa2, gla_ba, gla_norm_g, w_out, norm2_g, w_mlp1, w_mlp2):
    bp, lp, d = x_prompt.shape
    nb, ls, _ = x_sample.shape
    depth = w_in.shape[0]
    past = cache_diff_k.shape[2]
    assert bp == 1 and d == D_MODEL and ls == CHUNK and lp % CHUNK == 0 and past % CHUNK == 0
    ncp = lp // CHUNK
    m = lp + nb * ls

    xp, xs = x_prompt.reshape(lp, d), x_sample.reshape(nb * ls, d)

    emat = (jnp.arange(LANE)[:, None] == (jnp.arange(SSD_WIDTH)[None, :] // SSD_HEAD_DIM)).astype(BF16)
    segmat = ((jnp.arange(DIFF_WIDTH)[:, None] // DIFF_HD) == (jnp.arange(DIFF_WIDTH)[None, :] // DIFF_HD)).astype(BF16)
    slopes = jnp.exp2(-8.0 * jnp.arange(1, DIFF_HEADS + 1, dtype=F32) / DIFF_HEADS) * LOG2E
    eye_h = jnp.eye(GLA_HEADS, dtype=F32)
    w_in_b, w_out_b = _pack_w_in(w_in), w_out.astype(BF16)
    w_mlp1_b, w_mlp2_b = w_mlp1.astype(BF16), w_mlp2.astype(BF16)

    outs = {k: [] for k in ("kp", "vp", "cp", "hp", "sp", "ks", "vs", "cs", "hs", "ss")}
    for l in range(depth):
        lam_init = 0.8 - 0.6 * math.exp(-0.3 * l)
        proj = _in_proj(xp, xs, lp, norm1_g[l].reshape(1, d), w_in_b, l)

        cprev = jnp.concatenate([jnp.zeros((1, SSD_CONV - 1, SSD_CONV_DIM), F32), state_ssd_conv[l]], axis=0)
        cprev = jnp.pad(cprev, ((0, 0), (8 - (SSD_CONV - 1), 0), (0, 0)))
        cw, cb = ssd_conv_w[l], ssd_conv_b[l].reshape(1, -1)
        ssd_call = _ssd_call(
            proj, ncp, l, state_ssd.reshape(depth, nb, SSD_WIDTH, SSD_STATE),
            cprev[:, :, :SSD_WIDTH], cprev[:, :, SSD_WIDTH:],
            cw[:, :SSD_WIDTH], cb[:, :SSD_WIDTH], cw[:, SSD_WIDTH:], cb[:, SSD_WIDTH:],
            _pad_lanes(ssd_dt_bias[l]), _pad_lanes(ssd_a_log[l]),
            jnp.repeat(ssd_d[l], SSD_HEAD_DIM).reshape(1, -1), ssd_norm_g[l].reshape(1, -1), emat)
        xbc_raw = proj[:, C_XS:C_XS + SSD_CONV_DIM]
        outs["cp"].append(xbc_raw[lp - (SSD_CONV - 1):lp][None])
        outs["cs"].append(xbc_raw[lp:].reshape(nb, ls, SSD_CONV_DIM)[:, ls - (SSD_CONV - 1):])

        s_all = jnp.concatenate([jnp.zeros((1,) + state_gla.shape[2:], F32), state_gla[l]], axis=0)
        s_in = jnp.einsum('bhkv,hg->bhvgk', s_all, eye_h).reshape(nb + 1, GLA_WIDTH, GLA_HEADS * GLA_DK)
        wa_pad = jnp.zeros((LANE, GLA_HEADS * GLA_DK), F32).at[GLA_RANK:2 * GLA_RANK].set(gla_wa2[l]).astype(BF16)
        gla_call = _gla_call(proj, ncp, s_in, wa_pad, gla_ba[l].reshape(1, -1), gla_norm_g[l].reshape(1, -1))
        y_ssd, h_p, h_s, y_gla, s_out = _mixers(ncp, ssd_call, gla_call)
        outs["hp"].append(h_p.reshape(1, SSD_HEADS, SSD_HEAD_DIM, SSD_STATE))
        outs["hs"].append(h_s.reshape(nb, SSD_HEADS, SSD_HEAD_DIM, SSD_STATE))
        s5 = s_out.reshape(nb + 1, GLA_HEADS, GLA_DV, GLA_HEADS, GLA_DK)
        s_new = jnp.stack([s5[:, h, :, h, :] for h in range(GLA_HEADS)], axis=1).transpose(0, 1, 3, 2)
        outs["sp"].append(s_new[0:1])
        outs["ss"].append(s_new[1:])

        qg = jnp.tile(diff_qn_g[l], 2 * DIFF_HEADS).reshape(1, -1)
        kg = jnp.tile(diff_kn_g[l], 2 * DIFF_HEADS).reshape(1, -1)
        qa, qb, kn, kb, vt, kp4, vp4, ks4, vs4 = _attn_prep(proj, lp, qg, kg, segmat)
        og = diff_out_g[l].reshape(1, -1)
        ya_p = _attn_prompt(lp, lam_init, qa, qb, kb, vt, slopes, diff_lambda[l], og)
        ya_s = _attn_sample(lp, nb, l, lam_init, qa, qb, kn, proj, cache_diff_k, cache_diff_v,
                            slopes, diff_lambda[l], og)
        outs["kp"].append(kp4.reshape(1, lp, DIFF_HEADS, 2 * DIFF_HD))
        outs["ks"].append(ks4.reshape(nb, ls, DIFF_HEADS, 2 * DIFF_HD))
        outs["vp"].append(vp4.reshape(1, lp, DIFF_HEADS, DIFF_VD))
        outs["vs"].append(vs4.reshape(nb, ls, DIFF_HEADS, DIFF_VD))

        x1 = _out_proj(xp, xs, lp, y_ssd, ya_p, ya_s, y_gla, w_out_b, l)
        xp, xs = _mlp(x1, norm2_g[l].reshape(1, d), w_mlp1_b, w_mlp2_b, l), None

    st = {k: jnp.stack(v) for k, v in outs.items()}
    return (xp[:lp].reshape(1, lp, d), xp[lp:].reshape(nb, ls, d),
            st["kp"], st["vp"], st["cp"], st["hp"], st["sp"],
            st["ks"], st["vs"], st["cs"], st["hs"], st["ss"])
```
